```python
import math
import jax, jax.numpy as jnp
from jax import lax
import numpy as np

D_MODEL = 2048
BATCH = 8
SEQ = 4096
DEPTH = 2

PL_DIM = 256
D_FF = 4 * D_MODEL
NORM_EPS = 1e-6
N_EVEN = (DEPTH + 1) // 2
N_ODD = DEPTH // 2
S5_WIDTH = D_MODEL // 4
S5_GROUP = 16
S5_GROUPS = S5_WIDTH // S5_GROUP
S5_STATE = 64
SSD_WIDTH = D_MODEL - S5_WIDTH
SSD_HEAD_DIM = 64
SSD_HEADS = SSD_WIDTH // SSD_HEAD_DIM
SSD_GROUPS = 4
SSD_STATE = 128
SSD_CONV = 4
SSD_CHUNK = 128
SSD_CONV_DIM = SSD_WIDTH + 2 * SSD_GROUPS * SSD_STATE
EVEN_IN = S5_WIDTH + SSD_WIDTH + SSD_CONV_DIM + SSD_HEADS
EVEN_MIX = S5_WIDTH + SSD_WIDTH
RWKV_WIDTH = D_MODEL // 2
RWKV_HEAD_DIM = 64
RWKV_HEADS = RWKV_WIDTH // RWKV_HEAD_DIM
RWKV_DECAY_LORA = 96
RWKV_AAA_LORA = 96
RWKV_GATE_LORA = 256
RWKV_GN_EPS = 64e-5
RWKV_IN = 3 * RWKV_WIDTH + RWKV_DECAY_LORA + RWKV_AAA_LORA + RWKV_GATE_LORA
LRU_WIDTH = D_MODEL - RWKV_WIDTH
LRU_BLOCKS = 16
LRU_BLOCK = LRU_WIDTH // LRU_BLOCKS
LRU_CONV = 4
LRU_C = 8.0
ODD_IN = RWKV_IN + 2 * LRU_WIDTH
ODD_MIX = RWKV_WIDTH + LRU_WIDTH

kernel_name = 'hybrid_s5_ssd_rwkv7_rglru_trunk'


def rmsnorm(x, g):
    xf = x.astype(jnp.float32)
    y = xf * lax.rsqrt(jnp.mean(xf * xf, axis=-1, keepdims=True) + NORM_EPS)
    return (y * g.astype(jnp.float32)).astype(x.dtype)


def causal_dwconv(x, w, b):
    k = w.shape[0]
    y = lax.conv_general_dilated(x, w.astype(x.dtype)[:, None, :], window_strides=(1,),
                                 padding=[(k - 1, 0)], dimension_numbers=('NWC', 'WIO', 'NWC'),
                                 feature_group_count=x.shape[-1])
    return y + b.astype(x.dtype)


def token_shift(x):
    return jnp.pad(x, ((0, 0), (1, 0), (0, 0)))[:, :-1]


def s5_mixer(u, lam_re, lam_im, log_step, b_re, b_im, c_re, c_im, d_skip, glu_w, glu_b):
    f32 = jnp.float32
    bsz, seq, _ = u.shape
    uf = u.astype(f32)
    ug = uf.reshape(bsz, seq, S5_GROUPS, S5_GROUP)
    step = jnp.exp(log_step.astype(f32))[:, None]
    lr, li = lam_re.astype(f32), lam_im.astype(f32)
    mag = jnp.exp(lr * step)
    abar_re, abar_im = mag * jnp.cos(li * step), mag * jnp.sin(li * step)
    den = lr * lr + li * li
    nr = abar_re - 1.0
    coef_re = ((nr * lr + abar_im * li) / den)[..., None]
    coef_im = ((abar_im * lr - nr * li) / den)[..., None]
    br, bi = b_re.astype(f32), b_im.astype(f32)
    bbar_re = coef_re * br - coef_im * bi
    bbar_im = coef_re * bi + coef_im * br
    bu_re = jnp.einsum('bsgh,gph->bsgp', ug, bbar_re)
    bu_im = jnp.einsum('bsgh,gph->bsgp', ug, bbar_im)
    a_re = jnp.broadcast_to(abar_re, bu_re.shape)
    a_im = jnp.broadcast_to(abar_im, bu_re.shape)

    def combine(e1, e2):
        a1r, a1i, b1r, b1i = e1
        a2r, a2i, b2r, b2i = e2
        return (a2r * a1r - a2i * a1i, a2r * a1i + a2i * a1r,
                a2r * b1r - a2i * b1i + b2r, a2r * b1i + a2i * b1r + b2i)

    _, _, xr, xi = lax.associative_scan(combine, (a_re, a_im, bu_re, bu_im), axis=1)
    y = (jnp.einsum('ghp,bsgp->bsgh', c_re.astype(f32), xr)
         - jnp.einsum('ghp,bsgp->bsgh', c_im.astype(f32), xi))
    y = y.reshape(bsz, seq, S5_WIDTH) + d_skip.astype(f32) * uf
    act = jax.nn.gelu(y)
    return act * jax.nn.sigmoid(act @ glu_w.astype(f32) + glu_b.astype(f32))


def ssd_chunked(x, da, bm, cm):
    bsz, seq, nh, hd = x.shape
    nc, L, g = seq // SSD_CHUNK, SSD_CHUNK, SSD_GROUPS
    j = nh // g
    x = x.reshape(bsz, nc, L, g, j, hd)
    bm = bm.reshape(bsz, nc, L, g, SSD_STATE)
    cm = cm.reshape(bsz, nc, L, g, SSD_STATE)
    a_cum = jnp.cumsum(da.reshape(bsz, nc, L, g, j).transpose(0, 3, 4, 1, 2), axis=-1)
    mask = jnp.tril(jnp.ones((L, L), dtype=bool))
    seg = a_cum[..., :, None] - a_cum[..., None, :]
    decay = jnp.exp(jnp.where(mask, seg, -jnp.inf))
    scores = jnp.einsum('bclgn,bcsgn->bgcls', cm, bm)
    y_diag = jnp.einsum('bgjcls,bcsgjp->bclgjp', scores[:, :, None] * decay, x)
    decay_states = jnp.exp(a_cum[..., -1:] - a_cum).transpose(0, 3, 4, 1, 2)[..., None]
    states = jnp.einsum('bclgn,bclgjp->bcgjpn', bm, x * decay_states)
    chunk_decay = jnp.exp(a_cum[..., -1])

    def step(carry, inp):
        st, dec = inp
        return carry * dec[..., None, None] + st, carry

    init = jnp.zeros((bsz, g, j, hd, SSD_STATE), x.dtype)
    _, prev = lax.scan(step, init, (jnp.moveaxis(states, 1, 0), jnp.moveaxis(chunk_decay, -1, 0)))
    prev = jnp.moveaxis(prev, 0, 1)
    y_off = (jnp.einsum('bclgn,bcgjpn->bclgjp', cm, prev)
             * jnp.exp(a_cum).transpose(0, 3, 4, 1, 2)[..., None])
    return (y_diag + y_off).reshape(bsz, seq, nh, hd)


def ssd_mixer(z, xbc, dt_raw, conv_w, conv_b, dt_bias, a_log, d_skip, norm_g):
    f32 = jnp.float32
    bsz, seq, _ = z.shape
    xbc = jax.nn.silu(causal_dwconv(xbc, conv_w, conv_b).astype(f32))
    xs, bm, cm = jnp.split(xbc, [SSD_WIDTH, SSD_WIDTH + SSD_GROUPS * SSD_STATE], axis=-1)
    xs = xs.reshape(bsz, seq, SSD_HEADS, SSD_HEAD_DIM)
    bm = bm.reshape(bsz, seq, SSD_GROUPS, SSD_STATE)
    cm = cm.reshape(bsz, seq, SSD_GROUPS, SSD_STATE)
    dt = jax.nn.softplus(dt_raw.astype(f32) + dt_bias.astype(f32))
    da = dt * (-jnp.exp(a_log.astype(f32)))
    y = ssd_chunked(xs * dt[..., None], da, bm, cm) + xs * d_skip.astype(f32)[:, None]
    y = y.reshape(bsz, seq, SSD_WIDTH) * jax.nn.silu(z.astype(f32))
    y = y.reshape(bsz, seq, SSD_GROUPS, SSD_WIDTH // SSD_GROUPS)
    y = y * lax.rsqrt(jnp.mean(y * y, axis=-1, keepdims=True) + NORM_EPS)
    return y.reshape(bsz, seq, SSD_WIDTH) * norm_g.astype(f32)


def rwkv7_mixer(f, mu, w0, w_up, a0, a_up, g_up, k_k, k_a, r_k, ln_g, ln_b):
    f32 = jnp.float32
    f = f.astype(f32)
    f = f + (token_shift(f) - f) * mu.astype(f32)
    W = RWKV_WIDTH
    r, k, v, wl, al, gl = jnp.split(
        f, [W, 2 * W, 3 * W, 3 * W + RWKV_DECAY_LORA, 3 * W + RWKV_DECAY_LORA + RWKV_AAA_LORA], axis=-1)
    w = -jax.nn.softplus(-(w0.astype(f32) + jnp.tanh(wl) @ w_up.astype(f32))) - 0.5
    decay = jnp.exp(-jnp.exp(w))
    a = jax.nn.sigmoid(a0.astype(f32) + al @ a_up.astype(f32))
    g = jax.nn.sigmoid(gl) @ g_up.astype(f32)
    kk = k * k_k.astype(f32)
    k = k * (1.0 + (a - 1.0) * k_a.astype(f32))
    bsz, seq, _ = f.shape
    hs = lambda t: t.reshape(bsz, seq, RWKV_HEADS, RWKV_HEAD_DIM)
    r, k, v, kk, a, decay = hs(r), hs(k), hs(v), hs(kk), hs(a), hs(decay)
    kk = kk * lax.rsqrt(jnp.maximum(jnp.sum(kk * kk, axis=-1, keepdims=True), 1e-24))

    def step(state, inp):
        r_t, w_t, k_t, v_t, kk_t, a_t = inp
        sa = jnp.einsum('bhvk,bhk->bhv', state, -kk_t)
        state = (state * w_t[:, :, None, :] + sa[..., None] * (kk_t * a_t)[:, :, None, :]
                 + v_t[..., None] * k_t[:, :, None, :])
        return state, jnp.einsum('bhvk,bhk->bhv', state, r_t)

    tm = lambda t: jnp.moveaxis(t, 1, 0)
    init = jnp.zeros((bsz, RWKV_HEADS, RWKV_HEAD_DIM, RWKV_HEAD_DIM), f32)
    _, y = lax.scan(step, init, (tm(r), tm(decay), tm(k), tm(v), tm(kk), tm(a)))
    y = jnp.moveaxis(y, 0, 1)
    mean = jnp.mean(y, axis=-1, keepdims=True)
    var = jnp.mean(jnp.square(y - mean), axis=-1, keepdims=True)
    y = ((y - mean) * lax.rsqrt(var + RWKV_GN_EPS)).reshape(bsz, seq, W)
    y = y * ln_g.astype(f32) + ln_b.astype(f32)
    bonus = jnp.sum(r * k * r_k.astype(f32), axis=-1, keepdims=True) * v
    return (y + bonus.reshape(bsz, seq, W)) * g


def rglru_mixer(xl, gl, conv_w, conv_b, w_a, b_a, w_x, b_x, lam):
    f32 = jnp.float32
    bsz, seq, _ = xl.shape
    xc = causal_dwconv(xl, conv_w, conv_b).astype(f32)
    xb = xc.reshape(bsz, seq, LRU_BLOCKS, LRU_BLOCK)
    gate_r = jax.nn.sigmoid(jnp.einsum('bshi,hij->bshj', xb, w_a.astype(f32)) + b_a.astype(f32))
    gate_i = jax.nn.sigmoid(jnp.einsum('bshi,hij->bshj', xb, w_x.astype(f32)) + b_x.astype(f32))
    log_a = -LRU_C * gate_r * jax.nn.softplus(-lam.astype(f32))
    a = jnp.exp(log_a)
    mult = jnp.sqrt(jnp.maximum(-jnp.expm1(2.0 * log_a), 0.0))
    mult = mult.at[:, 0].set(1.0)
    bx = xb * gate_i * mult

    def combine(e1, e2):
        a1, b1 = e1
        a2, b2 = e2
        return a1 * a2, a2 * b1 + b2

    _, hseq = lax.associative_scan(combine, (a, bx), axis=1)
    return hseq.reshape(bsz, seq, LRU_WIDTH) * jax.nn.gelu(gl.astype(f32))


def even_mixer(hn, in_proj, out_proj, lam_re, lam_im, log_step, b_re, b_im, c_re, c_im, s5_d,
               glu_w, glu_b, conv_w, conv_b, dt_bias, a_log, ssd_d, ssd_norm):
    proj = hn @ in_proj
    u, z, xbc, dt_raw = jnp.split(
        proj, [S5_WIDTH, S5_WIDTH + SSD_WIDTH, S5_WIDTH + SSD_WIDTH + SSD_CONV_DIM], axis=-1)
    y_a = s5_mixer(u, lam_re, lam_im, log_step, b_re, b_im, c_re, c_im, s5_d, glu_w, glu_b)
    y_b = ssd_mixer(z, xbc, dt_raw, conv_w, conv_b, dt_bias, a_log, ssd_d, ssd_norm)
    y = jnp.concatenate([y_a, y_b], axis=-1).astype(hn.dtype)
    return y @ out_proj


def odd_mixer(hn, in_proj, out_proj, mu, w0, w_up, a0, a_up, g_up, k_k, k_a, r_k, ln_g, ln_b,
              conv_w, conv_b, w_a, b_a, w_x, b_x, lam):
    proj = hn @ in_proj
    rw, xl, gl = jnp.split(proj, [RWKV_IN, RWKV_IN + LRU_WIDTH], axis=-1)
    y_c = rwkv7_mixer(rw, mu, w0, w_up, a0, a_up, g_up, k_k, k_a, r_k, ln_g, ln_b)
    y_d = rglru_mixer(xl, gl, conv_w, conv_b, w_a, b_a, w_x, b_x, lam)
    y = jnp.concatenate([y_c, y_d], axis=-1).astype(hn.dtype)
    return y @ out_proj


def squared_relu_mlp(h, w1, w2):
    return jnp.square(jax.nn.relu(h @ w1)) @ w2


def _fwd_setup_inputs(seed: int = 0) -> dict:
    key = jax.random.key(seed)
    ks = iter(jax.random.split(key, 64))
    f32 = jnp.float32

    def nrm(shape, scale):
        return scale * jax.random.normal(next(ks), shape, f32)

    def unif(shape, lo, hi):
        return jax.random.uniform(next(ks), shape, f32, lo, hi)

    D = D_MODEL
    ne, no = N_EVEN, N_ODD
    x = nrm((BATCH, SEQ, D), 1.0)
    p = nrm((DEPTH, BATCH, SEQ, PL_DIM), 1.0)
    norm_mix = 1.0 + nrm((DEPTH, D), 0.02)
    norm_ffn = 1.0 + nrm((DEPTH, D), 0.02)
    norm_pl = 1.0 + nrm((DEPTH, D), 0.02)
    mlp_w1 = nrm((DEPTH, D, D_FF), D ** -0.5)
    mlp_w2 = nrm((DEPTH, D_FF, D), D_FF ** -0.5)
    pl_proj = nrm((DEPTH, PL_DIM, D), PL_DIM ** -0.5)
    pl_gate = nrm((DEPTH, D, D), D ** -0.5)
    e_in_proj = nrm((ne, D, EVEN_IN), D ** -0.5)
    e_out_proj = nrm((ne, EVEN_MIX, D), EVEN_MIX ** -0.5)
    s5_lam_re = -0.5 + nrm((ne, S5_GROUPS, S5_STATE), 0.01)
    s5_lam_im = math.pi * jnp.arange(S5_STATE, dtype=f32) + nrm((ne, S5_GROUPS, S5_STATE), 0.01)
    s5_log_step = unif((ne, S5_GROUPS), math.log(1e-3), math.log(1e-1))
    s5_b_re = nrm((ne, S5_GROUPS, S5_STATE, S5_GROUP), (2 * S5_GROUP) ** -0.5)
    s5_b_im = nrm((ne, S5_GROUPS, S5_STATE, S5_GROUP), (2 * S5_GROUP) ** -0.5)
    s5_c_re = nrm((ne, S5_GROUPS, S5_GROUP, S5_STATE), (2 * S5_STATE) ** -0.5)
    s5_c_im = nrm((ne, S5_GROUPS, S5_GROUP, S5_STATE), (2 * S5_STATE) ** -0.5)
    s5_d = nrm((ne, S5_WIDTH), 0.5)
    s5_glu_w = nrm((ne, S5_WIDTH, S5_WIDTH), S5_WIDTH ** -0.5)
    s5_glu_b = nrm((ne, S5_WIDTH), 0.02)
    ssd_conv_w = nrm((ne, SSD_CONV, SSD_CONV_DIM), SSD_CONV ** -0.5)
    ssd_conv_b = nrm((ne, SSD_CONV_DIM), 0.02)
    dt0 = jnp.exp(unif((ne, SSD_HEADS), math.log(1e-3), math.log(1e-1)))
    ssd_dt_bias = dt0 + jnp.log(-jnp.expm1(-dt0))
    ssd_a_log = jnp.log(unif((ne, SSD_HEADS), 1.0, 16.0))
    ssd_d = 1.0 + nrm((ne, SSD_HEADS), 0.02)
    ssd_norm = 1.0 + nrm((ne, SSD_WIDTH), 0.02)
    o_in_proj = nrm((no, D, ODD_IN), D ** -0.5)
    o_out_proj = nrm((no, ODD_MIX, D), ODD_MIX ** -0.5)
    rwkv_mu = unif((no, RWKV_IN), 0.0, 1.0)
    rwkv_w0 = jnp.linspace(-6.0, -1.0, RWKV_WIDTH, dtype=f32) + nrm((no, RWKV_WIDTH), 0.1)
    rwkv_w_up = nrm((no, RWKV_DECAY_LORA, RWKV_WIDTH), 0.5 * RWKV_DECAY_LORA ** -0.5)
    rwkv_a0 = nrm((no, RWKV_WIDTH), 0.1)
    rwkv_a_up = nrm((no, RWKV_AAA_LORA, RWKV_WIDTH), 0.5 * RWKV_AAA_LORA ** -0.5)
    rwkv_g_up = nrm((no, RWKV_GATE_LORA, RWKV_WIDTH), RWKV_GATE_LORA ** -0.5)
    rwkv_k_k = 0.85 + nrm((no, RWKV_WIDTH), 0.02)
    rwkv_k_a = 1.0 + nrm((no, RWKV_WIDTH), 0.02)
    rwkv_r_k = nrm((no, RWKV_HEADS, RWKV_HEAD_DIM), 0.1)
    rwkv_ln_g = 1.0 + nrm((no, RWKV_WIDTH), 0.02)
    rwkv_ln_b = nrm((no, RWKV_WIDTH), 0.02)
    lru_conv_w = nrm((no, LRU_CONV, LRU_WIDTH), LRU_CONV ** -0.5)
    lru_conv_b = nrm((no, LRU_WIDTH), 0.02)
    lru_w_a = nrm((no, LRU_BLOCKS, LRU_BLOCK, LRU_BLOCK), LRU_BLOCK ** -0.5)
    lru_b_a = nrm((no, LRU_BLOCKS, LRU_BLOCK), 0.02)
    lru_w_x = nrm((no, LRU_BLOCKS, LRU_BLOCK, LRU_BLOCK), LRU_BLOCK ** -0.5)
    lru_b_x = nrm((no, LRU_BLOCKS, LRU_BLOCK), 0.02)
    a_pow = unif((no, LRU_BLOCKS, LRU_BLOCK), 0.9, 0.999)
    a_base = a_pow ** (1.0 / LRU_C)
    lru_lam = jnp.log(a_base) - jnp.log1p(-a_base)
    norm_final = 1.0 + nrm((D,), 0.02)
    return {'x': x, 'p': p, 'norm_mix': norm_mix, 'norm_ffn': norm_ffn, 'norm_pl': norm_pl,
            'mlp_w1': mlp_w1, 'mlp_w2': mlp_w2, 'pl_proj': pl_proj, 'pl_gate': pl_gate,
            'e_in_proj': e_in_proj, 'e_out_proj': e_out_proj,
            's5_lam_re': s5_lam_re, 's5_lam_im': s5_lam_im, 's5_log_step': s5_log_step,
            's5_b_re': s5_b_re, 's5_b_im': s5_b_im, 's5_c_re': s5_c_re, 's5_c_im': s5_c_im,
            's5_d': s5_d, 's5_glu_w': s5_glu_w, 's5_glu_b': s5_glu_b,
            'ssd_conv_w': ssd_conv_w, 'ssd_conv_b': ssd_conv_b, 'ssd_dt_bias': ssd_dt_bias,
            'ssd_a_log': ssd_a_log, 'ssd_d': ssd_d, 'ssd_norm': ssd_norm,
            'o_in_proj': o_in_proj, 'o_out_proj': o_out_proj,
            'rwkv_mu': rwkv_mu, 'rwkv_w0': rwkv_w0, 'rwkv_w_up': rwkv_w_up, 'rwkv_a0': rwkv_a0,
            'rwkv_a_up': rwkv_a_up, 'rwkv_g_up': rwkv_g_up, 'rwkv_k_k': rwkv_k_k, 'rwkv_k_a': rwkv_k_a,
            'rwkv_r_k': rwkv_r_k, 'rwkv_ln_g': rwkv_ln_g, 'rwkv_ln_b': rwkv_ln_b,
            'lru_conv_w': lru_conv_w, 'lru_conv_b': lru_conv_b, 'lru_w_a': lru_w_a, 'lru_b_a': lru_b_a,
            'lru_w_x': lru_w_x, 'lru_b_x': lru_b_x, 'lru_lam': lru_lam, 'norm_final': norm_final}


def _fwd_reference(x, p, norm_mix, norm_ffn, norm_pl, mlp_w1, mlp_w2, pl_proj, pl_gate,
              e_in_proj, e_out_proj, s5_lam_re, s5_lam_im, s5_log_step, s5_b_re, s5_b_im,
              s5_c_re, s5_c_im, s5_d, s5_glu_w, s5_glu_b, ssd_conv_w, ssd_conv_b, ssd_dt_bias,
              ssd_a_log, ssd_d, ssd_norm, o_in_proj, o_out_proj, rwkv_mu, rwkv_w0, rwkv_w_up,
              rwkv_a0, rwkv_a_up, rwkv_g_up, rwkv_k_k, rwkv_k_a, rwkv_r_k, rwkv_ln_g, rwkv_ln_b,
              lru_conv_w, lru_conv_b, lru_w_a, lru_b_a, lru_w_x, lru_b_x, lru_lam, norm_final):
    h = x
    for i in range(DEPTH):
        hn = rmsnorm(h, norm_mix[i])
        j = i // 2
        if i % 2 == 0:
            mix = even_mixer(hn, e_in_proj[j], e_out_proj[j], s5_lam_re[j], s5_lam_im[j],
                             s5_log_step[j], s5_b_re[j], s5_b_im[j], s5_c_re[j], s5_c_im[j],
                             s5_d[j], s5_glu_w[j], s5_glu_b[j], ssd_conv_w[j], ssd_conv_b[j],
                             ssd_dt_bias[j], ssd_a_log[j], ssd_d[j], ssd_norm[j])
        else:
            mix = odd_mixer(hn, o_in_proj[j], o_out_proj[j], rwkv_mu[j], rwkv_w0[j], rwkv_w_up[j],
                            rwkv_a0[j], rwkv_a_up[j], rwkv_g_up[j], rwkv_k_k[j], rwkv_k_a[j],
                            rwkv_r_k[j], rwkv_ln_g[j], rwkv_ln_b[j], lru_conv_w[j], lru_conv_b[j],
                            lru_w_a[j], lru_b_a[j], lru_w_x[j], lru_b_x[j], lru_lam[j])
        h = h + mix
        h = h + squared_relu_mlp(rmsnorm(h, norm_ffn[i]), mlp_w1[i], mlp_w2[i])
        gate = jax.nn.sigmoid(rmsnorm(h, norm_pl[i]) @ pl_gate[i])
        h = h + gate * (p[i] @ pl_proj[i])
    return rmsnorm(h, norm_final)


import jax as _jax
import jax.numpy as _jnp

TWIN_FORMAT = 'train_step'
FWD_PARAMS = ['x', 'p', 'norm_mix', 'norm_ffn', 'norm_pl', 'mlp_w1', 'mlp_w2', 'pl_proj', 'pl_gate', 'e_in_proj', 'e_out_proj', 's5_lam_re', 's5_lam_im', 's5_log_step', 's5_b_re', 's5_b_im', 's5_c_re', 's5_c_im', 's5_d', 's5_glu_w', 's5_glu_b', 'ssd_conv_w', 'ssd_conv_b', 'ssd_dt_bias', 'ssd_a_log', 'ssd_d', 'ssd_norm', 'o_in_proj', 'o_out_proj', 'rwkv_mu', 'rwkv_w0', 'rwkv_w_up', 'rwkv_a0', 'rwkv_a_up', 'rwkv_g_up', 'rwkv_k_k', 'rwkv_k_a', 'rwkv_r_k', 'rwkv_ln_g', 'rwkv_ln_b', 'lru_conv_w', 'lru_conv_b', 'lru_w_a', 'lru_b_a', 'lru_w_x', 'lru_b_x', 'lru_lam', 'norm_final']
TWIN_WEIGHTS = ['norm_mix', 'norm_ffn', 'norm_pl', 'mlp_w1', 'mlp_w2', 'pl_proj', 'pl_gate', 'e_in_proj', 'e_out_proj', 's5_lam_re', 's5_lam_im', 's5_log_step', 's5_b_re', 's5_b_im', 's5_c_re', 's5_c_im', 's5_d', 's5_glu_w', 's5_glu_b', 'ssd_conv_w', 'ssd_conv_b', 'ssd_dt_bias', 'ssd_a_log', 'ssd_d', 'ssd_norm', 'o_in_proj', 'o_out_proj', 'rwkv_mu', 'rwkv_w0', 'rwkv_w_up', 'rwkv_a0', 'rwkv_a_up', 'rwkv_g_up', 'rwkv_k_k', 'rwkv_k_a', 'rwkv_r_k', 'rwkv_ln_g', 'rwkv_ln_b', 'lru_conv_w', 'lru_conv_b', 'lru_w_a', 'lru_b_a', 'lru_w_x', 'lru_b_x', 'lru_lam', 'norm_final']
TWIN_DIFF_INPUT = 'x'
TWIN_INPUTS = ['x', 'p', 'norm_mix', 'norm_ffn', 'norm_pl', 'mlp_w1', 'mlp_w2', 'pl_proj', 'pl_gate', 'e_in_proj', 'e_out_proj', 's5_lam_re', 's5_lam_im', 's5_log_step', 's5_b_re', 's5_b_im', 's5_c_re', 's5_c_im', 's5_d', 's5_glu_w', 's5_glu_b', 'ssd_conv_w', 'ssd_conv_b', 'ssd_dt_bias', 'ssd_a_log', 'ssd_d', 'ssd_norm', 'o_in_proj', 'o_out_proj', 'rwkv_mu', 'rwkv_w0', 'rwkv_w_up', 'rwkv_a0', 'rwkv_a_up', 'rwkv_g_up', 'rwkv_k_k', 'rwkv_k_a', 'rwkv_r_k', 'rwkv_ln_g', 'rwkv_ln_b', 'lru_conv_w', 'lru_conv_b', 'lru_w_a', 'lru_b_a', 'lru_w_x', 'lru_b_x', 'lru_lam', 'norm_final', 'loss_target', 'm_norm_mix', 'm_norm_ffn', 'm_norm_pl', 'm_mlp_w1', 'm_mlp_w2', 'm_pl_proj', 'm_pl_gate', 'm_e_in_proj', 'm_e_out_proj', 'm_s5_lam_re', 'm_s5_lam_im', 'm_s5_log_step', 'm_s5_b_re', 'm_s5_b_im', 'm_s5_c_re', 'm_s5_c_im', 'm_s5_d', 'm_s5_glu_w', 'm_s5_glu_b', 'm_ssd_conv_w', 'm_ssd_conv_b', 'm_ssd_dt_bias', 'm_ssd_a_log', 'm_ssd_d', 'm_ssd_norm', 'm_o_in_proj', 'm_o_out_proj', 'm_rwkv_mu', 'm_rwkv_w0', 'm_rwkv_w_up', 'm_rwkv_a0', 'm_rwkv_a_up', 'm_rwkv_g_up', 'm_rwkv_k_k', 'm_rwkv_k_a', 'm_rwkv_r_k', 'm_rwkv_ln_g', 'm_rwkv_ln_b', 'm_lru_conv_w', 'm_lru_conv_b', 'm_lru_w_a', 'm_lru_b_a', 'm_lru_w_x', 'm_lru_b_x', 'm_lru_lam', 'm_norm_final', 'v_norm_mix', 'v_norm_ffn', 'v_norm_pl', 'v_mlp_w1', 'v_mlp_w2', 'v_pl_proj', 'v_pl_gate', 'v_e_in_proj', 'v_e_out_proj', 'v_s5_lam_re', 'v_s5_lam_im', 'v_s5_log_step', 'v_s5_b_re', 'v_s5_b_im', 'v_s5_c_re', 'v_s5_c_im', 'v_s5_d', 'v_s5_glu_w', 'v_s5_glu_b', 'v_ssd_conv_w', 'v_ssd_conv_b', 'v_ssd_dt_bias', 'v_ssd_a_log', 'v_ssd_d', 'v_ssd_norm', 'v_o_in_proj', 'v_o_out_proj', 'v_rwkv_mu', 'v_rwkv_w0', 'v_rwkv_w_up', 'v_rwkv_a0', 'v_rwkv_a_up', 'v_rwkv_g_up', 'v_rwkv_k_k', 'v_rwkv_k_a', 'v_rwkv_r_k', 'v_rwkv_ln_g', 'v_rwkv_ln_b', 'v_lru_conv_w', 'v_lru_conv_b', 'v_lru_w_a', 'v_lru_b_a', 'v_lru_w_x', 'v_lru_b_x', 'v_lru_lam', 'v_norm_final']
TWIN_OUTPUTS = ['loss', 'grad_x', 'grad_norm_mix', 'grad_norm_ffn', 'grad_norm_pl', 'grad_mlp_w1', 'grad_mlp_w2', 'grad_pl_proj', 'grad_pl_gate', 'grad_e_in_proj', 'grad_e_out_proj', 'grad_s5_lam_re', 'grad_s5_lam_im', 'grad_s5_log_step', 'grad_s5_b_re', 'grad_s5_b_im', 'grad_s5_c_re', 'grad_s5_c_im', 'grad_s5_d', 'grad_s5_glu_w', 'grad_s5_glu_b', 'grad_ssd_conv_w', 'grad_ssd_conv_b', 'grad_ssd_dt_bias', 'grad_ssd_a_log', 'grad_ssd_d', 'grad_ssd_norm', 'grad_o_in_proj', 'grad_o_out_proj', 'grad_rwkv_mu', 'grad_rwkv_w0', 'grad_rwkv_w_up', 'grad_rwkv_a0', 'grad_rwkv_a_up', 'grad_rwkv_g_up', 'grad_rwkv_k_k', 'grad_rwkv_k_a', 'grad_rwkv_r_k', 'grad_rwkv_ln_g', 'grad_rwkv_ln_b', 'grad_lru_conv_w', 'grad_lru_conv_b', 'grad_lru_w_a', 'grad_lru_b_a', 'grad_lru_w_x', 'grad_lru_b_x', 'grad_lru_lam', 'grad_norm_final', 'delta_norm_mix', 'delta_norm_ffn', 'delta_norm_pl', 'delta_mlp_w1', 'delta_mlp_w2', 'delta_pl_proj', 'delta_pl_gate', 'delta_e_in_proj', 'delta_e_out_proj', 'delta_s5_lam_re', 'delta_s5_lam_im', 'delta_s5_log_step', 'delta_s5_b_re', 'delta_s5_b_im', 'delta_s5_c_re', 'delta_s5_c_im', 'delta_s5_d', 'delta_s5_glu_w', 'delta_s5_glu_b', 'delta_ssd_conv_w', 'delta_ssd_conv_b', 'delta_ssd_dt_bias', 'delta_ssd_a_log', 'delta_ssd_d', 'delta_ssd_norm', 'delta_o_in_proj', 'delta_o_out_proj', 'delta_rwkv_mu', 'delta_rwkv_w0', 'delta_rwkv_w_up', 'delta_rwkv_a0', 'delta_rwkv_a_up', 'delta_rwkv_g_up', 'delta_rwkv_k_k', 'delta_rwkv_k_a', 'delta_rwkv_r_k', 'delta_rwkv_ln_g', 'delta_rwkv_ln_b', 'delta_lru_conv_w', 'delta_lru_conv_b', 'delta_lru_w_a', 'delta_lru_b_a', 'delta_lru_w_x', 'delta_lru_b_x', 'delta_lru_lam', 'delta_norm_final', 'new_m_norm_mix', 'new_m_norm_ffn', 'new_m_norm_pl', 'new_m_mlp_w1', 'new_m_mlp_w2', 'new_m_pl_proj', 'new_m_pl_gate', 'new_m_e_in_proj', 'new_m_e_out_proj', 'new_m_s5_lam_re', 'new_m_s5_lam_im', 'new_m_s5_log_step', 'new_m_s5_b_re', 'new_m_s5_b_im', 'new_m_s5_c_re', 'new_m_s5_c_im', 'new_m_s5_d', 'new_m_s5_glu_w', 'new_m_s5_glu_b', 'new_m_ssd_conv_w', 'new_m_ssd_conv_b', 'new_m_ssd_dt_bias', 'new_m_ssd_a_log', 'new_m_ssd_d', 'new_m_ssd_norm', 'new_m_o_in_proj', 'new_m_o_out_proj', 'new_m_rwkv_mu', 'new_m_rwkv_w0', 'new_m_rwkv_w_up', 'new_m_rwkv_a0', 'new_m_rwkv_a_up', 'new_m_rwkv_g_up', 'new_m_rwkv_k_k', 'new_m_rwkv_k_a', 'new_m_rwkv_r_k', 'new_m_rwkv_ln_g', 'new_m_rwkv_ln_b', 'new_m_lru_conv_w', 'new_m_lru_conv_b', 'new_m_lru_w_a', 'new_m_lru_b_a', 'new_m_lru_w_x', 'new_m_lru_b_x', 'new_m_lru_lam', 'new_m_norm_final', 'new_v_norm_mix', 'new_v_norm_ffn', 'new_v_norm_pl', 'new_v_mlp_w1', 'new_v_mlp_w2', 'new_v_pl_proj', 'new_v_pl_gate', 'new_v_e_in_proj', 'new_v_e_out_proj', 'new_v_s5_lam_re', 'new_v_s5_lam_im', 'new_v_s5_log_step', 'new_v_s5_b_re', 'new_v_s5_b_im', 'new_v_s5_c_re', 'new_v_s5_c_im', 'new_v_s5_d', 'new_v_s5_glu_w', 'new_v_s5_glu_b', 'new_v_ssd_conv_w', 'new_v_ssd_conv_b', 'new_v_ssd_dt_bias', 'new_v_ssd_a_log', 'new_v_ssd_d', 'new_v_ssd_norm', 'new_v_o_in_proj', 'new_v_o_out_proj', 'new_v_rwkv_mu', 'new_v_rwkv_w0', 'new_v_rwkv_w_up', 'new_v_rwkv_a0', 'new_v_rwkv_a_up', 'new_v_rwkv_g_up', 'new_v_rwkv_k_k', 'new_v_rwkv_k_a', 'new_v_rwkv_r_k', 'new_v_rwkv_ln_g', 'new_v_rwkv_ln_b', 'new_v_lru_conv_w', 'new_v_lru_conv_b', 'new_v_lru_w_a', 'new_v_lru_b_a', 'new_v_lru_w_x', 'new_v_lru_b_x', 'new_v_lru_lam', 'new_v_norm_final']
TWIN_LEAF_KINDS = {'loss': 'loss', 'grad_x': 'grad_x', 'grad_norm_mix': 'grad_w', 'grad_norm_ffn': 'grad_w', 'grad_norm_pl': 'grad_w', 'grad_mlp_w1': 'grad_w', 'grad_mlp_w2': 'grad_w', 'grad_pl_proj': 'grad_w', 'grad_pl_gate': 'grad_w', 'grad_e_in_proj': 'grad_w', 'grad_e_out_proj': 'grad_w', 'grad_s5_lam_re': 'grad_w', 'grad_s5_lam_im': 'grad_w', 'grad_s5_log_step': 'grad_w', 'grad_s5_b_re': 'grad_w', 'grad_s5_b_im': 'grad_w', 'grad_s5_c_re': 'grad_w', 'grad_s5_c_im': 'grad_w', 'grad_s5_d': 'grad_w', 'grad_s5_glu_w': 'grad_w', 'grad_s5_glu_b': 'grad_w', 'grad_ssd_conv_w': 'grad_w', 'grad_ssd_conv_b': 'grad_w', 'grad_ssd_dt_bias': 'grad_w', 'grad_ssd_a_log': 'grad_w', 'grad_ssd_d': 'grad_w', 'grad_ssd_norm': 'grad_w', 'grad_o_in_proj': 'grad_w', 'grad_o_out_proj': 'grad_w', 'grad_rwkv_mu': 'grad_w', 'grad_rwkv_w0': 'grad_w', 'grad_rwkv_w_up': 'grad_w', 'grad_rwkv_a0': 'grad_w', 'grad_rwkv_a_up': 'grad_w', 'grad_rwkv_g_up': 'grad_w', 'grad_rwkv_k_k': 'grad_w', 'grad_rwkv_k_a': 'grad_w', 'grad_rwkv_r_k': 'grad_w', 'grad_rwkv_ln_g': 'grad_w', 'grad_rwkv_ln_b': 'grad_w', 'grad_lru_conv_w': 'grad_w', 'grad_lru_conv_b': 'grad_w', 'grad_lru_w_a': 'grad_w', 'grad_lru_b_a': 'grad_w', 'grad_lru_w_x': 'grad_w', 'grad_lru_b_x': 'grad_w', 'grad_lru_lam': 'grad_w', 'grad_norm_final': 'grad_w', 'delta_norm_mix': 'delta_w', 'delta_norm_ffn': 'delta_w', 'delta_norm_pl': 'delta_w', 'delta_mlp_w1': 'delta_w', 'delta_mlp_w2': 'delta_w', 'delta_pl_proj': 'delta_w', 'delta_pl_gate': 'delta_w', 'delta_e_in_proj': 'delta_w', 'delta_e_out_proj': 'delta_w', 'delta_s5_lam_re': 'delta_w', 'delta_s5_lam_im': 'delta_w', 'delta_s5_log_step': 'delta_w', 'delta_s5_b_re': 'delta_w', 'delta_s5_b_im': 'delta_w', 'delta_s5_c_re': 'delta_w', 'delta_s5_c_im': 'delta_w', 'delta_s5_d': 'delta_w', 'delta_s5_glu_w': 'delta_w', 'delta_s5_glu_b': 'delta_w', 'delta_ssd_conv_w': 'delta_w', 'delta_ssd_conv_b': 'delta_w', 'delta_ssd_dt_bias': 'delta_w', 'delta_ssd_a_log': 'delta_w', 'delta_ssd_d': 'delta_w', 'delta_ssd_norm': 'delta_w', 'delta_o_in_proj': 'delta_w', 'delta_o_out_proj': 'delta_w', 'delta_rwkv_mu': 'delta_w', 'delta_rwkv_w0': 'delta_w', 'delta_rwkv_w_up': 'delta_w', 'delta_rwkv_a0': 'delta_w', 'delta_rwkv_a_up': 'delta_w', 'delta_rwkv_g_up': 'delta_w', 'delta_rwkv_k_k': 'delta_w', 'delta_rwkv_k_a': 'delta_w', 'delta_rwkv_r_k': 'delta_w', 'delta_rwkv_ln_g': 'delta_w', 'delta_rwkv_ln_b': 'delta_w', 'delta_lru_conv_w': 'delta_w', 'delta_lru_conv_b': 'delta_w', 'delta_lru_w_a': 'delta_w', 'delta_lru_b_a': 'delta_w', 'delta_lru_w_x': 'delta_w', 'delta_lru_b_x': 'delta_w', 'delta_lru_lam': 'delta_w', 'delta_norm_final': 'delta_w', 'new_m_norm_mix': 'new_m', 'new_m_norm_ffn': 'new_m', 'new_m_norm_pl': 'new_m', 'new_m_mlp_w1': 'new_m', 'new_m_mlp_w2': 'new_m', 'new_m_pl_proj': 'new_m', 'new_m_pl_gate': 'new_m', 'new_m_e_in_proj': 'new_m', 'new_m_e_out_proj': 'new_m', 'new_m_s5_lam_re': 'new_m', 'new_m_s5_lam_im': 'new_m', 'new_m_s5_log_step': 'new_m', 'new_m_s5_b_re': 'new_m', 'new_m_s5_b_im': 'new_m', 'new_m_s5_c_re': 'new_m', 'new_m_s5_c_im': 'new_m', 'new_m_s5_d': 'new_m', 'new_m_s5_glu_w': 'new_m', 'new_m_s5_glu_b': 'new_m', 'new_m_ssd_conv_w': 'new_m', 'new_m_ssd_conv_b': 'new_m', 'new_m_ssd_dt_bias': 'new_m', 'new_m_ssd_a_log': 'new_m', 'new_m_ssd_d': 'new_m', 'new_m_ssd_norm': 'new_m', 'new_m_o_in_proj': 'new_m', 'new_m_o_out_proj': 'new_m', 'new_m_rwkv_mu': 'new_m', 'new_m_rwkv_w0': 'new_m', 'new_m_rwkv_w_up': 'new_m', 'new_m_rwkv_a0': 'new_m', 'new_m_rwkv_a_up': 'new_m', 'new_m_rwkv_g_up': 'new_m', 'new_m_rwkv_k_k': 'new_m', 'new_m_rwkv_k_a': 'new_m', 'new_m_rwkv_r_k': 'new_m', 'new_m_rwkv_ln_g': 'new_m', 'new_m_rwkv_ln_b': 'new_m', 'new_m_lru_conv_w': 'new_m', 'new_m_lru_conv_b': 'new_m', 'new_m_lru_w_a': 'new_m', 'new_m_lru_b_a': 'new_m', 'new_m_lru_w_x': 'new_m', 'new_m_lru_b_x': 'new_m', 'new_m_lru_lam': 'new_m', 'new_m_norm_final': 'new_m', 'new_v_norm_mix': 'new_v', 'new_v_norm_ffn': 'new_v', 'new_v_norm_pl': 'new_v', 'new_v_mlp_w1': 'new_v', 'new_v_mlp_w2': 'new_v', 'new_v_pl_proj': 'new_v', 'new_v_pl_gate': 'new_v', 'new_v_e_in_proj': 'new_v', 'new_v_e_out_proj': 'new_v', 'new_v_s5_lam_re': 'new_v', 'new_v_s5_lam_im': 'new_v', 'new_v_s5_log_step': 'new_v', 'new_v_s5_b_re': 'new_v', 'new_v_s5_b_im': 'new_v', 'new_v_s5_c_re': 'new_v', 'new_v_s5_c_im': 'new_v', 'new_v_s5_d': 'new_v', 'new_v_s5_glu_w': 'new_v', 'new_v_s5_glu_b': 'new_v', 'new_v_ssd_conv_w': 'new_v', 'new_v_ssd_conv_b': 'new_v', 'new_v_ssd_dt_bias': 'new_v', 'new_v_ssd_a_log': 'new_v', 'new_v_ssd_d': 'new_v', 'new_v_ssd_norm': 'new_v', 'new_v_o_in_proj': 'new_v', 'new_v_o_out_proj': 'new_v', 'new_v_rwkv_mu': 'new_v', 'new_v_rwkv_w0': 'new_v', 'new_v_rwkv_w_up': 'new_v', 'new_v_rwkv_a0': 'new_v', 'new_v_rwkv_a_up': 'new_v', 'new_v_rwkv_g_up': 'new_v', 'new_v_rwkv_k_k': 'new_v', 'new_v_rwkv_k_a': 'new_v', 'new_v_rwkv_r_k': 'new_v', 'new_v_rwkv_ln_g': 'new_v', 'new_v_rwkv_ln_b': 'new_v', 'new_v_lru_conv_w': 'new_v', 'new_v_lru_conv_b': 'new_v', 'new_v_lru_w_a': 'new_v', 'new_v_lru_b_a': 'new_v', 'new_v_lru_w_x': 'new_v', 'new_v_lru_b_x': 'new_v', 'new_v_lru_lam': 'new_v', 'new_v_norm_final': 'new_v'}


def _forward(args):
    return _fwd_reference(*[args[k] for k in FWD_PARAMS])


def _output_shape():
    def fwd():
        inp = _fwd_setup_inputs(0)
        return _fwd_reference(*[inp[k] for k in FWD_PARAMS])
    out = _jax.eval_shape(fwd)
    return out.shape, out.dtype

N_MICROBATCH = 1
ADAM_LR = 0.001
ADAM_B1 = 0.9
ADAM_B2 = 0.999
ADAM_EPS = 1e-08
ADAM_WD = 0.01
ADAM_STEP = 10
PER_EXAMPLE_BATCH_AXIS = {'x': 0, 'p': 1, 'loss_target': 0}
SHARED_INPUTS = []
_WEIGHT_DTYPES = {'norm_mix': _jnp.float32, 'norm_ffn': _jnp.float32, 'norm_pl': _jnp.float32, 'mlp_w1': _jnp.float32, 'mlp_w2': _jnp.float32, 'pl_proj': _jnp.float32, 'pl_gate': _jnp.float32, 'e_in_proj': _jnp.float32, 'e_out_proj': _jnp.float32, 's5_lam_re': _jnp.float32, 's5_lam_im': _jnp.float32, 's5_log_step': _jnp.float32, 's5_b_re': _jnp.float32, 's5_b_im': _jnp.float32, 's5_c_re': _jnp.float32, 's5_c_im': _jnp.float32, 's5_d': _jnp.float32, 's5_glu_w': _jnp.float32, 's5_glu_b': _jnp.float32, 'ssd_conv_w': _jnp.float32, 'ssd_conv_b': _jnp.float32, 'ssd_dt_bias': _jnp.float32, 'ssd_a_log': _jnp.float32, 'ssd_d': _jnp.float32, 'ssd_norm': _jnp.float32, 'o_in_proj': _jnp.float32, 'o_out_proj': _jnp.float32, 'rwkv_mu': _jnp.float32, 'rwkv_w0': _jnp.float32, 'rwkv_w_up': _jnp.float32, 'rwkv_a0': _jnp.float32, 'rwkv_a_up': _jnp.float32, 'rwkv_g_up': _jnp.float32, 'rwkv_k_k': _jnp.float32, 'rwkv_k_a': _jnp.float32, 'rwkv_r_k': _jnp.float32, 'rwkv_ln_g': _jnp.float32, 'rwkv_ln_b': _jnp.float32, 'lru_conv_w': _jnp.float32, 'lru_conv_b': _jnp.float32, 'lru_w_a': _jnp.float32, 'lru_b_a': _jnp.float32, 'lru_w_x': _jnp.float32, 'lru_b_x': _jnp.float32, 'lru_lam': _jnp.float32, 'norm_final': _jnp.float32}
MOMENT_SCALE = {'norm_mix': 8.560899e-02, 'norm_ffn': 6.334477e-02, 'norm_pl': 9.600384e-03, 'mlp_w1': 3.167487e-02, 'mlp_w2': 7.033358e-02, 'pl_proj': 2.392653e-02, 'pl_gate': 9.482925e-03, 'e_in_proj': 6.561912e-02, 'e_out_proj': 6.754170e-02, 's5_lam_re': 1.142130e-03, 's5_lam_im': 1.287335e-03, 's5_log_step': 1.042739e+00, 's5_b_re': 8.005930e-04, 's5_b_im': 7.942618e-04, 's5_c_re': 1.610645e-03, 's5_c_im': 1.571917e-03, 's5_d': 3.012225e-02, 's5_glu_w': 1.803049e-03, 's5_glu_b': 7.718983e-03, 'ssd_conv_w': 6.234604e-02, 'ssd_conv_b': 8.234261e-02, 'ssd_dt_bias': 1.651673e-01, 'ssd_a_log': 3.911416e-01, 'ssd_d': 5.219893e-01, 'ssd_norm': 7.999508e-02, 'o_in_proj': 4.270945e-02, 'o_out_proj': 4.944940e-02, 'rwkv_mu': 3.769891e-02, 'rwkv_w0': 1.215513e-02, 'rwkv_w_up': 2.926997e-03, 'rwkv_a0': 1.113683e-02, 'rwkv_a_up': 1.005531e-02, 'rwkv_g_up': 2.381350e-02, 'rwkv_k_k': 4.514874e-02, 'rwkv_k_a': 4.315947e-02, 'rwkv_r_k': 5.674046e-02, 'rwkv_ln_g': 2.371101e-02, 'rwkv_ln_b': 3.404156e-02, 'lru_conv_w': 8.038846e-02, 'lru_conv_b': 2.412670e-01, 'lru_w_a': 9.399460e-03, 'lru_b_a': 1.634791e-02, 'lru_w_x': 1.872164e-02, 'lru_b_x': 3.242948e-02, 'lru_lam': 4.296609e-02, 'norm_final': 1.621427e+01}


def _to_microbatches(a, axis):
    t = _jnp.moveaxis(a, axis, 0)
    t = t.reshape((N_MICROBATCH, t.shape[0] // N_MICROBATCH) + t.shape[1:])
    return _jnp.moveaxis(t, 1, axis + 1)


def setup_inputs(seed: int = 0) -> dict:
    inp = _fwd_setup_inputs(seed)
    key = _jax.random.fold_in(_jax.random.key(seed), 7919)
    shape, _ = _output_shape()
    out = dict(inp)
    out["loss_target"] = _jax.random.normal(_jax.random.fold_in(key, 0), shape, _jnp.float32)
    for i, name in enumerate(TWIN_WEIGHTS):
        w = inp[name].astype(_jnp.float32)
        if MOMENT_SCALE is None:
            s = _jnp.sqrt(_jnp.mean(_jnp.square(w)) + 1e-30)
        else:
            s = MOMENT_SCALE[name]
        km, kv = _jax.random.split(_jax.random.fold_in(key, i + 1))
        out[name] = w
        out["m_" + name] = s * _jax.random.normal(km, w.shape, _jnp.float32)
        out["v_" + name] = (s * s) * _jax.random.uniform(kv, w.shape, _jnp.float32, 0.5, 1.5)
    if N_MICROBATCH > 1:
        for name, axis in PER_EXAMPLE_BATCH_AXIS.items():
            out[name] = _to_microbatches(out[name], axis)
    return {'x': out['x'], 'p': out['p'], 'norm_mix': out['norm_mix'], 'norm_ffn': out['norm_ffn'], 'norm_pl': out['norm_pl'], 'mlp_w1': out['mlp_w1'], 'mlp_w2': out['mlp_w2'], 'pl_proj': out['pl_proj'], 'pl_gate': out['pl_gate'], 'e_in_proj': out['e_in_proj'], 'e_out_proj': out['e_out_proj'], 's5_lam_re': out['s5_lam_re'], 's5_lam_im': out['s5_lam_im'], 's5_log_step': out['s5_log_step'], 's5_b_re': out['s5_b_re'], 's5_b_im': out['s5_b_im'], 's5_c_re': out['s5_c_re'], 's5_c_im': out['s5_c_im'], 's5_d': out['s5_d'], 's5_glu_w': out['s5_glu_w'], 's5_glu_b': out['s5_glu_b'], 'ssd_conv_w': out['ssd_conv_w'], 'ssd_conv_b': out['ssd_conv_b'], 'ssd_dt_bias': out['ssd_dt_bias'], 'ssd_a_log': out['ssd_a_log'], 'ssd_d': out['ssd_d'], 'ssd_norm': out['ssd_norm'], 'o_in_proj': out['o_in_proj'], 'o_out_proj': out['o_out_proj'], 'rwkv_mu': out['rwkv_mu'], 'rwkv_w0': out['rwkv_w0'], 'rwkv_w_up': out['rwkv_w_up'], 'rwkv_a0': out['rwkv_a0'], 'rwkv_a_up': out['rwkv_a_up'], 'rwkv_g_up': out['rwkv_g_up'], 'rwkv_k_k': out['rwkv_k_k'], 'rwkv_k_a': out['rwkv_k_a'], 'rwkv_r_k': out['rwkv_r_k'], 'rwkv_ln_g': out['rwkv_ln_g'], 'rwkv_ln_b': out['rwkv_ln_b'], 'lru_conv_w': out['lru_conv_w'], 'lru_conv_b': out['lru_conv_b'], 'lru_w_a': out['lru_w_a'], 'lru_b_a': out['lru_b_a'], 'lru_w_x': out['lru_w_x'], 'lru_b_x': out['lru_b_x'], 'lru_lam': out['lru_lam'], 'norm_final': out['norm_final'], 'loss_target': out['loss_target'], 'm_norm_mix': out['m_norm_mix'], 'm_norm_ffn': out['m_norm_ffn'], 'm_norm_pl': out['m_norm_pl'], 'm_mlp_w1': out['m_mlp_w1'], 'm_mlp_w2': out['m_mlp_w2'], 'm_pl_proj': out['m_pl_proj'], 'm_pl_gate': out['m_pl_gate'], 'm_e_in_proj': out['m_e_in_proj'], 'm_e_out_proj': out['m_e_out_proj'], 'm_s5_lam_re': out['m_s5_lam_re'], 'm_s5_lam_im': out['m_s5_lam_im'], 'm_s5_log_step': out['m_s5_log_step'], 'm_s5_b_re': out['m_s5_b_re'], 'm_s5_b_im': out['m_s5_b_im'], 'm_s5_c_re': out['m_s5_c_re'], 'm_s5_c_im': out['m_s5_c_im'], 'm_s5_d': out['m_s5_d'], 'm_s5_glu_w': out['m_s5_glu_w'], 'm_s5_glu_b': out['m_s5_glu_b'], 'm_ssd_conv_w': out['m_ssd_conv_w'], 'm_ssd_conv_b': out['m_ssd_conv_b'], 'm_ssd_dt_bias': out['m_ssd_dt_bias'], 'm_ssd_a_log': out['m_ssd_a_log'], 'm_ssd_d': out['m_ssd_d'], 'm_ssd_norm': out['m_ssd_norm'], 'm_o_in_proj': out['m_o_in_proj'], 'm_o_out_proj': out['m_o_out_proj'], 'm_rwkv_mu': out['m_rwkv_mu'], 'm_rwkv_w0': out['m_rwkv_w0'], 'm_rwkv_w_up': out['m_rwkv_w_up'], 'm_rwkv_a0': out['m_rwkv_a0'], 'm_rwkv_a_up': out['m_rwkv_a_up'], 'm_rwkv_g_up': out['m_rwkv_g_up'], 'm_rwkv_k_k': out['m_rwkv_k_k'], 'm_rwkv_k_a': out['m_rwkv_k_a'], 'm_rwkv_r_k': out['m_rwkv_r_k'], 'm_rwkv_ln_g': out['m_rwkv_ln_g'], 'm_rwkv_ln_b': out['m_rwkv_ln_b'], 'm_lru_conv_w': out['m_lru_conv_w'], 'm_lru_conv_b': out['m_lru_conv_b'], 'm_lru_w_a': out['m_lru_w_a'], 'm_lru_b_a': out['m_lru_b_a'], 'm_lru_w_x': out['m_lru_w_x'], 'm_lru_b_x': out['m_lru_b_x'], 'm_lru_lam': out['m_lru_lam'], 'm_norm_final': out['m_norm_final'], 'v_norm_mix': out['v_norm_mix'], 'v_norm_ffn': out['v_norm_ffn'], 'v_norm_pl': out['v_norm_pl'], 'v_mlp_w1': out['v_mlp_w1'], 'v_mlp_w2': out['v_mlp_w2'], 'v_pl_proj': out['v_pl_proj'], 'v_pl_gate': out['v_pl_gate'], 'v_e_in_proj': out['v_e_in_proj'], 'v_e_out_proj': out['v_e_out_proj'], 'v_s5_lam_re': out['v_s5_lam_re'], 'v_s5_lam_im': out['v_s5_lam_im'], 'v_s5_log_step': out['v_s5_log_step'], 'v_s5_b_re': out['v_s5_b_re'], 'v_s5_b_im': out['v_s5_b_im'], 'v_s5_c_re': out['v_s5_c_re'], 'v_s5_c_im': out['v_s5_c_im'], 'v_s5_d': out['v_s5_d'], 'v_s5_glu_w': out['v_s5_glu_w'], 'v_s5_glu_b': out['v_s5_glu_b'], 'v_ssd_conv_w': out['v_ssd_conv_w'], 'v_ssd_conv_b': out['v_ssd_conv_b'], 'v_ssd_dt_bias': out['v_ssd_dt_bias'], 'v_ssd_a_log': out['v_ssd_a_log'], 'v_ssd_d': out['v_ssd_d'], 'v_ssd_norm': out['v_ssd_norm'], 'v_o_in_proj': out['v_o_in_proj'], 'v_o_out_proj': out['v_o_out_proj'], 'v_rwkv_mu': out['v_rwkv_mu'], 'v_rwkv_w0': out['v_rwkv_w0'], 'v_rwkv_w_up': out['v_rwkv_w_up'], 'v_rwkv_a0': out['v_rwkv_a0'], 'v_rwkv_a_up': out['v_rwkv_a_up'], 'v_rwkv_g_up': out['v_rwkv_g_up'], 'v_rwkv_k_k': out['v_rwkv_k_k'], 'v_rwkv_k_a': out['v_rwkv_k_a'], 'v_rwkv_r_k': out['v_rwkv_r_k'], 'v_rwkv_ln_g': out['v_rwkv_ln_g'], 'v_rwkv_ln_b': out['v_rwkv_ln_b'], 'v_lru_conv_w': out['v_lru_conv_w'], 'v_lru_conv_b': out['v_lru_conv_b'], 'v_lru_w_a': out['v_lru_w_a'], 'v_lru_b_a': out['v_lru_b_a'], 'v_lru_w_x': out['v_lru_w_x'], 'v_lru_b_x': out['v_lru_b_x'], 'v_lru_lam': out['v_lru_lam'], 'v_norm_final': out['v_norm_final']}


def _loss(weights, diff, rest, loss_target):
    with _jax.named_scope("forward"):
        args = {**rest, TWIN_DIFF_INPUT: diff, **{k: w.astype(_WEIGHT_DTYPES[k]) for k, w in weights.items()}}
        y = _forward(args)
    with _jax.named_scope("loss_head"):
        err = _jnp.square(y.astype(_jnp.float32) - loss_target)
        return 0.5 * _jnp.sum(_jnp.mean(err, axis=-1)) if err.ndim else 0.5 * err


def _adamw(w, g, m, v):
    m = ADAM_B1 * m + (1.0 - ADAM_B1) * g
    v = ADAM_B2 * v + (1.0 - ADAM_B2) * _jnp.square(g)
    m_hat = m / (1.0 - ADAM_B1 ** ADAM_STEP)
    v_hat = v / (1.0 - ADAM_B2 ** ADAM_STEP)
    delta = -ADAM_LR * (m_hat / (_jnp.sqrt(v_hat) + ADAM_EPS) + ADAM_WD * w)
    return delta, m, v


def reference(x, p, norm_mix, norm_ffn, norm_pl, mlp_w1, mlp_w2, pl_proj, pl_gate, e_in_proj, e_out_proj, s5_lam_re, s5_lam_im, s5_log_step, s5_b_re, s5_b_im, s5_c_re, s5_c_im, s5_d, s5_glu_w, s5_glu_b, ssd_conv_w, ssd_conv_b, ssd_dt_bias, ssd_a_log, ssd_d, ssd_norm, o_in_proj, o_out_proj, rwkv_mu, rwkv_w0, rwkv_w_up, rwkv_a0, rwkv_a_up, rwkv_g_up, rwkv_k_k, rwkv_k_a, rwkv_r_k, rwkv_ln_g, rwkv_ln_b, lru_conv_w, lru_conv_b, lru_w_a, lru_b_a, lru_w_x, lru_b_x, lru_lam, norm_final, loss_target, m_norm_mix, m_norm_ffn, m_norm_pl, m_mlp_w1, m_mlp_w2, m_pl_proj, m_pl_gate, m_e_in_proj, m_e_out_proj, m_s5_lam_re, m_s5_lam_im, m_s5_log_step, m_s5_b_re, m_s5_b_im, m_s5_c_re, m_s5_c_im, m_s5_d, m_s5_glu_w, m_s5_glu_b, m_ssd_conv_w, m_ssd_conv_b, m_ssd_dt_bias, m_ssd_a_log, m_ssd_d, m_ssd_norm, m_o_in_proj, m_o_out_proj, m_rwkv_mu, m_rwkv_w0, m_rwkv_w_up, m_rwkv_a0, m_rwkv_a_up, m_rwkv_g_up, m_rwkv_k_k, m_rwkv_k_a, m_rwkv_r_k, m_rwkv_ln_g, m_rwkv_ln_b, m_lru_conv_w, m_lru_conv_b, m_lru_w_a, m_lru_b_a, m_lru_w_x, m_lru_b_x, m_lru_lam, m_norm_final, v_norm_mix, v_norm_ffn, v_norm_pl, v_mlp_w1, v_mlp_w2, v_pl_proj, v_pl_gate, v_e_in_proj, v_e_out_proj, v_s5_lam_re, v_s5_lam_im, v_s5_log_step, v_s5_b_re, v_s5_b_im, v_s5_c_re, v_s5_c_im, v_s5_d, v_s5_glu_w, v_s5_glu_b, v_ssd_conv_w, v_ssd_conv_b, v_ssd_dt_bias, v_ssd_a_log, v_ssd_d, v_ssd_norm, v_o_in_proj, v_o_out_proj, v_rwkv_mu, v_rwkv_w0, v_rwkv_w_up, v_rwkv_a0, v_rwkv_a_up, v_rwkv_g_up, v_rwkv_k_k, v_rwkv_k_a, v_rwkv_r_k, v_rwkv_ln_g, v_rwkv_ln_b, v_lru_conv_w, v_lru_conv_b, v_lru_w_a, v_lru_b_a, v_lru_w_x, v_lru_b_x, v_lru_lam, v_norm_final):
    given = dict(x=x, p=p, norm_mix=norm_mix, norm_ffn=norm_ffn, norm_pl=norm_pl, mlp_w1=mlp_w1, mlp_w2=mlp_w2, pl_proj=pl_proj, pl_gate=pl_gate, e_in_proj=e_in_proj, e_out_proj=e_out_proj, s5_lam_re=s5_lam_re, s5_lam_im=s5_lam_im, s5_log_step=s5_log_step, s5_b_re=s5_b_re, s5_b_im=s5_b_im, s5_c_re=s5_c_re, s5_c_im=s5_c_im, s5_d=s5_d, s5_glu_w=s5_glu_w, s5_glu_b=s5_glu_b, ssd_conv_w=ssd_conv_w, ssd_conv_b=ssd_conv_b, ssd_dt_bias=ssd_dt_bias, ssd_a_log=ssd_a_log, ssd_d=ssd_d, ssd_norm=ssd_norm, o_in_proj=o_in_proj, o_out_proj=o_out_proj, rwkv_mu=rwkv_mu, rwkv_w0=rwkv_w0, rwkv_w_up=rwkv_w_up, rwkv_a0=rwkv_a0, rwkv_a_up=rwkv_a_up, rwkv_g_up=rwkv_g_up, rwkv_k_k=rwkv_k_k, rwkv_k_a=rwkv_k_a, rwkv_r_k=rwkv_r_k, rwkv_ln_g=rwkv_ln_g, rwkv_ln_b=rwkv_ln_b, lru_conv_w=lru_conv_w, lru_conv_b=lru_conv_b, lru_w_a=lru_w_a, lru_b_a=lru_b_a, lru_w_x=lru_w_x, lru_b_x=lru_b_x, lru_lam=lru_lam, norm_final=norm_final, loss_target=loss_target, m_norm_mix=m_norm_mix, m_norm_ffn=m_norm_ffn, m_norm_pl=m_norm_pl, m_mlp_w1=m_mlp_w1, m_mlp_w2=m_mlp_w2, m_pl_proj=m_pl_proj, m_pl_gate=m_pl_gate, m_e_in_proj=m_e_in_proj, m_e_out_proj=m_e_out_proj, m_s5_lam_re=m_s5_lam_re, m_s5_lam_im=m_s5_lam_im, m_s5_log_step=m_s5_log_step, m_s5_b_re=m_s5_b_re, m_s5_b_im=m_s5_b_im, m_s5_c_re=m_s5_c_re, m_s5_c_im=m_s5_c_im, m_s5_d=m_s5_d, m_s5_glu_w=m_s5_glu_w, m_s5_glu_b=m_s5_glu_b, m_ssd_conv_w=m_ssd_conv_w, m_ssd_conv_b=m_ssd_conv_b, m_ssd_dt_bias=m_ssd_dt_bias, m_ssd_a_log=m_ssd_a_log, m_ssd_d=m_ssd_d, m_ssd_norm=m_ssd_norm, m_o_in_proj=m_o_in_proj, m_o_out_proj=m_o_out_proj, m_rwkv_mu=m_rwkv_mu, m_rwkv_w0=m_rwkv_w0, m_rwkv_w_up=m_rwkv_w_up, m_rwkv_a0=m_rwkv_a0, m_rwkv_a_up=m_rwkv_a_up, m_rwkv_g_up=m_rwkv_g_up, m_rwkv_k_k=m_rwkv_k_k, m_rwkv_k_a=m_rwkv_k_a, m_rwkv_r_k=m_rwkv_r_k, m_rwkv_ln_g=m_rwkv_ln_g, m_rwkv_ln_b=m_rwkv_ln_b, m_lru_conv_w=m_lru_conv_w, m_lru_conv_b=m_lru_conv_b, m_lru_w_a=m_lru_w_a, m_lru_b_a=m_lru_b_a, m_lru_w_x=m_lru_w_x, m_lru_b_x=m_lru_b_x, m_lru_lam=m_lru_lam, m_norm_final=m_norm_final, v_norm_mix=v_norm_mix, v_norm_ffn=v_norm_ffn, v_norm_pl=v_norm_pl, v_mlp_w1=v_mlp_w1, v_mlp_w2=v_mlp_w2, v_pl_proj=v_pl_proj, v_pl_gate=v_pl_gate, v_e_in_proj=v_e_in_proj, v_e_out_proj=v_e_out_proj, v_s5_lam_re=v_s5_lam_re, v_s5_lam_im=v_s5_lam_im, v_s5_log_step=v_s5_log_step, v_s5_b_re=v_s5_b_re, v_s5_b_im=v_s5_b_im, v_s5_c_re=v_s5_c_re, v_s5_c_im=v_s5_c_im, v_s5_d=v_s5_d, v_s5_glu_w=v_s5_glu_w, v_s5_glu_b=v_s5_glu_b, v_ssd_conv_w=v_ssd_conv_w, v_ssd_conv_b=v_ssd_conv_b, v_ssd_dt_bias=v_ssd_dt_bias, v_ssd_a_log=v_ssd_a_log, v_ssd_d=v_ssd_d, v_ssd_norm=v_ssd_norm, v_o_in_proj=v_o_in_proj, v_o_out_proj=v_o_out_proj, v_rwkv_mu=v_rwkv_mu, v_rwkv_w0=v_rwkv_w0, v_rwkv_w_up=v_rwkv_w_up, v_rwkv_a0=v_rwkv_a0, v_rwkv_a_up=v_rwkv_a_up, v_rwkv_g_up=v_rwkv_g_up, v_rwkv_k_k=v_rwkv_k_k, v_rwkv_k_a=v_rwkv_k_a, v_rwkv_r_k=v_rwkv_r_k, v_rwkv_ln_g=v_rwkv_ln_g, v_rwkv_ln_b=v_rwkv_ln_b, v_lru_conv_w=v_lru_conv_w, v_lru_conv_b=v_lru_conv_b, v_lru_w_a=v_lru_w_a, v_lru_b_a=v_lru_b_a, v_lru_w_x=v_lru_w_x, v_lru_b_x=v_lru_b_x, v_lru_lam=v_lru_lam, v_norm_final=v_norm_final)
    weights = {n: given[n] for n in TWIN_WEIGHTS}
    shared = {n: given[n] for n in SHARED_INPUTS}
    per_example = {n: given[n] for n in ['x', 'p']}
    grad_fn = _jax.value_and_grad(_loss, argnums=(0, 1))

    def one_microbatch(ex, loss_target):
        ex = dict(ex)
        diff = ex.pop(TWIN_DIFF_INPUT)
        return grad_fn(weights, diff, {**shared, **ex}, loss_target)

    if N_MICROBATCH == 1:
        loss, (grad_w, grad_x) = one_microbatch(per_example, given["loss_target"])
    else:
        def body(carry, xs):
            loss_sum, grad_sum = carry
            l_k, (gw_k, gx_k) = one_microbatch(xs[0], xs[1])
            with _jax.named_scope("update"):
                return (loss_sum + l_k, _jax.tree.map(_jnp.add, grad_sum, gw_k)), gx_k

        init = (_jnp.zeros((), _jnp.float32), _jax.tree.map(_jnp.zeros_like, weights))
        (loss, grad_w), grad_x = _jax.lax.scan(body, init, (per_example, given["loss_target"]))
    with _jax.named_scope("update"):
        delta_w, new_m, new_v = {}, {}, {}
        for n in TWIN_WEIGHTS:
            delta_w[n], new_m[n], new_v[n] = _adamw(weights[n], grad_w[n], given["m_" + n], given["v_" + n])
    return (loss, grad_x, *[grad_w[n] for n in TWIN_WEIGHTS], *[delta_w[n] for n in TWIN_WEIGHTS],
            *[new_m[n] for n in TWIN_WEIGHTS], *[new_v[n] for n in TWIN_WEIGHTS])
```

```python
import functools
import math

import jax
import jax.numpy as jnp
from jax import lax
from jax.experimental import pallas as pl
from jax.experimental.pallas import tpu as pltpu

F32 = jnp.float32
BF16 = jnp.bfloat16
N_DEV = 8
LANES = 128
VMEM_LIMIT = 56 * 1024 * 1024
NORM_EPS = 1e-6
RWKV_GN_EPS = 64e-5
LRU_C = 8.0
ADAM_LR, ADAM_B1, ADAM_B2, ADAM_EPS, ADAM_WD, ADAM_STEP = 0.001, 0.9, 0.999, 1e-08, 0.01, 10
SSD_CHUNK = 128
RWKV_CHUNK = 32
HEAD = 64


def _cparams(*sem):
    return pltpu.CompilerParams(dimension_semantics=sem, vmem_limit_bytes=VMEM_LIMIT)


def _pick(n, prefs):
    for t in prefs:
        if n % t == 0:
            return t
    return n


def _matmul(a, b, mode, *, name, add=None, out_dtype=F32):
    if mode == "nn":
        (m, k), (k2, n) = a.shape, b.shape
    elif mode == "nt":
        (m, k), (n, k2) = a.shape, b.shape
    else:
        (k, m), (k2, n) = a.shape, b.shape
    assert k == k2, (a.shape, b.shape, mode)
    tm, tn, tk = _pick(m, (1024, 512, 256, 128)), _pick(n, (1024, 512, 256, 128)), _pick(k, (512, 256, 128))
    nk = k // tk
    dn = {"nn": (((1,), (0,)), ((), ())), "nt": (((1,), (1,)), ((), ())), "tn": (((0,), (0,)), ((), ()))}[mode]
    a_spec = pl.BlockSpec((tk, tm), lambda i, j, kk: (kk, i)) if mode == "tn" else pl.BlockSpec((tm, tk), lambda i, j, kk: (i, kk))
    b_spec = pl.BlockSpec((tn, tk), lambda i, j, kk: (j, kk)) if mode == "nt" else pl.BlockSpec((tk, tn), lambda i, j, kk: (kk, j))
    o_spec = pl.BlockSpec((tm, tn), lambda i, j, kk: (i, j))

    def body(a_ref, b_ref, *rest):
        if add is None:
            o_ref, acc = rest
        else:
            add_ref, o_ref, acc = rest
        kk = pl.program_id(2)

        @pl.when(kk == 0)
        def _():
            acc[...] = jnp.zeros_like(acc) if add is None else add_ref[...].astype(F32)

        acc[...] += lax.dot_general(a_ref[...].astype(BF16), b_ref[...].astype(BF16), dn, preferred_element_type=F32)

        @pl.when(kk == nk - 1)
        def _():
            o_ref[...] = acc[...].astype(o_ref.dtype)

    ins, specs = [a, b], [a_spec, b_spec]
    if add is not None:
        ins.append(add)
        specs.append(o_spec)
    return pl.pallas_call(
        body, grid=(m // tm, n // tn, nk), in_specs=specs, out_specs=o_spec,
        out_shape=jax.ShapeDtypeStruct((m, n), out_dtype), scratch_shapes=[pltpu.VMEM((tm, tn), F32)],
        compiler_params=_cparams("parallel", "parallel", "arbitrary"), name=name)(*ins)


def _row(x, width=None, block=0):
    return (x, x.shape[1] if width is None else width, block)


def _rw_specs(rows, params, ts):
    specs = [pl.BlockSpec((ts, w), functools.partial(lambda i, b: (i, b), b=bi)) for (_, w, bi) in rows]
    specs += [pl.BlockSpec(p.shape, functools.partial(lambda i, nd: (0,) * nd, nd=p.ndim)) for p in params]
    return specs


def _rw_fwd(f, rows, params, outs, *, name, ts=256):
    s = rows[0][0].shape[0]
    ts = min(ts, s)
    nr, npar = len(rows), len(params)

    def body(*refs):
        row0 = pl.program_id(0) * ts
        res = f(row0, *[r[...] for r in refs[:nr + npar]])
        for o_ref, val in zip(refs[nr + npar:], res, strict=True):
            o_ref[...] = val.astype(o_ref.dtype)

    out = pl.pallas_call(
        body, grid=(s // ts,), in_specs=_rw_specs(rows, params, ts),
        out_specs=[pl.BlockSpec((ts, w), lambda i: (i, 0)) for (w, _) in outs],
        out_shape=[jax.ShapeDtypeStruct((s, w), dt) for (w, dt) in outs],
        compiler_params=_cparams("parallel"), name=name)(*[r[0] for r in rows], *params)
    return tuple(out)


def _rw_bwd(f, rows, params, cts, *, name, ts=256, row_grads=None, param_grads=None, add_rows=None):
    s = rows[0][0].shape[0]
    ts = min(ts, s)
    ct_groups = [list(c) if isinstance(c, (list, tuple)) else [c] for c in cts]
    cts = [c for grp in ct_groups for c in grp]
    nr, npar, nct = len(rows), len(params), len(cts)
    row_grads = list(range(nr)) if row_grads is None else list(row_grads)
    param_grads = list(range(npar)) if param_grads is None else list(param_grads)
    add_rows = add_rows or {}
    add_keys = sorted(add_rows)

    def body(*refs):
        i = pl.program_id(0)
        row0 = i * ts
        vals = [r[...] for r in refs[:nr + npar]]
        for pi in param_grads:
            vals[nr + pi] = vals[nr + pi].astype(F32)
        ct_refs = list(refs[nr + npar:nr + npar + nct])
        add_refs = dict(zip(add_keys, refs[nr + npar + nct:nr + npar + nct + len(add_keys)]))
        o_refs = refs[nr + npar + nct + len(add_keys):]
        res, vjp = jax.vjp(functools.partial(f, row0), *vals)
        ct_vals = []
        for grp, r in zip(ct_groups, res, strict=True):
            ct_vals.append(sum(ct_refs.pop(0)[...].astype(r.dtype) for _ in grp))
        grads = vjp(tuple(ct_vals))
        for o_ref, ri in zip(o_refs[:len(row_grads)], row_grads):
            g = grads[ri]
            if ri in add_refs:
                g = g + add_refs[ri][...]
            o_ref[...] = g.astype(o_ref.dtype)

        @pl.when(i == 0)
        def _():
            for o_ref in o_refs[len(row_grads):]:
                o_ref[...] = jnp.zeros_like(o_ref)

        for o_ref, pi in zip(o_refs[len(row_grads):], param_grads):
            o_ref[...] += grads[nr + pi].astype(F32)

    in_specs = _rw_specs(rows, params, ts)
    in_specs += [pl.BlockSpec((ts, c.shape[1]), lambda i: (i, 0)) for c in cts]
    in_specs += [pl.BlockSpec((ts, add_rows[k].shape[1]), lambda i: (i, 0)) for k in add_keys]
    out_specs = [pl.BlockSpec((ts, rows[ri][1]), lambda i: (i, 0)) for ri in row_grads]
    out_specs += [pl.BlockSpec(params[pi].shape, functools.partial(lambda i, nd: (0,) * nd, nd=params[pi].ndim)) for pi in param_grads]
    out_shape = [jax.ShapeDtypeStruct((s, rows[ri][1]), F32) for ri in row_grads]
    out_shape += [jax.ShapeDtypeStruct(params[pi].shape, F32) for pi in param_grads]
    out = pl.pallas_call(
        body, grid=(s // ts,), in_specs=in_specs, out_specs=out_specs, out_shape=out_shape,
        compiler_params=_cparams("arbitrary"), name=name)(*[r[0] for r in rows], *params, *cts, *[add_rows[k] for k in add_keys])
    return tuple(out[:len(row_grads)]), tuple(out[len(row_grads):])


@functools.partial(jax.custom_vjp, nondiff_argnums=(1,))
def _shift_down(x, k):
    rows = lax.broadcasted_iota(jnp.int32, x.shape, 0)
    return jnp.where(rows < k, 0.0, pltpu.roll(x, k, 0))


def _shift_down_fwd(x, k):
    return _shift_down(x, k), None


def _shift_down_bwd(k, _, g):
    n = g.shape[0]
    rows = lax.broadcasted_iota(jnp.int32, g.shape, 0)
    return (jnp.where(rows >= n - k, 0.0, pltpu.roll(g, n - k, 0)),)


_shift_down.defvjp(_shift_down_fwd, _shift_down_bwd)


def _ct_fwd(f, x, xmap, ntiles, params, out_width, *, name, ct=LANES):
    s = x.shape[0]

    def body(*refs):
        refs[-1][...] = f(*[r[...] for r in refs[:-1]])

    in_specs = [pl.BlockSpec((s, ct), lambda j: (0, xmap(j)))]
    in_specs += [pl.BlockSpec((p.shape[0], ct), functools.partial(lambda j, o: (0, o + j), o=o)) for (p, o) in params]
    return pl.pallas_call(
        body, grid=(ntiles,), in_specs=in_specs, out_specs=pl.BlockSpec((s, ct), lambda j: (0, j)),
        out_shape=jax.ShapeDtypeStruct((s, out_width), F32), compiler_params=_cparams("parallel"), name=name)(x, *[p for p, _ in params])


def _ct_bwd(f, x, xmap, ntiles, params, g, *, name, ct=LANES):
    s = x.shape[0]
    npar, ng = len(params), len(g)

    def body(*refs):
        vals = [r[...] for r in refs[:1 + npar]]
        _, vjp = jax.vjp(f, *vals)
        grads = vjp(sum(r[...] for r in refs[1 + npar:1 + npar + ng]))
        for o_ref, gr in zip(refs[1 + npar + ng:], grads, strict=True):
            o_ref[...] = gr

    in_specs = [pl.BlockSpec((s, ct), lambda j: (0, xmap(j)))]
    pspecs = [pl.BlockSpec((p.shape[0], ct), functools.partial(lambda j, o: (0, o + j), o=o)) for (p, o) in params]
    in_specs += pspecs + [pl.BlockSpec((s, ct), lambda j: (0, j))] * ng
    out = pl.pallas_call(
        body, grid=(ntiles,), in_specs=in_specs, out_specs=[pl.BlockSpec((s, ct), lambda j: (0, j))] + pspecs,
        out_shape=[jax.ShapeDtypeStruct((s, ntiles * ct), F32)] + [jax.ShapeDtypeStruct(p.shape, F32) for p, _ in params],
        compiler_params=_cparams("parallel"), name=name)(x, *[p for p, _ in params], *g)
    return out[0], tuple(out[1:])


def _block_vjp(f, args, cts, *, name):
    outs = jax.eval_shape(f, *args)
    if cts is None:
        def body(*refs):
            for o_ref, v in zip(refs[len(args):], f(*[r[...] for r in refs[:len(args)]]), strict=True):
                o_ref[...] = v
        return tuple(pl.pallas_call(body, out_shape=[jax.ShapeDtypeStruct(o.shape, o.dtype) for o in outs], name=name)(*args))

    def body(*refs):
        n = len(args)
        _, vjp = jax.vjp(f, *[r[...] for r in refs[:n]])
        for o_ref, gr in zip(refs[n + len(cts):], vjp(tuple(r[...] for r in refs[n:n + len(cts)])), strict=True):
            o_ref[...] = gr
    return tuple(pl.pallas_call(body, out_shape=[jax.ShapeDtypeStruct(a.shape, a.dtype) for a in args], name=name)(*args, *cts))


def _softplus(x):
    return jnp.maximum(x, 0.0) + jnp.log(1.0 + jnp.exp(-jnp.abs(x)))


def _dot_bf16(a, b):
    return jnp.dot(a.astype(BF16), b.astype(BF16), preferred_element_type=F32)


def _dot3(x, m):
    hi = x.astype(BF16)
    r1 = x - hi.astype(F32)
    mid = r1.astype(BF16)
    lo = (r1 - mid.astype(F32)).astype(BF16)
    return (jnp.dot(hi, m, preferred_element_type=F32) + jnp.dot(mid, m, preferred_element_type=F32)
            + jnp.dot(lo, m, preferred_element_type=F32))


@jax.custom_vjp
def _lin(x, m, mt):
    return _dot3(x, m)


def _lin_fwd(x, m, mt):
    return _dot3(x, m), (m, mt)


def _lin_bwd(res, g):
    m, mt = res
    return _dot3(g, mt), jnp.zeros_like(m), jnp.zeros_like(mt)


_lin.defvjp(_lin_fwd, _lin_bwd)


def _head_ones(n, head=HEAD):
    i = jnp.arange(n) // head
    return (i[:, None] == i[None, :]).astype(BF16)


def _rmsnorm_tile(_, h, g):
    return (h * lax.rsqrt(jnp.mean(h * h, axis=-1, keepdims=True) + NORM_EPS) * g,)


def _cscan_fwd(bu, a, *, name, ct=256):
    s, c2 = bu.shape
    c = c2 // 2
    nt = c // ct

    def body(br_ref, bi_ref, ar_ref, ai_ref, xr_ref, xi_ref):
        ar, ai = ar_ref[...], ai_ref[...]

        def step(t, carry):
            hr, hi = carry
            nr = ar * hr - ai * hi + br_ref[pl.ds(t, 1), :]
            ni = ar * hi + ai * hr + bi_ref[pl.ds(t, 1), :]
            xr_ref[pl.ds(t, 1), :] = nr
            xi_ref[pl.ds(t, 1), :] = ni
            return nr, ni

        z = jnp.zeros((1, ct), F32)
        lax.fori_loop(0, s, step, (z, z))

    re = lambda j: (0, j)
    im = lambda j: (0, nt + j)
    xr, xi = pl.pallas_call(
        body, grid=(nt,),
        in_specs=[pl.BlockSpec((s, ct), re), pl.BlockSpec((s, ct), im), pl.BlockSpec((1, ct), re), pl.BlockSpec((1, ct), im)],
        out_specs=[pl.BlockSpec((s, ct), re), pl.BlockSpec((s, ct), re)],
        out_shape=[jax.ShapeDtypeStruct((s, c), F32)] * 2, compiler_params=_cparams("parallel"), name=name)(bu, bu, a, a)
    return xr, xi


def _cscan_bwd(gr, gi, xr, xi, a, *, name, ct=256):
    s, c = gr.shape
    nt = c // ct

    def body(gr_ref, gi_ref, xr_ref, xi_ref, ar_ref, ai_ref, dr_ref, di_ref, dar_ref, dai_ref):
        ar, ai = ar_ref[...], ai_ref[...]

        def step(i, carry):
            dr, di, accr, acci = carry
            t = s - 1 - i
            nr = gr_ref[pl.ds(t, 1), :] + ar * dr + ai * di
            ni = gi_ref[pl.ds(t, 1), :] + ar * di - ai * dr
            dr_ref[pl.ds(t, 1), :] = nr
            di_ref[pl.ds(t, 1), :] = ni
            tp = jnp.maximum(t - 1, 0)
            live = (t > 0).astype(F32)
            pr = xr_ref[pl.ds(tp, 1), :] * live
            pi = xi_ref[pl.ds(tp, 1), :] * live
            return nr, ni, accr + nr * pr + ni * pi, acci + ni * pr - nr * pi

        z = jnp.zeros((1, ct), F32)
        _, _, accr, acci = lax.fori_loop(0, s, step, (z, z, z, z))
        dar_ref[...] = accr
        dai_ref[...] = acci

    re = lambda j: (0, j)
    im = lambda j: (0, nt + j)
    blk = pl.BlockSpec((s, ct), re)
    dr, di, dar, dai = pl.pallas_call(
        body, grid=(nt,),
        in_specs=[blk, blk, blk, blk, pl.BlockSpec((1, ct), re), pl.BlockSpec((1, ct), im)],
        out_specs=[blk, blk, pl.BlockSpec((1, ct), re), pl.BlockSpec((1, ct), re)],
        out_shape=[jax.ShapeDtypeStruct((s, c), F32)] * 2 + [jax.ShapeDtypeStruct((1, c), F32)] * 2,
        compiler_params=_cparams("parallel"), name=name)(gr, gi, xr, xi, a, a)
    return dr, di, dar, dai


def _rscan_fwd(a, b, *, name, ct=256):
    s, c = a.shape

    def body(a_ref, b_ref, h_ref):
        def step(t, h):
            h = a_ref[pl.ds(t, 1), :] * h + b_ref[pl.ds(t, 1), :]
            h_ref[pl.ds(t, 1), :] = h
            return h
        lax.fori_loop(0, s, step, jnp.zeros((1, ct), F32))

    blk = pl.BlockSpec((s, ct), lambda j: (0, j))
    return pl.pallas_call(body, grid=(c // ct,), in_specs=[blk, blk], out_specs=blk,
                          out_shape=jax.ShapeDtypeStruct((s, c), F32), compiler_params=_cparams("parallel"), name=name)(a, b)


def _rscan_bwd(g, a, h, *, name, ct=256):
    s, c = a.shape

    def body(g_ref, a_ref, h_ref, db_ref, da_ref):
        def step(i, carry):
            d, an = carry
            t = s - 1 - i
            d = g_ref[pl.ds(t, 1), :] + an * d
            db_ref[pl.ds(t, 1), :] = d
            tp = jnp.maximum(t - 1, 0)
            da_ref[pl.ds(t, 1), :] = d * h_ref[pl.ds(tp, 1), :] * (t > 0).astype(F32)
            return d, a_ref[pl.ds(t, 1), :]
        z = jnp.zeros((1, ct), F32)
        lax.fori_loop(0, s, step, (z, z))

    blk = pl.BlockSpec((s, ct), lambda j: (0, j))
    db, da = pl.pallas_call(body, grid=(c // ct,), in_specs=[blk, blk, blk], out_specs=[blk, blk],
                            out_shape=[jax.ShapeDtypeStruct((s, c), F32)] * 2, compiler_params=_cparams("parallel"), name=name)(g, a, h)
    return db, da


def _dot3l(m, x):
    hi = x.astype(BF16)
    r1 = x - hi.astype(F32)
    mid = r1.astype(BF16)
    lo = (r1 - mid.astype(F32)).astype(BF16)
    return (jnp.dot(m, hi, preferred_element_type=F32) + jnp.dot(m, mid, preferred_element_type=F32)
            + jnp.dot(m, lo, preferred_element_type=F32))


@jax.custom_vjp
def _linl(x, m, mt):
    return _dot3l(m, x)


def _linl_fwd(x, m, mt):
    return _dot3l(m, x), (m, mt)


def _linl_bwd(res, g):
    m, mt = res
    return _dot3l(mt, g), jnp.zeros_like(m), jnp.zeros_like(mt)


_linl.defvjp(_linl_fwd, _linl_bwd)


def _ssd_chunk(g, xs, bm, cm, z, dtraw, dt_bias, a_log, dskip, ng, st0, st1, st2):
    n = xs.shape[0]
    lane = lax.broadcasted_iota(jnp.int32, (1, LANES), 1)
    sub = lax.broadcasted_iota(jnp.int32, (LANES, 1), 0)
    row = lax.broadcasted_iota(jnp.int32, (n, n), 0)
    col = lax.broadcasted_iota(jnp.int32, (n, n), 1)
    tril = row >= col
    tril_m = tril.astype(BF16)
    triu_m = (row <= col).astype(BF16)
    lane_lo = lane < HEAD
    sub_lo = sub < HEAD
    dt = _softplus(dtraw + dt_bias)
    da = dt * (-jnp.exp(a_log))
    acum = _linl(da, tril_m, triu_m)
    acum_t = acum.T
    scores = lax.dot_general(cm.astype(BF16), bm.astype(BF16), (((1,), (1,)), ((), ())), preferred_element_type=F32)

    def head(h):
        sel = lane == h
        acol = jnp.sum(jnp.where(sel, acum, 0.0), axis=1, keepdims=True)
        arow = jnp.sum(jnp.where(sub == h, acum_t, 0.0), axis=0, keepdims=True)
        dtcol = jnp.sum(jnp.where(sel, dt, 0.0), axis=1, keepdims=True)
        dsk = jnp.sum(jnp.where(sel, dskip, 0.0), axis=1, keepdims=True)
        decay = jnp.exp(jnp.where(tril, acol - arow, -jnp.inf))
        alast = acol[n - 1:n, :]
        return acol, dtcol, dsk, decay, alast

    ys, new = [], []
    for q, st in enumerate((st0, st1, st2)):
        a_acol, a_dt, a_dsk, a_decay, a_last = head(g * 6 + 2 * q)
        b_acol, b_dt, b_dsk, b_decay, b_last = head(g * 6 + 2 * q + 1)
        xp = xs[:, q * LANES:(q + 1) * LANES]
        xdt = xp * jnp.where(lane_lo, a_dt, b_dt)
        yd = jnp.where(lane_lo, _dot_bf16(scores * a_decay, xdt), _dot_bf16(scores * b_decay, xdt))
        xw = xdt * jnp.where(lane_lo, jnp.exp(a_last - a_acol), jnp.exp(b_last - b_acol))
        states = lax.dot_general(xw.astype(BF16), bm.astype(BF16), (((0,), (0,)), ((), ())), preferred_element_type=F32)
        yo = lax.dot_general(cm.astype(BF16), st.astype(BF16), (((1,), (1,)), ((), ())), preferred_element_type=F32)
        yo = yo * jnp.where(lane_lo, jnp.exp(a_acol), jnp.exp(b_acol))
        new.append(st * jnp.where(sub_lo, jnp.exp(a_last), jnp.exp(b_last)) + states)
        ys.append(yd + yo + xp * jnp.where(lane_lo, a_dsk, b_dsk))
    y = jnp.concatenate(ys, axis=1)
    y = y * (z * jax.nn.sigmoid(z))
    y = y * lax.rsqrt(jnp.mean(y * y, axis=-1, keepdims=True) + NORM_EPS) * ng
    return y, new[0], new[1], new[2]


def _ssd_specs(nc, rev):
    cidx = (lambda c: nc - 1 - c) if rev else (lambda c: c)
    gw = 3 * LANES
    return [
        pl.BlockSpec((SSD_CHUNK, gw), lambda c, g: (cidx(c), g)),
        pl.BlockSpec((SSD_CHUNK, LANES), lambda c, g: (cidx(c), 12 + g)),
        pl.BlockSpec((SSD_CHUNK, LANES), lambda c, g: (cidx(c), 16 + g)),
        pl.BlockSpec((SSD_CHUNK, gw), lambda c, g: (cidx(c), g)),
        pl.BlockSpec((SSD_CHUNK, LANES), lambda c, g: (cidx(c), 36)),
        pl.BlockSpec((1, LANES), lambda c, g: (0, 0)),
        pl.BlockSpec((1, LANES), lambda c, g: (0, 0)),
        pl.BlockSpec((1, LANES), lambda c, g: (0, 0)),
        pl.BlockSpec((1, gw), lambda c, g: (0, g)),
    ], cidx


def _ssd_fwd(conv, proj, dt_bias, a_log, dskip, norm_g, *, name):
    s = conv.shape[0]
    nc = s // SSD_CHUNK
    in_specs, _ = _ssd_specs(nc, False)

    def body(xs, bm, cm, z, dtr, dtb, alog, dsk, ng, y_ref, sv_ref, st):
        c, g = pl.program_id(0), pl.program_id(1)

        @pl.when(c == 0)
        def _():
            for q in range(3):
                st[g * 3 + q] = jnp.zeros((LANES, LANES), F32)

        olds = [st[g * 3 + q] for q in range(3)]
        for q in range(3):
            sv_ref[0, 0, q] = olds[q]
        y, n0, n1, n2 = _ssd_chunk(g, xs[...], bm[...], cm[...], z[...], dtr[...], dtb[...], alog[...], dsk[...], ng[...], *olds)
        y_ref[...] = y
        for q, v in enumerate((n0, n1, n2)):
            st[g * 3 + q] = v

    return pl.pallas_call(
        body, grid=(nc, 4), in_specs=in_specs,
        out_specs=[pl.BlockSpec((SSD_CHUNK, 3 * LANES), lambda c, g: (c, g)),
                   pl.BlockSpec((1, 1, 3, LANES, LANES), lambda c, g: (c, g, 0, 0, 0))],
        out_shape=[jax.ShapeDtypeStruct((s, 12 * LANES), F32), jax.ShapeDtypeStruct((nc, 4, 3, LANES, LANES), F32)],
        scratch_shapes=[pltpu.VMEM((12, LANES, LANES), F32)],
        compiler_params=_cparams("arbitrary", "arbitrary"), name=name)(conv, conv, conv, proj, proj, dt_bias, a_log, dskip, norm_g)


def _ssd_bwd(conv, proj, dt_bias, a_log, dskip, norm_g, saved, dy, *, name):
    s = conv.shape[0]
    nc = s // SSD_CHUNK
    in_specs, cidx = _ssd_specs(nc, True)
    gw = 3 * LANES
    in_specs += [pl.BlockSpec((1, 1, 3, LANES, LANES), lambda c, g: (cidx(c), g, 0, 0, 0)),
                 pl.BlockSpec((SSD_CHUNK, gw), lambda c, g: (cidx(c), g))]

    def body(xs, bm, cm, z, dtr, dtb, alog, dsk, ng, sv, dy_ref, dxs, dbm, dcm, dz, ddt, ddtb, dalog, ddsk, dng, dst):
        c, g = pl.program_id(0), pl.program_id(1)

        @pl.when(c == 0)
        def _():
            for q in range(3):
                dst[g * 3 + q] = jnp.zeros((LANES, LANES), F32)

        @pl.when((c == 0) & (g == 0))
        def _():
            ddtb[...] = jnp.zeros_like(ddtb)
            dalog[...] = jnp.zeros_like(dalog)
            ddsk[...] = jnp.zeros_like(ddsk)
            dng[...] = jnp.zeros_like(dng)

        @pl.when(g == 0)
        def _():
            ddt[...] = jnp.zeros_like(ddt)

        olds = [sv[0, 0, q] for q in range(3)]
        _, vjp = jax.vjp(functools.partial(_ssd_chunk, g), xs[...], bm[...], cm[...], z[...], dtr[...], dtb[...], alog[...],
                         dsk[...], ng[...], *olds)
        gr = vjp((dy_ref[...], dst[g * 3], dst[g * 3 + 1], dst[g * 3 + 2]))
        dxs[...], dbm[...], dcm[...], dz[...] = gr[0], gr[1], gr[2], gr[3]
        ddt[...] += gr[4]
        ddtb[...] += gr[5]
        dalog[...] += gr[6]
        ddsk[...] += gr[7]
        dng[g] += gr[8]
        for q in range(3):
            dst[g * 3 + q] = gr[9 + q]

    const = lambda shape: pl.BlockSpec(shape, lambda c, g: (0,) * len(shape))
    return pl.pallas_call(
        body, grid=(nc, 4), in_specs=in_specs,
        out_specs=[pl.BlockSpec((SSD_CHUNK, gw), lambda c, g: (cidx(c), g)),
                   pl.BlockSpec((SSD_CHUNK, LANES), lambda c, g: (cidx(c), g)),
                   pl.BlockSpec((SSD_CHUNK, LANES), lambda c, g: (cidx(c), g)),
                   pl.BlockSpec((SSD_CHUNK, gw), lambda c, g: (cidx(c), g)),
                   pl.BlockSpec((SSD_CHUNK, LANES), lambda c, g: (cidx(c), 0)),
                   const((1, LANES)), const((1, LANES)), const((1, LANES)), const((4, 1, gw))],
        out_shape=[jax.ShapeDtypeStruct((s, 12 * LANES), F32), jax.ShapeDtypeStruct((s, 4 * LANES), F32),
                   jax.ShapeDtypeStruct((s, 4 * LANES), F32), jax.ShapeDtypeStruct((s, 12 * LANES), F32),
                   jax.ShapeDtypeStruct((s, LANES), F32), jax.ShapeDtypeStruct((1, LANES), F32),
                   jax.ShapeDtypeStruct((1, LANES), F32), jax.ShapeDtypeStruct((1, LANES), F32),
                   jax.ShapeDtypeStruct((4, 1, gw), F32)],
        scratch_shapes=[pltpu.VMEM((12, LANES, LANES), F32)],
        compiler_params=_cparams("arbitrary", "arbitrary"), name=name)(
            conv, conv, conv, proj, proj, dt_bias, a_log, dskip, norm_g, saved, dy)


def _rwkv_consts():
    lanes = 16 * HEAD
    hl = jnp.arange(lanes) // HEAD
    e = (jnp.arange(16)[:, None] == hl[None, :]).astype(BF16)
    return e, e.T, _head_ones(lanes)


def _rwkv_step(sm, w, kk, b, k, r, v16, e, et, j):
    sa = _lin(sm * (-kk), j, j)
    vmat = _lin(v16, e, et)
    sn = sm * w + sa * b + vmat * k
    return sn, _lin(sn * r, et, e)


def _rwkv_fwd(w, kk, b, k, fp, v3, *, name):
    s, lanes = w.shape
    t_ = min(RWKV_CHUNK, s)
    nc = s // t_
    e, et, j = _rwkv_consts()
    rowspec = pl.BlockSpec((t_, lanes), lambda c: (c, 0))
    cspec = lambda a: pl.BlockSpec(a.shape, lambda c: (0, 0))

    def body(w_ref, kk_ref, b_ref, k_ref, r_ref, v_ref, e_ref, et_ref, j_ref, y_ref, sv_ref, st):
        c = pl.program_id(0)

        @pl.when(c == 0)
        def _():
            st[...] = jnp.zeros_like(st)

        sv_ref[0] = st[...]
        ev, etv, jv = e_ref[...], et_ref[...], j_ref[...]

        def step(t, carry):
            row = lambda ref: ref[pl.ds(t, 1), :]
            sn, y16 = _rwkv_step(st[...], row(w_ref), row(kk_ref), row(b_ref), row(k_ref), row(r_ref), v_ref[t], ev, etv, jv)
            st[...] = sn
            y_ref[t] = y16
            return carry

        lax.fori_loop(0, t_, step, 0)

    return pl.pallas_call(
        body, grid=(nc,),
        in_specs=[rowspec] * 5 + [pl.BlockSpec((t_, HEAD, 16), lambda c: (c, 0, 0)), cspec(e), cspec(et), cspec(j)],
        out_specs=[pl.BlockSpec((t_, HEAD, 16), lambda c: (c, 0, 0)), pl.BlockSpec((1, HEAD, lanes), lambda c: (c, 0, 0))],
        out_shape=[jax.ShapeDtypeStruct((s, HEAD, 16), F32), jax.ShapeDtypeStruct((nc, HEAD, lanes), F32)],
        scratch_shapes=[pltpu.VMEM((HEAD, lanes), F32)],
        compiler_params=_cparams("arbitrary"), name=name)(w, kk, b, k, fp, v3, e, et, j)


def _rwkv_bwd(w, kk, b, k, fp, v3, saved, dy3, *, name):
    s, lanes = w.shape
    t_ = min(RWKV_CHUNK, s)
    nc = s // t_
    e, et, j = _rwkv_consts()
    rev = lambda c: nc - 1 - c
    rowspec = pl.BlockSpec((t_, lanes), lambda c: (rev(c), 0))
    v3spec = pl.BlockSpec((t_, HEAD, 16), lambda c: (rev(c), 0, 0))
    cspec = lambda a: pl.BlockSpec(a.shape, lambda c: (0, 0))

    def body(w_ref, kk_ref, b_ref, k_ref, r_ref, v_ref, sv_ref, dy_ref, e_ref, et_ref, j_ref,
             dw_ref, dkk_ref, db_ref, dk_ref, dr_ref, dv_ref, dst, hist):
        c = pl.program_id(0)

        @pl.when(c == 0)
        def _():
            dst[...] = jnp.zeros_like(dst)

        ev, etv, jv = e_ref[...], et_ref[...], j_ref[...]
        row = lambda ref, t: ref[pl.ds(t, 1), :]

        def replay(t, sm):
            hist[t] = sm
            sn, _ = _rwkv_step(sm, row(w_ref, t), row(kk_ref, t), row(b_ref, t), row(k_ref, t), row(r_ref, t), v_ref[t], ev, etv, jv)
            return sn

        lax.fori_loop(0, t_, replay, sv_ref[0])

        def back(i, carry):
            t = t_ - 1 - i
            f = lambda sm, w_, kk_, b_, k_, r_, v_: _rwkv_step(sm, w_, kk_, b_, k_, r_, v_, ev, etv, jv)
            _, vjp = jax.vjp(f, hist[t], row(w_ref, t), row(kk_ref, t), row(b_ref, t), row(k_ref, t), row(r_ref, t), v_ref[t])
            dsm, dw, dkk, db, dk, dr, dv = vjp((dst[...], dy_ref[t]))
            dst[...] = dsm
            dw_ref[pl.ds(t, 1), :] = dw
            dkk_ref[pl.ds(t, 1), :] = dkk
            db_ref[pl.ds(t, 1), :] = db
            dk_ref[pl.ds(t, 1), :] = dk
            dr_ref[pl.ds(t, 1), :] = dr
            dv_ref[t] = dv
            return carry

        lax.fori_loop(0, t_, back, 0)

    return pl.pallas_call(
        body, grid=(nc,),
        in_specs=[rowspec] * 5 + [v3spec, pl.BlockSpec((1, HEAD, lanes), lambda c: (rev(c), 0, 0)), v3spec, cspec(e), cspec(et), cspec(j)],
        out_specs=[rowspec] * 5 + [v3spec],
        out_shape=[jax.ShapeDtypeStruct((s, lanes), F32)] * 5 + [jax.ShapeDtypeStruct((s, HEAD, 16), F32)],
        scratch_shapes=[pltpu.VMEM((HEAD, lanes), F32), pltpu.VMEM((t_, HEAD, lanes), F32)],
        compiler_params=_cparams("arbitrary"), name=name)(w, kk, b, k, fp, v3, saved, dy3, e, et, j)


def _blockdiag(blocks):
    g, a, b = blocks.shape
    return jnp.einsum("gab,gk->gakb", blocks, jnp.eye(g, dtype=blocks.dtype)).reshape(g * a, g * b)


def _blockdiag_t(dense, g):
    ga, gb = dense.shape
    idx = jnp.arange(g)
    return dense.reshape(g, ga // g, g, gb // g)[idx, :, idx, :]


def _pad_cols(x, n):
    return jnp.pad(x, ((0, 0), (0, n - x.shape[1])))


def _pad_rows(x, n):
    return jnp.pad(x, ((0, n - x.shape[0]), (0, 0)))


E_PROJ = 5120


def _even_in_cols(w):
    return jnp.concatenate([w[:, 512:2048], w[:, 0:512], w[:, 2048:4632], jnp.zeros((w.shape[0], E_PROJ - 4632), w.dtype)], axis=1)


def _even_in_cols_t(dw):
    return jnp.concatenate([dw[:, 1536:2048], dw[:, 0:1536], dw[:, 2048:4632]], axis=1)


O_PROJ = 5632


def _odd_in_cols(w):
    z32 = jnp.zeros((w.shape[0], 32), w.dtype)
    return jnp.concatenate([w[:, 0:3072], w[:, 3520:5568], w[:, 3264:3520], w[:, 3072:3168], z32, w[:, 3168:3264], z32], axis=1)


def _odd_in_cols_t(dw):
    return jnp.concatenate([dw[:, 0:3072], dw[:, 5376:5472], dw[:, 5504:5600], dw[:, 5120:5376], dw[:, 3072:5120]], axis=1)


def _mu_cols(mu):
    z32 = jnp.zeros((1, 32), mu.dtype)
    return jnp.concatenate([mu[:, 0:3072], mu[:, 3264:3520], mu[:, 3072:3168], z32, mu[:, 3168:3264], z32], axis=1)


def _mu_cols_t(d):
    return jnp.concatenate([d[:, 0:3072], d[:, 3328:3424], d[:, 3456:3552], d[:, 3072:3328]], axis=1)


def _conv_taps(x, w, b):
    y = b + w[3:4] * x
    for k in range(3):
        y = y + w[k:k + 1] * _shift_down(x, 3 - k)
    return y


def _conv_silu(x, w, b):
    y = _conv_taps(x, w, b)
    return y * jax.nn.sigmoid(y)


def _tshift(x, mu):
    return x + (_shift_down(x, 1) - x) * mu


def _relu2(_, a):
    r = jnp.maximum(a, 0.0)
    return (r * r,)


def _pl_gate(_, h, gl, e):
    return (h + jax.nn.sigmoid(gl) * e,)


def _s5_param(lr, li, ls, br, bi):
    step = jnp.exp(ls)
    mag = jnp.exp(lr * step)
    ar, ai = mag * jnp.cos(li * step), mag * jnp.sin(li * step)
    den = lr * lr + li * li
    nr = ar - 1.0
    cr = (nr * lr + ai * li) / den
    ci = (ai * lr - nr * li) / den
    return ar, ai, cr * br - ci * bi, cr * bi + ci * br


def _s5_post(_, ylin, u, d, gw, gb):
    act = jax.nn.gelu(ylin + d * u)
    return (act * jax.nn.sigmoid(_dot_bf16(act, gw) + gb),)


def _rwkv_pre(_, k, gl, wl, al, w0, w_up, a0, a_up, g_up, k_k, k_a, j):
    w = -_softplus(-(w0 + _dot_bf16(jnp.tanh(wl), w_up))) - 0.5
    decay = jnp.exp(-jnp.exp(w))
    a = jax.nn.sigmoid(a0 + _dot_bf16(al, a_up))
    g = _dot_bf16(jax.nn.sigmoid(gl), g_up)
    kk = k * k_k
    k2 = k * (1.0 + (a - 1.0) * k_a)
    kkn = kk * lax.rsqrt(jnp.maximum(_lin(kk * kk, j, j), 1e-24))
    return decay, kkn, kkn * a, k2, g


def _rwkv_post(_, y, r, k2, v, g, r_k, ln_g, ln_b, j):
    mean = _lin(y, j, j) * (1.0 / HEAD)
    yc = y - mean
    var = _lin(yc * yc, j, j) * (1.0 / HEAD)
    yn = yc * lax.rsqrt(var + RWKV_GN_EPS) * ln_g + ln_b
    return ((yn + _lin(r * k2 * r_k, j, j) * v) * g,)


def _lru_pre(row0, pre, xc, bax, lam):
    n = xc.shape[1]
    gr = jax.nn.sigmoid(pre[:, :n] + bax[:, :n])
    gi = jax.nn.sigmoid(pre[:, n:] + bax[:, n:])
    log_a = -LRU_C * gr * _softplus(-lam)
    m2 = -jnp.tanh(log_a) * (jnp.exp(2.0 * log_a) + 1.0)
    mult = jnp.sqrt(jnp.maximum(m2, 0.0))
    rowid = row0 + lax.broadcasted_iota(jnp.int32, (xc.shape[0], 1), 0)
    mult = jnp.where(rowid == 0, 1.0, mult)
    return jnp.exp(log_a), xc * gi * mult


def _lru_post(_, h, gl2):
    return (h * jax.nn.gelu(gl2),)


def _even_prep(w):
    sp = (w["s5_lam_re"].reshape(32, 64), w["s5_lam_im"].reshape(32, 64), w["s5_log_step"].reshape(32, 1),
          w["s5_b_re"].reshape(32, 64, 16).transpose(2, 0, 1), w["s5_b_im"].reshape(32, 64, 16).transpose(2, 0, 1))
    ar, ai, bbr, bbi = _block_vjp(_s5_param, sp, None, name="s5_param")
    bblk = lambda bb: _blockdiag(bb.transpose(1, 0, 2))
    cblk = lambda c: _blockdiag(c.reshape(32, 16, 64).transpose(0, 2, 1))
    pad = lambda x: _pad_cols(x.reshape(1, 24), LANES)
    return dict(
        sp=sp, a_row=jnp.concatenate([ar.reshape(1, 2048), ai.reshape(1, 2048)], axis=1),
        b_re=bblk(bbr), b_im=bblk(bbi), c_re=cblk(w["s5_c_re"]), c_imn=-cblk(w["s5_c_im"]),
        d=w["s5_d"].reshape(1, 512), gw=w["s5_glu_w"].reshape(512, 512), gb=w["s5_glu_b"].reshape(1, 512),
        conv_w=w["ssd_conv_w"].reshape(4, 2560), conv_b=w["ssd_conv_b"].reshape(1, 2560),
        dt_bias=pad(w["ssd_dt_bias"]), a_log=pad(w["ssd_a_log"]), dskip=pad(w["ssd_d"]), norm=w["ssd_norm"].reshape(1, 1536))


_E_XMAP = lambda j: 16 + j


def _even_fwd(proj, p):
    u = proj[:, 1536:2048]
    bur = _matmul(u, p["b_re"], "nn", name="s5_bu_re")
    bui = _matmul(u, p["b_im"], "nn", name="s5_bu_im")
    xr, xi = _cscan_fwd(jnp.concatenate([bur, bui], axis=1), p["a_row"], name="s5_scan")
    ylin = _matmul(xi, p["c_imn"], "nn", name="s5_y_im", add=_matmul(xr, p["c_re"], "nn", name="s5_y_re"))
    (ya,) = _rw_fwd(_s5_post, [_row(ylin), _row(u)], [p["d"], p["gw"], p["gb"]], [(512, F32)], name="s5_post")
    conv = _ct_fwd(_conv_silu, proj, _E_XMAP, 20, [(p["conv_w"], 0), (p["conv_b"], 0)], 2560, name="ssd_conv")
    yb, saved = _ssd_fwd(conv, proj, p["dt_bias"], p["a_log"], p["dskip"], p["norm"], name="ssd_scan")
    return jnp.concatenate([ya, yb], axis=1), (u, xr, xi, ylin, conv, saved)


def _even_bwd(proj, p, res, dy):
    u, xr, xi, ylin, conv, saved = res
    s = proj.shape[0]
    dxs, dbm, dcm, dz, ddt, ddtb, dalog, ddsk, dng = _ssd_bwd(
        conv, proj, p["dt_bias"], p["a_log"], p["dskip"], p["norm"], saved, dy[:, 512:], name="ssd_scan_bwd")
    dxbc, (dcw, dcb) = _ct_bwd(_conv_silu, proj, _E_XMAP, 20, [(p["conv_w"], 0), (p["conv_b"], 0)],
                               [jnp.concatenate([dxs, dbm, dcm], axis=1)], name="ssd_conv_bwd")
    (dylin, du), (dd, dgw, dgb) = _rw_bwd(_s5_post, [_row(ylin), _row(u)], [p["d"], p["gw"], p["gb"]], [dy[:, :512]], name="s5_post_bwd")
    dxr = _matmul(dylin, p["c_re"], "nt", name="s5_dxr")
    dxi = _matmul(dylin, p["c_imn"], "nt", name="s5_dxi")
    dc_re = _matmul(xr, dylin, "tn", name="s5_dc_re")
    dc_imn = _matmul(xi, dylin, "tn", name="s5_dc_im")
    dbr, dbi, dar, dai = _cscan_bwd(dxr, dxi, xr, xi, p["a_row"], name="s5_scan_bwd")
    du = _matmul(dbr, p["b_re"], "nt", name="s5_du_re", add=du)
    du = _matmul(dbi, p["b_im"], "nt", name="s5_du_im", add=du)
    db_re = _matmul(u, dbr, "tn", name="s5_db_re")
    db_im = _matmul(u, dbi, "tn", name="s5_db_im")
    unblk = lambda d: _blockdiag_t(d, 32).transpose(1, 0, 2)
    g_sp = _block_vjp(_s5_param, p["sp"], (dar.reshape(32, 64), dai.reshape(32, 64), unblk(db_re), unblk(db_im)), name="s5_param_bwd")
    dproj = jnp.concatenate([dz, du, dxbc, ddt, jnp.zeros((s, E_PROJ - 4736), F32)], axis=1)
    uncblk = lambda d: _blockdiag_t(d, 32).transpose(0, 2, 1)
    grads = dict(
        s5_lam_re=g_sp[0].reshape(1, 32, 64), s5_lam_im=g_sp[1].reshape(1, 32, 64), s5_log_step=g_sp[2].reshape(1, 32),
        s5_b_re=g_sp[3].transpose(1, 2, 0)[None], s5_b_im=g_sp[4].transpose(1, 2, 0)[None],
        s5_c_re=uncblk(dc_re)[None], s5_c_im=-uncblk(dc_imn)[None], s5_d=dd, s5_glu_w=dgw[None], s5_glu_b=dgb,
        ssd_conv_w=dcw[None], ssd_conv_b=dcb, ssd_dt_bias=ddtb[:, :24], ssd_a_log=dalog[:, :24], ssd_d=ddsk[:, :24],
        ssd_norm=dng.reshape(1, 1536))
    return dproj, grads


def _odd_prep(w):
    pad128 = lambda x: _pad_rows(x, LANES)
    return dict(
        mu=_mu_cols(w["rwkv_mu"].reshape(1, 3520)), w0=w["rwkv_w0"].reshape(1, 1024), w_up=pad128(w["rwkv_w_up"].reshape(96, 1024)),
        a0=w["rwkv_a0"].reshape(1, 1024), a_up=pad128(w["rwkv_a_up"].reshape(96, 1024)), g_up=w["rwkv_g_up"].reshape(256, 1024),
        k_k=w["rwkv_k_k"].reshape(1, 1024), k_a=w["rwkv_k_a"].reshape(1, 1024), r_k=w["rwkv_r_k"].reshape(1, 1024),
        ln_g=w["rwkv_ln_g"].reshape(1, 1024), ln_b=w["rwkv_ln_b"].reshape(1, 1024), j=_head_ones(1024),
        conv_w=w["lru_conv_w"].reshape(4, 1024), conv_b=w["lru_conv_b"].reshape(1, 1024),
        wax=jnp.concatenate([_blockdiag(w["lru_w_a"].reshape(16, 64, 64)), _blockdiag(w["lru_w_x"].reshape(16, 64, 64))], axis=1),
        bax=jnp.concatenate([w["lru_b_a"].reshape(1, 1024), w["lru_b_x"].reshape(1, 1024)], axis=1), lam=w["lru_lam"].reshape(1, 1024))


_O_XMAP = lambda j: jnp.where(j < 24, j, j + 16)
_O_LMAP = lambda j: 24 + j


def _to_heads(x):
    return x.reshape(x.shape[0], 16, HEAD).transpose(0, 2, 1)


def _from_heads(x3):
    return x3.transpose(0, 2, 1).reshape(x3.shape[0], 16 * HEAD)


def _odd_rows(fp, y, k2, g):
    pre = [_row(fp, 1024, 1), _row(fp, 256, 12), _row(fp, 128, 26), _row(fp, 128, 27)]
    post = None if y is None else [_row(y), _row(fp, 1024, 0), _row(k2), _row(fp, 1024, 2), _row(g)]
    return pre, post


def _odd_fwd(proj, p):
    fp = _ct_fwd(_tshift, proj, _O_XMAP, 28, [(p["mu"], 0)], 3584, name="rwkv_shift")
    pre_rows, _ = _odd_rows(fp, None, None, None)
    pre_params = [p["w0"], p["w_up"], p["a0"], p["a_up"], p["g_up"], p["k_k"], p["k_a"], p["j"]]
    decay, kkn, b, k2, g = _rw_fwd(_rwkv_pre, pre_rows, pre_params, [(1024, F32)] * 5, name="rwkv_pre")
    v3 = _to_heads(fp[:, 2048:3072])
    y3, saved = _rwkv_fwd(decay, kkn, b, k2, fp, v3, name="rwkv_scan")
    y = _from_heads(y3)
    _, post_rows = _odd_rows(fp, y, k2, g)
    (yc,) = _rw_fwd(_rwkv_post, post_rows, [p["r_k"], p["ln_g"], p["ln_b"], p["j"]], [(1024, F32)], name="rwkv_post")
    xc = _ct_fwd(_conv_taps, proj, _O_LMAP, 8, [(p["conv_w"], 0), (p["conv_b"], 0)], 1024, name="lru_conv")
    pre = _matmul(xc, p["wax"], "nn", name="lru_gates")
    a, bx = _rw_fwd(_lru_pre, [_row(pre), _row(xc)], [p["bax"], p["lam"]], [(1024, F32)] * 2, name="lru_pre")
    hseq = _rscan_fwd(a, bx, name="lru_scan")
    (yd,) = _rw_fwd(_lru_post, [_row(hseq), _row(proj, 1024, 4)], [], [(1024, F32)], name="lru_post")
    return jnp.concatenate([yc, yd], axis=1), (fp, decay, kkn, b, k2, g, v3, saved, y, xc, pre, a, hseq)


def _odd_bwd(proj, p, res, dy):
    fp, decay, kkn, b, k2, g, v3, saved, y, xc, pre, a, hseq = res
    s = proj.shape[0]
    (dh, dgl2), _ = _rw_bwd(_lru_post, [_row(hseq), _row(proj, 1024, 4)], [], [dy[:, 1024:]], name="lru_post_bwd")
    dbx, da = _rscan_bwd(dh, a, hseq, name="lru_scan_bwd")
    (dpre, dxc), (dbax, dlam) = _rw_bwd(_lru_pre, [_row(pre), _row(xc)], [p["bax"], p["lam"]], [da, dbx], name="lru_pre_bwd")
    dxc = _matmul(dpre, p["wax"], "nt", name="lru_gates_dx", add=dxc)
    dwax = _matmul(xc, dpre, "tn", name="lru_gates_dw")
    dxl, (dlcw, dlcb) = _ct_bwd(_conv_taps, proj, _O_LMAP, 8, [(p["conv_w"], 0), (p["conv_b"], 0)], [dxc], name="lru_conv_bwd")
    pre_rows, post_rows = _odd_rows(fp, y, k2, g)
    (dyn, dr1, dk2a, dv1, dg), (dr_k, dln_g, dln_b) = _rw_bwd(
        _rwkv_post, post_rows, [p["r_k"], p["ln_g"], p["ln_b"], p["j"]], [dy[:, :1024]], name="rwkv_post_bwd", param_grads=[0, 1, 2])
    ddecay, dkkn, db, dk2b, dr2, dv3 = _rwkv_bwd(decay, kkn, b, k2, fp, v3, saved, _to_heads(dyn), name="rwkv_scan_bwd")
    pre_params = [p["w0"], p["w_up"], p["a0"], p["a_up"], p["g_up"], p["k_k"], p["k_a"], p["j"]]
    (dk, dgl, dwl, dal), (dw0, dw_up, da0, da_up, dg_up, dk_k, dk_a) = _rw_bwd(
        _rwkv_pre, pre_rows, pre_params, [ddecay, dkkn, db, [dk2a, dk2b], dg], name="rwkv_pre_bwd", param_grads=list(range(7)))
    z = lambda n: jnp.zeros((s, n), F32)
    g1 = jnp.concatenate([dr1, dk, dv1, dgl, dwl, dal], axis=1)
    g2 = jnp.concatenate([dr2, z(1024), _from_heads(dv3), z(512)], axis=1)
    dfp, (dmu,) = _ct_bwd(_tshift, proj, _O_XMAP, 28, [(p["mu"], 0)], [g1, g2], name="rwkv_shift_bwd")
    dproj = jnp.concatenate([dfp[:, :3072], dxl, dgl2, dfp[:, 3072:]], axis=1)
    grads = dict(
        rwkv_mu=_mu_cols_t(dmu), rwkv_w0=dw0, rwkv_w_up=dw_up[:96][None], rwkv_a0=da0, rwkv_a_up=da_up[:96][None], rwkv_g_up=dg_up[None],
        rwkv_k_k=dk_k, rwkv_k_a=dk_a, rwkv_r_k=dr_k.reshape(1, 16, 64), rwkv_ln_g=dln_g, rwkv_ln_b=dln_b,
        lru_conv_w=dlcw[None], lru_conv_b=dlcb, lru_w_a=_blockdiag_t(dwax[:, :1024], 16)[None], lru_w_x=_blockdiag_t(dwax[:, 1024:], 16)[None],
        lru_b_a=dbax[:, :1024].reshape(1, 16, 64), lru_b_x=dbax[:, 1024:].reshape(1, 16, 64), lru_lam=dlam.reshape(1, 16, 64))
    return dproj, grads


def _my_index():
    return 4 * lax.axis_index("x") + 2 * lax.axis_index("y") + lax.axis_index("c")


def _peer(k):
    x, y, c = lax.axis_index("x"), lax.axis_index("y"), lax.axis_index("c")
    return (1 - x if k & 4 else x, 1 - y if k & 2 else y, 1 - c if k & 1 else c)


def _exchange(src, *, gather, name):
    r = src.shape[-2]

    def body(src_ref, out_ref, send_sems, recv_sems, local_sem):
        me = _my_index()
        mine = src_ref if gather else src_ref.at[me]
        local = pltpu.make_async_copy(mine, out_ref.at[me], local_sem)
        local.start()

        def copy(k):
            return pltpu.make_async_remote_copy(
                src_ref=src_ref if gather else src_ref.at[jnp.bitwise_xor(me, k)], dst_ref=out_ref.at[me],
                send_sem=send_sems.at[k - 1], recv_sem=recv_sems.at[k - 1], device_id=_peer(k), device_id_type=pl.DeviceIdType.MESH)

        def arrival(k):
            return pltpu.make_async_remote_copy(
                src_ref=mine, dst_ref=out_ref.at[jnp.bitwise_xor(me, k)],
                send_sem=send_sems.at[k - 1], recv_sem=recv_sems.at[k - 1], device_id=_peer(k), device_id_type=pl.DeviceIdType.MESH)

        sends = [copy(k) for k in range(1, N_DEV)]
        for cp in sends:
            cp.start()
        for k in range(1, N_DEV):
            arrival(k).wait_recv()
        for cp in sends:
            cp.wait_send()
        local.wait()

    return pl.pallas_call(
        body, out_shape=jax.ShapeDtypeStruct((N_DEV, r, LANES), src.dtype),
        in_specs=[pl.BlockSpec(memory_space=pl.ANY)], out_specs=pl.BlockSpec(memory_space=pl.ANY),
        scratch_shapes=[pltpu.SemaphoreType.DMA((N_DEV - 1,)), pltpu.SemaphoreType.DMA((N_DEV - 1,)), pltpu.SemaphoreType.DMA],
        name=name)(src)


PACK_ALIGN = 16 * LANES
PACK_ROWS = 512


def _pack(arrs, dtype, lead=False):
    parts = []
    for a in arrs:
        flat = a.reshape((a.shape[0], -1) if lead else (1, -1)).astype(dtype)
        parts.append(jnp.pad(flat, ((0, 0), (0, -flat.shape[1] % PACK_ALIGN))))
    buf = jnp.concatenate(parts, axis=1)
    buf = jnp.pad(buf, ((0, 0), (0, -buf.shape[1] % (PACK_ROWS * LANES))))
    buf = buf.reshape(buf.shape[0], -1, LANES)
    return buf if lead else buf[0]


def _unpack(buf, shapes, lead=False):
    flat = buf.reshape(buf.shape[0], -1) if lead else buf.reshape(1, -1)
    out, off = [], 0
    for shp in shapes:
        n = math.prod(shp)
        piece = flat[:, off:off + n]
        out.append(piece.reshape(((flat.shape[0],) if lead else ()) + tuple(shp)))
        off += n + (-n % PACK_ALIGN)
    return out


def _unshard(parts, axis):
    moved = jnp.moveaxis(parts, 0, axis)
    shp = list(moved.shape)
    return moved.reshape(shp[:axis] + [shp[axis] * shp[axis + 1]] + shp[axis + 2:])


def _to_parts(full, axis):
    shp = list(full.shape)
    split = full.reshape(shp[:axis] + [N_DEV, shp[axis] // N_DEV] + shp[axis + 1:])
    return jnp.moveaxis(split, axis, 0)


def _adamw(gparts, w, m, v, *, name):
    r = w.shape[0]
    tr = _pick(r, (PACK_ROWS,))

    def body(g_ref, w_ref, m_ref, v_ref, go, do, mo, vo):
        g = g_ref[0].astype(F32)
        for d in range(1, N_DEV):
            g = g + g_ref[d].astype(F32)
        m1 = ADAM_B1 * m_ref[...] + (1.0 - ADAM_B1) * g
        v1 = ADAM_B2 * v_ref[...] + (1.0 - ADAM_B2) * jnp.square(g)
        m_hat = m1 / (1.0 - ADAM_B1 ** ADAM_STEP)
        v_hat = v1 / (1.0 - ADAM_B2 ** ADAM_STEP)
        go[...] = g
        do[...] = -ADAM_LR * (m_hat / (jnp.sqrt(v_hat) + ADAM_EPS) + ADAM_WD * w_ref[...])
        mo[...] = m1
        vo[...] = v1

    blk = pl.BlockSpec((tr, LANES), lambda i: (i, 0))
    return pl.pallas_call(
        body, grid=(r // tr,), in_specs=[pl.BlockSpec((N_DEV, tr, LANES), lambda i: (0, i, 0)), blk, blk, blk], out_specs=[blk] * 4,
        out_shape=[jax.ShapeDtypeStruct((r, LANES), F32)] * 4, compiler_params=_cparams("parallel"), name=name)(gparts, w, m, v)


def _loss_and_grad(h, g, tgt, *, name, ts=256):
    s, d = h.shape
    ts = min(ts, s)

    def tile_loss(hv, gv, tv):
        (y,) = _rmsnorm_tile(0, hv, gv)
        return 0.5 * jnp.sum(jnp.mean(jnp.square(y - tv), axis=-1))

    def body(h_ref, g_ref, t_ref, l_ref, dh_ref, dg_ref):
        @pl.when(pl.program_id(0) == 0)
        def _():
            l_ref[...] = jnp.zeros_like(l_ref)
            dg_ref[...] = jnp.zeros_like(dg_ref)

        tv = t_ref[...]
        loss, vjp = jax.vjp(lambda hv, gv: tile_loss(hv, gv, tv), h_ref[...], g_ref[...])
        dh, dg = vjp(jnp.ones((), F32))
        l_ref[...] += loss
        dh_ref[...] = dh
        dg_ref[...] += dg

    row = pl.BlockSpec((ts, d), lambda i: (i, 0))
    return pl.pallas_call(
        body, grid=(s // ts,), in_specs=[row, pl.BlockSpec((1, d), lambda i: (0, 0)), row],
        out_specs=[pl.BlockSpec((8, LANES), lambda i: (0, 0)), row, pl.BlockSpec((1, d), lambda i: (0, 0))],
        out_shape=[jax.ShapeDtypeStruct((8, LANES), F32), jax.ShapeDtypeStruct((s, d), F32), jax.ShapeDtypeStruct((1, d), F32)],
        compiler_params=_cparams("arbitrary"), name=name)(h, g, tgt)


def _norm(h, g, name):
    return _rw_fwd(_rmsnorm_tile, [_row(h)], [g], [(h.shape[1], F32)], name=name)[0]


def _norm_bwd(h, g, dhn, dres, name):
    (dh,), (dg,) = _rw_bwd(_rmsnorm_tile, [_row(h)], [g], [dhn], add_rows={0: dres}, name=name)
    return dh, dg


def _layer_fwd(h, p_i, lw, mixer_fwd, mp, tag):
    hn = _norm(h, lw["norm_mix"], f"{tag}_norm_mix")
    proj = _matmul(hn, lw["w_in"], "nn", name=f"{tag}_in_proj")
    y, mres = mixer_fwd(proj, mp)
    h1 = _matmul(y, lw["w_out"], "nn", name=f"{tag}_out_proj", add=h)
    hn2 = _norm(h1, lw["norm_ffn"], f"{tag}_norm_ffn")
    a1 = _matmul(hn2, lw["w1"], "nn", name=f"{tag}_mlp_up")
    (act,) = _rw_fwd(_relu2, [_row(a1)], [], [(a1.shape[1], F32)], name=f"{tag}_relu2", ts=64)
    h2 = _matmul(act, lw["w2"], "nn", name=f"{tag}_mlp_down", add=h1)
    hn3 = _norm(h2, lw["norm_pl"], f"{tag}_norm_pl")
    gl = _matmul(hn3, lw["w_gate"], "nn", name=f"{tag}_pl_gate")
    e = _matmul(p_i, lw["w_pl"], "nn", name=f"{tag}_pl_proj")
    (h3,) = _rw_fwd(_pl_gate, [_row(h2), _row(gl), _row(e)], [], [(h.shape[1], F32)], name=f"{tag}_pl_mix")
    return h3, (h, hn, proj, y, mres, h1, hn2, a1, act, h2, hn3, gl, e)


def _layer_bwd(dh3, p_i, lw, mixer_bwd, mp, saved, tag):
    h, hn, proj, y, mres, h1, hn2, a1, act, h2, hn3, gl, e = saved
    (dgl, de), _ = _rw_bwd(_pl_gate, [_row(h2), _row(gl), _row(e)], [], [dh3], row_grads=[1, 2], name=f"{tag}_pl_mix_bwd")
    g = dict(w_pl=_matmul(p_i, de, "tn", name=f"{tag}_pl_proj_dw"), w_gate=_matmul(hn3, dgl, "tn", name=f"{tag}_pl_gate_dw"))
    dhn3 = _matmul(dgl, lw["w_gate"], "nt", name=f"{tag}_pl_gate_dx")
    dh2, g["norm_pl"] = _norm_bwd(h2, lw["norm_pl"], dhn3, dh3, f"{tag}_norm_pl_bwd")
    dact = _matmul(dh2, lw["w2"], "nt", name=f"{tag}_mlp_down_dx")
    g["w2"] = _matmul(act, dh2, "tn", name=f"{tag}_mlp_down_dw")
    (da1,), _ = _rw_bwd(_relu2, [_row(a1)], [], [dact], name=f"{tag}_relu2_bwd", ts=64)
    g["w1"] = _matmul(hn2, da1, "tn", name=f"{tag}_mlp_up_dw")
    dhn2 = _matmul(da1, lw["w1"], "nt", name=f"{tag}_mlp_up_dx")
    dh1, g["norm_ffn"] = _norm_bwd(h1, lw["norm_ffn"], dhn2, dh2, f"{tag}_norm_ffn_bwd")
    dy = _matmul(dh1, lw["w_out"], "nt", name=f"{tag}_out_proj_dx")
    g["w_out"] = _matmul(y, dh1, "tn", name=f"{tag}_out_proj_dw")
    dproj, mg = mixer_bwd(proj, mp, mres, dy)
    g["w_in"] = _matmul(hn, dproj, "tn", name=f"{tag}_in_proj_dw")
    dhn = _matmul(dproj, lw["w_in"], "nt", name=f"{tag}_in_proj_dx")
    dh, g["norm_mix"] = _norm_bwd(h, lw["norm_mix"], dhn, dh1, f"{tag}_norm_mix_bwd")
    return dh, g, mg


WEIGHTS = (
    ("norm_mix", None), ("norm_ffn", None), ("norm_pl", None), ("mlp_w1", 2), ("mlp_w2", 1), ("pl_proj", 2), ("pl_gate", 1),
    ("e_in_proj", 2), ("e_out_proj", 1), ("s5_lam_re", None), ("s5_lam_im", None), ("s5_log_step", None), ("s5_b_re", None),
    ("s5_b_im", None), ("s5_c_re", None), ("s5_c_im", None), ("s5_d", None), ("s5_glu_w", 1), ("s5_glu_b", None),
    ("ssd_conv_w", 2), ("ssd_conv_b", None), ("ssd_dt_bias", None), ("ssd_a_log", None), ("ssd_d", None), ("ssd_norm", None),
    ("o_in_proj", 2), ("o_out_proj", 1), ("rwkv_mu", 1), ("rwkv_w0", 1), ("rwkv_w_up", 2), ("rwkv_a0", 1), ("rwkv_a_up", 2),
    ("rwkv_g_up", 2), ("rwkv_k_k", 1), ("rwkv_k_a", 1), ("rwkv_r_k", None), ("rwkv_ln_g", 1), ("rwkv_ln_b", 1),
    ("lru_conv_w", 2), ("lru_conv_b", 1), ("lru_w_a", None), ("lru_b_a", None), ("lru_w_x", None), ("lru_b_x", None),
    ("lru_lam", None), ("norm_final", None))
MATMUL_WEIGHTS = ("mlp_w1", "mlp_w2", "pl_proj", "pl_gate", "e_in_proj", "e_out_proj", "s5_glu_w", "o_in_proj", "o_out_proj",
                  "rwkv_w_up", "rwkv_a_up", "rwkv_g_up")


def _gather_weights(a):
    axis = dict(WEIGHTS)
    full = {n: a[n] for n, ax in WEIGHTS if ax is None}
    for names, dtype, tag in ((MATMUL_WEIGHTS, BF16, "bf16"), ([n for n, ax in WEIGHTS if ax is not None and n not in MATMUL_WEIGHTS], F32, "f32")):
        got = _exchange(_pack([a[n] for n in names], dtype), gather=True, name=f"gather_weights_{tag}")
        for n, parts in zip(names, _unpack(got, [a[n].shape for n in names], lead=True)):
            full[n] = _unshard(parts, axis[n])
    return full


def kernel(x, p, norm_mix, norm_ffn, norm_pl, mlp_w1, mlp_w2, pl_proj, pl_gate, e_in_proj, e_out_proj, s5_lam_re, s5_lam_im, s5_log_step, s5_b_re, s5_b_im, s5_c_re, s5_c_im, s5_d, s5_glu_w, s5_glu_b, ssd_conv_w, ssd_conv_b, ssd_dt_bias, ssd_a_log, ssd_d, ssd_norm, o_in_proj, o_out_proj, rwkv_mu, rwkv_w0, rwkv_w_up, rwkv_a0, rwkv_a_up, rwkv_g_up, rwkv_k_k, rwkv_k_a, rwkv_r_k, rwkv_ln_g, rwkv_ln_b, lru_conv_w, lru_conv_b, lru_w_a, lru_b_a, lru_w_x, lru_b_x, lru_lam, norm_final, loss_target, m_norm_mix, m_norm_ffn, m_norm_pl, m_mlp_w1, m_mlp_w2, m_pl_proj, m_pl_gate, m_e_in_proj, m_e_out_proj, m_s5_lam_re, m_s5_lam_im, m_s5_log_step, m_s5_b_re, m_s5_b_im, m_s5_c_re, m_s5_c_im, m_s5_d, m_s5_glu_w, m_s5_glu_b, m_ssd_conv_w, m_ssd_conv_b, m_ssd_dt_bias, m_ssd_a_log, m_ssd_d, m_ssd_norm, m_o_in_proj, m_o_out_proj, m_rwkv_mu, m_rwkv_w0, m_rwkv_w_up, m_rwkv_a0, m_rwkv_a_up, m_rwkv_g_up, m_rwkv_k_k, m_rwkv_k_a, m_rwkv_r_k, m_rwkv_ln_g, m_rwkv_ln_b, m_lru_conv_w, m_lru_conv_b, m_lru_w_a, m_lru_b_a, m_lru_w_x, m_lru_b_x, m_lru_lam, m_norm_final, v_norm_mix, v_norm_ffn, v_norm_pl, v_mlp_w1, v_mlp_w2, v_pl_proj, v_pl_gate, v_e_in_proj, v_e_out_proj, v_s5_lam_re, v_s5_lam_im, v_s5_log_step, v_s5_b_re, v_s5_b_im, v_s5_c_re, v_s5_c_im, v_s5_d, v_s5_glu_w, v_s5_glu_b, v_ssd_conv_w, v_ssd_conv_b, v_ssd_dt_bias, v_ssd_a_log, v_ssd_d, v_ssd_norm, v_o_in_proj, v_o_out_proj, v_rwkv_mu, v_rwkv_w0, v_rwkv_w_up, v_rwkv_a0, v_rwkv_a_up, v_rwkv_g_up, v_rwkv_k_k, v_rwkv_k_a, v_rwkv_r_k, v_rwkv_ln_g, v_rwkv_ln_b, v_lru_conv_w, v_lru_conv_b, v_lru_w_a, v_lru_b_a, v_lru_w_x, v_lru_b_x, v_lru_lam, v_norm_final):
    a = dict(locals())
    full = _gather_weights(a)
    d_model = x.shape[-1]
    row = lambda v: v.reshape(1, d_model)

    layers = []
    for i, (mixer_fwd, mixer_bwd, prep, w_in, cols, out_name) in enumerate((
            (_even_fwd, _even_bwd, _even_prep, "e_in_proj", _even_in_cols, "e_out_proj"),
            (_odd_fwd, _odd_bwd, _odd_prep, "o_in_proj", _odd_in_cols, "o_out_proj"))):
        lw = dict(norm_mix=row(norm_mix[i]), norm_ffn=row(norm_ffn[i]), norm_pl=row(norm_pl[i]), w_in=cols(full[w_in][0]),
                  w_out=full[out_name][0], w1=full["mlp_w1"][i], w2=full["mlp_w2"][i], w_gate=full["pl_gate"][i], w_pl=full["pl_proj"][i])
        layers.append((lw, mixer_fwd, mixer_bwd, prep(full), p[i, 0], f"l{i}"))

    h = x[0]
    saved = []
    for lw, mixer_fwd, _, mp, p_i, tag in layers:
        h, sv = _layer_fwd(h, p_i, lw, mixer_fwd, mp, tag)
        saved.append(sv)
    loss_blk, dh, dg_final = _loss_and_grad(h, row(norm_final), loss_target[0], name="loss")
    loss = lax.psum(loss_blk[0, 0], ("x", "y", "c"))

    lgrads = [None, None]
    mgrads = {}
    for i in (1, 0):
        lw, _, mixer_bwd, mp, p_i, tag = layers[i]
        dh, lgrads[i], mg = _layer_bwd(dh, p_i, lw, mixer_bwd, mp, saved[i], tag)
        mgrads.update(mg)
    stack = lambda key: jnp.stack([lgrads[0][key], lgrads[1][key]])
    grads = dict(mgrads)
    grads.update(
        norm_mix=stack("norm_mix").reshape(2, d_model), norm_ffn=stack("norm_ffn").reshape(2, d_model),
        norm_pl=stack("norm_pl").reshape(2, d_model), mlp_w1=stack("w1"), mlp_w2=stack("w2"), pl_proj=stack("w_pl"), pl_gate=stack("w_gate"),
        e_in_proj=_even_in_cols_t(lgrads[0]["w_in"])[None], e_out_proj=lgrads[0]["w_out"][None],
        o_in_proj=_odd_in_cols_t(lgrads[1]["w_in"])[None], o_out_proj=lgrads[1]["w_out"][None], norm_final=dg_final.reshape(d_model))

    sh = [(n, ax) for n, ax in WEIGHTS if ax is not None]
    slabs = _pack([_to_parts(grads[n], ax) for n, ax in sh], BF16, lead=True)
    got = _exchange(slabs, gather=False, name="scatter_grads")
    sh_out = _adamw(got, *[_pack([a[pre + n] for n, _ in sh], F32) for pre in ("", "m_", "v_")], name="adamw_sharded")
    rp = [n for n, ax in WEIGHTS if ax is None]
    parts = _exchange(_pack([grads[n].reshape(a[n].shape) for n in rp], F32), gather=True, name="gather_small_grads")
    rp_out = _adamw(parts, *[_pack([a[pre + n] for n in rp], F32) for pre in ("", "m_", "v_")], name="adamw_replicated")

    outs = []
    for k in range(4):
        by_name = dict(zip([n for n, _ in sh], _unpack(sh_out[k], [a[n].shape for n, _ in sh])))
        by_name.update(zip(rp, _unpack(rp_out[k], [a[n].shape for n in rp])))
        outs.append([by_name[n] for n, _ in WEIGHTS])
    return (loss, dh[None], *outs[0], *outs[1], *outs[2], *outs[3])
```

```python
import functools
import math

import jax
import jax.numpy as jnp
from jax import lax
from jax.experimental import pallas as pl
from jax.experimental.pallas import tpu as pltpu

F32 = jnp.float32
BF16 = jnp.bfloat16
N_DEV = 8
LANES = 128
VMEM_LIMIT = 56 * 1024 * 1024
NORM_EPS = 1e-6
RWKV_GN_EPS = 64e-5
LRU_C = 8.0
ADAM_LR, ADAM_B1, ADAM_B2, ADAM_EPS, ADAM_WD, ADAM_STEP = 0.001, 0.9, 0.999, 1e-08, 0.01, 10
SSD_CHUNK = 128
RWKV_CHUNK = 32
HEAD = 64


def _cparams(*sem):
    return pltpu.CompilerParams(dimension_semantics=sem, vmem_limit_bytes=VMEM_LIMIT)


def _pick(n, prefs):
    for t in prefs:
        if n % t == 0:
            return t
    return n


def _matmul(a, b, mode, *, name, add=None, out_dtype=F32):
    if mode == "nn":
        (m, k), (k2, n) = a.shape, b.shape
    elif mode == "nt":
        (m, k), (n, k2) = a.shape, b.shape
    else:
        (k, m), (k2, n) = a.shape, b.shape
    assert k == k2, (a.shape, b.shape, mode)
    tm, tn, tk = _pick(m, (1024, 512, 256, 128)), _pick(n, (1024, 512, 256, 128)), _pick(k, (512, 256, 128))
    nk = k // tk
    dn = {"nn": (((1,), (0,)), ((), ())), "nt": (((1,), (1,)), ((), ())), "tn": (((0,), (0,)), ((), ()))}[mode]
    a_spec = pl.BlockSpec((tk, tm), lambda i, j, kk: (kk, i)) if mode == "tn" else pl.BlockSpec((tm, tk), lambda i, j, kk: (i, kk))
    b_spec = pl.BlockSpec((tn, tk), lambda i, j, kk: (j, kk)) if mode == "nt" else pl.BlockSpec((tk, tn), lambda i, j, kk: (kk, j))
    o_spec = pl.BlockSpec((tm, tn), lambda i, j, kk: (i, j))

    def body(a_ref, b_ref, *rest):
        if add is None:
            o_ref, acc = rest
        else:
            add_ref, o_ref, acc = rest
        kk = pl.program_id(2)

        @pl.when(kk == 0)
        def _():
            acc[...] = jnp.zeros_like(acc) if add is None else add_ref[...].astype(F32)

        acc[...] += lax.dot_general(a_ref[...].astype(BF16), b_ref[...].astype(BF16), dn, preferred_element_type=F32)

        @pl.when(kk == nk - 1)
        def _():
            o_ref[...] = acc[...].astype(o_ref.dtype)

    ins, specs = [a, b], [a_spec, b_spec]
    if add is not None:
        ins.append(add)
        specs.append(o_spec)
    return pl.pallas_call(
        body, grid=(m // tm, n // tn, nk), in_specs=specs, out_specs=o_spec,
        out_shape=jax.ShapeDtypeStruct((m, n), out_dtype), scratch_shapes=[pltpu.VMEM((tm, tn), F32)],
        compiler_params=_cparams("parallel", "parallel", "arbitrary"), name=name)(*ins)


def _row(x, width=None, block=0):
    return (x, x.shape[1] if width is None else width, block)


def _rw_specs(rows, params, ts):
    specs = [pl.BlockSpec((ts, w), functools.partial(lambda i, b: (i, b), b=bi)) for (_, w, bi) in rows]
    specs += [pl.BlockSpec(p.shape, functools.partial(lambda i, nd: (0,) * nd, nd=p.ndim)) for p in params]
    return specs


def _rw_fwd(f, rows, params, outs, *, name, ts=256):
    s = rows[0][0].shape[0]
    ts = min(ts, s)
    nr, npar = len(rows), len(params)

    def body(*refs):
        row0 = pl.program_id(0) * ts
        res = f(row0, *[r[...] for r in refs[:nr + npar]])
        for o_ref, val in zip(refs[nr + npar:], res, strict=True):
            o_ref[...] = val.astype(o_ref.dtype)

    out = pl.pallas_call(
        body, grid=(s // ts,), in_specs=_rw_specs(rows, params, ts),
        out_specs=[pl.BlockSpec((ts, w), lambda i: (i, 0)) for (w, _) in outs],
        out_shape=[jax.ShapeDtypeStruct((s, w), dt) for (w, dt) in outs],
        compiler_params=_cparams("parallel"), name=name)(*[r[0] for r in rows], *params)
    return tuple(out)


def _rw_bwd(f, rows, params, cts, *, name, ts=256, row_grads=None, param_grads=None, add_rows=None):
    s = rows[0][0].shape[0]
    ts = min(ts, s)
    ct_groups = [list(c) if isinstance(c, (list, tuple)) else [c] for c in cts]
    cts = [c for grp in ct_groups for c in grp]
    nr, npar, nct = len(rows), len(params), len(cts)
    row_grads = list(range(nr)) if row_grads is None else list(row_grads)
    param_grads = list(range(npar)) if param_grads is None else list(param_grads)
    add_rows = add_rows or {}
    add_keys = sorted(add_rows)

    def body(*refs):
        i = pl.program_id(0)
        row0 = i * ts
        vals = [r[...] for r in refs[:nr + npar]]
        for pi in param_grads:
            vals[nr + pi] = vals[nr + pi].astype(F32)
        ct_refs = list(refs[nr + npar:nr + npar + nct])
        add_refs = dict(zip(add_keys, refs[nr + npar + nct:nr + npar + nct + len(add_keys)]))
        o_refs = refs[nr + npar + nct + len(add_keys):]
        res, vjp = jax.vjp(functools.partial(f, row0), *vals)
        ct_vals = []
        for grp, r in zip(ct_groups, res, strict=True):
            ct_vals.append(sum(ct_refs.pop(0)[...].astype(r.dtype) for _ in grp))
        grads = vjp(tuple(ct_vals))
        for o_ref, ri in zip(o_refs[:len(row_grads)], row_grads):
            g = grads[ri]
            if ri in add_refs:
                g = g + add_refs[ri][...]
            o_ref[...] = g.astype(o_ref.dtype)

        @pl.when(i == 0)
        def _():
            for o_ref in o_refs[len(row_grads):]:
                o_ref[...] = jnp.zeros_like(o_ref)

        for o_ref, pi in zip(o_refs[len(row_grads):], param_grads):
            o_ref[...] += grads[nr + pi].astype(F32)

    in_specs = _rw_specs(rows, params, ts)
    in_specs += [pl.BlockSpec((ts, c.shape[1]), lambda i: (i, 0)) for c in cts]
    in_specs += [pl.BlockSpec((ts, add_rows[k].shape[1]), lambda i: (i, 0)) for k in add_keys]
    out_specs = [pl.BlockSpec((ts, rows[ri][1]), lambda i: (i, 0)) for ri in row_grads]
    out_specs += [pl.BlockSpec(params[pi].shape, functools.partial(lambda i, nd: (0,) * nd, nd=params[pi].ndim)) for pi in param_grads]
    out_shape = [jax.ShapeDtypeStruct((s, rows[ri][1]), F32) for ri in row_grads]
    out_shape += [jax.ShapeDtypeStruct(params[pi].shape, F32) for pi in param_grads]
    out = pl.pallas_call(
        body, grid=(s // ts,), in_specs=in_specs, out_specs=out_specs, out_shape=out_shape,
        compiler_params=_cparams("arbitrary"), name=name)(*[r[0] for r in rows], *params, *cts, *[add_rows[k] for k in add_keys])
    return tuple(out[:len(row_grads)]), tuple(out[len(row_grads):])


@functools.partial(jax.custom_vjp, nondiff_argnums=(1,))
def _shift_down(x, k):
    rows = lax.broadcasted_iota(jnp.int32, x.shape, 0)
    return jnp.where(rows < k, 0.0, pltpu.roll(x, k, 0))


def _shift_down_fwd(x, k):
    return _shift_down(x, k), None


def _shift_down_bwd(k, _, g):
    n = g.shape[0]
    rows = lax.broadcasted_iota(jnp.int32, g.shape, 0)
    return (jnp.where(rows >= n - k, 0.0, pltpu.roll(g, n - k, 0)),)


_shift_down.defvjp(_shift_down_fwd, _shift_down_bwd)


def _ct_fwd(f, x, xmap, ntiles, params, out_width, *, name, ct=LANES):
    s = x.shape[0]

    def body(*refs):
        refs[-1][...] = f(*[r[...] for r in refs[:-1]])

    in_specs = [pl.BlockSpec((s, ct), lambda j: (0, xmap(j)))]
    in_specs += [pl.BlockSpec((p.shape[0], ct), functools.partial(lambda j, o: (0, o + j), o=o)) for (p, o) in params]
    return pl.pallas_call(
        body, grid=(ntiles,), in_specs=in_specs, out_specs=pl.BlockSpec((s, ct), lambda j: (0, j)),
        out_shape=jax.ShapeDtypeStruct((s, out_width), F32), compiler_params=_cparams("parallel"), name=name)(x, *[p for p, _ in params])


def _ct_bwd(f, x, xmap, ntiles, params, g, *, name, ct=LANES):
    s = x.shape[0]
    npar, ng = len(params), len(g)

    def body(*refs):
        vals = [r[...] for r in refs[:1 + npar]]
        _, vjp = jax.vjp(f, *vals)
        grads = vjp(sum(r[...] for r in refs[1 + npar:1 + npar + ng]))
        for o_ref, gr in zip(refs[1 + npar + ng:], grads, strict=True):
            o_ref[...] = gr

    in_specs = [pl.BlockSpec((s, ct), lambda j: (0, xmap(j)))]
    pspecs = [pl.BlockSpec((p.shape[0], ct), functools.partial(lambda j, o: (0, o + j), o=o)) for (p, o) in params]
    in_specs += pspecs + [pl.BlockSpec((s, ct), lambda j: (0, j))] * ng
    out = pl.pallas_call(
        body, grid=(ntiles,), in_specs=in_specs, out_specs=[pl.BlockSpec((s, ct), lambda j: (0, j))] + pspecs,
        out_shape=[jax.ShapeDtypeStruct((s, ntiles * ct), F32)] + [jax.ShapeDtypeStruct(p.shape, F32) for p, _ in params],
        compiler_params=_cparams("parallel"), name=name)(x, *[p for p, _ in params], *g)
    return out[0], tuple(out[1:])


def _block_vjp(f, args, cts, *, name):
    outs = jax.eval_shape(f, *args)
    if cts is None:
        def body(*refs):
            for o_ref, v in zip(refs[len(args):], f(*[r[...] for r in refs[:len(args)]]), strict=True):
                o_ref[...] = v
        return tuple(pl.pallas_call(body, out_shape=[jax.ShapeDtypeStruct(o.shape, o.dtype) for o in outs], name=name)(*args))

    def body(*refs):
        n = len(args)
        _, vjp = jax.vjp(f, *[r[...] for r in refs[:n]])
        for o_ref, gr in zip(refs[n + len(cts):], vjp(tuple(r[...] for r in refs[n:n + len(cts)])), strict=True):
            o_ref[...] = gr
    return tuple(pl.pallas_call(body, out_shape=[jax.ShapeDtypeStruct(a.shape, a.dtype) for a in args], name=name)(*args, *cts))


def _softplus(x):
    return jnp.maximum(x, 0.0) + jnp.log(1.0 + jnp.exp(-jnp.abs(x)))


def _dot_bf16(a, b):
    return jnp.dot(a.astype(BF16), b.astype(BF16), preferred_element_type=F32)


def _dot3(x, m):
    hi = x.astype(BF16)
    r1 = x - hi.astype(F32)
    mid = r1.astype(BF16)
    lo = (r1 - mid.astype(F32)).astype(BF16)
    return (jnp.dot(hi, m, preferred_element_type=F32) + jnp.dot(mid, m, preferred_element_type=F32)
            + jnp.dot(lo, m, preferred_element_type=F32))


@jax.custom_vjp
def _lin(x, m, mt):
    return _dot3(x, m)


def _lin_fwd(x, m, mt):
    return _dot3(x, m), (m, mt)


def _lin_bwd(res, g):
    m, mt = res
    return _dot3(g, mt), jnp.zeros_like(m), jnp.zeros_like(mt)


_lin.defvjp(_lin_fwd, _lin_bwd)


def _head_ones(n, head=HEAD):
    i = jnp.arange(n) // head
    return (i[:, None] == i[None, :]).astype(BF16)


def _rmsnorm_tile(_, h, g):
    return (h * lax.rsqrt(jnp.mean(h * h, axis=-1, keepdims=True) + NORM_EPS) * g,)


def _cscan_fwd(bu, a, *, name, ct=256):
    s, c2 = bu.shape
    c = c2 // 2
    nt = c // ct

    def body(br_ref, bi_ref, ar_ref, ai_ref, xr_ref, xi_ref):
        ar, ai = ar_ref[...], ai_ref[...]

        def step(t, carry):
            hr, hi = carry
            nr = ar * hr - ai * hi + br_ref[pl.ds(t, 1), :]
            ni = ar * hi + ai * hr + bi_ref[pl.ds(t, 1), :]
            xr_ref[pl.ds(t, 1), :] = nr
            xi_ref[pl.ds(t, 1), :] = ni
            return nr, ni

        z = jnp.zeros((1, ct), F32)
        lax.fori_loop(0, s, step, (z, z))

    re = lambda j: (0, j)
    im = lambda j: (0, nt + j)
    xr, xi = pl.pallas_call(
        body, grid=(nt,),
        in_specs=[pl.BlockSpec((s, ct), re), pl.BlockSpec((s, ct), im), pl.BlockSpec((1, ct), re), pl.BlockSpec((1, ct), im)],
        out_specs=[pl.BlockSpec((s, ct), re), pl.BlockSpec((s, ct), re)],
        out_shape=[jax.ShapeDtypeStruct((s, c), F32)] * 2, compiler_params=_cparams("parallel"), name=name)(bu, bu, a, a)
    return xr, xi


def _cscan_bwd(gr, gi, xr, xi, a, *, name, ct=256):
    s, c = gr.shape
    nt = c // ct

    def body(gr_ref, gi_ref, xr_ref, xi_ref, ar_ref, ai_ref, dr_ref, di_ref, dar_ref, dai_ref):
        ar, ai = ar_ref[...], ai_ref[...]

        def step(i, carry):
            dr, di, accr, acci = carry
            t = s - 1 - i
            nr = gr_ref[pl.ds(t, 1), :] + ar * dr + ai * di
            ni = gi_ref[pl.ds(t, 1), :] + ar * di - ai * dr
            dr_ref[pl.ds(t, 1), :] = nr
            di_ref[pl.ds(t, 1), :] = ni
            tp = jnp.maximum(t - 1, 0)
            live = (t > 0).astype(F32)
            pr = xr_ref[pl.ds(tp, 1), :] * live
            pi = xi_ref[pl.ds(tp, 1), :] * live
            return nr, ni, accr + nr * pr + ni * pi, acci + ni * pr - nr * pi

        z = jnp.zeros((1, ct), F32)
        _, _, accr, acci = lax.fori_loop(0, s, step, (z, z, z, z))
        dar_ref[...] = accr
        dai_ref[...] = acci

    re = lambda j: (0, j)
    im = lambda j: (0, nt + j)
    blk = pl.BlockSpec((s, ct), re)
    dr, di, dar, dai = pl.pallas_call(
        body, grid=(nt,),
        in_specs=[blk, blk, blk, blk, pl.BlockSpec((1, ct), re), pl.BlockSpec((1, ct), im)],
        out_specs=[blk, blk, pl.BlockSpec((1, ct), re), pl.BlockSpec((1, ct), re)],
        out_shape=[jax.ShapeDtypeStruct((s, c), F32)] * 2 + [jax.ShapeDtypeStruct((1, c), F32)] * 2,
        compiler_params=_cparams("parallel"), name=name)(gr, gi, xr, xi, a, a)
    return dr, di, dar, dai


def _rscan_fwd(a, b, *, name, ct=256):
    s, c = a.shape

    def body(a_ref, b_ref, h_ref):
        def step(t, h):
            h = a_ref[pl.ds(t, 1), :] * h + b_ref[pl.ds(t, 1), :]
            h_ref[pl.ds(t, 1), :] = h
            return h
        lax.fori_loop(0, s, step, jnp.zeros((1, ct), F32))

    blk = pl.BlockSpec((s, ct), lambda j: (0, j))
    return pl.pallas_call(body, grid=(c // ct,), in_specs=[blk, blk], out_specs=blk,
                          out_shape=jax.ShapeDtypeStruct((s, c), F32), compiler_params=_cparams("parallel"), name=name)(a, b)


def _rscan_bwd(g, a, h, *, name, ct=256):
    s, c = a.shape

    def body(g_ref, a_ref, h_ref, db_ref, da_ref):
        def step(i, carry):
            d, an = carry
            t = s - 1 - i
            d = g_ref[pl.ds(t, 1), :] + an * d
            db_ref[pl.ds(t, 1), :] = d
            tp = jnp.maximum(t - 1, 0)
            da_ref[pl.ds(t, 1), :] = d * h_ref[pl.ds(tp, 1), :] * (t > 0).astype(F32)
            return d, a_ref[pl.ds(t, 1), :]
        z = jnp.zeros((1, ct), F32)
        lax.fori_loop(0, s, step, (z, z))

    blk = pl.BlockSpec((s, ct), lambda j: (0, j))
    db, da = pl.pallas_call(body, grid=(c // ct,), in_specs=[blk, blk, blk], out_specs=[blk, blk],
                            out_shape=[jax.ShapeDtypeStruct((s, c), F32)] * 2, compiler_params=_cparams("parallel"), name=name)(g, a, h)
    return db, da


def _dot3l(m, x):
    hi = x.astype(BF16)
    r1 = x - hi.astype(F32)
    mid = r1.astype(BF16)
    lo = (r1 - mid.astype(F32)).astype(BF16)
    return (jnp.dot(m, hi, preferred_element_type=F32) + jnp.dot(m, mid, preferred_element_type=F32)
            + jnp.dot(m, lo, preferred_element_type=F32))


@jax.custom_vjp
def _linl(x, m, mt):
    return _dot3l(m, x)


def _linl_fwd(x, m, mt):
    return _dot3l(m, x), (m, mt)


def _linl_bwd(res, g):
    m, mt = res
    return _dot3l(mt, g), jnp.zeros_like(m), jnp.zeros_like(mt)


_linl.defvjp(_linl_fwd, _linl_bwd)


def _ssd_chunk(g, xs, bm, cm, z, dtraw, dt_bias, a_log, dskip, ng, st0, st1, st2):
    n = xs.shape[0]
    lane = lax.broadcasted_iota(jnp.int32, (1, LANES), 1)
    sub = lax.broadcasted_iota(jnp.int32, (LANES, 1), 0)
    row = lax.broadcasted_iota(jnp.int32, (n, n), 0)
    col = lax.broadcasted_iota(jnp.int32, (n, n), 1)
    tril = row >= col
    tril_m = tril.astype(BF16)
    triu_m = (row <= col).astype(BF16)
    lane_lo = lane < HEAD
    sub_lo = sub < HEAD
    dt = _softplus(dtraw + dt_bias)
    da = dt * (-jnp.exp(a_log))
    acum = _linl(da, tril_m, triu_m)
    acum_t = acum.T
    scores = lax.dot_general(cm.astype(BF16), bm.astype(BF16), (((1,), (1,)), ((), ())), preferred_element_type=F32)

    def head(h):
        sel = lane == h
        acol = jnp.sum(jnp.where(sel, acum, 0.0), axis=1, keepdims=True)
        arow = jnp.sum(jnp.where(sub == h, acum_t, 0.0), axis=0, keepdims=True)
        dtcol = jnp.sum(jnp.where(sel, dt, 0.0), axis=1, keepdims=True)
        dsk = jnp.sum(jnp.where(sel, dskip, 0.0), axis=1, keepdims=True)
        decay = jnp.exp(jnp.where(tril, acol - arow, -jnp.inf))
        alast = acol[n - 1:n, :]
        return acol, dtcol, dsk, decay, alast

    ys, new = [], []
    for q, st in enumerate((st0, st1, st2)):
        a_acol, a_dt, a_dsk, a_decay, a_last = head(g * 6 + 2 * q)
        b_acol, b_dt, b_dsk, b_decay, b_last = head(g * 6 + 2 * q + 1)
        xp = xs[:, q * LANES:(q + 1) * LANES]
        xdt = xp * jnp.where(lane_lo, a_dt, b_dt)
        yd = jnp.where(lane_lo, _dot_bf16(scores * a_decay, xdt), _dot_bf16(scores * b_decay, xdt))
        xw = xdt * jnp.where(lane_lo, jnp.exp(a_last - a_acol), jnp.exp(b_last - b_acol))
        states = lax.dot_general(xw.astype(BF16), bm.astype(BF16), (((0,), (0,)), ((), ())), preferred_element_type=F32)
        yo = lax.dot_general(cm.astype(BF16), st.astype(BF16), (((1,), (1,)), ((), ())), preferred_element_type=F32)
        yo = yo * jnp.where(lane_lo, jnp.exp(a_acol), jnp.exp(b_acol))
        new.append(st * jnp.where(sub_lo, jnp.exp(a_last), jnp.exp(b_last)) + states)
        ys.append(yd + yo + xp * jnp.where(lane_lo, a_dsk, b_dsk))
    y = jnp.concatenate(ys, axis=1)
    y = y * (z * jax.nn.sigmoid(z))
    y = y * lax.rsqrt(jnp.mean(y * y, axis=-1, keepdims=True) + NORM_EPS) * ng
    return y, new[0], new[1], new[2]


def _ssd_specs(nc, rev):
    cidx = (lambda c: nc - 1 - c) if rev else (lambda c: c)
    gw = 3 * LANES
    return [
        pl.BlockSpec((SSD_CHUNK, gw), lambda c, g: (cidx(c), g)),
        pl.BlockSpec((SSD_CHUNK, LANES), lambda c, g: (cidx(c), 12 + g)),
        pl.BlockSpec((SSD_CHUNK, LANES), lambda c, g: (cidx(c), 16 + g)),
        pl.BlockSpec((SSD_CHUNK, gw), lambda c, g: (cidx(c), g)),
        pl.BlockSpec((SSD_CHUNK, LANES), lambda c, g: (cidx(c), 36)),
        pl.BlockSpec((1, LANES), lambda c, g: (0, 0)),
        pl.BlockSpec((1, LANES), lambda c, g: (0, 0)),
        pl.BlockSpec((1, LANES), lambda c, g: (0, 0)),
        pl.BlockSpec((1, gw), lambda c, g: (0, g)),
    ], cidx


def _ssd_fwd(conv, proj, dt_bias, a_log, dskip, norm_g, *, name):
    s = conv.shape[0]
    nc = s // SSD_CHUNK
    in_specs, _ = _ssd_specs(nc, False)

    def body(xs, bm, cm, z, dtr, dtb, alog, dsk, ng, y_ref, sv_ref, st):
        c, g = pl.program_id(0), pl.program_id(1)

        @pl.when(c == 0)
        def _():
            for q in range(3):
                st[g * 3 + q] = jnp.zeros((LANES, LANES), F32)

        olds = [st[g * 3 + q] for q in range(3)]
        for q in range(3):
            sv_ref[0, 0, q] = olds[q]
        y, n0, n1, n2 = _ssd_chunk(g, xs[...], bm[...], cm[...], z[...], dtr[...], dtb[...], alog[...], dsk[...], ng[...], *olds)
        y_ref[...] = y
        for q, v in enumerate((n0, n1, n2)):
            st[g * 3 + q] = v

    return pl.pallas_call(
        body, grid=(nc, 4), in_specs=in_specs,
        out_specs=[pl.BlockSpec((SSD_CHUNK, 3 * LANES), lambda c, g: (c, g)),
                   pl.BlockSpec((1, 1, 3, LANES, LANES), lambda c, g: (c, g, 0, 0, 0))],
        out_shape=[jax.ShapeDtypeStruct((s, 12 * LANES), F32), jax.ShapeDtypeStruct((nc, 4, 3, LANES, LANES), F32)],
        scratch_shapes=[pltpu.VMEM((12, LANES, LANES), F32)],
        compiler_params=_cparams("arbitrary", "arbitrary"), name=name)(conv, conv, conv, proj, proj, dt_bias, a_log, dskip, norm_g)


def _ssd_bwd(conv, proj, dt_bias, a_log, dskip, norm_g, saved, dy, *, name):
    s = conv.shape[0]
    nc = s // SSD_CHUNK
    in_specs, cidx = _ssd_specs(nc, True)
    gw = 3 * LANES
    in_specs += [pl.BlockSpec((1, 1, 3, LANES, LANES), lambda c, g: (cidx(c), g, 0, 0, 0)),
                 pl.BlockSpec((SSD_CHUNK, gw), lambda c, g: (cidx(c), g))]

    def body(xs, bm, cm, z, dtr, dtb, alog, dsk, ng, sv, dy_ref, dxs, dbm, dcm, dz, ddt, ddtb, dalog, ddsk, dng, dst):
        c, g = pl.program_id(0), pl.program_id(1)

        @pl.when(c == 0)
        def _():
            for q in range(3):
                dst[g * 3 + q] = jnp.zeros((LANES, LANES), F32)

        @pl.when((c == 0) & (g == 0))
        def _():
            ddtb[...] = jnp.zeros_like(ddtb)
            dalog[...] = jnp.zeros_like(dalog)
            ddsk[...] = jnp.zeros_like(ddsk)
            dng[...] = jnp.zeros_like(dng)

        @pl.when(g == 0)
        def _():
            ddt[...] = jnp.zeros_like(ddt)

        olds = [sv[0, 0, q] for q in range(3)]
        _, vjp = jax.vjp(functools.partial(_ssd_chunk, g), xs[...], bm[...], cm[...], z[...], dtr[...], dtb[...], alog[...],
                         dsk[...], ng[...], *olds)
        gr = vjp((dy_ref[...], dst[g * 3], dst[g * 3 + 1], dst[g * 3 + 2]))
        dxs[...], dbm[...], dcm[...], dz[...] = gr[0], gr[1], gr[2], gr[3]
        ddt[...] += gr[4]
        ddtb[...] += gr[5]
        dalog[...] += gr[6]
        ddsk[...] += gr[7]
        dng[g] += gr[8]
        for q in range(3):
            dst[g * 3 + q] = gr[9 + q]

    const = lambda shape: pl.BlockSpec(shape, lambda c, g: (0,) * len(shape))
    return pl.pallas_call(
        body, grid=(nc, 4), in_specs=in_specs,
        out_specs=[pl.BlockSpec((SSD_CHUNK, gw), lambda c, g: (cidx(c), g)),
                   pl.BlockSpec((SSD_CHUNK, LANES), lambda c, g: (cidx(c), g)),
                   pl.BlockSpec((SSD_CHUNK, LANES), lambda c, g: (cidx(c), g)),
                   pl.BlockSpec((SSD_CHUNK, gw), lambda c, g: (cidx(c), g)),
                   pl.BlockSpec((SSD_CHUNK, LANES), lambda c, g: (cidx(c), 0)),
                   const((1, LANES)), const((1, LANES)), const((1, LANES)), const((4, 1, gw))],
        out_shape=[jax.ShapeDtypeStruct((s, 12 * LANES), F32), jax.ShapeDtypeStruct((s, 4 * LANES), F32),
                   jax.ShapeDtypeStruct((s, 4 * LANES), F32), jax.ShapeDtypeStruct((s, 12 * LANES), F32),
                   jax.ShapeDtypeStruct((s, LANES), F32), jax.ShapeDtypeStruct((1, LANES), F32),
                   jax.ShapeDtypeStruct((1, LANES), F32), jax.ShapeDtypeStruct((1, LANES), F32),
                   jax.ShapeDtypeStruct((4, 1, gw), F32)],
        scratch_shapes=[pltpu.VMEM((12, LANES, LANES), F32)],
        compiler_params=_cparams("arbitrary", "arbitrary"), name=name)(
            conv, conv, conv, proj, proj, dt_bias, a_log, dskip, norm_g, saved, dy)


MXU_TILE = 256
RWKV_GROUP = 8


def _rwkv_consts():
    lanes = 16 * HEAD
    hl = jnp.arange(lanes) // HEAD
    e = (jnp.arange(16)[:, None] == hl[None, :]).astype(BF16)
    return jnp.tile(e, (3, 1)), e.T, _head_ones(MXU_TILE)


def _head_sums(x, j):
    n = x.shape[0]
    x4 = jnp.concatenate([x[:, i * MXU_TILE:(i + 1) * MXU_TILE] for i in range(4)], axis=0)
    hi = x4.astype(BF16)
    mid = (x4 - hi.astype(F32)).astype(BF16)
    out = jnp.dot(jnp.concatenate([hi, mid], axis=0), j, preferred_element_type=F32)
    s4 = out[:4 * n] + out[4 * n:]
    return jnp.concatenate([s4[i * n:(i + 1) * n] for i in range(4)], axis=1)


def _fold8(x):
    return jnp.sum(x.reshape(x.shape[0] // 8, 8, x.shape[1]), axis=0)


def _split3(x):
    hi = x.astype(BF16)
    r1 = x - hi.astype(F32)
    mid = r1.astype(BF16)
    return jnp.concatenate([hi, mid, (r1 - mid.astype(F32)).astype(BF16)], axis=-1)


def _rwkv_expand(src3, dst, e3, t_):
    for g0 in range(0, t_, RWKV_GROUP):
        n = min(RWKV_GROUP, t_ - g0)
        flat = src3[g0:g0 + n].reshape(n * HEAD, e3.shape[0])
        dst[g0:g0 + n] = jnp.dot(flat, e3, preferred_element_type=F32).reshape(n, HEAD, e3.shape[1])


def _rwkv_reduce(src, dst3, et, t_):
    for g0 in range(0, t_, RWKV_GROUP):
        n = min(RWKV_GROUP, t_ - g0)
        x = src[g0:g0 + n].reshape(n * HEAD, et.shape[0])
        hi = x.astype(BF16)
        mid = (x - hi.astype(F32)).astype(BF16)
        out = jnp.dot(hi, et, preferred_element_type=F32) + jnp.dot(mid, et, preferred_element_type=F32)
        dst3[g0:g0 + n] = out.reshape(n, HEAD, 16)


def _rwkv_fwd(w, kk, b, k, fp, v3, *, name):
    s, lanes = w.shape
    t_ = min(RWKV_CHUNK, s)
    nc = s // t_
    e, et, j = _rwkv_consts()
    rowspec = pl.BlockSpec((t_, lanes), lambda c: (c, 0))
    cspec = lambda a: pl.BlockSpec(a.shape, lambda c: (0, 0))

    def body(w_ref, kk_ref, b_ref, k_ref, r_ref, v_ref, e_ref, et_ref, j_ref, y_ref, sv_ref, st, vm, zz):
        c = pl.program_id(0)

        @pl.when(c == 0)
        def _():
            st[...] = jnp.zeros_like(st)

        sv_ref[0] = st[...]
        jv = j_ref[...]
        _rwkv_expand(v_ref, vm, e_ref[...], t_)

        def step(t, sm):
            row = lambda ref: ref[pl.ds(t, 1), :]
            sa = _head_sums(sm * (-row(kk_ref)), jv)
            sn = sm * row(w_ref) + sa * row(b_ref) + vm[t] * row(k_ref)
            zz[t] = sn * row(r_ref)
            return sn

        st[...] = lax.fori_loop(0, t_, step, st[...])
        _rwkv_reduce(zz, y_ref, et_ref[...], t_)

    return pl.pallas_call(
        body, grid=(nc,),
        in_specs=[rowspec] * 5 + [pl.BlockSpec((t_, HEAD, 48), lambda c: (c, 0, 0)), cspec(e), cspec(et), cspec(j)],
        out_specs=[pl.BlockSpec((t_, HEAD, 16), lambda c: (c, 0, 0)), pl.BlockSpec((1, HEAD, lanes), lambda c: (c, 0, 0))],
        out_shape=[jax.ShapeDtypeStruct((s, HEAD, 16), F32), jax.ShapeDtypeStruct((nc, HEAD, lanes), F32)],
        scratch_shapes=[pltpu.VMEM((HEAD, lanes), F32), pltpu.VMEM((t_, HEAD, lanes), F32), pltpu.VMEM((t_, HEAD, lanes), F32)],
        compiler_params=_cparams("arbitrary"), name=name)(w, kk, b, k, fp, v3, e, et, j)


def _rwkv_bwd(w, kk, b, k, fp, v3, saved, dy3, *, name):
    s, lanes = w.shape
    t_ = min(RWKV_CHUNK, s)
    nc = s // t_
    e, et, j = _rwkv_consts()
    rev = lambda c: nc - 1 - c
    rowspec = pl.BlockSpec((t_, lanes), lambda c: (rev(c), 0))
    v3spec = pl.BlockSpec((t_, HEAD, 16), lambda c: (rev(c), 0, 0))
    s3spec = pl.BlockSpec((t_, HEAD, 48), lambda c: (rev(c), 0, 0))
    cspec = lambda a: pl.BlockSpec(a.shape, lambda c: (0, 0))

    def body(w_ref, kk_ref, b_ref, k_ref, r_ref, v_ref, sv_ref, dy_ref, e_ref, et_ref, j_ref,
             dw_ref, dkk_ref, db_ref, dk_ref, dr_ref, dv_ref, dst, h_sm, h_sa, vm, dz, pw, pkk, pb, pk, pr):
        c = pl.program_id(0)

        @pl.when(c == 0)
        def _():
            dst[...] = jnp.zeros_like(dst)

        jv = j_ref[...]
        row = lambda ref, t: ref[pl.ds(t, 1), :]
        _rwkv_expand(v_ref, vm, e_ref[...], t_)
        _rwkv_expand(dy_ref, dz, e_ref[...], t_)

        def replay(t, sm):
            h_sm[t] = sm
            sa = _head_sums(sm * (-row(kk_ref, t)), jv)
            h_sa[t] = sa
            return sm * row(w_ref, t) + sa * row(b_ref, t) + vm[t] * row(k_ref, t)

        h_sm[t_] = lax.fori_loop(0, t_, replay, sv_ref[0])

        def back(i, dcarry):
            t = t_ - 1 - i
            sm, sa, dzt = h_sm[t], h_sa[t], dz[t]
            dsn = dcarry + dzt * row(r_ref, t)
            pr[t] = _fold8(dzt * h_sm[t + 1])
            pw[t] = _fold8(dsn * sm)
            pb[t] = _fold8(dsn * sa)
            pk[t] = _fold8(dsn * vm[t])
            vm[t] = dsn * row(k_ref, t)
            dx = _head_sums(dsn * row(b_ref, t), jv)
            pkk[t] = _fold8(dx * sm)
            return dsn * row(w_ref, t) - dx * row(kk_ref, t)

        dst[...] = lax.fori_loop(0, t_, back, dst[...])
        _rwkv_reduce(vm, dv_ref, et_ref[...], t_)
        dw_ref[...] = jnp.sum(pw[...], axis=1)
        dkk_ref[...] = -jnp.sum(pkk[...], axis=1)
        db_ref[...] = jnp.sum(pb[...], axis=1)
        dk_ref[...] = jnp.sum(pk[...], axis=1)
        dr_ref[...] = jnp.sum(pr[...], axis=1)

    big = lambda n: pltpu.VMEM((n, HEAD, lanes), F32)
    part = pltpu.VMEM((t_, 8, lanes), F32)
    return pl.pallas_call(
        body, grid=(nc,),
        in_specs=[rowspec] * 5 + [s3spec, pl.BlockSpec((1, HEAD, lanes), lambda c: (rev(c), 0, 0)), s3spec, cspec(e), cspec(et), cspec(j)],
        out_specs=[rowspec] * 5 + [v3spec],
        out_shape=[jax.ShapeDtypeStruct((s, lanes), F32)] * 5 + [jax.ShapeDtypeStruct((s, HEAD, 16), F32)],
        scratch_shapes=[pltpu.VMEM((HEAD, lanes), F32), big(t_ + 1), big(t_), big(t_), big(t_)] + [part] * 5,
        compiler_params=_cparams("arbitrary"), name=name)(w, kk, b, k, fp, v3, saved, dy3, e, et, j)


def _blockdiag(blocks):
    g, _, b = blocks.shape
    return jnp.concatenate([jnp.pad(blocks[i], ((0, 0), (i * b, (g - 1 - i) * b))) for i in range(g)], axis=0)


def _blockdiag_t(dense, g):
    a, b = dense.shape[0] // g, dense.shape[1] // g
    return jnp.stack([dense[i * a:(i + 1) * a, i * b:(i + 1) * b] for i in range(g)])


def _pad_cols(x, n):
    return jnp.pad(x, ((0, 0), (0, n - x.shape[1])))


def _pad_rows(x, n):
    return jnp.pad(x, ((0, n - x.shape[0]), (0, 0)))


E_PROJ = 5120


def _even_in_cols(w):
    return jnp.concatenate([w[:, 512:2048], w[:, 0:512], w[:, 2048:4632], jnp.zeros((w.shape[0], E_PROJ - 4632), w.dtype)], axis=1)


def _even_in_cols_t(dw):
    return jnp.concatenate([dw[:, 1536:2048], dw[:, 0:1536], dw[:, 2048:4632]], axis=1)


O_PROJ = 5632


def _odd_in_cols(w):
    z32 = jnp.zeros((w.shape[0], 32), w.dtype)
    return jnp.concatenate([w[:, 0:3072], w[:, 3520:5568], w[:, 3264:3520], w[:, 3072:3168], z32, w[:, 3168:3264], z32], axis=1)


def _odd_in_cols_t(dw):
    return jnp.concatenate([dw[:, 0:3072], dw[:, 5376:5472], dw[:, 5504:5600], dw[:, 5120:5376], dw[:, 3072:5120]], axis=1)


def _mu_cols(mu):
    z32 = jnp.zeros((1, 32), mu.dtype)
    return jnp.concatenate([mu[:, 0:3072], mu[:, 3264:3520], mu[:, 3072:3168], z32, mu[:, 3168:3264], z32], axis=1)


def _mu_cols_t(d):
    return jnp.concatenate([d[:, 0:3072], d[:, 3328:3424], d[:, 3456:3552], d[:, 3072:3328]], axis=1)


def _conv_taps(x, w, b):
    y = b + w[3:4] * x
    for k in range(3):
        y = y + w[k:k + 1] * _shift_down(x, 3 - k)
    return y


def _conv_silu(x, w, b):
    y = _conv_taps(x, w, b)
    return y * jax.nn.sigmoid(y)


def _tshift(x, mu):
    return x + (_shift_down(x, 1) - x) * mu


def _relu2(_, a):
    r = jnp.maximum(a, 0.0)
    return (r * r,)


def _pl_gate(_, h, gl, e):
    return (h + jax.nn.sigmoid(gl) * e,)


def _s5_param(lr, li, ls, br, bi):
    step = jnp.exp(ls)
    mag = jnp.exp(lr * step)
    ar, ai = mag * jnp.cos(li * step), mag * jnp.sin(li * step)
    den = lr * lr + li * li
    nr = ar - 1.0
    cr = (nr * lr + ai * li) / den
    ci = (ai * lr - nr * li) / den
    return ar, ai, cr * br - ci * bi, cr * bi + ci * br


def _s5_post(_, ylin, u, d, gw, gb):
    act = jax.nn.gelu(ylin + d * u)
    return (act * jax.nn.sigmoid(_dot_bf16(act, gw) + gb),)


def _rwkv_pre(_, k, gl, wl, al, w0, w_up, a0, a_up, g_up, k_k, k_a, j):
    w = -_softplus(-(w0 + _dot_bf16(jnp.tanh(wl), w_up))) - 0.5
    decay = jnp.exp(-jnp.exp(w))
    a = jax.nn.sigmoid(a0 + _dot_bf16(al, a_up))
    g = _dot_bf16(jax.nn.sigmoid(gl), g_up)
    kk = k * k_k
    k2 = k * (1.0 + (a - 1.0) * k_a)
    kkn = kk * lax.rsqrt(jnp.maximum(_lin(kk * kk, j, j), 1e-24))
    return decay, kkn, kkn * a, k2, g


def _rwkv_post(_, y, r, k2, v, g, r_k, ln_g, ln_b, j):
    mean = _lin(y, j, j) * (1.0 / HEAD)
    yc = y - mean
    var = _lin(yc * yc, j, j) * (1.0 / HEAD)
    yn = yc * lax.rsqrt(var + RWKV_GN_EPS) * ln_g + ln_b
    return ((yn + _lin(r * k2 * r_k, j, j) * v) * g,)


def _lru_pre(row0, pre, xc, bax, lam):
    n = xc.shape[1]
    gr = jax.nn.sigmoid(pre[:, :n] + bax[:, :n])
    gi = jax.nn.sigmoid(pre[:, n:] + bax[:, n:])
    log_a = -LRU_C * gr * _softplus(-lam)
    m2 = -jnp.tanh(log_a) * (jnp.exp(2.0 * log_a) + 1.0)
    mult = jnp.sqrt(jnp.maximum(m2, 0.0))
    rowid = row0 + lax.broadcasted_iota(jnp.int32, (xc.shape[0], 1), 0)
    mult = jnp.where(rowid == 0, 1.0, mult)
    return jnp.exp(log_a), xc * gi * mult


def _lru_post(_, h, gl2):
    return (h * jax.nn.gelu(gl2),)


def _even_prep(w):
    sp = (w["s5_lam_re"].reshape(32, 64), w["s5_lam_im"].reshape(32, 64), w["s5_log_step"].reshape(32, 1),
          w["s5_b_re"].reshape(32, 64, 16).transpose(2, 0, 1), w["s5_b_im"].reshape(32, 64, 16).transpose(2, 0, 1))
    ar, ai, bbr, bbi = _block_vjp(_s5_param, sp, None, name="s5_param")
    bblk = lambda bb: _blockdiag(bb.transpose(1, 0, 2))
    cblk = lambda c: _blockdiag(c.reshape(32, 16, 64).transpose(0, 2, 1))
    pad = lambda x: _pad_cols(x.reshape(1, 24), LANES)
    return dict(
        sp=sp, a_row=jnp.concatenate([ar.reshape(1, 2048), ai.reshape(1, 2048)], axis=1),
        b_re=bblk(bbr), b_im=bblk(bbi), c_re=cblk(w["s5_c_re"]), c_imn=-cblk(w["s5_c_im"]),
        d=w["s5_d"].reshape(1, 512), gw=w["s5_glu_w"].reshape(512, 512), gb=w["s5_glu_b"].reshape(1, 512),
        conv_w=w["ssd_conv_w"].reshape(4, 2560), conv_b=w["ssd_conv_b"].reshape(1, 2560),
        dt_bias=pad(w["ssd_dt_bias"]), a_log=pad(w["ssd_a_log"]), dskip=pad(w["ssd_d"]), norm=w["ssd_norm"].reshape(1, 1536))


_E_XMAP = lambda j: 16 + j


def _even_fwd(proj, p):
    u = proj[:, 1536:2048]
    bur = _matmul(u, p["b_re"], "nn", name="s5_bu_re")
    bui = _matmul(u, p["b_im"], "nn", name="s5_bu_im")
    xr, xi = _cscan_fwd(jnp.concatenate([bur, bui], axis=1), p["a_row"], name="s5_scan")
    ylin = _matmul(xi, p["c_imn"], "nn", name="s5_y_im", add=_matmul(xr, p["c_re"], "nn", name="s5_y_re"))
    (ya,) = _rw_fwd(_s5_post, [_row(ylin), _row(u)], [p["d"], p["gw"], p["gb"]], [(512, F32)], name="s5_post")
    conv = _ct_fwd(_conv_silu, proj, _E_XMAP, 20, [(p["conv_w"], 0), (p["conv_b"], 0)], 2560, name="ssd_conv")
    yb, saved = _ssd_fwd(conv, proj, p["dt_bias"], p["a_log"], p["dskip"], p["norm"], name="ssd_scan")
    return jnp.concatenate([ya, yb], axis=1), (u, xr, xi, ylin, conv, saved)


def _even_bwd(proj, p, res, dy):
    u, xr, xi, ylin, conv, saved = res
    s = proj.shape[0]
    dxs, dbm, dcm, dz, ddt, ddtb, dalog, ddsk, dng = _ssd_bwd(
        conv, proj, p["dt_bias"], p["a_log"], p["dskip"], p["norm"], saved, dy[:, 512:], name="ssd_scan_bwd")
    dxbc, (dcw, dcb) = _ct_bwd(_conv_silu, proj, _E_XMAP, 20, [(p["conv_w"], 0), (p["conv_b"], 0)],
                               [jnp.concatenate([dxs, dbm, dcm], axis=1)], name="ssd_conv_bwd")
    (dylin, du), (dd, dgw, dgb) = _rw_bwd(_s5_post, [_row(ylin), _row(u)], [p["d"], p["gw"], p["gb"]], [dy[:, :512]], name="s5_post_bwd")
    dxr = _matmul(dylin, p["c_re"], "nt", name="s5_dxr")
    dxi = _matmul(dylin, p["c_imn"], "nt", name="s5_dxi")
    dc_re = _matmul(xr, dylin, "tn", name="s5_dc_re")
    dc_imn = _matmul(xi, dylin, "tn", name="s5_dc_im")
    dbr, dbi, dar, dai = _cscan_bwd(dxr, dxi, xr, xi, p["a_row"], name="s5_scan_bwd")
    du = _matmul(dbr, p["b_re"], "nt", name="s5_du_re", add=du)
    du = _matmul(dbi, p["b_im"], "nt", name="s5_du_im", add=du)
    db_re = _matmul(u, dbr, "tn", name="s5_db_re")
    db_im = _matmul(u, dbi, "tn", name="s5_db_im")
    unblk = lambda d: _blockdiag_t(d, 32).transpose(1, 0, 2)
    g_sp = _block_vjp(_s5_param, p["sp"], (dar.reshape(32, 64), dai.reshape(32, 64), unblk(db_re), unblk(db_im)), name="s5_param_bwd")
    dproj = jnp.concatenate([dz, du, dxbc, ddt, jnp.zeros((s, E_PROJ - 4736), F32)], axis=1)
    uncblk = lambda d: _blockdiag_t(d, 32).transpose(0, 2, 1)
    grads = dict(
        s5_lam_re=g_sp[0].reshape(1, 32, 64), s5_lam_im=g_sp[1].reshape(1, 32, 64), s5_log_step=g_sp[2].reshape(1, 32),
        s5_b_re=g_sp[3].transpose(1, 2, 0)[None], s5_b_im=g_sp[4].transpose(1, 2, 0)[None],
        s5_c_re=uncblk(dc_re)[None], s5_c_im=-uncblk(dc_imn)[None], s5_d=dd, s5_glu_w=dgw[None], s5_glu_b=dgb,
        ssd_conv_w=dcw[None], ssd_conv_b=dcb, ssd_dt_bias=ddtb[:, :24], ssd_a_log=dalog[:, :24], ssd_d=ddsk[:, :24],
        ssd_norm=dng.reshape(1, 1536))
    return dproj, grads


def _odd_prep(w):
    pad128 = lambda x: _pad_rows(x, LANES)
    return dict(
        mu=_mu_cols(w["rwkv_mu"].reshape(1, 3520)), w0=w["rwkv_w0"].reshape(1, 1024), w_up=pad128(w["rwkv_w_up"].reshape(96, 1024)),
        a0=w["rwkv_a0"].reshape(1, 1024), a_up=pad128(w["rwkv_a_up"].reshape(96, 1024)), g_up=w["rwkv_g_up"].reshape(256, 1024),
        k_k=w["rwkv_k_k"].reshape(1, 1024), k_a=w["rwkv_k_a"].reshape(1, 1024), r_k=w["rwkv_r_k"].reshape(1, 1024),
        ln_g=w["rwkv_ln_g"].reshape(1, 1024), ln_b=w["rwkv_ln_b"].reshape(1, 1024), j=_head_ones(1024),
        conv_w=w["lru_conv_w"].reshape(4, 1024), conv_b=w["lru_conv_b"].reshape(1, 1024),
        wax=jnp.concatenate([_blockdiag(w["lru_w_a"].reshape(16, 64, 64)), _blockdiag(w["lru_w_x"].reshape(16, 64, 64))], axis=1),
        bax=jnp.concatenate([w["lru_b_a"].reshape(1, 1024), w["lru_b_x"].reshape(1, 1024)], axis=1), lam=w["lru_lam"].reshape(1, 1024))


_O_XMAP = lambda j: jnp.where(j < 24, j, j + 16)
_O_LMAP = lambda j: 24 + j


def _to_heads(x):
    return x.reshape(x.shape[0], 16, HEAD).transpose(0, 2, 1)


def _from_heads(x3):
    return x3.transpose(0, 2, 1).reshape(x3.shape[0], 16 * HEAD)


def _odd_rows(fp, y, k2, g):
    pre = [_row(fp, 1024, 1), _row(fp, 256, 12), _row(fp, 128, 26), _row(fp, 128, 27)]
    post = None if y is None else [_row(y), _row(fp, 1024, 0), _row(k2), _row(fp, 1024, 2), _row(g)]
    return pre, post


def _odd_fwd(proj, p):
    fp = _ct_fwd(_tshift, proj, _O_XMAP, 28, [(p["mu"], 0)], 3584, name="rwkv_shift")
    pre_rows, _ = _odd_rows(fp, None, None, None)
    pre_params = [p["w0"], p["w_up"], p["a0"], p["a_up"], p["g_up"], p["k_k"], p["k_a"], p["j"]]
    decay, kkn, b, k2, g = _rw_fwd(_rwkv_pre, pre_rows, pre_params, [(1024, F32)] * 5, name="rwkv_pre")
    v3 = _split3(_to_heads(fp[:, 2048:3072]))
    y3, saved = _rwkv_fwd(decay, kkn, b, k2, fp, v3, name="rwkv_scan")
    y = _from_heads(y3)
    _, post_rows = _odd_rows(fp, y, k2, g)
    (yc,) = _rw_fwd(_rwkv_post, post_rows, [p["r_k"], p["ln_g"], p["ln_b"], p["j"]], [(1024, F32)], name="rwkv_post")
    xc = _ct_fwd(_conv_taps, proj, _O_LMAP, 8, [(p["conv_w"], 0), (p["conv_b"], 0)], 1024, name="lru_conv")
    pre = _matmul(xc, p["wax"], "nn", name="lru_gates")
    a, bx = _rw_fwd(_lru_pre, [_row(pre), _row(xc)], [p["bax"], p["lam"]], [(1024, F32)] * 2, name="lru_pre")
    hseq = _rscan_fwd(a, bx, name="lru_scan")
    (yd,) = _rw_fwd(_lru_post, [_row(hseq), _row(proj, 1024, 4)], [], [(1024, F32)], name="lru_post")
    return jnp.concatenate([yc, yd], axis=1), (fp, decay, kkn, b, k2, g, v3, saved, y, xc, pre, a, hseq)


def _odd_bwd(proj, p, res, dy):
    fp, decay, kkn, b, k2, g, v3, saved, y, xc, pre, a, hseq = res
    s = proj.shape[0]
    (dh, dgl2), _ = _rw_bwd(_lru_post, [_row(hseq), _row(proj, 1024, 4)], [], [dy[:, 1024:]], name="lru_post_bwd")
    dbx, da = _rscan_bwd(dh, a, hseq, name="lru_scan_bwd")
    (dpre, dxc), (dbax, dlam) = _rw_bwd(_lru_pre, [_row(pre), _row(xc)], [p["bax"], p["lam"]], [da, dbx], name="lru_pre_bwd")
    dxc = _matmul(dpre, p["wax"], "nt", name="lru_gates_dx", add=dxc)
    dwax = _matmul(xc, dpre, "tn", name="lru_gates_dw")
    dxl, (dlcw, dlcb) = _ct_bwd(_conv_taps, proj, _O_LMAP, 8, [(p["conv_w"], 0), (p["conv_b"], 0)], [dxc], name="lru_conv_bwd")
    pre_rows, post_rows = _odd_rows(fp, y, k2, g)
    (dyn, dr1, dk2a, dv1, dg), (dr_k, dln_g, dln_b) = _rw_bwd(
        _rwkv_post, post_rows, [p["r_k"], p["ln_g"], p["ln_b"], p["j"]], [dy[:, :1024]], name="rwkv_post_bwd", param_grads=[0, 1, 2])
    ddecay, dkkn, db, dk2b, dr2, dv3 = _rwkv_bwd(decay, kkn, b, k2, fp, v3, saved, _split3(_to_heads(dyn)), name="rwkv_scan_bwd")
    pre_params = [p["w0"], p["w_up"], p["a0"], p["a_up"], p["g_up"], p["k_k"], p["k_a"], p["j"]]
    (dk, dgl, dwl, dal), (dw0, dw_up, da0, da_up, dg_up, dk_k, dk_a) = _rw_bwd(
        _rwkv_pre, pre_rows, pre_params, [ddecay, dkkn, db, [dk2a, dk2b], dg], name="rwkv_pre_bwd", param_grads=list(range(7)))
    z = lambda n: jnp.zeros((s, n), F32)
    g1 = jnp.concatenate([dr1, dk, dv1, dgl, dwl, dal], axis=1)
    g2 = jnp.concatenate([dr2, z(1024), _from_heads(dv3), z(512)], axis=1)
    dfp, (dmu,) = _ct_bwd(_tshift, proj, _O_XMAP, 28, [(p["mu"], 0)], [g1, g2], name="rwkv_shift_bwd")
    dproj = jnp.concatenate([dfp[:, :3072], dxl, dgl2, dfp[:, 3072:]], axis=1)
    grads = dict(
        rwkv_mu=_mu_cols_t(dmu), rwkv_w0=dw0, rwkv_w_up=dw_up[:96][None], rwkv_a0=da0, rwkv_a_up=da_up[:96][None], rwkv_g_up=dg_up[None],
        rwkv_k_k=dk_k, rwkv_k_a=dk_a, rwkv_r_k=dr_k.reshape(1, 16, 64), rwkv_ln_g=dln_g, rwkv_ln_b=dln_b,
        lru_conv_w=dlcw[None], lru_conv_b=dlcb, lru_w_a=_blockdiag_t(dwax[:, :1024], 16)[None], lru_w_x=_blockdiag_t(dwax[:, 1024:], 16)[None],
        lru_b_a=dbax[:, :1024].reshape(1, 16, 64), lru_b_x=dbax[:, 1024:].reshape(1, 16, 64), lru_lam=dlam.reshape(1, 16, 64))
    return dproj, grads


def _my_index():
    return 4 * lax.axis_index("x") + 2 * lax.axis_index("y") + lax.axis_index("c")


def _peer(k):
    x, y, c = lax.axis_index("x"), lax.axis_index("y"), lax.axis_index("c")
    return (1 - x if k & 4 else x, 1 - y if k & 2 else y, 1 - c if k & 1 else c)


def _exchange(src, *, gather, name):
    r = src.shape[-2]

    def body(src_ref, out_ref, send_sems, recv_sems, local_sem):
        me = _my_index()
        mine = src_ref if gather else src_ref.at[me]
        local = pltpu.make_async_copy(mine, out_ref.at[me], local_sem)
        local.start()

        def copy(k):
            return pltpu.make_async_remote_copy(
                src_ref=src_ref if gather else src_ref.at[jnp.bitwise_xor(me, k)], dst_ref=out_ref.at[me],
                send_sem=send_sems.at[k - 1], recv_sem=recv_sems.at[k - 1], device_id=_peer(k), device_id_type=pl.DeviceIdType.MESH)

        def arrival(k):
            return pltpu.make_async_remote_copy(
                src_ref=mine, dst_ref=out_ref.at[jnp.bitwise_xor(me, k)],
                send_sem=send_sems.at[k - 1], recv_sem=recv_sems.at[k - 1], device_id=_peer(k), device_id_type=pl.DeviceIdType.MESH)

        sends = [copy(k) for k in range(1, N_DEV)]
        for cp in sends:
            cp.start()
        for k in range(1, N_DEV):
            arrival(k).wait_recv()
        for cp in sends:
            cp.wait_send()
        local.wait()

    return pl.pallas_call(
        body, out_shape=jax.ShapeDtypeStruct((N_DEV, r, LANES), src.dtype),
        in_specs=[pl.BlockSpec(memory_space=pl.ANY)], out_specs=pl.BlockSpec(memory_space=pl.ANY),
        scratch_shapes=[pltpu.SemaphoreType.DMA((N_DEV - 1,)), pltpu.SemaphoreType.DMA((N_DEV - 1,)), pltpu.SemaphoreType.DMA],
        name=name)(src)


PACK_ALIGN = 16 * LANES
PACK_ROWS = 512


def _pack(arrs, dtype, lead=False):
    parts, rows = [], 0
    for a in arrs:
        n_lead = a.shape[0] if lead else 1
        n = a.size // n_lead
        a = a.astype(dtype)
        if n % PACK_ALIGN:
            a = jnp.pad(a.reshape(n_lead, n), ((0, 0), (0, -n % PACK_ALIGN)))
        parts.append(a.reshape(n_lead, -1, LANES))
        rows += parts[-1].shape[1]
    if rows % PACK_ROWS:
        parts.append(jnp.zeros((parts[0].shape[0], -rows % PACK_ROWS, LANES), dtype))
    buf = jnp.concatenate(parts, axis=1)
    return buf if lead else buf[0]


def _unpack(buf, shapes, lead=False):
    buf = buf if lead else buf[None]
    out, off = [], 0
    for shp in shapes:
        n = math.prod(shp)
        rows = (n + (-n % PACK_ALIGN)) // LANES
        piece = buf[:, off:off + rows]
        if n % PACK_ALIGN:
            piece = piece.reshape(buf.shape[0], rows * LANES)[:, :n]
        out.append(piece.reshape(((buf.shape[0],) if lead else ()) + tuple(shp)))
        off += rows
    return out


def _unshard(parts, axis):
    moved = jnp.moveaxis(parts, 0, axis)
    shp = list(moved.shape)
    return moved.reshape(shp[:axis] + [shp[axis] * shp[axis + 1]] + shp[axis + 2:])


def _to_parts(full, axis):
    shp = list(full.shape)
    split = full.reshape(shp[:axis] + [N_DEV, shp[axis] // N_DEV] + shp[axis + 1:])
    return jnp.moveaxis(split, axis, 0)


def _adamw(gparts, w, m, v, *, name):
    r = w.shape[0]
    tr = _pick(r, (PACK_ROWS,))

    def body(g_ref, w_ref, m_ref, v_ref, go, do, mo, vo):
        g = g_ref[0].astype(F32)
        for d in range(1, N_DEV):
            g = g + g_ref[d].astype(F32)
        m1 = ADAM_B1 * m_ref[...] + (1.0 - ADAM_B1) * g
        v1 = ADAM_B2 * v_ref[...] + (1.0 - ADAM_B2) * jnp.square(g)
        m_hat = m1 / (1.0 - ADAM_B1 ** ADAM_STEP)
        v_hat = v1 / (1.0 - ADAM_B2 ** ADAM_STEP)
        go[...] = g
        do[...] = -ADAM_LR * (m_hat / (jnp.sqrt(v_hat) + ADAM_EPS) + ADAM_WD * w_ref[...])
        mo[...] = m1
        vo[...] = v1

    blk = pl.BlockSpec((tr, LANES), lambda i: (i, 0))
    return pl.pallas_call(
        body, grid=(r // tr,), in_specs=[pl.BlockSpec((N_DEV, tr, LANES), lambda i: (0, i, 0)), blk, blk, blk], out_specs=[blk] * 4,
        out_shape=[jax.ShapeDtypeStruct((r, LANES), F32)] * 4, compiler_params=_cparams("parallel"), name=name)(gparts, w, m, v)


def _loss_and_grad(h, g, tgt, *, name, ts=256):
    s, d = h.shape
    ts = min(ts, s)

    def tile_loss(hv, gv, tv):
        (y,) = _rmsnorm_tile(0, hv, gv)
        return 0.5 * jnp.sum(jnp.mean(jnp.square(y - tv), axis=-1))

    def body(h_ref, g_ref, t_ref, l_ref, dh_ref, dg_ref):
        @pl.when(pl.program_id(0) == 0)
        def _():
            l_ref[...] = jnp.zeros_like(l_ref)
            dg_ref[...] = jnp.zeros_like(dg_ref)

        tv = t_ref[...]
        loss, vjp = jax.vjp(lambda hv, gv: tile_loss(hv, gv, tv), h_ref[...], g_ref[...])
        dh, dg = vjp(jnp.ones((), F32))
        l_ref[...] += loss
        dh_ref[...] = dh
        dg_ref[...] += dg

    row = pl.BlockSpec((ts, d), lambda i: (i, 0))
    return pl.pallas_call(
        body, grid=(s // ts,), in_specs=[row, pl.BlockSpec((1, d), lambda i: (0, 0)), row],
        out_specs=[pl.BlockSpec((8, LANES), lambda i: (0, 0)), row, pl.BlockSpec((1, d), lambda i: (0, 0))],
        out_shape=[jax.ShapeDtypeStruct((8, LANES), F32), jax.ShapeDtypeStruct((s, d), F32), jax.ShapeDtypeStruct((1, d), F32)],
        compiler_params=_cparams("arbitrary"), name=name)(h, g, tgt)


def _norm(h, g, name):
    return _rw_fwd(_rmsnorm_tile, [_row(h)], [g], [(h.shape[1], F32)], name=name)[0]


def _norm_bwd(h, g, dhn, dres, name):
    (dh,), (dg,) = _rw_bwd(_rmsnorm_tile, [_row(h)], [g], [dhn], add_rows={0: dres}, name=name)
    return dh, dg


def _layer_fwd(h, p_i, lw, mixer_fwd, mp, tag):
    hn = _norm(h, lw["norm_mix"], f"{tag}_norm_mix")
    proj = _matmul(hn, lw["w_in"], "nn", name=f"{tag}_in_proj")
    y, mres = mixer_fwd(proj, mp)
    h1 = _matmul(y, lw["w_out"], "nn", name=f"{tag}_out_proj", add=h)
    hn2 = _norm(h1, lw["norm_ffn"], f"{tag}_norm_ffn")
    a1 = _matmul(hn2, lw["w1"], "nn", name=f"{tag}_mlp_up")
    (act,) = _rw_fwd(_relu2, [_row(a1)], [], [(a1.shape[1], F32)], name=f"{tag}_relu2", ts=64)
    h2 = _matmul(act, lw["w2"], "nn", name=f"{tag}_mlp_down", add=h1)
    hn3 = _norm(h2, lw["norm_pl"], f"{tag}_norm_pl")
    gl = _matmul(hn3, lw["w_gate"], "nn", name=f"{tag}_pl_gate")
    e = _matmul(p_i, lw["w_pl"], "nn", name=f"{tag}_pl_proj")
    (h3,) = _rw_fwd(_pl_gate, [_row(h2), _row(gl), _row(e)], [], [(h.shape[1], F32)], name=f"{tag}_pl_mix")
    return h3, (h, hn, proj, y, mres, h1, hn2, a1, act, h2, hn3, gl, e)


def _layer_bwd(dh3, p_i, lw, mixer_bwd, mp, saved, tag):
    h, hn, proj, y, mres, h1, hn2, a1, act, h2, hn3, gl, e = saved
    (dgl, de), _ = _rw_bwd(_pl_gate, [_row(h2), _row(gl), _row(e)], [], [dh3], row_grads=[1, 2], name=f"{tag}_pl_mix_bwd")
    g = dict(w_pl=_matmul(p_i, de, "tn", name=f"{tag}_pl_proj_dw"), w_gate=_matmul(hn3, dgl, "tn", name=f"{tag}_pl_gate_dw"))
    dhn3 = _matmul(dgl, lw["w_gate"], "nt", name=f"{tag}_pl_gate_dx")
    dh2, g["norm_pl"] = _norm_bwd(h2, lw["norm_pl"], dhn3, dh3, f"{tag}_norm_pl_bwd")
    dact = _matmul(dh2, lw["w2"], "nt", name=f"{tag}_mlp_down_dx")
    g["w2"] = _matmul(act, dh2, "tn", name=f"{tag}_mlp_down_dw")
    (da1,), _ = _rw_bwd(_relu2, [_row(a1)], [], [dact], name=f"{tag}_relu2_bwd", ts=64)
    g["w1"] = _matmul(hn2, da1, "tn", name=f"{tag}_mlp_up_dw")
    dhn2 = _matmul(da1, lw["w1"], "nt", name=f"{tag}_mlp_up_dx")
    dh1, g["norm_ffn"] = _norm_bwd(h1, lw["norm_ffn"], dhn2, dh2, f"{tag}_norm_ffn_bwd")
    dy = _matmul(dh1, lw["w_out"], "nt", name=f"{tag}_out_proj_dx")
    g["w_out"] = _matmul(y, dh1, "tn", name=f"{tag}_out_proj_dw")
    dproj, mg = mixer_bwd(proj, mp, mres, dy)
    g["w_in"] = _matmul(hn, dproj, "tn", name=f"{tag}_in_proj_dw")
    dhn = _matmul(dproj, lw["w_in"], "nt", name=f"{tag}_in_proj_dx")
    dh, g["norm_mix"] = _norm_bwd(h, lw["norm_mix"], dhn, dh1, f"{tag}_norm_mix_bwd")
    return dh, g, mg


WEIGHTS = (
    ("norm_mix", None), ("norm_ffn", None), ("norm_pl", None), ("mlp_w1", 2), ("mlp_w2", 1), ("pl_proj", 2), ("pl_gate", 1),
    ("e_in_proj", 2), ("e_out_proj", 1), ("s5_lam_re", None), ("s5_lam_im", None), ("s5_log_step", None), ("s5_b_re", None),
    ("s5_b_im", None), ("s5_c_re", None), ("s5_c_im", None), ("s5_d", None), ("s5_glu_w", 1), ("s5_glu_b", None),
    ("ssd_conv_w", 2), ("ssd_conv_b", None), ("ssd_dt_bias", None), ("ssd_a_log", None), ("ssd_d", None), ("ssd_norm", None),
    ("o_in_proj", 2), ("o_out_proj", 1), ("rwkv_mu", 1), ("rwkv_w0", 1), ("rwkv_w_up", 2), ("rwkv_a0", 1), ("rwkv_a_up", 2),
    ("rwkv_g_up", 2), ("rwkv_k_k", 1), ("rwkv_k_a", 1), ("rwkv_r_k", None), ("rwkv_ln_g", 1), ("rwkv_ln_b", 1),
    ("lru_conv_w", 2), ("lru_conv_b", 1), ("lru_w_a", None), ("lru_b_a", None), ("lru_w_x", None), ("lru_b_x", None),
    ("lru_lam", None), ("norm_final", None))
MATMUL_WEIGHTS = ("mlp_w1", "mlp_w2", "pl_proj", "pl_gate", "e_in_proj", "e_out_proj", "s5_glu_w", "o_in_proj", "o_out_proj",
                  "rwkv_w_up", "rwkv_a_up", "rwkv_g_up")


def _gather_weights(a):
    axis = dict(WEIGHTS)
    full = {n: a[n] for n, ax in WEIGHTS if ax is None}
    for names, dtype, tag in ((MATMUL_WEIGHTS, BF16, "bf16"), ([n for n, ax in WEIGHTS if ax is not None and n not in MATMUL_WEIGHTS], F32, "f32")):
        got = _exchange(_pack([a[n] for n in names], dtype), gather=True, name=f"gather_weights_{tag}")
        for n, parts in zip(names, _unpack(got, [a[n].shape for n in names], lead=True)):
            full[n] = _unshard(parts, axis[n])
    return full


def kernel(x, p, norm_mix, norm_ffn, norm_pl, mlp_w1, mlp_w2, pl_proj, pl_gate, e_in_proj, e_out_proj, s5_lam_re, s5_lam_im, s5_log_step, s5_b_re, s5_b_im, s5_c_re, s5_c_im, s5_d, s5_glu_w, s5_glu_b, ssd_conv_w, ssd_conv_b, ssd_dt_bias, ssd_a_log, ssd_d, ssd_norm, o_in_proj, o_out_proj, rwkv_mu, rwkv_w0, rwkv_w_up, rwkv_a0, rwkv_a_up, rwkv_g_up, rwkv_k_k, rwkv_k_a, rwkv_r_k, rwkv_ln_g, rwkv_ln_b, lru_conv_w, lru_conv_b, lru_w_a, lru_b_a, lru_w_x, lru_b_x, lru_lam, norm_final, loss_target, m_norm_mix, m_norm_ffn, m_norm_pl, m_mlp_w1, m_mlp_w2, m_pl_proj, m_pl_gate, m_e_in_proj, m_e_out_proj, m_s5_lam_re, m_s5_lam_im, m_s5_log_step, m_s5_b_re, m_s5_b_im, m_s5_c_re, m_s5_c_im, m_s5_d, m_s5_glu_w, m_s5_glu_b, m_ssd_conv_w, m_ssd_conv_b, m_ssd_dt_bias, m_ssd_a_log, m_ssd_d, m_ssd_norm, m_o_in_proj, m_o_out_proj, m_rwkv_mu, m_rwkv_w0, m_rwkv_w_up, m_rwkv_a0, m_rwkv_a_up, m_rwkv_g_up, m_rwkv_k_k, m_rwkv_k_a, m_rwkv_r_k, m_rwkv_ln_g, m_rwkv_ln_b, m_lru_conv_w, m_lru_conv_b, m_lru_w_a, m_lru_b_a, m_lru_w_x, m_lru_b_x, m_lru_lam, m_norm_final, v_norm_mix, v_norm_ffn, v_norm_pl, v_mlp_w1, v_mlp_w2, v_pl_proj, v_pl_gate, v_e_in_proj, v_e_out_proj, v_s5_lam_re, v_s5_lam_im, v_s5_log_step, v_s5_b_re, v_s5_b_im, v_s5_c_re, v_s5_c_im, v_s5_d, v_s5_glu_w, v_s5_glu_b, v_ssd_conv_w, v_ssd_conv_b, v_ssd_dt_bias, v_ssd_a_log, v_ssd_d, v_ssd_norm, v_o_in_proj, v_o_out_proj, v_rwkv_mu, v_rwkv_w0, v_rwkv_w_up, v_rwkv_a0, v_rwkv_a_up, v_rwkv_g_up, v_rwkv_k_k, v_rwkv_k_a, v_rwkv_r_k, v_rwkv_ln_g, v_rwkv_ln_b, v_lru_conv_w, v_lru_conv_b, v_lru_w_a, v_lru_b_a, v_lru_w_x, v_lru_b_x, v_lru_lam, v_norm_final):
    a = dict(locals())
    full = _gather_weights(a)
    d_model = x.shape[-1]
    row = lambda v: v.reshape(1, d_model)

    layers = []
    for i, (mixer_fwd, mixer_bwd, prep, w_in, cols, out_name) in enumerate((
            (_even_fwd, _even_bwd, _even_prep, "e_in_proj", _even_in_cols, "e_out_proj"),
            (_odd_fwd, _odd_bwd, _odd_prep, "o_in_proj", _odd_in_cols, "o_out_proj"))):
        lw = dict(norm_mix=row(norm_mix[i]), norm_ffn=row(norm_ffn[i]), norm_pl=row(norm_pl[i]), w_in=cols(full[w_in][0]),
                  w_out=full[out_name][0], w1=full["mlp_w1"][i], w2=full["mlp_w2"][i], w_gate=full["pl_gate"][i], w_pl=full["pl_proj"][i])
        layers.append((lw, mixer_fwd, mixer_bwd, prep(full), p[i, 0], f"l{i}"))

    h = x[0]
    saved = []
    for lw, mixer_fwd, _, mp, p_i, tag in layers:
        h, sv = _layer_fwd(h, p_i, lw, mixer_fwd, mp, tag)
        saved.append(sv)
    loss_blk, dh, dg_final = _loss_and_grad(h, row(norm_final), loss_target[0], name="loss")
    loss = lax.psum(loss_blk[0, 0], ("x", "y", "c"))

    lgrads = [None, None]
    mgrads = {}
    for i in (1, 0):
        lw, _, mixer_bwd, mp, p_i, tag = layers[i]
        dh, lgrads[i], mg = _layer_bwd(dh, p_i, lw, mixer_bwd, mp, saved[i], tag)
        mgrads.update(mg)
    stack = lambda key: jnp.stack([lgrads[0][key], lgrads[1][key]])
    grads = dict(mgrads)
    grads.update(
        norm_mix=stack("norm_mix").reshape(2, d_model), norm_ffn=stack("norm_ffn").reshape(2, d_model),
        norm_pl=stack("norm_pl").reshape(2, d_model), mlp_w1=stack("w1"), mlp_w2=stack("w2"), pl_proj=stack("w_pl"), pl_gate=stack("w_gate"),
        e_in_proj=_even_in_cols_t(lgrads[0]["w_in"])[None], e_out_proj=lgrads[0]["w_out"][None],
        o_in_proj=_odd_in_cols_t(lgrads[1]["w_in"])[None], o_out_proj=lgrads[1]["w_out"][None], norm_final=dg_final.reshape(d_model))

    sh = [(n, ax) for n, ax in WEIGHTS if ax is not None]
    slabs = _pack([_to_parts(grads[n], ax) for n, ax in sh], BF16, lead=True)
    got = _exchange(slabs, gather=False, name="scatter_grads")
    sh_out = _adamw(got, *[_pack([a[pre + n] for n, _ in sh], F32) for pre in ("", "m_", "v_")], name="adamw_sharded")
    rp = [n for n, ax in WEIGHTS if ax is None]
    parts = _exchange(_pack([grads[n].reshape(a[n].shape) for n in rp], F32), gather=True, name="gather_small_grads")
    rp_out = _adamw(parts, *[_pack([a[pre + n] for n in rp], F32) for pre in ("", "m_", "v_")], name="adamw_replicated")

    outs = []
    for k in range(4):
        by_name = dict(zip([n for n, _ in sh], _unpack(sh_out[k], [a[n].shape for n, _ in sh])))
        by_name.update(zip(rp, _unpack(rp_out[k], [a[n].shape for n in rp])))
        outs.append([by_name[n] for n, _ in WEIGHTS])
    return (loss, dh[None], *outs[0], *outs[1], *outs[2], *outs[3])
```

```python
import functools
import math

import jax
import jax.numpy as jnp
from jax import lax
from jax.experimental import pallas as pl
from jax.experimental.pallas import tpu as pltpu

F32 = jnp.float32
BF16 = jnp.bfloat16
N_DEV = 8
LANES = 128
VMEM_LIMIT = 56 * 1024 * 1024
MATMUL_VMEM = 46 * 1024 * 1024
NORM_EPS = 1e-6
RWKV_GN_EPS = 64e-5
LRU_C = 8.0
ADAM_LR, ADAM_B1, ADAM_B2, ADAM_EPS, ADAM_WD, ADAM_STEP = 0.001, 0.9, 0.999, 1e-08, 0.01, 10
SSD_CHUNK = 128
RWKV_CHUNK = 32
HEAD = 64


def _cparams(*sem):
    return pltpu.CompilerParams(dimension_semantics=sem, vmem_limit_bytes=VMEM_LIMIT)


def _pick(n, prefs):
    for t in prefs:
        if n % t == 0:
            return t
    return n


def _matmul(a, b, mode, *, name, add=None, out_dtype=F32):
    if mode == "nn":
        (m, k), (k2, n) = a.shape, b.shape
    elif mode == "nt":
        (m, k), (n, k2) = a.shape, b.shape
    else:
        (k, m), (k2, n) = a.shape, b.shape
    assert k == k2, (a.shape, b.shape, mode)
    tm, tn = _pick(m, (1024, 512, 256, 128)), _pick(n, (1024, 512, 256, 128))
    fits = lambda t: 2 * t * (tm * a.dtype.itemsize + tn * b.dtype.itemsize) + (3 if add is None else 5) * tm * tn * 4 <= MATMUL_VMEM
    tk = next((t for t in (2048, 1024, 512, 256, 128) if k % t == 0 and fits(t)), k)
    nk = k // tk
    dn = {"nn": (((1,), (0,)), ((), ())), "nt": (((1,), (1,)), ((), ())), "tn": (((0,), (0,)), ((), ()))}[mode]
    a_spec = pl.BlockSpec((tk, tm), lambda i, j, kk: (kk, i)) if mode == "tn" else pl.BlockSpec((tm, tk), lambda i, j, kk: (i, kk))
    b_spec = pl.BlockSpec((tn, tk), lambda i, j, kk: (j, kk)) if mode == "nt" else pl.BlockSpec((tk, tn), lambda i, j, kk: (kk, j))
    o_spec = pl.BlockSpec((tm, tn), lambda i, j, kk: (i, j))

    def body(a_ref, b_ref, *rest):
        if add is None:
            o_ref, acc = rest
        else:
            add_ref, o_ref, acc = rest
        kk = pl.program_id(2)
        prod = lambda: lax.dot_general(a_ref[...].astype(BF16), b_ref[...].astype(BF16), dn, preferred_element_type=F32)
        first = lambda: prod() if add is None else prod() + add_ref[...].astype(F32)
        if nk == 1:
            o_ref[...] = first().astype(o_ref.dtype)
            return

        @pl.when(kk == 0)
        def _():
            acc[...] = first()

        @pl.when((kk > 0) & (kk < nk - 1))
        def _():
            acc[...] += prod()

        @pl.when(kk == nk - 1)
        def _():
            o_ref[...] = (acc[...] + prod()).astype(o_ref.dtype)

    ins, specs = [a, b], [a_spec, b_spec]
    if add is not None:
        ins.append(add)
        specs.append(o_spec)
    return pl.pallas_call(
        body, grid=(m // tm, n // tn, nk), in_specs=specs, out_specs=o_spec,
        out_shape=jax.ShapeDtypeStruct((m, n), out_dtype), scratch_shapes=[pltpu.VMEM((tm, tn), F32)],
        compiler_params=_cparams("parallel", "parallel", "arbitrary"), name=name)(*ins)


def _row(x, width=None, block=0):
    return (x, x.shape[1] if width is None else width, block)


def _rw_specs(rows, params, ts):
    specs = [pl.BlockSpec((ts, w), functools.partial(lambda i, b: (i, b), b=bi)) for (_, w, bi) in rows]
    specs += [pl.BlockSpec(p.shape, functools.partial(lambda i, nd: (0,) * nd, nd=p.ndim)) for p in params]
    return specs


def _rw_fwd(f, rows, params, outs, *, name, ts=256):
    s = rows[0][0].shape[0]
    ts = min(ts, s)
    nr, npar = len(rows), len(params)

    def body(*refs):
        row0 = pl.program_id(0) * ts
        res = f(row0, *[r[...] for r in refs[:nr + npar]])
        for o_ref, val in zip(refs[nr + npar:], res, strict=True):
            o_ref[...] = val.astype(o_ref.dtype)

    out = pl.pallas_call(
        body, grid=(s // ts,), in_specs=_rw_specs(rows, params, ts),
        out_specs=[pl.BlockSpec((ts, w), lambda i: (i, 0)) for (w, _) in outs],
        out_shape=[jax.ShapeDtypeStruct((s, w), dt) for (w, dt) in outs],
        compiler_params=_cparams("parallel"), name=name)(*[r[0] for r in rows], *params)
    return tuple(out)


def _rw_bwd(f, rows, params, cts, *, name, ts=256, row_grads=None, param_grads=None, add_rows=None):
    s = rows[0][0].shape[0]
    ts = min(ts, s)
    ct_groups = [list(c) if isinstance(c, (list, tuple)) else [c] for c in cts]
    cts = [c for grp in ct_groups for c in grp]
    nr, npar, nct = len(rows), len(params), len(cts)
    row_grads = list(range(nr)) if row_grads is None else list(row_grads)
    param_grads = list(range(npar)) if param_grads is None else list(param_grads)
    add_rows = add_rows or {}
    add_keys = sorted(add_rows)

    def body(*refs):
        i = pl.program_id(0)
        row0 = i * ts
        vals = [r[...] for r in refs[:nr + npar]]
        for pi in param_grads:
            vals[nr + pi] = vals[nr + pi].astype(F32)
        ct_refs = list(refs[nr + npar:nr + npar + nct])
        add_refs = dict(zip(add_keys, refs[nr + npar + nct:nr + npar + nct + len(add_keys)]))
        o_refs = refs[nr + npar + nct + len(add_keys):]
        res, vjp = jax.vjp(functools.partial(f, row0), *vals)
        ct_vals = []
        for grp, r in zip(ct_groups, res, strict=True):
            ct_vals.append(sum(ct_refs.pop(0)[...].astype(r.dtype) for _ in grp))
        grads = vjp(tuple(ct_vals))
        for o_ref, ri in zip(o_refs[:len(row_grads)], row_grads):
            g = grads[ri]
            if ri in add_refs:
                g = g + add_refs[ri][...]
            o_ref[...] = g.astype(o_ref.dtype)

        @pl.when(i == 0)
        def _():
            for o_ref in o_refs[len(row_grads):]:
                o_ref[...] = jnp.zeros_like(o_ref)

        for o_ref, pi in zip(o_refs[len(row_grads):], param_grads):
            o_ref[...] += grads[nr + pi].astype(F32)

    in_specs = _rw_specs(rows, params, ts)
    in_specs += [pl.BlockSpec((ts, c.shape[1]), lambda i: (i, 0)) for c in cts]
    in_specs += [pl.BlockSpec((ts, add_rows[k].shape[1]), lambda i: (i, 0)) for k in add_keys]
    out_specs = [pl.BlockSpec((ts, rows[ri][1]), lambda i: (i, 0)) for ri in row_grads]
    out_specs += [pl.BlockSpec(params[pi].shape, functools.partial(lambda i, nd: (0,) * nd, nd=params[pi].ndim)) for pi in param_grads]
    out_shape = [jax.ShapeDtypeStruct((s, rows[ri][1]), F32) for ri in row_grads]
    out_shape += [jax.ShapeDtypeStruct(params[pi].shape, F32) for pi in param_grads]
    out = pl.pallas_call(
        body, grid=(s // ts,), in_specs=in_specs, out_specs=out_specs, out_shape=out_shape,
        compiler_params=_cparams("arbitrary"), name=name)(*[r[0] for r in rows], *params, *cts, *[add_rows[k] for k in add_keys])
    return tuple(out[:len(row_grads)]), tuple(out[len(row_grads):])


@functools.partial(jax.custom_vjp, nondiff_argnums=(1,))
def _shift_down(x, k):
    rows = lax.broadcasted_iota(jnp.int32, x.shape, 0)
    return jnp.where(rows < k, 0.0, pltpu.roll(x, k, 0))


def _shift_down_fwd(x, k):
    return _shift_down(x, k), None


def _shift_down_bwd(k, _, g):
    n = g.shape[0]
    rows = lax.broadcasted_iota(jnp.int32, g.shape, 0)
    return (jnp.where(rows >= n - k, 0.0, pltpu.roll(g, n - k, 0)),)


_shift_down.defvjp(_shift_down_fwd, _shift_down_bwd)


def _ct_fwd(f, x, xmap, ntiles, params, out_width, *, name, ct=LANES):
    s = x.shape[0]

    def body(*refs):
        refs[-1][...] = f(*[r[...] for r in refs[:-1]])

    in_specs = [pl.BlockSpec((s, ct), lambda j: (0, xmap(j)))]
    in_specs += [pl.BlockSpec((p.shape[0], ct), functools.partial(lambda j, o: (0, o + j), o=o)) for (p, o) in params]
    return pl.pallas_call(
        body, grid=(ntiles,), in_specs=in_specs, out_specs=pl.BlockSpec((s, ct), lambda j: (0, j)),
        out_shape=jax.ShapeDtypeStruct((s, out_width), F32), compiler_params=_cparams("parallel"), name=name)(x, *[p for p, _ in params])


def _ct_bwd(f, x, xmap, ntiles, params, g, *, name, ct=LANES):
    s = x.shape[0]
    npar, ng = len(params), len(g)

    def body(*refs):
        vals = [r[...] for r in refs[:1 + npar]]
        _, vjp = jax.vjp(f, *vals)
        grads = vjp(sum(r[...] for r in refs[1 + npar:1 + npar + ng]))
        for o_ref, gr in zip(refs[1 + npar + ng:], grads, strict=True):
            o_ref[...] = gr

    in_specs = [pl.BlockSpec((s, ct), lambda j: (0, xmap(j)))]
    pspecs = [pl.BlockSpec((p.shape[0], ct), functools.partial(lambda j, o: (0, o + j), o=o)) for (p, o) in params]
    in_specs += pspecs + [pl.BlockSpec((s, ct), lambda j: (0, j))] * ng
    out = pl.pallas_call(
        body, grid=(ntiles,), in_specs=in_specs, out_specs=[pl.BlockSpec((s, ct), lambda j: (0, j))] + pspecs,
        out_shape=[jax.ShapeDtypeStruct((s, ntiles * ct), F32)] + [jax.ShapeDtypeStruct(p.shape, F32) for p, _ in params],
        compiler_params=_cparams("parallel"), name=name)(x, *[p for p, _ in params], *g)
    return out[0], tuple(out[1:])


def _block_vjp(f, args, cts, *, name):
    outs = jax.eval_shape(f, *args)
    if cts is None:
        def body(*refs):
            for o_ref, v in zip(refs[len(args):], f(*[r[...] for r in refs[:len(args)]]), strict=True):
                o_ref[...] = v
        return tuple(pl.pallas_call(body, out_shape=[jax.ShapeDtypeStruct(o.shape, o.dtype) for o in outs], name=name)(*args))

    def body(*refs):
        n = len(args)
        _, vjp = jax.vjp(f, *[r[...] for r in refs[:n]])
        for o_ref, gr in zip(refs[n + len(cts):], vjp(tuple(r[...] for r in refs[n:n + len(cts)])), strict=True):
            o_ref[...] = gr
    return tuple(pl.pallas_call(body, out_shape=[jax.ShapeDtypeStruct(a.shape, a.dtype) for a in args], name=name)(*args, *cts))


def _softplus(x):
    return jnp.maximum(x, 0.0) + jnp.log(1.0 + jnp.exp(-jnp.abs(x)))


def _dot_bf16(a, b):
    return jnp.dot(a.astype(BF16), b.astype(BF16), preferred_element_type=F32)


def _dot3(x, m):
    hi = x.astype(BF16)
    r1 = x - hi.astype(F32)
    mid = r1.astype(BF16)
    lo = (r1 - mid.astype(F32)).astype(BF16)
    return (jnp.dot(hi, m, preferred_element_type=F32) + jnp.dot(mid, m, preferred_element_type=F32)
            + jnp.dot(lo, m, preferred_element_type=F32))


@jax.custom_vjp
def _lin(x, m, mt):
    return _dot3(x, m)


def _lin_fwd(x, m, mt):
    return _dot3(x, m), (m, mt)


def _lin_bwd(res, g):
    m, mt = res
    return _dot3(g, mt), jnp.zeros_like(m), jnp.zeros_like(mt)


_lin.defvjp(_lin_fwd, _lin_bwd)


def _head_ones(n, head=HEAD):
    i = jnp.arange(n) // head
    return (i[:, None] == i[None, :]).astype(BF16)


def _rmsnorm_tile(_, h, g):
    return (h * lax.rsqrt(jnp.mean(h * h, axis=-1, keepdims=True) + NORM_EPS) * g,)


def _cscan_fwd(bu, a, *, name, ct=256):
    s, c2 = bu.shape
    c = c2 // 2
    nt = c // ct

    def body(br_ref, bi_ref, ar_ref, ai_ref, xr_ref, xi_ref):
        ar, ai = ar_ref[...], ai_ref[...]

        def step(t, carry):
            hr, hi = carry
            nr = ar * hr - ai * hi + br_ref[pl.ds(t, 1), :]
            ni = ar * hi + ai * hr + bi_ref[pl.ds(t, 1), :]
            xr_ref[pl.ds(t, 1), :] = nr
            xi_ref[pl.ds(t, 1), :] = ni
            return nr, ni

        z = jnp.zeros((1, ct), F32)
        lax.fori_loop(0, s, step, (z, z))

    re = lambda j: (0, j)
    im = lambda j: (0, nt + j)
    xr, xi = pl.pallas_call(
        body, grid=(nt,),
        in_specs=[pl.BlockSpec((s, ct), re), pl.BlockSpec((s, ct), im), pl.BlockSpec((1, ct), re), pl.BlockSpec((1, ct), im)],
        out_specs=[pl.BlockSpec((s, ct), re), pl.BlockSpec((s, ct), re)],
        out_shape=[jax.ShapeDtypeStruct((s, c), F32)] * 2, compiler_params=_cparams("parallel"), name=name)(bu, bu, a, a)
    return xr, xi


def _cscan_bwd(gr, gi, xr, xi, a, *, name, ct=256):
    s, c = gr.shape
    nt = c // ct

    def body(gr_ref, gi_ref, xr_ref, xi_ref, ar_ref, ai_ref, dr_ref, di_ref, dar_ref, dai_ref):
        ar, ai = ar_ref[...], ai_ref[...]

        def step(i, carry):
            dr, di, accr, acci = carry
            t = s - 1 - i
            nr = gr_ref[pl.ds(t, 1), :] + ar * dr + ai * di
            ni = gi_ref[pl.ds(t, 1), :] + ar * di - ai * dr
            dr_ref[pl.ds(t, 1), :] = nr
            di_ref[pl.ds(t, 1), :] = ni
            tp = jnp.maximum(t - 1, 0)
            live = (t > 0).astype(F32)
            pr = xr_ref[pl.ds(tp, 1), :] * live
            pi = xi_ref[pl.ds(tp, 1), :] * live
            return nr, ni, accr + nr * pr + ni * pi, acci + ni * pr - nr * pi

        z = jnp.zeros((1, ct), F32)
        _, _, accr, acci = lax.fori_loop(0, s, step, (z, z, z, z))
        dar_ref[...] = accr
        dai_ref[...] = acci

    re = lambda j: (0, j)
    im = lambda j: (0, nt + j)
    blk = pl.BlockSpec((s, ct), re)
    dr, di, dar, dai = pl.pallas_call(
        body, grid=(nt,),
        in_specs=[blk, blk, blk, blk, pl.BlockSpec((1, ct), re), pl.BlockSpec((1, ct), im)],
        out_specs=[blk, blk, pl.BlockSpec((1, ct), re), pl.BlockSpec((1, ct), re)],
        out_shape=[jax.ShapeDtypeStruct((s, c), F32)] * 2 + [jax.ShapeDtypeStruct((1, c), F32)] * 2,
        compiler_params=_cparams("parallel"), name=name)(gr, gi, xr, xi, a, a)
    return dr, di, dar, dai


def _rscan_fwd(a, b, *, name, ct=256):
    s, c = a.shape

    def body(a_ref, b_ref, h_ref):
        def step(t, h):
            h = a_ref[pl.ds(t, 1), :] * h + b_ref[pl.ds(t, 1), :]
            h_ref[pl.ds(t, 1), :] = h
            return h
        lax.fori_loop(0, s, step, jnp.zeros((1, ct), F32))

    blk = pl.BlockSpec((s, ct), lambda j: (0, j))
    return pl.pallas_call(body, grid=(c // ct,), in_specs=[blk, blk], out_specs=blk,
                          out_shape=jax.ShapeDtypeStruct((s, c), F32), compiler_params=_cparams("parallel"), name=name)(a, b)


def _rscan_bwd(g, a, h, *, name, ct=256):
    s, c = a.shape

    def body(g_ref, a_ref, h_ref, db_ref, da_ref):
        def step(i, carry):
            d, an = carry
            t = s - 1 - i
            d = g_ref[pl.ds(t, 1), :] + an * d
            db_ref[pl.ds(t, 1), :] = d
            tp = jnp.maximum(t - 1, 0)
            da_ref[pl.ds(t, 1), :] = d * h_ref[pl.ds(tp, 1), :] * (t > 0).astype(F32)
            return d, a_ref[pl.ds(t, 1), :]
        z = jnp.zeros((1, ct), F32)
        lax.fori_loop(0, s, step, (z, z))

    blk = pl.BlockSpec((s, ct), lambda j: (0, j))
    db, da = pl.pallas_call(body, grid=(c // ct,), in_specs=[blk, blk, blk], out_specs=[blk, blk],
                            out_shape=[jax.ShapeDtypeStruct((s, c), F32)] * 2, compiler_params=_cparams("parallel"), name=name)(g, a, h)
    return db, da


def _dot3l(m, x):
    hi = x.astype(BF16)
    r1 = x - hi.astype(F32)
    mid = r1.astype(BF16)
    lo = (r1 - mid.astype(F32)).astype(BF16)
    return (jnp.dot(m, hi, preferred_element_type=F32) + jnp.dot(m, mid, preferred_element_type=F32)
            + jnp.dot(m, lo, preferred_element_type=F32))


@jax.custom_vjp
def _linl(x, m, mt):
    return _dot3l(m, x)


def _linl_fwd(x, m, mt):
    return _dot3l(m, x), (m, mt)


def _linl_bwd(res, g):
    m, mt = res
    return _dot3l(mt, g), jnp.zeros_like(m), jnp.zeros_like(mt)


_linl.defvjp(_linl_fwd, _linl_bwd)


def _ssd_chunk(g, xs, bm, cm, z, dtraw, dt_bias, a_log, dskip, ng, st0, st1, st2):
    n = xs.shape[0]
    lane = lax.broadcasted_iota(jnp.int32, (1, LANES), 1)
    sub = lax.broadcasted_iota(jnp.int32, (LANES, 1), 0)
    row = lax.broadcasted_iota(jnp.int32, (n, n), 0)
    col = lax.broadcasted_iota(jnp.int32, (n, n), 1)
    tril = row >= col
    tril_m = tril.astype(BF16)
    triu_m = (row <= col).astype(BF16)
    lane_lo = lane < HEAD
    sub_lo = sub < HEAD
    dt = _softplus(dtraw + dt_bias)
    da = dt * (-jnp.exp(a_log))
    acum = _linl(da, tril_m, triu_m)
    acum_t = acum.T
    scores = lax.dot_general(cm.astype(BF16), bm.astype(BF16), (((1,), (1,)), ((), ())), preferred_element_type=F32)

    def head(h):
        sel = lane == h
        acol = jnp.sum(jnp.where(sel, acum, 0.0), axis=1, keepdims=True)
        arow = jnp.sum(jnp.where(sub == h, acum_t, 0.0), axis=0, keepdims=True)
        dtcol = jnp.sum(jnp.where(sel, dt, 0.0), axis=1, keepdims=True)
        dsk = jnp.sum(jnp.where(sel, dskip, 0.0), axis=1, keepdims=True)
        decay = jnp.exp(jnp.where(tril, acol - arow, -jnp.inf))
        alast = acol[n - 1:n, :]
        return acol, dtcol, dsk, decay, alast

    ys, new = [], []
    for q, st in enumerate((st0, st1, st2)):
        a_acol, a_dt, a_dsk, a_decay, a_last = head(g * 6 + 2 * q)
        b_acol, b_dt, b_dsk, b_decay, b_last = head(g * 6 + 2 * q + 1)
        xp = xs[:, q * LANES:(q + 1) * LANES]
        xdt = xp * jnp.where(lane_lo, a_dt, b_dt)
        yd = jnp.where(lane_lo, _dot_bf16(scores * a_decay, xdt), _dot_bf16(scores * b_decay, xdt))
        xw = xdt * jnp.where(lane_lo, jnp.exp(a_last - a_acol), jnp.exp(b_last - b_acol))
        states = lax.dot_general(xw.astype(BF16), bm.astype(BF16), (((0,), (0,)), ((), ())), preferred_element_type=F32)
        yo = lax.dot_general(cm.astype(BF16), st.astype(BF16), (((1,), (1,)), ((), ())), preferred_element_type=F32)
        yo = yo * jnp.where(lane_lo, jnp.exp(a_acol), jnp.exp(b_acol))
        new.append(st * jnp.where(sub_lo, jnp.exp(a_last), jnp.exp(b_last)) + states)
        ys.append(yd + yo + xp * jnp.where(lane_lo, a_dsk, b_dsk))
    y = jnp.concatenate(ys, axis=1)
    y = y * (z * jax.nn.sigmoid(z))
    y = y * lax.rsqrt(jnp.mean(y * y, axis=-1, keepdims=True) + NORM_EPS) * ng
    return y, new[0], new[1], new[2]


def _ssd_specs(nc, rev):
    cidx = (lambda c: nc - 1 - c) if rev else (lambda c: c)
    gw = 3 * LANES
    return [
        pl.BlockSpec((SSD_CHUNK, gw), lambda c, g: (cidx(c), g)),
        pl.BlockSpec((SSD_CHUNK, LANES), lambda c, g: (cidx(c), 12 + g)),
        pl.BlockSpec((SSD_CHUNK, LANES), lambda c, g: (cidx(c), 16 + g)),
        pl.BlockSpec((SSD_CHUNK, gw), lambda c, g: (cidx(c), g)),
        pl.BlockSpec((SSD_CHUNK, LANES), lambda c, g: (cidx(c), 36)),
        pl.BlockSpec((1, LANES), lambda c, g: (0, 0)),
        pl.BlockSpec((1, LANES), lambda c, g: (0, 0)),
        pl.BlockSpec((1, LANES), lambda c, g: (0, 0)),
        pl.BlockSpec((1, gw), lambda c, g: (0, g)),
    ], cidx


def _ssd_fwd(conv, proj, dt_bias, a_log, dskip, norm_g, *, name):
    s = conv.shape[0]
    nc = s // SSD_CHUNK
    in_specs, _ = _ssd_specs(nc, False)

    def body(xs, bm, cm, z, dtr, dtb, alog, dsk, ng, y_ref, sv_ref, st):
        c, g = pl.program_id(0), pl.program_id(1)

        @pl.when(c == 0)
        def _():
            for q in range(3):
                st[g * 3 + q] = jnp.zeros((LANES, LANES), F32)

        olds = [st[g * 3 + q] for q in range(3)]
        for q in range(3):
            sv_ref[0, 0, q] = olds[q]
        y, n0, n1, n2 = _ssd_chunk(g, xs[...], bm[...], cm[...], z[...], dtr[...], dtb[...], alog[...], dsk[...], ng[...], *olds)
        y_ref[...] = y
        for q, v in enumerate((n0, n1, n2)):
            st[g * 3 + q] = v

    return pl.pallas_call(
        body, grid=(nc, 4), in_specs=in_specs,
        out_specs=[pl.BlockSpec((SSD_CHUNK, 3 * LANES), lambda c, g: (c, g)),
                   pl.BlockSpec((1, 1, 3, LANES, LANES), lambda c, g: (c, g, 0, 0, 0))],
        out_shape=[jax.ShapeDtypeStruct((s, 12 * LANES), F32), jax.ShapeDtypeStruct((nc, 4, 3, LANES, LANES), F32)],
        scratch_shapes=[pltpu.VMEM((12, LANES, LANES), F32)],
        compiler_params=_cparams("arbitrary", "arbitrary"), name=name)(conv, conv, conv, proj, proj, dt_bias, a_log, dskip, norm_g)


def _ssd_bwd(conv, proj, dt_bias, a_log, dskip, norm_g, saved, dy, *, name):
    s = conv.shape[0]
    nc = s // SSD_CHUNK
    in_specs, cidx = _ssd_specs(nc, True)
    gw = 3 * LANES
    in_specs += [pl.BlockSpec((1, 1, 3, LANES, LANES), lambda c, g: (cidx(c), g, 0, 0, 0)),
                 pl.BlockSpec((SSD_CHUNK, gw), lambda c, g: (cidx(c), g))]

    def body(xs, bm, cm, z, dtr, dtb, alog, dsk, ng, sv, dy_ref, dxs, dbm, dcm, dz, ddt, ddtb, dalog, ddsk, dng, dst):
        c, g = pl.program_id(0), pl.program_id(1)

        @pl.when(c == 0)
        def _():
            for q in range(3):
                dst[g * 3 + q] = jnp.zeros((LANES, LANES), F32)

        @pl.when((c == 0) & (g == 0))
        def _():
            ddtb[...] = jnp.zeros_like(ddtb)
            dalog[...] = jnp.zeros_like(dalog)
            ddsk[...] = jnp.zeros_like(ddsk)
            dng[...] = jnp.zeros_like(dng)

        @pl.when(g == 0)
        def _():
            ddt[...] = jnp.zeros_like(ddt)

        olds = [sv[0, 0, q] for q in range(3)]
        _, vjp = jax.vjp(functools.partial(_ssd_chunk, g), xs[...], bm[...], cm[...], z[...], dtr[...], dtb[...], alog[...],
                         dsk[...], ng[...], *olds)
        gr = vjp((dy_ref[...], dst[g * 3], dst[g * 3 + 1], dst[g * 3 + 2]))
        dxs[...], dbm[...], dcm[...], dz[...] = gr[0], gr[1], gr[2], gr[3]
        ddt[...] += gr[4]
        ddtb[...] += gr[5]
        dalog[...] += gr[6]
        ddsk[...] += gr[7]
        dng[g] += gr[8]
        for q in range(3):
            dst[g * 3 + q] = gr[9 + q]

    const = lambda shape: pl.BlockSpec(shape, lambda c, g: (0,) * len(shape))
    return pl.pallas_call(
        body, grid=(nc, 4), in_specs=in_specs,
        out_specs=[pl.BlockSpec((SSD_CHUNK, gw), lambda c, g: (cidx(c), g)),
                   pl.BlockSpec((SSD_CHUNK, LANES), lambda c, g: (cidx(c), g)),
                   pl.BlockSpec((SSD_CHUNK, LANES), lambda c, g: (cidx(c), g)),
                   pl.BlockSpec((SSD_CHUNK, gw), lambda c, g: (cidx(c), g)),
                   pl.BlockSpec((SSD_CHUNK, LANES), lambda c, g: (cidx(c), 0)),
                   const((1, LANES)), const((1, LANES)), const((1, LANES)), const((4, 1, gw))],
        out_shape=[jax.ShapeDtypeStruct((s, 12 * LANES), F32), jax.ShapeDtypeStruct((s, 4 * LANES), F32),
                   jax.ShapeDtypeStruct((s, 4 * LANES), F32), jax.ShapeDtypeStruct((s, 12 * LANES), F32),
                   jax.ShapeDtypeStruct((s, LANES), F32), jax.ShapeDtypeStruct((1, LANES), F32),
                   jax.ShapeDtypeStruct((1, LANES), F32), jax.ShapeDtypeStruct((1, LANES), F32),
                   jax.ShapeDtypeStruct((4, 1, gw), F32)],
        scratch_shapes=[pltpu.VMEM((12, LANES, LANES), F32)],
        compiler_params=_cparams("arbitrary", "arbitrary"), name=name)(
            conv, conv, conv, proj, proj, dt_bias, a_log, dskip, norm_g, saved, dy)


MXU_TILE = 256
RWKV_GROUP = 8


def _rwkv_consts():
    lanes = 16 * HEAD
    hl = jnp.arange(lanes) // HEAD
    e = (jnp.arange(16)[:, None] == hl[None, :]).astype(BF16)
    return jnp.tile(e, (3, 1)), e.T, _head_ones(MXU_TILE)


def _head_sums(x, j):
    n = x.shape[0]
    x4 = jnp.concatenate([x[:, i * MXU_TILE:(i + 1) * MXU_TILE] for i in range(4)], axis=0)
    hi = x4.astype(BF16)
    mid = (x4 - hi.astype(F32)).astype(BF16)
    out = jnp.dot(jnp.concatenate([hi, mid], axis=0), j, preferred_element_type=F32)
    s4 = out[:4 * n] + out[4 * n:]
    return jnp.concatenate([s4[i * n:(i + 1) * n] for i in range(4)], axis=1)


def _fold8(x):
    return jnp.sum(x.reshape(x.shape[0] // 8, 8, x.shape[1]), axis=0)


def _split3(x):
    hi = x.astype(BF16)
    r1 = x - hi.astype(F32)
    mid = r1.astype(BF16)
    return jnp.concatenate([hi, mid, (r1 - mid.astype(F32)).astype(BF16)], axis=-1)


def _rwkv_expand(src3, dst, e3, t_):
    for g0 in range(0, t_, RWKV_GROUP):
        n = min(RWKV_GROUP, t_ - g0)
        flat = src3[g0:g0 + n].reshape(n * HEAD, e3.shape[0])
        dst[g0:g0 + n] = jnp.dot(flat, e3, preferred_element_type=F32).reshape(n, HEAD, e3.shape[1])


def _rwkv_reduce(src, dst3, et, t_):
    for g0 in range(0, t_, RWKV_GROUP):
        n = min(RWKV_GROUP, t_ - g0)
        x = src[g0:g0 + n].reshape(n * HEAD, et.shape[0])
        hi = x.astype(BF16)
        mid = (x - hi.astype(F32)).astype(BF16)
        out = jnp.dot(hi, et, preferred_element_type=F32) + jnp.dot(mid, et, preferred_element_type=F32)
        dst3[g0:g0 + n] = out.reshape(n, HEAD, 16)


def _rwkv_fwd(w, kk, b, k, fp, v3, *, name, carry=None):
    s, lanes = w.shape
    t_ = min(RWKV_CHUNK, s)
    nc = s // t_
    e, et, j = _rwkv_consts()
    rowspec = pl.BlockSpec((t_, lanes), lambda c: (c, 0))
    cspec = lambda a: pl.BlockSpec(a.shape, lambda c: (0, 0))

    def body(w_ref, kk_ref, b_ref, k_ref, r_ref, v_ref, e_ref, et_ref, j_ref, y_ref, sv_ref, st, vm, zz):
        c = pl.program_id(0)

        @pl.when(c == 0)
        def _():
            st[...] = jnp.zeros_like(st)

        sv_ref[0] = st[...]
        jv = j_ref[...]
        _rwkv_expand(v_ref, vm, e_ref[...], t_)

        def step(t, sm):
            row = lambda ref: ref[pl.ds(t, 1), :]
            sa = _head_sums(sm * (-row(kk_ref)), jv)
            sn = sm * row(w_ref) + sa * row(b_ref) + vm[t] * row(k_ref)
            zz[t] = sn * row(r_ref)
            return sn

        st[...] = lax.fori_loop(0, t_, step, st[...])
        _rwkv_reduce(zz, y_ref, et_ref[...], t_)

    return _call_with_exchange(
        body, nc, carry, [w, kk, b, k, fp, v3, e, et, j],
        [rowspec] * 5 + [pl.BlockSpec((t_, HEAD, 48), lambda c: (c, 0, 0)), cspec(e), cspec(et), cspec(j)],
        [pl.BlockSpec((t_, HEAD, 16), lambda c: (c, 0, 0)), pl.BlockSpec((1, HEAD, lanes), lambda c: (c, 0, 0))],
        [jax.ShapeDtypeStruct((s, HEAD, 16), F32), jax.ShapeDtypeStruct((nc, HEAD, lanes), F32)],
        [pltpu.VMEM((HEAD, lanes), F32), pltpu.VMEM((t_, HEAD, lanes), F32), pltpu.VMEM((t_, HEAD, lanes), F32)], name)


def _call_with_exchange(body, n_steps, carry, ins, in_specs, out_specs, out_shape, scratch, name):
    if carry is not None:
        src, gather = carry
        n_in, n_out, inner = len(ins), len(out_shape), body

        def body(*refs):
            ex = _Exchange(refs[n_in], refs[n_in + 1 + n_out], *refs[-3:], gather)
            step = pl.program_id(0)

            @pl.when(step == 0)
            def _():
                ex.start()

            inner(*refs[:n_in], *refs[n_in + 1:n_in + 1 + n_out], *refs[n_in + 2 + n_out:-3])

            @pl.when(step == n_steps - 1)
            def _():
                ex.wait()

        hbm = pl.BlockSpec(memory_space=pl.ANY)
        ins, in_specs, out_specs = ins + [src], in_specs + [hbm], out_specs + [hbm]
        out_shape = out_shape + [jax.ShapeDtypeStruct((N_DEV, src.shape[-2], LANES), src.dtype)]
        scratch = scratch + list(_Exchange.SCRATCH)
    return pl.pallas_call(body, grid=(n_steps,), in_specs=in_specs, out_specs=out_specs, out_shape=out_shape,
                          scratch_shapes=scratch, compiler_params=_cparams("arbitrary"), name=name)(*ins)


def _rwkv_bwd(w, kk, b, k, fp, v3, saved, dy3, *, name, carry=None):
    s, lanes = w.shape
    t_ = min(RWKV_CHUNK, s)
    nc = s // t_
    e, et, j = _rwkv_consts()
    rev = lambda c: nc - 1 - c
    rowspec = pl.BlockSpec((t_, lanes), lambda c: (rev(c), 0))
    v3spec = pl.BlockSpec((t_, HEAD, 16), lambda c: (rev(c), 0, 0))
    s3spec = pl.BlockSpec((t_, HEAD, 48), lambda c: (rev(c), 0, 0))
    cspec = lambda a: pl.BlockSpec(a.shape, lambda c: (0, 0))

    def body(w_ref, kk_ref, b_ref, k_ref, r_ref, v_ref, sv_ref, dy_ref, e_ref, et_ref, j_ref,
             dw_ref, dkk_ref, db_ref, dk_ref, dr_ref, dv_ref, dst, h_sm, h_sa, vm, dz, pw, pkk, pb, pk, pr):
        c = pl.program_id(0)

        @pl.when(c == 0)
        def _():
            dst[...] = jnp.zeros_like(dst)

        jv = j_ref[...]
        row = lambda ref, t: ref[pl.ds(t, 1), :]
        _rwkv_expand(v_ref, vm, e_ref[...], t_)
        _rwkv_expand(dy_ref, dz, e_ref[...], t_)

        def replay(t, sm):
            h_sm[t] = sm
            sa = _head_sums(sm * (-row(kk_ref, t)), jv)
            h_sa[t] = sa
            return sm * row(w_ref, t) + sa * row(b_ref, t) + vm[t] * row(k_ref, t)

        h_sm[t_] = lax.fori_loop(0, t_, replay, sv_ref[0])

        def back(i, dcarry):
            t = t_ - 1 - i
            sm, sa, dzt = h_sm[t], h_sa[t], dz[t]
            dsn = dcarry + dzt * row(r_ref, t)
            pr[t] = _fold8(dzt * h_sm[t + 1])
            pw[t] = _fold8(dsn * sm)
            pb[t] = _fold8(dsn * sa)
            pk[t] = _fold8(dsn * vm[t])
            vm[t] = dsn * row(k_ref, t)
            dx = _head_sums(dsn * row(b_ref, t), jv)
            pkk[t] = _fold8(dx * sm)
            return dsn * row(w_ref, t) - dx * row(kk_ref, t)

        dst[...] = lax.fori_loop(0, t_, back, dst[...])
        _rwkv_reduce(vm, dv_ref, et_ref[...], t_)
        dw_ref[...] = jnp.sum(pw[...], axis=1)
        dkk_ref[...] = -jnp.sum(pkk[...], axis=1)
        db_ref[...] = jnp.sum(pb[...], axis=1)
        dk_ref[...] = jnp.sum(pk[...], axis=1)
        dr_ref[...] = jnp.sum(pr[...], axis=1)

    big = lambda n: pltpu.VMEM((n, HEAD, lanes), F32)
    part = pltpu.VMEM((t_, 8, lanes), F32)
    return _call_with_exchange(
        body, nc, carry, [w, kk, b, k, fp, v3, saved, dy3, e, et, j],
        [rowspec] * 5 + [s3spec, pl.BlockSpec((1, HEAD, lanes), lambda c: (rev(c), 0, 0)), s3spec, cspec(e), cspec(et), cspec(j)],
        [rowspec] * 5 + [v3spec],
        [jax.ShapeDtypeStruct((s, lanes), F32)] * 5 + [jax.ShapeDtypeStruct((s, HEAD, 16), F32)],
        [pltpu.VMEM((HEAD, lanes), F32), big(t_ + 1), big(t_), big(t_), big(t_)] + [part] * 5, name)


def _blockdiag(blocks):
    g, _, b = blocks.shape
    return jnp.concatenate([jnp.pad(blocks[i], ((0, 0), (i * b, (g - 1 - i) * b))) for i in range(g)], axis=0)


def _blockdiag_t(dense, g):
    a, b = dense.shape[0] // g, dense.shape[1] // g
    return jnp.stack([dense[i * a:(i + 1) * a, i * b:(i + 1) * b] for i in range(g)])


def _pad_cols(x, n):
    return jnp.pad(x, ((0, 0), (0, n - x.shape[1])))


def _pad_rows(x, n):
    return jnp.pad(x, ((0, n - x.shape[0]), (0, 0)))


E_PROJ = 5120


def _even_in_cols(w):
    return jnp.concatenate([w[:, 512:2048], w[:, 0:512], w[:, 2048:4632], jnp.zeros((w.shape[0], E_PROJ - 4632), w.dtype)], axis=1)


def _even_in_cols_t(dw):
    return jnp.concatenate([dw[:, 1536:2048], dw[:, 0:1536], dw[:, 2048:4632]], axis=1)


O_PROJ = 5632


def _odd_in_cols(w):
    z32 = jnp.zeros((w.shape[0], 32), w.dtype)
    return jnp.concatenate([w[:, 0:3072], w[:, 3520:5568], w[:, 3264:3520], w[:, 3072:3168], z32, w[:, 3168:3264], z32], axis=1)


def _odd_in_cols_t(dw):
    return jnp.concatenate([dw[:, 0:3072], dw[:, 5376:5472], dw[:, 5504:5600], dw[:, 5120:5376], dw[:, 3072:5120]], axis=1)


def _mu_cols(mu):
    z32 = jnp.zeros((1, 32), mu.dtype)
    return jnp.concatenate([mu[:, 0:3072], mu[:, 3264:3520], mu[:, 3072:3168], z32, mu[:, 3168:3264], z32], axis=1)


def _mu_cols_t(d):
    return jnp.concatenate([d[:, 0:3072], d[:, 3328:3424], d[:, 3456:3552], d[:, 3072:3328]], axis=1)


def _conv_taps(x, w, b):
    y = b + w[3:4] * x
    for k in range(3):
        y = y + w[k:k + 1] * _shift_down(x, 3 - k)
    return y


def _conv_silu(x, w, b):
    y = _conv_taps(x, w, b)
    return y * jax.nn.sigmoid(y)


def _tshift(x, mu):
    return x + (_shift_down(x, 1) - x) * mu


def _relu2(_, a):
    r = jnp.maximum(a, 0.0)
    return (r * r,)


def _pl_gate(_, h, gl, e):
    return (h + jax.nn.sigmoid(gl) * e,)


def _s5_param(lr, li, ls, br, bi):
    step = jnp.exp(ls)
    mag = jnp.exp(lr * step)
    ar, ai = mag * jnp.cos(li * step), mag * jnp.sin(li * step)
    den = lr * lr + li * li
    nr = ar - 1.0
    cr = (nr * lr + ai * li) / den
    ci = (ai * lr - nr * li) / den
    return ar, ai, cr * br - ci * bi, cr * bi + ci * br


def _s5_post(_, ylin, u, d, gw, gb):
    act = jax.nn.gelu(ylin + d * u)
    return (act * jax.nn.sigmoid(_dot_bf16(act, gw) + gb),)


def _rwkv_pre(_, k, gl, wl, al, w0, w_up, a0, a_up, g_up, k_k, k_a, j):
    w = -_softplus(-(w0 + _dot_bf16(jnp.tanh(wl), w_up))) - 0.5
    decay = jnp.exp(-jnp.exp(w))
    a = jax.nn.sigmoid(a0 + _dot_bf16(al, a_up))
    g = _dot_bf16(jax.nn.sigmoid(gl), g_up)
    kk = k * k_k
    k2 = k * (1.0 + (a - 1.0) * k_a)
    kkn = kk * lax.rsqrt(jnp.maximum(_lin(kk * kk, j, j), 1e-24))
    return decay, kkn, kkn * a, k2, g


def _rwkv_post(_, y, r, k2, v, g, r_k, ln_g, ln_b, j):
    mean = _lin(y, j, j) * (1.0 / HEAD)
    yc = y - mean
    var = _lin(yc * yc, j, j) * (1.0 / HEAD)
    yn = yc * lax.rsqrt(var + RWKV_GN_EPS) * ln_g + ln_b
    return ((yn + _lin(r * k2 * r_k, j, j) * v) * g,)


def _lru_pre(row0, pre, xc, bax, lam):
    n = xc.shape[1]
    gr = jax.nn.sigmoid(pre[:, :n] + bax[:, :n])
    gi = jax.nn.sigmoid(pre[:, n:] + bax[:, n:])
    log_a = -LRU_C * gr * _softplus(-lam)
    m2 = -jnp.tanh(log_a) * (jnp.exp(2.0 * log_a) + 1.0)
    mult = jnp.sqrt(jnp.maximum(m2, 0.0))
    rowid = row0 + lax.broadcasted_iota(jnp.int32, (xc.shape[0], 1), 0)
    mult = jnp.where(rowid == 0, 1.0, mult)
    return jnp.exp(log_a), xc * gi * mult


def _lru_post(_, h, gl2):
    return (h * jax.nn.gelu(gl2),)


def _even_prep(w):
    sp = (w["s5_lam_re"].reshape(32, 64), w["s5_lam_im"].reshape(32, 64), w["s5_log_step"].reshape(32, 1),
          w["s5_b_re"].reshape(32, 64, 16).transpose(2, 0, 1), w["s5_b_im"].reshape(32, 64, 16).transpose(2, 0, 1))
    ar, ai, bbr, bbi = _block_vjp(_s5_param, sp, None, name="s5_param")
    bblk = lambda bb: _blockdiag(bb.transpose(1, 0, 2))
    cblk = lambda c: _blockdiag(c.reshape(32, 16, 64).transpose(0, 2, 1))
    pad = lambda x: _pad_cols(x.reshape(1, 24), LANES)
    return dict(
        sp=sp, a_row=jnp.concatenate([ar.reshape(1, 2048), ai.reshape(1, 2048)], axis=1),
        b_re=bblk(bbr), b_im=bblk(bbi), c_re=cblk(w["s5_c_re"]), c_imn=-cblk(w["s5_c_im"]),
        d=w["s5_d"].reshape(1, 512), gw=w["s5_glu_w"].reshape(512, 512), gb=w["s5_glu_b"].reshape(1, 512),
        conv_w=w["ssd_conv_w"].reshape(4, 2560), conv_b=w["ssd_conv_b"].reshape(1, 2560),
        dt_bias=pad(w["ssd_dt_bias"]), a_log=pad(w["ssd_a_log"]), dskip=pad(w["ssd_d"]), norm=w["ssd_norm"].reshape(1, 1536))


_E_XMAP = lambda j: 16 + j


def _even_fwd(proj, p, carry=None):
    assert carry is None
    u = proj[:, 1536:2048]
    bur = _matmul(u, p["b_re"], "nn", name="s5_bu_re")
    bui = _matmul(u, p["b_im"], "nn", name="s5_bu_im")
    xr, xi = _cscan_fwd(jnp.concatenate([bur, bui], axis=1), p["a_row"], name="s5_scan")
    ylin = _matmul(xi, p["c_imn"], "nn", name="s5_y_im", add=_matmul(xr, p["c_re"], "nn", name="s5_y_re"))
    (ya,) = _rw_fwd(_s5_post, [_row(ylin), _row(u)], [p["d"], p["gw"], p["gb"]], [(512, F32)], name="s5_post")
    conv = _ct_fwd(_conv_silu, proj, _E_XMAP, 20, [(p["conv_w"], 0), (p["conv_b"], 0)], 2560, name="ssd_conv")
    yb, saved = _ssd_fwd(conv, proj, p["dt_bias"], p["a_log"], p["dskip"], p["norm"], name="ssd_scan")
    return jnp.concatenate([ya, yb], axis=1), (u, xr, xi, ylin, conv, saved), None


def _even_bwd(proj, p, res, dy, carry=None):
    assert carry is None
    u, xr, xi, ylin, conv, saved = res
    s = proj.shape[0]
    dxs, dbm, dcm, dz, ddt, ddtb, dalog, ddsk, dng = _ssd_bwd(
        conv, proj, p["dt_bias"], p["a_log"], p["dskip"], p["norm"], saved, dy[:, 512:], name="ssd_scan_bwd")
    dxbc, (dcw, dcb) = _ct_bwd(_conv_silu, proj, _E_XMAP, 20, [(p["conv_w"], 0), (p["conv_b"], 0)],
                               [jnp.concatenate([dxs, dbm, dcm], axis=1)], name="ssd_conv_bwd")
    (dylin, du), (dd, dgw, dgb) = _rw_bwd(_s5_post, [_row(ylin), _row(u)], [p["d"], p["gw"], p["gb"]], [dy[:, :512]], name="s5_post_bwd")
    dxr = _matmul(dylin, p["c_re"], "nt", name="s5_dxr")
    dxi = _matmul(dylin, p["c_imn"], "nt", name="s5_dxi")
    dc_re = _matmul(xr, dylin, "tn", name="s5_dc_re")
    dc_imn = _matmul(xi, dylin, "tn", name="s5_dc_im")
    dbr, dbi, dar, dai = _cscan_bwd(dxr, dxi, xr, xi, p["a_row"], name="s5_scan_bwd")
    du = _matmul(dbr, p["b_re"], "nt", name="s5_du_re", add=du)
    du = _matmul(dbi, p["b_im"], "nt", name="s5_du_im", add=du)
    db_re = _matmul(u, dbr, "tn", name="s5_db_re")
    db_im = _matmul(u, dbi, "tn", name="s5_db_im")
    unblk = lambda d: _blockdiag_t(d, 32).transpose(1, 0, 2)
    g_sp = _block_vjp(_s5_param, p["sp"], (dar.reshape(32, 64), dai.reshape(32, 64), unblk(db_re), unblk(db_im)), name="s5_param_bwd")
    dproj = jnp.concatenate([dz, du, dxbc, ddt, jnp.zeros((s, E_PROJ - 4736), F32)], axis=1)
    uncblk = lambda d: _blockdiag_t(d, 32).transpose(0, 2, 1)
    grads = dict(
        s5_lam_re=g_sp[0].reshape(1, 32, 64), s5_lam_im=g_sp[1].reshape(1, 32, 64), s5_log_step=g_sp[2].reshape(1, 32),
        s5_b_re=g_sp[3].transpose(1, 2, 0)[None], s5_b_im=g_sp[4].transpose(1, 2, 0)[None],
        s5_c_re=uncblk(dc_re)[None], s5_c_im=-uncblk(dc_imn)[None], s5_d=dd, s5_glu_w=dgw[None], s5_glu_b=dgb,
        ssd_conv_w=dcw[None], ssd_conv_b=dcb, ssd_dt_bias=ddtb[:, :24], ssd_a_log=dalog[:, :24], ssd_d=ddsk[:, :24],
        ssd_norm=dng.reshape(1, 1536))
    return dproj, grads, None


def _odd_prep(w):
    pad128 = lambda x: _pad_rows(x, LANES)
    return dict(
        mu=_mu_cols(w["rwkv_mu"].reshape(1, 3520)), w0=w["rwkv_w0"].reshape(1, 1024), w_up=pad128(w["rwkv_w_up"].reshape(96, 1024)),
        a0=w["rwkv_a0"].reshape(1, 1024), a_up=pad128(w["rwkv_a_up"].reshape(96, 1024)), g_up=w["rwkv_g_up"].reshape(256, 1024),
        k_k=w["rwkv_k_k"].reshape(1, 1024), k_a=w["rwkv_k_a"].reshape(1, 1024), r_k=w["rwkv_r_k"].reshape(1, 1024),
        ln_g=w["rwkv_ln_g"].reshape(1, 1024), ln_b=w["rwkv_ln_b"].reshape(1, 1024), j=_head_ones(1024),
        conv_w=w["lru_conv_w"].reshape(4, 1024), conv_b=w["lru_conv_b"].reshape(1, 1024),
        wax=jnp.concatenate([_blockdiag(w["lru_w_a"].reshape(16, 64, 64)), _blockdiag(w["lru_w_x"].reshape(16, 64, 64))], axis=1),
        bax=jnp.concatenate([w["lru_b_a"].reshape(1, 1024), w["lru_b_x"].reshape(1, 1024)], axis=1), lam=w["lru_lam"].reshape(1, 1024))


_O_XMAP = lambda j: jnp.where(j < 24, j, j + 16)
_O_LMAP = lambda j: 24 + j


def _to_heads(x):
    return x.reshape(x.shape[0], 16, HEAD).transpose(0, 2, 1)


def _from_heads(x3):
    return x3.transpose(0, 2, 1).reshape(x3.shape[0], 16 * HEAD)


def _odd_rows(fp, y, k2, g):
    pre = [_row(fp, 1024, 1), _row(fp, 256, 12), _row(fp, 128, 26), _row(fp, 128, 27)]
    post = None if y is None else [_row(y), _row(fp, 1024, 0), _row(k2), _row(fp, 1024, 2), _row(g)]
    return pre, post


def _odd_fwd(proj, p, carry=None):
    fp = _ct_fwd(_tshift, proj, _O_XMAP, 28, [(p["mu"], 0)], 3584, name="rwkv_shift")
    pre_rows, _ = _odd_rows(fp, None, None, None)
    pre_params = [p["w0"], p["w_up"], p["a0"], p["a_up"], p["g_up"], p["k_k"], p["k_a"], p["j"]]
    decay, kkn, b, k2, g = _rw_fwd(_rwkv_pre, pre_rows, pre_params, [(1024, F32)] * 5, name="rwkv_pre")
    v3 = _split3(_to_heads(fp[:, 2048:3072]))
    y3, saved, *carried = _rwkv_fwd(decay, kkn, b, k2, fp, v3, name="rwkv_scan", carry=carry)
    y = _from_heads(y3)
    _, post_rows = _odd_rows(fp, y, k2, g)
    (yc,) = _rw_fwd(_rwkv_post, post_rows, [p["r_k"], p["ln_g"], p["ln_b"], p["j"]], [(1024, F32)], name="rwkv_post")
    xc = _ct_fwd(_conv_taps, proj, _O_LMAP, 8, [(p["conv_w"], 0), (p["conv_b"], 0)], 1024, name="lru_conv")
    pre = _matmul(xc, p["wax"], "nn", name="lru_gates")
    a, bx = _rw_fwd(_lru_pre, [_row(pre), _row(xc)], [p["bax"], p["lam"]], [(1024, F32)] * 2, name="lru_pre")
    hseq = _rscan_fwd(a, bx, name="lru_scan")
    (yd,) = _rw_fwd(_lru_post, [_row(hseq), _row(proj, 1024, 4)], [], [(1024, F32)], name="lru_post")
    return jnp.concatenate([yc, yd], axis=1), (fp, decay, kkn, b, k2, g, v3, saved, y, xc, pre, a, hseq), (carried[0] if carried else None)


def _odd_bwd(proj, p, res, dy, carry=None):
    fp, decay, kkn, b, k2, g, v3, saved, y, xc, pre, a, hseq = res
    s = proj.shape[0]
    (dh, dgl2), _ = _rw_bwd(_lru_post, [_row(hseq), _row(proj, 1024, 4)], [], [dy[:, 1024:]], name="lru_post_bwd")
    dbx, da = _rscan_bwd(dh, a, hseq, name="lru_scan_bwd")
    (dpre, dxc), (dbax, dlam) = _rw_bwd(_lru_pre, [_row(pre), _row(xc)], [p["bax"], p["lam"]], [da, dbx], name="lru_pre_bwd")
    dxc = _matmul(dpre, p["wax"], "nt", name="lru_gates_dx", add=dxc)
    dwax = _matmul(xc, dpre, "tn", name="lru_gates_dw")
    dxl, (dlcw, dlcb) = _ct_bwd(_conv_taps, proj, _O_LMAP, 8, [(p["conv_w"], 0), (p["conv_b"], 0)], [dxc], name="lru_conv_bwd")
    pre_rows, post_rows = _odd_rows(fp, y, k2, g)
    (dyn, dr1, dk2a, dv1, dg), (dr_k, dln_g, dln_b) = _rw_bwd(
        _rwkv_post, post_rows, [p["r_k"], p["ln_g"], p["ln_b"], p["j"]], [dy[:, :1024]], name="rwkv_post_bwd", param_grads=[0, 1, 2])
    ddecay, dkkn, db, dk2b, dr2, dv3, *carried = _rwkv_bwd(
        decay, kkn, b, k2, fp, v3, saved, _split3(_to_heads(dyn)), name="rwkv_scan_bwd", carry=carry)
    pre_params = [p["w0"], p["w_up"], p["a0"], p["a_up"], p["g_up"], p["k_k"], p["k_a"], p["j"]]
    (dk, dgl, dwl, dal), (dw0, dw_up, da0, da_up, dg_up, dk_k, dk_a) = _rw_bwd(
        _rwkv_pre, pre_rows, pre_params, [ddecay, dkkn, db, [dk2a, dk2b], dg], name="rwkv_pre_bwd", param_grads=list(range(7)))
    z = lambda n: jnp.zeros((s, n), F32)
    g1 = jnp.concatenate([dr1, dk, dv1, dgl, dwl, dal], axis=1)
    g2 = jnp.concatenate([dr2, z(1024), _from_heads(dv3), z(512)], axis=1)
    dfp, (dmu,) = _ct_bwd(_tshift, proj, _O_XMAP, 28, [(p["mu"], 0)], [g1, g2], name="rwkv_shift_bwd")
    dproj = jnp.concatenate([dfp[:, :3072], dxl, dgl2, dfp[:, 3072:]], axis=1)
    grads = dict(
        rwkv_mu=_mu_cols_t(dmu), rwkv_w0=dw0, rwkv_w_up=dw_up[:96][None], rwkv_a0=da0, rwkv_a_up=da_up[:96][None], rwkv_g_up=dg_up[None],
        rwkv_k_k=dk_k, rwkv_k_a=dk_a, rwkv_r_k=dr_k.reshape(1, 16, 64), rwkv_ln_g=dln_g, rwkv_ln_b=dln_b,
        lru_conv_w=dlcw[None], lru_conv_b=dlcb, lru_w_a=_blockdiag_t(dwax[:, :1024], 16)[None], lru_w_x=_blockdiag_t(dwax[:, 1024:], 16)[None],
        lru_b_a=dbax[:, :1024].reshape(1, 16, 64), lru_b_x=dbax[:, 1024:].reshape(1, 16, 64), lru_lam=dlam.reshape(1, 16, 64))
    return dproj, grads, (carried[0] if carried else None)


def _my_index():
    return 4 * lax.axis_index("x") + 2 * lax.axis_index("y") + lax.axis_index("c")


def _peer(k):
    x, y, c = lax.axis_index("x"), lax.axis_index("y"), lax.axis_index("c")
    return (1 - x if k & 4 else x, 1 - y if k & 2 else y, 1 - c if k & 1 else c)


class _Exchange:
    SCRATCH = (pltpu.SemaphoreType.DMA((N_DEV - 1,)), pltpu.SemaphoreType.DMA((N_DEV - 1,)), pltpu.SemaphoreType.DMA)

    def __init__(self, src_ref, out_ref, send_sems, recv_sems, local_sem, gather):
        me = _my_index()
        mine = src_ref if gather else src_ref.at[me]
        self.local = pltpu.make_async_copy(mine, out_ref.at[me], local_sem)
        rdma = lambda src, dst, k: pltpu.make_async_remote_copy(
            src_ref=src, dst_ref=dst, send_sem=send_sems.at[k - 1], recv_sem=recv_sems.at[k - 1],
            device_id=_peer(k), device_id_type=pl.DeviceIdType.MESH)
        ks = range(1, N_DEV)
        self.sends = [rdma(src_ref if gather else src_ref.at[jnp.bitwise_xor(me, k)], out_ref.at[me], k) for k in ks]
        self.arrivals = [rdma(mine, out_ref.at[jnp.bitwise_xor(me, k)], k) for k in ks]

    def start(self):
        self.local.start()
        for cp in self.sends:
            cp.start()

    def wait(self):
        for cp in self.arrivals:
            cp.wait_recv()
        for cp in self.sends:
            cp.wait_send()
        self.local.wait()


def _exchange(src, *, gather, name):
    def body(src_ref, out_ref, *sems):
        ex = _Exchange(src_ref, out_ref, *sems, gather)
        ex.start()
        ex.wait()

    return pl.pallas_call(
        body, out_shape=jax.ShapeDtypeStruct((N_DEV, src.shape[-2], LANES), src.dtype),
        in_specs=[pl.BlockSpec(memory_space=pl.ANY)], out_specs=pl.BlockSpec(memory_space=pl.ANY),
        scratch_shapes=list(_Exchange.SCRATCH), name=name)(src)


PACK_ALIGN = 16 * LANES
PACK_ROWS = 512


def _pack(arrs, dtype, lead=False):
    parts, rows = [], 0
    for a in arrs:
        n_lead = a.shape[0] if lead else 1
        n = a.size // n_lead
        a = a.astype(dtype)
        if n % PACK_ALIGN:
            a = jnp.pad(a.reshape(n_lead, n), ((0, 0), (0, -n % PACK_ALIGN)))
        parts.append(a.reshape(n_lead, -1, LANES))
        rows += parts[-1].shape[1]
    if rows % PACK_ROWS:
        parts.append(jnp.zeros((parts[0].shape[0], -rows % PACK_ROWS, LANES), dtype))
    buf = jnp.concatenate(parts, axis=1)
    return buf if lead else buf[0]


def _unpack(buf, shapes, lead=False):
    buf = buf if lead else buf[None]
    out, off = [], 0
    for shp in shapes:
        n = math.prod(shp)
        rows = (n + (-n % PACK_ALIGN)) // LANES
        piece = buf[:, off:off + rows]
        if n % PACK_ALIGN:
            piece = piece.reshape(buf.shape[0], rows * LANES)[:, :n]
        out.append(piece.reshape(((buf.shape[0],) if lead else ()) + tuple(shp)))
        off += rows
    return out


def _unshard(parts, axis):
    moved = jnp.moveaxis(parts, 0, axis)
    shp = list(moved.shape)
    return moved.reshape(shp[:axis] + [shp[axis] * shp[axis + 1]] + shp[axis + 2:])


def _to_parts(full, axis):
    shp = list(full.shape)
    split = full.reshape(shp[:axis] + [N_DEV, shp[axis] // N_DEV] + shp[axis + 1:])
    return jnp.moveaxis(split, axis, 0)


def _adamw(gparts, w, m, v, *, name):
    r = w.shape[0]
    tr = _pick(r, (PACK_ROWS,))

    def body(g_ref, w_ref, m_ref, v_ref, go, do, mo, vo):
        g = g_ref[0].astype(F32)
        for d in range(1, N_DEV):
            g = g + g_ref[d].astype(F32)
        m1 = ADAM_B1 * m_ref[...] + (1.0 - ADAM_B1) * g
        v1 = ADAM_B2 * v_ref[...] + (1.0 - ADAM_B2) * jnp.square(g)
        m_hat = m1 / (1.0 - ADAM_B1 ** ADAM_STEP)
        v_hat = v1 / (1.0 - ADAM_B2 ** ADAM_STEP)
        go[...] = g
        do[...] = -ADAM_LR * (m_hat / (jnp.sqrt(v_hat) + ADAM_EPS) + ADAM_WD * w_ref[...])
        mo[...] = m1
        vo[...] = v1

    blk = pl.BlockSpec((tr, LANES), lambda i: (i, 0))
    return pl.pallas_call(
        body, grid=(r // tr,), in_specs=[pl.BlockSpec((N_DEV, tr, LANES), lambda i: (0, i, 0)), blk, blk, blk], out_specs=[blk] * 4,
        out_shape=[jax.ShapeDtypeStruct((r, LANES), F32)] * 4, compiler_params=_cparams("parallel"), name=name)(gparts, w, m, v)


def _loss_and_grad(h, g, tgt, *, name, ts=256):
    s, d = h.shape
    ts = min(ts, s)

    def tile_loss(hv, gv, tv):
        (y,) = _rmsnorm_tile(0, hv, gv)
        return 0.5 * jnp.sum(jnp.mean(jnp.square(y - tv), axis=-1))

    def body(h_ref, g_ref, t_ref, l_ref, dh_ref, dg_ref):
        @pl.when(pl.program_id(0) == 0)
        def _():
            l_ref[...] = jnp.zeros_like(l_ref)
            dg_ref[...] = jnp.zeros_like(dg_ref)

        tv = t_ref[...]
        loss, vjp = jax.vjp(lambda hv, gv: tile_loss(hv, gv, tv), h_ref[...], g_ref[...])
        dh, dg = vjp(jnp.ones((), F32))
        l_ref[...] += loss
        dh_ref[...] = dh
        dg_ref[...] += dg

    row = pl.BlockSpec((ts, d), lambda i: (i, 0))
    return pl.pallas_call(
        body, grid=(s // ts,), in_specs=[row, pl.BlockSpec((1, d), lambda i: (0, 0)), row],
        out_specs=[pl.BlockSpec((8, LANES), lambda i: (0, 0)), row, pl.BlockSpec((1, d), lambda i: (0, 0))],
        out_shape=[jax.ShapeDtypeStruct((8, LANES), F32), jax.ShapeDtypeStruct((s, d), F32), jax.ShapeDtypeStruct((1, d), F32)],
        compiler_params=_cparams("arbitrary"), name=name)(h, g, tgt)


def _norm(h, g, name):
    return _rw_fwd(_rmsnorm_tile, [_row(h)], [g], [(h.shape[1], F32)], name=name)[0]


def _norm_bwd(h, g, dhn, dres, name):
    (dh,), (dg,) = _rw_bwd(_rmsnorm_tile, [_row(h)], [g], [dhn], add_rows={0: dres}, name=name)
    return dh, dg


def _layer_fwd(h, p_i, lw, mixer_fwd, mp, tag, carry=None, late=None):
    hn = _norm(h, lw["norm_mix"], f"{tag}_norm_mix")
    proj = _matmul(hn, lw["w_in"], "nn", name=f"{tag}_in_proj")
    y, mres, carried = mixer_fwd(proj, mp, carry)
    if late is not None:
        lw = {**lw, **late(carried)}
    h1 = _matmul(y, lw["w_out"], "nn", name=f"{tag}_out_proj", add=h)
    hn2 = _norm(h1, lw["norm_ffn"], f"{tag}_norm_ffn")
    a1 = _matmul(hn2, lw["w1"], "nn", name=f"{tag}_mlp_up")
    (act,) = _rw_fwd(_relu2, [_row(a1)], [], [(a1.shape[1], F32)], name=f"{tag}_relu2", ts=64)
    h2 = _matmul(act, lw["w2"], "nn", name=f"{tag}_mlp_down", add=h1)
    hn3 = _norm(h2, lw["norm_pl"], f"{tag}_norm_pl")
    gl = _matmul(hn3, lw["w_gate"], "nn", name=f"{tag}_pl_gate")
    e = _matmul(p_i, lw["w_pl"], "nn", name=f"{tag}_pl_proj")
    (h3,) = _rw_fwd(_pl_gate, [_row(h2), _row(gl), _row(e)], [], [(h.shape[1], F32)], name=f"{tag}_pl_mix")
    return h3, (h, hn, proj, y, mres, h1, hn2, a1, act, h2, hn3, gl, e), lw


def _layer_bwd(dh3, p_i, lw, mixer_bwd, mp, saved, tag, early=None):
    h, hn, proj, y, mres, h1, hn2, a1, act, h2, hn3, gl, e = saved
    (dgl, de), _ = _rw_bwd(_pl_gate, [_row(h2), _row(gl), _row(e)], [], [dh3], row_grads=[1, 2], name=f"{tag}_pl_mix_bwd")
    g = dict(w_pl=_matmul(p_i, de, "tn", name=f"{tag}_pl_proj_dw"), w_gate=_matmul(hn3, dgl, "tn", name=f"{tag}_pl_gate_dw"))
    dhn3 = _matmul(dgl, lw["w_gate"], "nt", name=f"{tag}_pl_gate_dx")
    dh2, g["norm_pl"] = _norm_bwd(h2, lw["norm_pl"], dhn3, dh3, f"{tag}_norm_pl_bwd")
    dact = _matmul(dh2, lw["w2"], "nt", name=f"{tag}_mlp_down_dx")
    g["w2"] = _matmul(act, dh2, "tn", name=f"{tag}_mlp_down_dw")
    (da1,), _ = _rw_bwd(_relu2, [_row(a1)], [], [dact], name=f"{tag}_relu2_bwd", ts=64)
    g["w1"] = _matmul(hn2, da1, "tn", name=f"{tag}_mlp_up_dw")
    dhn2 = _matmul(da1, lw["w1"], "nt", name=f"{tag}_mlp_up_dx")
    dh1, g["norm_ffn"] = _norm_bwd(h1, lw["norm_ffn"], dhn2, dh2, f"{tag}_norm_ffn_bwd")
    dy = _matmul(dh1, lw["w_out"], "nt", name=f"{tag}_out_proj_dx")
    g["w_out"] = _matmul(y, dh1, "tn", name=f"{tag}_out_proj_dw")
    dproj, mg, carried = mixer_bwd(proj, mp, mres, dy, None if early is None else (early(g), False))
    g["w_in"] = _matmul(hn, dproj, "tn", name=f"{tag}_in_proj_dw")
    dhn = _matmul(dproj, lw["w_in"], "nt", name=f"{tag}_in_proj_dx")
    dh, g["norm_mix"] = _norm_bwd(h, lw["norm_mix"], dhn, dh1, f"{tag}_norm_mix_bwd")
    return dh, g, mg, carried


WEIGHTS = (
    ("norm_mix", None), ("norm_ffn", None), ("norm_pl", None), ("mlp_w1", 2), ("mlp_w2", 1), ("pl_proj", 2), ("pl_gate", 1),
    ("e_in_proj", 2), ("e_out_proj", 1), ("s5_lam_re", None), ("s5_lam_im", None), ("s5_log_step", None), ("s5_b_re", None),
    ("s5_b_im", None), ("s5_c_re", None), ("s5_c_im", None), ("s5_d", None), ("s5_glu_w", 1), ("s5_glu_b", None),
    ("ssd_conv_w", 2), ("ssd_conv_b", None), ("ssd_dt_bias", None), ("ssd_a_log", None), ("ssd_d", None), ("ssd_norm", None),
    ("o_in_proj", 2), ("o_out_proj", 1), ("rwkv_mu", 1), ("rwkv_w0", 1), ("rwkv_w_up", 2), ("rwkv_a0", 1), ("rwkv_a_up", 2),
    ("rwkv_g_up", 2), ("rwkv_k_k", 1), ("rwkv_k_a", 1), ("rwkv_r_k", None), ("rwkv_ln_g", 1), ("rwkv_ln_b", 1),
    ("lru_conv_w", 2), ("lru_conv_b", 1), ("lru_w_a", None), ("lru_b_a", None), ("lru_w_x", None), ("lru_b_x", None),
    ("lru_lam", None), ("norm_final", None))
MATMUL_WEIGHTS = ("mlp_w1", "mlp_w2", "pl_proj", "pl_gate", "e_in_proj", "e_out_proj", "s5_glu_w", "o_in_proj", "o_out_proj",
                  "rwkv_w_up", "rwkv_a_up", "rwkv_g_up")


LATE_WEIGHTS = (("mlp_w1", 1), ("mlp_w2", 1), ("pl_gate", 1), ("pl_proj", 1))
EARLY_GRADS = LATE_WEIGHTS + (("o_out_proj", 0),)


def kernel(x, p, norm_mix, norm_ffn, norm_pl, mlp_w1, mlp_w2, pl_proj, pl_gate, e_in_proj, e_out_proj, s5_lam_re, s5_lam_im, s5_log_step, s5_b_re, s5_b_im, s5_c_re, s5_c_im, s5_d, s5_glu_w, s5_glu_b, ssd_conv_w, ssd_conv_b, ssd_dt_bias, ssd_a_log, ssd_d, ssd_norm, o_in_proj, o_out_proj, rwkv_mu, rwkv_w0, rwkv_w_up, rwkv_a0, rwkv_a_up, rwkv_g_up, rwkv_k_k, rwkv_k_a, rwkv_r_k, rwkv_ln_g, rwkv_ln_b, lru_conv_w, lru_conv_b, lru_w_a, lru_b_a, lru_w_x, lru_b_x, lru_lam, norm_final, loss_target, m_norm_mix, m_norm_ffn, m_norm_pl, m_mlp_w1, m_mlp_w2, m_pl_proj, m_pl_gate, m_e_in_proj, m_e_out_proj, m_s5_lam_re, m_s5_lam_im, m_s5_log_step, m_s5_b_re, m_s5_b_im, m_s5_c_re, m_s5_c_im, m_s5_d, m_s5_glu_w, m_s5_glu_b, m_ssd_conv_w, m_ssd_conv_b, m_ssd_dt_bias, m_ssd_a_log, m_ssd_d, m_ssd_norm, m_o_in_proj, m_o_out_proj, m_rwkv_mu, m_rwkv_w0, m_rwkv_w_up, m_rwkv_a0, m_rwkv_a_up, m_rwkv_g_up, m_rwkv_k_k, m_rwkv_k_a, m_rwkv_r_k, m_rwkv_ln_g, m_rwkv_ln_b, m_lru_conv_w, m_lru_conv_b, m_lru_w_a, m_lru_b_a, m_lru_w_x, m_lru_b_x, m_lru_lam, m_norm_final, v_norm_mix, v_norm_ffn, v_norm_pl, v_mlp_w1, v_mlp_w2, v_pl_proj, v_pl_gate, v_e_in_proj, v_e_out_proj, v_s5_lam_re, v_s5_lam_im, v_s5_log_step, v_s5_b_re, v_s5_b_im, v_s5_c_re, v_s5_c_im, v_s5_d, v_s5_glu_w, v_s5_glu_b, v_ssd_conv_w, v_ssd_conv_b, v_ssd_dt_bias, v_ssd_a_log, v_ssd_d, v_ssd_norm, v_o_in_proj, v_o_out_proj, v_rwkv_mu, v_rwkv_w0, v_rwkv_w_up, v_rwkv_a0, v_rwkv_a_up, v_rwkv_g_up, v_rwkv_k_k, v_rwkv_k_a, v_rwkv_r_k, v_rwkv_ln_g, v_rwkv_ln_b, v_lru_conv_w, v_lru_conv_b, v_lru_w_a, v_lru_b_a, v_lru_w_x, v_lru_b_x, v_lru_lam, v_norm_final):
    a = dict(locals())
    d_model = x.shape[-1]
    row = lambda v: v.reshape(1, d_model)
    axis = dict(WEIGHTS)
    keys = [(n, i) for n, ax in WEIGHTS if ax is not None for i in range(a[n].shape[0])]
    shard = lambda pre, key: a[pre + key[0]][key[1]]
    piece_axis = lambda key: axis[key[0]] - 1

    def gathered(got, ks):
        parts = _unpack(got, [shard("", k).shape for k in ks], lead=True)
        return {k: _unshard(pt, piece_axis(k)) for k, pt in zip(ks, parts)}

    bf16_keys = [k for k in keys if k[0] in MATMUL_WEIGHTS and k not in LATE_WEIGHTS]
    f32_keys = [k for k in keys if k[0] not in MATMUL_WEIGHTS]
    full = gathered(_exchange(_pack([shard("", k) for k in bf16_keys], BF16), gather=True, name="gather_weights_bf16"), bf16_keys)
    full.update(gathered(_exchange(_pack([shard("", k) for k in f32_keys], F32), gather=True, name="gather_weights_f32"), f32_keys))
    by_name = {n: (a[n] if ax is None else full.get((n, 0))) for n, ax in WEIGHTS}

    def late(got):
        fl = gathered(got, LATE_WEIGHTS)
        return dict(w1=fl["mlp_w1", 1], w2=fl["mlp_w2", 1], w_gate=fl["pl_gate", 1], w_pl=fl["pl_proj", 1])

    lw0 = dict(norm_mix=row(norm_mix[0]), norm_ffn=row(norm_ffn[0]), norm_pl=row(norm_pl[0]), w_in=_even_in_cols(full["e_in_proj", 0]),
               w_out=full["e_out_proj", 0], w1=full["mlp_w1", 0], w2=full["mlp_w2", 0], w_gate=full["pl_gate", 0], w_pl=full["pl_proj", 0])
    lw1 = dict(norm_mix=row(norm_mix[1]), norm_ffn=row(norm_ffn[1]), norm_pl=row(norm_pl[1]), w_in=_odd_in_cols(full["o_in_proj", 0]),
               w_out=full["o_out_proj", 0])
    mp0, mp1 = _even_prep(by_name), _odd_prep(by_name)

    h1, saved0, lw0 = _layer_fwd(x[0], p[0, 0], lw0, _even_fwd, mp0, "l0")
    late_src = _pack([shard("", k) for k in LATE_WEIGHTS], BF16)
    h2, saved1, lw1 = _layer_fwd(h1, p[1, 0], lw1, _odd_fwd, mp1, "l1", carry=(late_src, True), late=late)
    loss_blk, dh, dg_final = _loss_and_grad(h2, row(norm_final), loss_target[0], name="loss")
    loss = lax.psum(loss_blk[0, 0], ("x", "y", "c"))

    def slabs(grad_of, ks):
        return _pack([_to_parts(grad_of[k], piece_axis(k)) for k in ks], BF16, lead=True)

    early_grads = lambda g: slabs({("mlp_w1", 1): g["w1"], ("mlp_w2", 1): g["w2"], ("pl_gate", 1): g["w_gate"],
                                   ("pl_proj", 1): g["w_pl"], ("o_out_proj", 0): g["w_out"]}, EARLY_GRADS)
    dh, g1, mg1, early_got = _layer_bwd(dh, p[1, 0], lw1, _odd_bwd, mp1, saved1, "l1", early=early_grads)
    dh, g0, mg0, _ = _layer_bwd(dh, p[0, 0], lw0, _even_bwd, mp0, saved0, "l0")

    main_keys = [k for k in keys if k not in EARLY_GRADS]
    piece_grad = {("mlp_w1", 0): g0["w1"], ("mlp_w2", 0): g0["w2"], ("pl_gate", 0): g0["w_gate"], ("pl_proj", 0): g0["w_pl"],
                  ("e_in_proj", 0): _even_in_cols_t(g0["w_in"]), ("e_out_proj", 0): g0["w_out"], ("o_in_proj", 0): _odd_in_cols_t(g1["w_in"])}
    mixer_grads = {**mg0, **mg1}
    for k in main_keys:
        if k not in piece_grad:
            piece_grad[k] = mixer_grads[k[0]].reshape(
                tuple((N_DEV if d == piece_axis(k) else 1) * n for d, n in enumerate(shard("", k).shape)))
    main_got = _exchange(slabs(piece_grad, main_keys), gather=False, name="scatter_grads")
    piece_out = {}
    for got, ks, tag in ((early_got, EARLY_GRADS, "early"), (main_got, main_keys, "main")):
        res = _adamw(got, *[_pack([shard(pre, k) for k in ks], F32) for pre in ("", "m_", "v_")], name=f"adamw_sharded_{tag}")
        for j in range(4):
            piece_out.update({(j, k): v for k, v in zip(ks, _unpack(res[j], [shard("", k).shape for k in ks]))})
    rp = [n for n, ax in WEIGHTS if ax is None]
    rp_grads = {**mixer_grads, "norm_final": dg_final,
                **{n: jnp.stack([g0[n], g1[n]]) for n in ("norm_mix", "norm_ffn", "norm_pl")}}
    parts = _exchange(_pack([rp_grads[n].reshape(a[n].shape) for n in rp], F32), gather=True, name="gather_small_grads")
    rp_res = _adamw(parts, *[_pack([a[pre + n] for n in rp], F32) for pre in ("", "m_", "v_")], name="adamw_replicated")

    outs = []
    for j in range(4):
        rp_out = dict(zip(rp, _unpack(rp_res[j], [a[n].shape for n in rp])))
        outs.extend(rp_out[n] if ax is None else jnp.stack([piece_out[j, (n, i)] for i in range(a[n].shape[0])]) for n, ax in WEIGHTS)
    return (loss, dh[None], *outs)
```

```python
import functools
import math

import jax
import jax.numpy as jnp
from jax import lax
from jax.experimental import pallas as pl
from jax.experimental.pallas import tpu as pltpu

F32 = jnp.float32
BF16 = jnp.bfloat16
N_DEV = 8
LANES = 128
VMEM_LIMIT = 56 * 1024 * 1024
MATMUL_VMEM = 46 * 1024 * 1024
NORM_EPS = 1e-6
RWKV_GN_EPS = 64e-5
LRU_C = 8.0
ADAM_LR, ADAM_B1, ADAM_B2, ADAM_EPS, ADAM_WD, ADAM_STEP = 0.001, 0.9, 0.999, 1e-08, 0.01, 10
SSD_CHUNK = 128
RWKV_CHUNK = 32
HEAD = 64


def _cparams(*sem):
    return pltpu.CompilerParams(dimension_semantics=sem, vmem_limit_bytes=VMEM_LIMIT)


def _pick(n, prefs):
    for t in prefs:
        if n % t == 0:
            return t
    return n


def _relu2_of(val, _):
    r = jnp.maximum(val, 0.0)
    return r * r


def _relu2_grad(dact, act):
    return dact * (2.0 * jnp.sqrt(act))


def _matmul(a, b, mode, *, name, add=None, out_dtype=F32, post=None, aux=None):
    if mode == "nn":
        (m, k), (k2, n) = a.shape, b.shape
    elif mode == "nt":
        (m, k), (n, k2) = a.shape, b.shape
    else:
        (k, m), (k2, n) = a.shape, b.shape
    assert k == k2, (a.shape, b.shape, mode)
    tm, tn = _pick(m, (1024, 512, 256, 128)), _pick(n, (1024, 512, 256, 128))
    extras = [x for x in (add, aux) if x is not None]
    fits = lambda t: 2 * t * (tm * a.dtype.itemsize + tn * b.dtype.itemsize) + (3 + 2 * len(extras)) * tm * tn * 4 <= MATMUL_VMEM
    tk = next((t for t in (2048, 1024, 512, 256, 128) if k % t == 0 and fits(t)), k)
    nk = k // tk
    dn = {"nn": (((1,), (0,)), ((), ())), "nt": (((1,), (1,)), ((), ())), "tn": (((0,), (0,)), ((), ()))}[mode]
    a_spec = pl.BlockSpec((tk, tm), lambda i, j, kk: (kk, i)) if mode == "tn" else pl.BlockSpec((tm, tk), lambda i, j, kk: (i, kk))
    b_spec = pl.BlockSpec((tn, tk), lambda i, j, kk: (j, kk)) if mode == "nt" else pl.BlockSpec((tk, tn), lambda i, j, kk: (kk, j))
    o_spec = pl.BlockSpec((tm, tn), lambda i, j, kk: (i, j))

    def body(a_ref, b_ref, *rest):
        extra_refs, (o_ref, acc) = list(rest[:len(extras)]), rest[len(extras):]
        add_ref = extra_refs.pop(0) if add is not None else None
        aux_ref = extra_refs.pop(0) if aux is not None else None
        kk = pl.program_id(2)
        prod = lambda: lax.dot_general(a_ref[...].astype(BF16), b_ref[...].astype(BF16), dn, preferred_element_type=F32)
        first = lambda: prod() if add is None else prod() + add_ref[...].astype(F32)

        def write(val):
            if post is not None:
                val = post(val, None if aux is None else aux_ref[...])
            o_ref[...] = val.astype(o_ref.dtype)

        if nk == 1:
            write(first())
            return

        @pl.when(kk == 0)
        def _():
            acc[...] = first()

        @pl.when((kk > 0) & (kk < nk - 1))
        def _():
            acc[...] += prod()

        @pl.when(kk == nk - 1)
        def _():
            write(acc[...] + prod())

    ins, specs = [a, b] + extras, [a_spec, b_spec] + [o_spec] * len(extras)
    return pl.pallas_call(
        body, grid=(m // tm, n // tn, nk), in_specs=specs, out_specs=o_spec,
        out_shape=jax.ShapeDtypeStruct((m, n), out_dtype), scratch_shapes=[pltpu.VMEM((tm, tn), F32)],
        compiler_params=_cparams("parallel", "parallel", "arbitrary"), name=name)(*ins)


def _row(x, width=None, block=0):
    return (x, x.shape[1] if width is None else width, block)


def _rw_specs(rows, params, ts):
    specs = [pl.BlockSpec((ts, w), functools.partial(lambda i, b: (i, b), b=bi)) for (_, w, bi) in rows]
    specs += [pl.BlockSpec(p.shape, functools.partial(lambda i, nd: (0,) * nd, nd=p.ndim)) for p in params]
    return specs


def _rw_fwd(f, rows, params, outs, *, name, ts=256):
    s = rows[0][0].shape[0]
    ts = min(ts, s)
    nr, npar = len(rows), len(params)

    def body(*refs):
        row0 = pl.program_id(0) * ts
        res = f(row0, *[r[...] for r in refs[:nr + npar]])
        for o_ref, val in zip(refs[nr + npar:], res, strict=True):
            o_ref[...] = val.astype(o_ref.dtype)

    out = pl.pallas_call(
        body, grid=(s // ts,), in_specs=_rw_specs(rows, params, ts),
        out_specs=[pl.BlockSpec((ts, w), lambda i: (i, 0)) for (w, _) in outs],
        out_shape=[jax.ShapeDtypeStruct((s, w), dt) for (w, dt) in outs],
        compiler_params=_cparams("parallel"), name=name)(*[r[0] for r in rows], *params)
    return tuple(out)


def _rw_bwd(f, rows, params, cts, *, name, ts=256, row_grads=None, param_grads=None, add_rows=None):
    s = rows[0][0].shape[0]
    ts = min(ts, s)
    ct_groups = [list(c) if isinstance(c, (list, tuple)) else [c] for c in cts]
    cts = [c for grp in ct_groups for c in grp]
    nr, npar, nct = len(rows), len(params), len(cts)
    row_grads = list(range(nr)) if row_grads is None else list(row_grads)
    param_grads = list(range(npar)) if param_grads is None else list(param_grads)
    add_rows = add_rows or {}
    add_keys = sorted(add_rows)

    def body(*refs):
        i = pl.program_id(0)
        row0 = i * ts
        vals = [r[...] for r in refs[:nr + npar]]
        for pi in param_grads:
            vals[nr + pi] = vals[nr + pi].astype(F32)
        ct_refs = list(refs[nr + npar:nr + npar + nct])
        add_refs = dict(zip(add_keys, refs[nr + npar + nct:nr + npar + nct + len(add_keys)]))
        o_refs = refs[nr + npar + nct + len(add_keys):]
        res, vjp = jax.vjp(functools.partial(f, row0), *vals)
        ct_vals = []
        for grp, r in zip(ct_groups, res, strict=True):
            ct_vals.append(sum(ct_refs.pop(0)[...].astype(r.dtype) for _ in grp))
        grads = vjp(tuple(ct_vals))
        for o_ref, ri in zip(o_refs[:len(row_grads)], row_grads):
            g = grads[ri]
            if ri in add_refs:
                g = g + add_refs[ri][...]
            o_ref[...] = g.astype(o_ref.dtype)

        @pl.when(i == 0)
        def _():
            for o_ref in o_refs[len(row_grads):]:
                o_ref[...] = jnp.zeros_like(o_ref)

        for o_ref, pi in zip(o_refs[len(row_grads):], param_grads):
            o_ref[...] += grads[nr + pi].astype(F32)

    in_specs = _rw_specs(rows, params, ts)
    in_specs += [pl.BlockSpec((ts, c.shape[1]), lambda i: (i, 0)) for c in cts]
    in_specs += [pl.BlockSpec((ts, add_rows[k].shape[1]), lambda i: (i, 0)) for k in add_keys]
    out_specs = [pl.BlockSpec((ts, rows[ri][1]), lambda i: (i, 0)) for ri in row_grads]
    out_specs += [pl.BlockSpec(params[pi].shape, functools.partial(lambda i, nd: (0,) * nd, nd=params[pi].ndim)) for pi in param_grads]
    out_shape = [jax.ShapeDtypeStruct((s, rows[ri][1]), F32) for ri in row_grads]
    out_shape += [jax.ShapeDtypeStruct(params[pi].shape, F32) for pi in param_grads]
    out = pl.pallas_call(
        body, grid=(s // ts,), in_specs=in_specs, out_specs=out_specs, out_shape=out_shape,
        compiler_params=_cparams("arbitrary"), name=name)(*[r[0] for r in rows], *params, *cts, *[add_rows[k] for k in add_keys])
    return tuple(out[:len(row_grads)]), tuple(out[len(row_grads):])


@functools.partial(jax.custom_vjp, nondiff_argnums=(1,))
def _shift_down(x, k):
    rows = lax.broadcasted_iota(jnp.int32, x.shape, 0)
    return jnp.where(rows < k, 0.0, pltpu.roll(x, k, 0))


def _shift_down_fwd(x, k):
    return _shift_down(x, k), None


def _shift_down_bwd(k, _, g):
    n = g.shape[0]
    rows = lax.broadcasted_iota(jnp.int32, g.shape, 0)
    return (jnp.where(rows >= n - k, 0.0, pltpu.roll(g, n - k, 0)),)


_shift_down.defvjp(_shift_down_fwd, _shift_down_bwd)


def _ct_fwd(f, x, xmap, ntiles, params, out_width, *, name, ct=LANES):
    s = x.shape[0]

    def body(*refs):
        refs[-1][...] = f(*[r[...] for r in refs[:-1]])

    in_specs = [pl.BlockSpec((s, ct), lambda j: (0, xmap(j)))]
    in_specs += [pl.BlockSpec((p.shape[0], ct), functools.partial(lambda j, o: (0, o + j), o=o)) for (p, o) in params]
    return pl.pallas_call(
        body, grid=(ntiles,), in_specs=in_specs, out_specs=pl.BlockSpec((s, ct), lambda j: (0, j)),
        out_shape=jax.ShapeDtypeStruct((s, out_width), F32), compiler_params=_cparams("parallel"), name=name)(x, *[p for p, _ in params])


def _ct_bwd(f, x, xmap, ntiles, params, g, *, name, ct=LANES):
    s = x.shape[0]
    npar, ng = len(params), len(g)

    def body(*refs):
        vals = [r[...] for r in refs[:1 + npar]]
        _, vjp = jax.vjp(f, *vals)
        grads = vjp(sum(r[...] for r in refs[1 + npar:1 + npar + ng]))
        for o_ref, gr in zip(refs[1 + npar + ng:], grads, strict=True):
            o_ref[...] = gr

    in_specs = [pl.BlockSpec((s, ct), lambda j: (0, xmap(j)))]
    pspecs = [pl.BlockSpec((p.shape[0], ct), functools.partial(lambda j, o: (0, o + j), o=o)) for (p, o) in params]
    in_specs += pspecs + [pl.BlockSpec((s, ct), lambda j: (0, j))] * ng
    out = pl.pallas_call(
        body, grid=(ntiles,), in_specs=in_specs, out_specs=[pl.BlockSpec((s, ct), lambda j: (0, j))] + pspecs,
        out_shape=[jax.ShapeDtypeStruct((s, ntiles * ct), F32)] + [jax.ShapeDtypeStruct(p.shape, F32) for p, _ in params],
        compiler_params=_cparams("parallel"), name=name)(x, *[p for p, _ in params], *g)
    return out[0], tuple(out[1:])


def _block_vjp(f, args, cts, *, name):
    outs = jax.eval_shape(f, *args)
    if cts is None:
        def body(*refs):
            for o_ref, v in zip(refs[len(args):], f(*[r[...] for r in refs[:len(args)]]), strict=True):
                o_ref[...] = v
        return tuple(pl.pallas_call(body, out_shape=[jax.ShapeDtypeStruct(o.shape, o.dtype) for o in outs], name=name)(*args))

    def body(*refs):
        n = len(args)
        _, vjp = jax.vjp(f, *[r[...] for r in refs[:n]])
        for o_ref, gr in zip(refs[n + len(cts):], vjp(tuple(r[...] for r in refs[n:n + len(cts)])), strict=True):
            o_ref[...] = gr
    return tuple(pl.pallas_call(body, out_shape=[jax.ShapeDtypeStruct(a.shape, a.dtype) for a in args], name=name)(*args, *cts))


def _softplus(x):
    return jnp.maximum(x, 0.0) + jnp.log(1.0 + jnp.exp(-jnp.abs(x)))


def _dot_bf16(a, b):
    return jnp.dot(a.astype(BF16), b.astype(BF16), preferred_element_type=F32)


def _dot3(x, m):
    hi = x.astype(BF16)
    r1 = x - hi.astype(F32)
    mid = r1.astype(BF16)
    lo = (r1 - mid.astype(F32)).astype(BF16)
    return (jnp.dot(hi, m, preferred_element_type=F32) + jnp.dot(mid, m, preferred_element_type=F32)
            + jnp.dot(lo, m, preferred_element_type=F32))


@jax.custom_vjp
def _lin(x, m, mt):
    return _dot3(x, m)


def _lin_fwd(x, m, mt):
    return _dot3(x, m), (m, mt)


def _lin_bwd(res, g):
    m, mt = res
    return _dot3(g, mt), jnp.zeros_like(m), jnp.zeros_like(mt)


_lin.defvjp(_lin_fwd, _lin_bwd)


def _head_ones(n, head=HEAD):
    i = jnp.arange(n) // head
    return (i[:, None] == i[None, :]).astype(BF16)


def _rmsnorm_tile(_, h, g):
    return (h * lax.rsqrt(jnp.mean(h * h, axis=-1, keepdims=True) + NORM_EPS) * g,)


def _cscan_fwd(bu, a, *, name, ct=256, carry=None):
    s, c2 = bu.shape
    c = c2 // 2
    nt = c // ct

    def body(br_ref, bi_ref, ar_ref, ai_ref, xr_ref, xi_ref):
        ar, ai = ar_ref[...], ai_ref[...]

        def step(t, carry):
            hr, hi = carry
            nr = ar * hr - ai * hi + br_ref[pl.ds(t, 1), :]
            ni = ar * hi + ai * hr + bi_ref[pl.ds(t, 1), :]
            xr_ref[pl.ds(t, 1), :] = nr
            xi_ref[pl.ds(t, 1), :] = ni
            return nr, ni

        z = jnp.zeros((1, ct), F32)
        lax.fori_loop(0, s, step, (z, z))

    re = lambda j: (0, j)
    im = lambda j: (0, nt + j)
    return _call_with_exchange(
        body, nt, carry, [bu, bu, a, a],
        [pl.BlockSpec((s, ct), re), pl.BlockSpec((s, ct), im), pl.BlockSpec((1, ct), re), pl.BlockSpec((1, ct), im)],
        [pl.BlockSpec((s, ct), re), pl.BlockSpec((s, ct), re)], [jax.ShapeDtypeStruct((s, c), F32)] * 2, [], name)


def _cscan_bwd(gr, gi, xr, xi, a, *, name, ct=256, carry=None):
    s, c = gr.shape
    nt = c // ct

    def body(gr_ref, gi_ref, xr_ref, xi_ref, ar_ref, ai_ref, dr_ref, di_ref, dar_ref, dai_ref):
        ar, ai = ar_ref[...], ai_ref[...]

        def step(i, carry):
            dr, di, accr, acci = carry
            t = s - 1 - i
            nr = gr_ref[pl.ds(t, 1), :] + ar * dr + ai * di
            ni = gi_ref[pl.ds(t, 1), :] + ar * di - ai * dr
            dr_ref[pl.ds(t, 1), :] = nr
            di_ref[pl.ds(t, 1), :] = ni
            tp = jnp.maximum(t - 1, 0)
            live = (t > 0).astype(F32)
            pr = xr_ref[pl.ds(tp, 1), :] * live
            pi = xi_ref[pl.ds(tp, 1), :] * live
            return nr, ni, accr + nr * pr + ni * pi, acci + ni * pr - nr * pi

        z = jnp.zeros((1, ct), F32)
        _, _, accr, acci = lax.fori_loop(0, s, step, (z, z, z, z))
        dar_ref[...] = accr
        dai_ref[...] = acci

    re = lambda j: (0, j)
    im = lambda j: (0, nt + j)
    blk = pl.BlockSpec((s, ct), re)
    return _call_with_exchange(
        body, nt, carry, [gr, gi, xr, xi, a, a],
        [blk, blk, blk, blk, pl.BlockSpec((1, ct), re), pl.BlockSpec((1, ct), im)],
        [blk, blk, pl.BlockSpec((1, ct), re), pl.BlockSpec((1, ct), re)],
        [jax.ShapeDtypeStruct((s, c), F32)] * 2 + [jax.ShapeDtypeStruct((1, c), F32)] * 2, [], name)


def _rscan_fwd(a, b, *, name, ct=256):
    s, c = a.shape

    def body(a_ref, b_ref, h_ref):
        def step(t, h):
            h = a_ref[pl.ds(t, 1), :] * h + b_ref[pl.ds(t, 1), :]
            h_ref[pl.ds(t, 1), :] = h
            return h
        lax.fori_loop(0, s, step, jnp.zeros((1, ct), F32))

    blk = pl.BlockSpec((s, ct), lambda j: (0, j))
    return pl.pallas_call(body, grid=(c // ct,), in_specs=[blk, blk], out_specs=blk,
                          out_shape=jax.ShapeDtypeStruct((s, c), F32), compiler_params=_cparams("parallel"), name=name)(a, b)


def _rscan_bwd(g, a, h, *, name, ct=256):
    s, c = a.shape

    def body(g_ref, a_ref, h_ref, db_ref, da_ref):
        def step(i, carry):
            d, an = carry
            t = s - 1 - i
            d = g_ref[pl.ds(t, 1), :] + an * d
            db_ref[pl.ds(t, 1), :] = d
            tp = jnp.maximum(t - 1, 0)
            da_ref[pl.ds(t, 1), :] = d * h_ref[pl.ds(tp, 1), :] * (t > 0).astype(F32)
            return d, a_ref[pl.ds(t, 1), :]
        z = jnp.zeros((1, ct), F32)
        lax.fori_loop(0, s, step, (z, z))

    blk = pl.BlockSpec((s, ct), lambda j: (0, j))
    db, da = pl.pallas_call(body, grid=(c // ct,), in_specs=[blk, blk, blk], out_specs=[blk, blk],
                            out_shape=[jax.ShapeDtypeStruct((s, c), F32)] * 2, compiler_params=_cparams("parallel"), name=name)(g, a, h)
    return db, da


def _dot3l(m, x):
    hi = x.astype(BF16)
    r1 = x - hi.astype(F32)
    mid = r1.astype(BF16)
    lo = (r1 - mid.astype(F32)).astype(BF16)
    return (jnp.dot(m, hi, preferred_element_type=F32) + jnp.dot(m, mid, preferred_element_type=F32)
            + jnp.dot(m, lo, preferred_element_type=F32))


@jax.custom_vjp
def _linl(x, m, mt):
    return _dot3l(m, x)


def _linl_fwd(x, m, mt):
    return _dot3l(m, x), (m, mt)


def _linl_bwd(res, g):
    m, mt = res
    return _dot3l(mt, g), jnp.zeros_like(m), jnp.zeros_like(mt)


_linl.defvjp(_linl_fwd, _linl_bwd)


def _ssd_chunk(g, xs, bm, cm, z, dtraw, dt_bias, a_log, dskip, ng, st0, st1, st2):
    n = xs.shape[0]
    lane = lax.broadcasted_iota(jnp.int32, (1, LANES), 1)
    sub = lax.broadcasted_iota(jnp.int32, (LANES, 1), 0)
    row = lax.broadcasted_iota(jnp.int32, (n, n), 0)
    col = lax.broadcasted_iota(jnp.int32, (n, n), 1)
    tril = row >= col
    tril_m = tril.astype(BF16)
    triu_m = (row <= col).astype(BF16)
    lane_lo = lane < HEAD
    sub_lo = sub < HEAD
    dt = _softplus(dtraw + dt_bias)
    da = dt * (-jnp.exp(a_log))
    acum = _linl(da, tril_m, triu_m)
    acum_t = acum.T
    scores = lax.dot_general(cm.astype(BF16), bm.astype(BF16), (((1,), (1,)), ((), ())), preferred_element_type=F32)

    def head(h):
        sel = lane == h
        acol = jnp.sum(jnp.where(sel, acum, 0.0), axis=1, keepdims=True)
        arow = jnp.sum(jnp.where(sub == h, acum_t, 0.0), axis=0, keepdims=True)
        dtcol = jnp.sum(jnp.where(sel, dt, 0.0), axis=1, keepdims=True)
        dsk = jnp.sum(jnp.where(sel, dskip, 0.0), axis=1, keepdims=True)
        decay = jnp.exp(jnp.where(tril, acol - arow, -jnp.inf))
        alast = acol[n - 1:n, :]
        return acol, dtcol, dsk, decay, alast

    ys, new = [], []
    for q, st in enumerate((st0, st1, st2)):
        a_acol, a_dt, a_dsk, a_decay, a_last = head(g * 6 + 2 * q)
        b_acol, b_dt, b_dsk, b_decay, b_last = head(g * 6 + 2 * q + 1)
        xp = xs[:, q * LANES:(q + 1) * LANES]
        xdt = xp * jnp.where(lane_lo, a_dt, b_dt)
        yd = jnp.where(lane_lo, _dot_bf16(scores * a_decay, xdt), _dot_bf16(scores * b_decay, xdt))
        xw = xdt * jnp.where(lane_lo, jnp.exp(a_last - a_acol), jnp.exp(b_last - b_acol))
        states = lax.dot_general(xw.astype(BF16), bm.astype(BF16), (((0,), (0,)), ((), ())), preferred_element_type=F32)
        yo = lax.dot_general(cm.astype(BF16), st.astype(BF16), (((1,), (1,)), ((), ())), preferred_element_type=F32)
        yo = yo * jnp.where(lane_lo, jnp.exp(a_acol), jnp.exp(b_acol))
        new.append(st * jnp.where(sub_lo, jnp.exp(a_last), jnp.exp(b_last)) + states)
        ys.append(yd + yo + xp * jnp.where(lane_lo, a_dsk, b_dsk))
    y = jnp.concatenate(ys, axis=1)
    y = y * (z * jax.nn.sigmoid(z))
    y = y * lax.rsqrt(jnp.mean(y * y, axis=-1, keepdims=True) + NORM_EPS) * ng
    return y, new[0], new[1], new[2]


def _ssd_specs(nc, rev):
    cidx = (lambda c: nc - 1 - c) if rev else (lambda c: c)
    gw = 3 * LANES
    return [
        pl.BlockSpec((SSD_CHUNK, gw), lambda c, g: (cidx(c), g)),
        pl.BlockSpec((SSD_CHUNK, LANES), lambda c, g: (cidx(c), 12 + g)),
        pl.BlockSpec((SSD_CHUNK, LANES), lambda c, g: (cidx(c), 16 + g)),
        pl.BlockSpec((SSD_CHUNK, gw), lambda c, g: (cidx(c), g)),
        pl.BlockSpec((SSD_CHUNK, LANES), lambda c, g: (cidx(c), 36)),
        pl.BlockSpec((1, LANES), lambda c, g: (0, 0)),
        pl.BlockSpec((1, LANES), lambda c, g: (0, 0)),
        pl.BlockSpec((1, LANES), lambda c, g: (0, 0)),
        pl.BlockSpec((1, gw), lambda c, g: (0, g)),
    ], cidx


def _ssd_fwd(conv, proj, dt_bias, a_log, dskip, norm_g, *, name):
    s = conv.shape[0]
    nc = s // SSD_CHUNK
    in_specs, _ = _ssd_specs(nc, False)

    def body(xs, bm, cm, z, dtr, dtb, alog, dsk, ng, y_ref, sv_ref, st):
        c, g = pl.program_id(0), pl.program_id(1)

        @pl.when(c == 0)
        def _():
            for q in range(3):
                st[g * 3 + q] = jnp.zeros((LANES, LANES), F32)

        olds = [st[g * 3 + q] for q in range(3)]
        for q in range(3):
            sv_ref[0, 0, q] = olds[q]
        y, n0, n1, n2 = _ssd_chunk(g, xs[...], bm[...], cm[...], z[...], dtr[...], dtb[...], alog[...], dsk[...], ng[...], *olds)
        y_ref[...] = y
        for q, v in enumerate((n0, n1, n2)):
            st[g * 3 + q] = v

    return pl.pallas_call(
        body, grid=(nc, 4), in_specs=in_specs,
        out_specs=[pl.BlockSpec((SSD_CHUNK, 3 * LANES), lambda c, g: (c, g)),
                   pl.BlockSpec((1, 1, 3, LANES, LANES), lambda c, g: (c, g, 0, 0, 0))],
        out_shape=[jax.ShapeDtypeStruct((s, 12 * LANES), F32), jax.ShapeDtypeStruct((nc, 4, 3, LANES, LANES), F32)],
        scratch_shapes=[pltpu.VMEM((12, LANES, LANES), F32)],
        compiler_params=_cparams("arbitrary", "arbitrary"), name=name)(conv, conv, conv, proj, proj, dt_bias, a_log, dskip, norm_g)


def _ssd_bwd(conv, proj, dt_bias, a_log, dskip, norm_g, saved, dy, *, name, carry=None):
    s = conv.shape[0]
    nc = s // SSD_CHUNK
    in_specs, cidx = _ssd_specs(nc, True)
    gw = 3 * LANES
    in_specs += [pl.BlockSpec((1, 1, 3, LANES, LANES), lambda c, g: (cidx(c), g, 0, 0, 0)),
                 pl.BlockSpec((SSD_CHUNK, gw), lambda c, g: (cidx(c), g))]

    def body(xs, bm, cm, z, dtr, dtb, alog, dsk, ng, sv, dy_ref, dxs, dbm, dcm, dz, ddt, ddtb, dalog, ddsk, dng, dst):
        c, g = pl.program_id(0), pl.program_id(1)

        @pl.when(c == 0)
        def _():
            for q in range(3):
                dst[g * 3 + q] = jnp.zeros((LANES, LANES), F32)

        @pl.when((c == 0) & (g == 0))
        def _():
            ddtb[...] = jnp.zeros_like(ddtb)
            dalog[...] = jnp.zeros_like(dalog)
            ddsk[...] = jnp.zeros_like(ddsk)
            dng[...] = jnp.zeros_like(dng)

        @pl.when(g == 0)
        def _():
            ddt[...] = jnp.zeros_like(ddt)

        olds = [sv[0, 0, q] for q in range(3)]
        _, vjp = jax.vjp(functools.partial(_ssd_chunk, g), xs[...], bm[...], cm[...], z[...], dtr[...], dtb[...], alog[...],
                         dsk[...], ng[...], *olds)
        gr = vjp((dy_ref[...], dst[g * 3], dst[g * 3 + 1], dst[g * 3 + 2]))
        dxs[...], dbm[...], dcm[...], dz[...] = gr[0], gr[1], gr[2], gr[3]
        ddt[...] += gr[4]
        ddtb[...] += gr[5]
        dalog[...] += gr[6]
        ddsk[...] += gr[7]
        dng[g] += gr[8]
        for q in range(3):
            dst[g * 3 + q] = gr[9 + q]

    const = lambda shape: pl.BlockSpec(shape, lambda c, g: (0,) * len(shape))
    return _call_with_exchange(
        body, (nc, 4), carry, [conv, conv, conv, proj, proj, dt_bias, a_log, dskip, norm_g, saved, dy], in_specs,
        [pl.BlockSpec((SSD_CHUNK, gw), lambda c, g: (cidx(c), g)),
         pl.BlockSpec((SSD_CHUNK, LANES), lambda c, g: (cidx(c), g)),
         pl.BlockSpec((SSD_CHUNK, LANES), lambda c, g: (cidx(c), g)),
         pl.BlockSpec((SSD_CHUNK, gw), lambda c, g: (cidx(c), g)),
         pl.BlockSpec((SSD_CHUNK, LANES), lambda c, g: (cidx(c), 0)),
         const((1, LANES)), const((1, LANES)), const((1, LANES)), const((4, 1, gw))],
        [jax.ShapeDtypeStruct((s, 12 * LANES), F32), jax.ShapeDtypeStruct((s, 4 * LANES), F32),
         jax.ShapeDtypeStruct((s, 4 * LANES), F32), jax.ShapeDtypeStruct((s, 12 * LANES), F32),
         jax.ShapeDtypeStruct((s, LANES), F32), jax.ShapeDtypeStruct((1, LANES), F32),
         jax.ShapeDtypeStruct((1, LANES), F32), jax.ShapeDtypeStruct((1, LANES), F32),
         jax.ShapeDtypeStruct((4, 1, gw), F32)],
        [pltpu.VMEM((12, LANES, LANES), F32)], name)


MXU_TILE = 256
RWKV_GROUP = 8


def _rwkv_consts():
    lanes = 16 * HEAD
    hl = jnp.arange(lanes) // HEAD
    e = (jnp.arange(16)[:, None] == hl[None, :]).astype(BF16)
    return jnp.tile(e, (3, 1)), e.T, _head_ones(MXU_TILE)


def _head_sums(x, j):
    n = x.shape[0]
    x4 = jnp.concatenate([x[:, i * MXU_TILE:(i + 1) * MXU_TILE] for i in range(4)], axis=0)
    hi = x4.astype(BF16)
    mid = (x4 - hi.astype(F32)).astype(BF16)
    out = jnp.dot(jnp.concatenate([hi, mid], axis=0), j, preferred_element_type=F32)
    s4 = out[:4 * n] + out[4 * n:]
    return jnp.concatenate([s4[i * n:(i + 1) * n] for i in range(4)], axis=1)


def _fold8(x):
    return jnp.sum(x.reshape(x.shape[0] // 8, 8, x.shape[1]), axis=0)


def _split3(x):
    hi = x.astype(BF16)
    r1 = x - hi.astype(F32)
    mid = r1.astype(BF16)
    return jnp.concatenate([hi, mid, (r1 - mid.astype(F32)).astype(BF16)], axis=-1)


def _rwkv_expand(src3, dst, e3, t_):
    for g0 in range(0, t_, RWKV_GROUP):
        n = min(RWKV_GROUP, t_ - g0)
        flat = src3[g0:g0 + n].reshape(n * HEAD, e3.shape[0])
        dst[g0:g0 + n] = jnp.dot(flat, e3, preferred_element_type=F32).reshape(n, HEAD, e3.shape[1])


def _rwkv_reduce(src, dst3, et, t_):
    for g0 in range(0, t_, RWKV_GROUP):
        n = min(RWKV_GROUP, t_ - g0)
        x = src[g0:g0 + n].reshape(n * HEAD, et.shape[0])
        hi = x.astype(BF16)
        mid = (x - hi.astype(F32)).astype(BF16)
        out = jnp.dot(hi, et, preferred_element_type=F32) + jnp.dot(mid, et, preferred_element_type=F32)
        dst3[g0:g0 + n] = out.reshape(n, HEAD, 16)


def _rwkv_fwd(w, kk, b, k, fp, v3, *, name, carry=None):
    s, lanes = w.shape
    t_ = min(RWKV_CHUNK, s)
    nc = s // t_
    e, et, j = _rwkv_consts()
    rowspec = pl.BlockSpec((t_, lanes), lambda c: (c, 0))
    cspec = lambda a: pl.BlockSpec(a.shape, lambda c: (0, 0))

    def body(w_ref, kk_ref, b_ref, k_ref, r_ref, v_ref, e_ref, et_ref, j_ref, y_ref, sv_ref, st, vm, zz):
        c = pl.program_id(0)

        @pl.when(c == 0)
        def _():
            st[...] = jnp.zeros_like(st)

        sv_ref[0] = st[...]
        jv = j_ref[...]
        _rwkv_expand(v_ref, vm, e_ref[...], t_)

        def step(t, sm):
            row = lambda ref: ref[pl.ds(t, 1), :]
            sa = _head_sums(sm * (-row(kk_ref)), jv)
            sn = sm * row(w_ref) + sa * row(b_ref) + vm[t] * row(k_ref)
            zz[t] = sn * row(r_ref)
            return sn

        st[...] = lax.fori_loop(0, t_, step, st[...])
        _rwkv_reduce(zz, y_ref, et_ref[...], t_)

    return _call_with_exchange(
        body, nc, carry, [w, kk, b, k, fp, v3, e, et, j],
        [rowspec] * 5 + [pl.BlockSpec((t_, HEAD, 48), lambda c: (c, 0, 0)), cspec(e), cspec(et), cspec(j)],
        [pl.BlockSpec((t_, HEAD, 16), lambda c: (c, 0, 0)), pl.BlockSpec((1, HEAD, lanes), lambda c: (c, 0, 0))],
        [jax.ShapeDtypeStruct((s, HEAD, 16), F32), jax.ShapeDtypeStruct((nc, HEAD, lanes), F32)],
        [pltpu.VMEM((HEAD, lanes), F32), pltpu.VMEM((t_, HEAD, lanes), F32), pltpu.VMEM((t_, HEAD, lanes), F32)], name)


def _call_with_exchange(body, grid, carry, ins, in_specs, out_specs, out_shape, scratch, name):
    grid = (grid,) if isinstance(grid, int) else tuple(grid)
    if carry is not None:
        src, gather = carry
        n_in, n_out, inner = len(ins), len(out_shape), body

        def body(*refs):
            ex = _Exchange(refs[n_in], refs[n_in + 1 + n_out], *refs[-3:], gather)
            steps = [pl.program_id(d) for d in range(len(grid))]

            @pl.when(functools.reduce(jnp.logical_and, [s == 0 for s in steps]))
            def _():
                ex.start()

            inner(*refs[:n_in], *refs[n_in + 1:n_in + 1 + n_out], *refs[n_in + 2 + n_out:-3])

            @pl.when(functools.reduce(jnp.logical_and, [s == n - 1 for s, n in zip(steps, grid)]))
            def _():
                ex.wait()

        hbm = pl.BlockSpec(memory_space=pl.ANY)
        ins, in_specs, out_specs = list(ins) + [src], list(in_specs) + [hbm], list(out_specs) + [hbm]
        out_shape = list(out_shape) + [jax.ShapeDtypeStruct((N_DEV, src.shape[-2], LANES), src.dtype)]
        scratch = list(scratch) + list(_Exchange.SCRATCH)
    return pl.pallas_call(body, grid=grid, in_specs=in_specs, out_specs=out_specs, out_shape=out_shape,
                          scratch_shapes=scratch, compiler_params=_cparams(*["arbitrary"] * len(grid)), name=name)(*ins)


def _rwkv_bwd(w, kk, b, k, fp, v3, saved, dy3, *, name, carry=None):
    s, lanes = w.shape
    t_ = min(RWKV_CHUNK, s)
    nc = s // t_
    e, et, j = _rwkv_consts()
    rev = lambda c: nc - 1 - c
    rowspec = pl.BlockSpec((t_, lanes), lambda c: (rev(c), 0))
    v3spec = pl.BlockSpec((t_, HEAD, 16), lambda c: (rev(c), 0, 0))
    s3spec = pl.BlockSpec((t_, HEAD, 48), lambda c: (rev(c), 0, 0))
    cspec = lambda a: pl.BlockSpec(a.shape, lambda c: (0, 0))

    def body(w_ref, kk_ref, b_ref, k_ref, r_ref, v_ref, sv_ref, dy_ref, e_ref, et_ref, j_ref,
             dw_ref, dkk_ref, db_ref, dk_ref, dr_ref, dv_ref, dst, h_sm, h_sa, vm, dz, pw, pkk, pb, pk, pr):
        c = pl.program_id(0)

        @pl.when(c == 0)
        def _():
            dst[...] = jnp.zeros_like(dst)

        jv = j_ref[...]
        row = lambda ref, t: ref[pl.ds(t, 1), :]
        _rwkv_expand(v_ref, vm, e_ref[...], t_)
        _rwkv_expand(dy_ref, dz, e_ref[...], t_)

        def replay(t, sm):
            h_sm[t] = sm
            sa = _head_sums(sm * (-row(kk_ref, t)), jv)
            h_sa[t] = sa
            return sm * row(w_ref, t) + sa * row(b_ref, t) + vm[t] * row(k_ref, t)

        h_sm[t_] = lax.fori_loop(0, t_, replay, sv_ref[0])

        def back(i, dcarry):
            t = t_ - 1 - i
            sm, sa, dzt = h_sm[t], h_sa[t], dz[t]
            dsn = dcarry + dzt * row(r_ref, t)
            pr[t] = _fold8(dzt * h_sm[t + 1])
            pw[t] = _fold8(dsn * sm)
            pb[t] = _fold8(dsn * sa)
            pk[t] = _fold8(dsn * vm[t])
            vm[t] = dsn * row(k_ref, t)
            dx = _head_sums(dsn * row(b_ref, t), jv)
            pkk[t] = _fold8(dx * sm)
            return dsn * row(w_ref, t) - dx * row(kk_ref, t)

        dst[...] = lax.fori_loop(0, t_, back, dst[...])
        _rwkv_reduce(vm, dv_ref, et_ref[...], t_)
        dw_ref[...] = jnp.sum(pw[...], axis=1)
        dkk_ref[...] = -jnp.sum(pkk[...], axis=1)
        db_ref[...] = jnp.sum(pb[...], axis=1)
        dk_ref[...] = jnp.sum(pk[...], axis=1)
        dr_ref[...] = jnp.sum(pr[...], axis=1)

    big = lambda n: pltpu.VMEM((n, HEAD, lanes), F32)
    part = pltpu.VMEM((t_, 8, lanes), F32)
    return _call_with_exchange(
        body, nc, carry, [w, kk, b, k, fp, v3, saved, dy3, e, et, j],
        [rowspec] * 5 + [s3spec, pl.BlockSpec((1, HEAD, lanes), lambda c: (rev(c), 0, 0)), s3spec, cspec(e), cspec(et), cspec(j)],
        [rowspec] * 5 + [v3spec],
        [jax.ShapeDtypeStruct((s, lanes), F32)] * 5 + [jax.ShapeDtypeStruct((s, HEAD, 16), F32)],
        [pltpu.VMEM((HEAD, lanes), F32), big(t_ + 1), big(t_), big(t_), big(t_)] + [part] * 5, name)


def _blockdiag(blocks):
    g, a, b = blocks.shape
    on_diag = (jnp.arange(g)[:, None, None, None] == jnp.arange(g)[None, None, :, None])
    return jnp.where(on_diag, blocks[:, :, None, :], 0).reshape(g * a, g * b)


def _blockdiag_t(dense, g):
    a, b = dense.shape[0] // g, dense.shape[1] // g
    on_diag = (jnp.arange(g)[:, None, None, None] == jnp.arange(g)[None, None, :, None])
    return jnp.sum(jnp.where(on_diag, dense.reshape(g, a, g, b), 0), axis=2)


def _pad_cols(x, n):
    return jnp.pad(x, ((0, 0), (0, n - x.shape[1])))


def _pad_rows(x, n):
    return jnp.pad(x, ((0, n - x.shape[0]), (0, 0)))


E_PROJ = 5120


def _even_in_cols(w):
    return jnp.concatenate([w[:, 512:2048], w[:, 0:512], w[:, 2048:4632], jnp.zeros((w.shape[0], E_PROJ - 4632), w.dtype)], axis=1)


def _even_in_cols_t(dw):
    return jnp.concatenate([dw[:, 1536:2048], dw[:, 0:1536], dw[:, 2048:4632]], axis=1)


O_PROJ = 5632


def _odd_in_cols(w):
    z32 = jnp.zeros((w.shape[0], 32), w.dtype)
    return jnp.concatenate([w[:, 0:3072], w[:, 3520:5568], w[:, 3264:3520], w[:, 3072:3168], z32, w[:, 3168:3264], z32], axis=1)


def _odd_in_cols_t(dw):
    return jnp.concatenate([dw[:, 0:3072], dw[:, 5376:5472], dw[:, 5504:5600], dw[:, 5120:5376], dw[:, 3072:5120]], axis=1)


def _mu_cols(mu):
    z32 = jnp.zeros((1, 32), mu.dtype)
    return jnp.concatenate([mu[:, 0:3072], mu[:, 3264:3520], mu[:, 3072:3168], z32, mu[:, 3168:3264], z32], axis=1)


def _mu_cols_t(d):
    return jnp.concatenate([d[:, 0:3072], d[:, 3328:3424], d[:, 3456:3552], d[:, 3072:3328]], axis=1)


def _conv_taps(x, w, b):
    y = b + w[3:4] * x
    for k in range(3):
        y = y + w[k:k + 1] * _shift_down(x, 3 - k)
    return y


def _conv_silu(x, w, b):
    y = _conv_taps(x, w, b)
    return y * jax.nn.sigmoid(y)


def _tshift(x, mu):
    return x + (_shift_down(x, 1) - x) * mu


def _pl_gate(_, h, gl, e):
    return (h + jax.nn.sigmoid(gl) * e,)


def _s5_param(lr, li, ls, br, bi):
    step = jnp.exp(ls)
    mag = jnp.exp(lr * step)
    ar, ai = mag * jnp.cos(li * step), mag * jnp.sin(li * step)
    den = lr * lr + li * li
    nr = ar - 1.0
    cr = (nr * lr + ai * li) / den
    ci = (ai * lr - nr * li) / den
    return ar, ai, cr * br - ci * bi, cr * bi + ci * br


def _s5_post(_, ylin, u, d, gw, gb):
    act = jax.nn.gelu(ylin + d * u)
    return (act * jax.nn.sigmoid(_dot_bf16(act, gw) + gb),)


def _rwkv_pre(_, k, gl, wl, al, w0, w_up, a0, a_up, g_up, k_k, k_a, j):
    w = -_softplus(-(w0 + _dot_bf16(jnp.tanh(wl), w_up))) - 0.5
    decay = jnp.exp(-jnp.exp(w))
    a = jax.nn.sigmoid(a0 + _dot_bf16(al, a_up))
    g = _dot_bf16(jax.nn.sigmoid(gl), g_up)
    kk = k * k_k
    k2 = k * (1.0 + (a - 1.0) * k_a)
    kkn = kk * lax.rsqrt(jnp.maximum(_lin(kk * kk, j, j), 1e-24))
    return decay, kkn, kkn * a, k2, g


def _rwkv_post(_, y, r, k2, v, g, r_k, ln_g, ln_b, j):
    mean = _lin(y, j, j) * (1.0 / HEAD)
    yc = y - mean
    var = _lin(yc * yc, j, j) * (1.0 / HEAD)
    yn = yc * lax.rsqrt(var + RWKV_GN_EPS) * ln_g + ln_b
    return ((yn + _lin(r * k2 * r_k, j, j) * v) * g,)


def _lru_pre(row0, pre, xc, bax, lam):
    n = xc.shape[1]
    gr = jax.nn.sigmoid(pre[:, :n] + bax[:, :n])
    gi = jax.nn.sigmoid(pre[:, n:] + bax[:, n:])
    log_a = -LRU_C * gr * _softplus(-lam)
    m2 = -jnp.tanh(log_a) * (jnp.exp(2.0 * log_a) + 1.0)
    mult = jnp.sqrt(jnp.maximum(m2, 0.0))
    rowid = row0 + lax.broadcasted_iota(jnp.int32, (xc.shape[0], 1), 0)
    mult = jnp.where(rowid == 0, 1.0, mult)
    return jnp.exp(log_a), xc * gi * mult


def _lru_post(_, h, gl2):
    return (h * jax.nn.gelu(gl2),)


def _even_prep(w):
    sp = (w["s5_lam_re"].reshape(32, 64), w["s5_lam_im"].reshape(32, 64), w["s5_log_step"].reshape(32, 1),
          w["s5_b_re"].reshape(32, 64, 16).transpose(2, 0, 1), w["s5_b_im"].reshape(32, 64, 16).transpose(2, 0, 1))
    ar, ai, bbr, bbi = _block_vjp(_s5_param, sp, None, name="s5_param")
    bblk = lambda bb: _blockdiag(bb.transpose(1, 0, 2))
    cblk = lambda c: _blockdiag(c.reshape(32, 16, 64).transpose(0, 2, 1))
    pad = lambda x: _pad_cols(x.reshape(1, 24), LANES)
    return dict(
        sp=sp, a_row=jnp.concatenate([ar.reshape(1, 2048), ai.reshape(1, 2048)], axis=1),
        b_re=bblk(bbr), b_im=bblk(bbi), c_re=cblk(w["s5_c_re"]), c_imn=-cblk(w["s5_c_im"]),
        d=w["s5_d"].reshape(1, 512), gw=w["s5_glu_w"].reshape(512, 512), gb=w["s5_glu_b"].reshape(1, 512),
        conv_w=w["ssd_conv_w"].reshape(4, 2560), conv_b=w["ssd_conv_b"].reshape(1, 2560),
        dt_bias=pad(w["ssd_dt_bias"]), a_log=pad(w["ssd_a_log"]), dskip=pad(w["ssd_d"]), norm=w["ssd_norm"].reshape(1, 1536))


_E_XMAP = lambda j: 16 + j


def _even_fwd(proj, p, carry=None):
    u = proj[:, 1536:2048]
    bur = _matmul(u, p["b_re"], "nn", name="s5_bu_re")
    bui = _matmul(u, p["b_im"], "nn", name="s5_bu_im")
    xr, xi, *carried = _cscan_fwd(jnp.concatenate([bur, bui], axis=1), p["a_row"], name="s5_scan", carry=carry)
    ylin = _matmul(xi, p["c_imn"], "nn", name="s5_y_im", add=_matmul(xr, p["c_re"], "nn", name="s5_y_re"))
    (ya,) = _rw_fwd(_s5_post, [_row(ylin), _row(u)], [p["d"], p["gw"], p["gb"]], [(512, F32)], name="s5_post")
    conv = _ct_fwd(_conv_silu, proj, _E_XMAP, 20, [(p["conv_w"], 0), (p["conv_b"], 0)], 2560, name="ssd_conv")
    yb, saved = _ssd_fwd(conv, proj, p["dt_bias"], p["a_log"], p["dskip"], p["norm"], name="ssd_scan")
    return jnp.concatenate([ya, yb], axis=1), (u, xr, xi, ylin, conv, saved), (carried[0] if carried else None)


def _even_bwd(proj, p, res, dy, carry=None):
    carry_ssd, carry_s5 = carry if carry is not None else (None, None)
    u, xr, xi, ylin, conv, saved = res
    s = proj.shape[0]
    dxs, dbm, dcm, dz, ddt, ddtb, dalog, ddsk, dng, *got_ssd = _ssd_bwd(
        conv, proj, p["dt_bias"], p["a_log"], p["dskip"], p["norm"], saved, dy[:, 512:], name="ssd_scan_bwd", carry=carry_ssd)
    dxbc, (dcw, dcb) = _ct_bwd(_conv_silu, proj, _E_XMAP, 20, [(p["conv_w"], 0), (p["conv_b"], 0)],
                               [jnp.concatenate([dxs, dbm, dcm], axis=1)], name="ssd_conv_bwd")
    (dylin, du), (dd, dgw, dgb) = _rw_bwd(_s5_post, [_row(ylin), _row(u)], [p["d"], p["gw"], p["gb"]], [dy[:, :512]], name="s5_post_bwd")
    dxr = _matmul(dylin, p["c_re"], "nt", name="s5_dxr")
    dxi = _matmul(dylin, p["c_imn"], "nt", name="s5_dxi")
    dc_re = _matmul(xr, dylin, "tn", name="s5_dc_re")
    dc_imn = _matmul(xi, dylin, "tn", name="s5_dc_im")
    dbr, dbi, dar, dai, *got_s5 = _cscan_bwd(dxr, dxi, xr, xi, p["a_row"], name="s5_scan_bwd", carry=carry_s5)
    du = _matmul(dbr, p["b_re"], "nt", name="s5_du_re", add=du)
    du = _matmul(dbi, p["b_im"], "nt", name="s5_du_im", add=du)
    db_re = _matmul(u, dbr, "tn", name="s5_db_re")
    db_im = _matmul(u, dbi, "tn", name="s5_db_im")
    unblk = lambda d: _blockdiag_t(d, 32).transpose(1, 0, 2)
    g_sp = _block_vjp(_s5_param, p["sp"], (dar.reshape(32, 64), dai.reshape(32, 64), unblk(db_re), unblk(db_im)), name="s5_param_bwd")
    dproj = jnp.concatenate([dz, du, dxbc, ddt, jnp.zeros((s, E_PROJ - 4736), F32)], axis=1)
    uncblk = lambda d: _blockdiag_t(d, 32).transpose(0, 2, 1)
    grads = dict(
        s5_lam_re=g_sp[0].reshape(1, 32, 64), s5_lam_im=g_sp[1].reshape(1, 32, 64), s5_log_step=g_sp[2].reshape(1, 32),
        s5_b_re=g_sp[3].transpose(1, 2, 0)[None], s5_b_im=g_sp[4].transpose(1, 2, 0)[None],
        s5_c_re=uncblk(dc_re)[None], s5_c_im=-uncblk(dc_imn)[None], s5_d=dd, s5_glu_w=dgw[None], s5_glu_b=dgb,
        ssd_conv_w=dcw[None], ssd_conv_b=dcb, ssd_dt_bias=ddtb[:, :24], ssd_a_log=dalog[:, :24], ssd_d=ddsk[:, :24],
        ssd_norm=dng.reshape(1, 1536))
    return dproj, grads, (got_ssd[0] if got_ssd else None, got_s5[0] if got_s5 else None)


def _odd_prep(w):
    pad128 = lambda x: _pad_rows(x, LANES)
    return dict(
        mu=_mu_cols(w["rwkv_mu"].reshape(1, 3520)), w0=w["rwkv_w0"].reshape(1, 1024), w_up=pad128(w["rwkv_w_up"].reshape(96, 1024)),
        a0=w["rwkv_a0"].reshape(1, 1024), a_up=pad128(w["rwkv_a_up"].reshape(96, 1024)), g_up=w["rwkv_g_up"].reshape(256, 1024),
        k_k=w["rwkv_k_k"].reshape(1, 1024), k_a=w["rwkv_k_a"].reshape(1, 1024), r_k=w["rwkv_r_k"].reshape(1, 1024),
        ln_g=w["rwkv_ln_g"].reshape(1, 1024), ln_b=w["rwkv_ln_b"].reshape(1, 1024), j=_head_ones(1024),
        conv_w=w["lru_conv_w"].reshape(4, 1024), conv_b=w["lru_conv_b"].reshape(1, 1024),
        wax=jnp.concatenate([_blockdiag(w["lru_w_a"].reshape(16, 64, 64)), _blockdiag(w["lru_w_x"].reshape(16, 64, 64))], axis=1),
        bax=jnp.concatenate([w["lru_b_a"].reshape(1, 1024), w["lru_b_x"].reshape(1, 1024)], axis=1), lam=w["lru_lam"].reshape(1, 1024))


_O_XMAP = lambda j: jnp.where(j < 24, j, j + 16)
_O_LMAP = lambda j: 24 + j


def _to_heads(x):
    return x.reshape(x.shape[0], 16, HEAD).transpose(0, 2, 1)


def _from_heads(x3):
    return x3.transpose(0, 2, 1).reshape(x3.shape[0], 16 * HEAD)


def _odd_rows(fp, y, k2, g):
    pre = [_row(fp, 1024, 1), _row(fp, 256, 12), _row(fp, 128, 26), _row(fp, 128, 27)]
    post = None if y is None else [_row(y), _row(fp, 1024, 0), _row(k2), _row(fp, 1024, 2), _row(g)]
    return pre, post


def _odd_fwd(proj, p, carry=None):
    fp = _ct_fwd(_tshift, proj, _O_XMAP, 28, [(p["mu"], 0)], 3584, name="rwkv_shift")
    pre_rows, _ = _odd_rows(fp, None, None, None)
    pre_params = [p["w0"], p["w_up"], p["a0"], p["a_up"], p["g_up"], p["k_k"], p["k_a"], p["j"]]
    decay, kkn, b, k2, g = _rw_fwd(_rwkv_pre, pre_rows, pre_params, [(1024, F32)] * 5, name="rwkv_pre")
    v3 = _split3(_to_heads(fp[:, 2048:3072]))
    y3, saved, *carried = _rwkv_fwd(decay, kkn, b, k2, fp, v3, name="rwkv_scan", carry=carry)
    y = _from_heads(y3)
    _, post_rows = _odd_rows(fp, y, k2, g)
    (yc,) = _rw_fwd(_rwkv_post, post_rows, [p["r_k"], p["ln_g"], p["ln_b"], p["j"]], [(1024, F32)], name="rwkv_post")
    xc = _ct_fwd(_conv_taps, proj, _O_LMAP, 8, [(p["conv_w"], 0), (p["conv_b"], 0)], 1024, name="lru_conv")
    pre = _matmul(xc, p["wax"], "nn", name="lru_gates")
    a, bx = _rw_fwd(_lru_pre, [_row(pre), _row(xc)], [p["bax"], p["lam"]], [(1024, F32)] * 2, name="lru_pre")
    hseq = _rscan_fwd(a, bx, name="lru_scan")
    (yd,) = _rw_fwd(_lru_post, [_row(hseq), _row(proj, 1024, 4)], [], [(1024, F32)], name="lru_post")
    return jnp.concatenate([yc, yd], axis=1), (fp, decay, kkn, b, k2, g, v3, saved, y, xc, pre, a, hseq), (carried[0] if carried else None)


def _odd_bwd(proj, p, res, dy, carry=None):
    fp, decay, kkn, b, k2, g, v3, saved, y, xc, pre, a, hseq = res
    s = proj.shape[0]
    (dh, dgl2), _ = _rw_bwd(_lru_post, [_row(hseq), _row(proj, 1024, 4)], [], [dy[:, 1024:]], name="lru_post_bwd")
    dbx, da = _rscan_bwd(dh, a, hseq, name="lru_scan_bwd")
    (dpre, dxc), (dbax, dlam) = _rw_bwd(_lru_pre, [_row(pre), _row(xc)], [p["bax"], p["lam"]], [da, dbx], name="lru_pre_bwd")
    dxc = _matmul(dpre, p["wax"], "nt", name="lru_gates_dx", add=dxc)
    dwax = _matmul(xc, dpre, "tn", name="lru_gates_dw")
    dxl, (dlcw, dlcb) = _ct_bwd(_conv_taps, proj, _O_LMAP, 8, [(p["conv_w"], 0), (p["conv_b"], 0)], [dxc], name="lru_conv_bwd")
    pre_rows, post_rows = _odd_rows(fp, y, k2, g)
    (dyn, dr1, dk2a, dv1, dg), (dr_k, dln_g, dln_b) = _rw_bwd(
        _rwkv_post, post_rows, [p["r_k"], p["ln_g"], p["ln_b"], p["j"]], [dy[:, :1024]], name="rwkv_post_bwd", param_grads=[0, 1, 2])
    ddecay, dkkn, db, dk2b, dr2, dv3, *carried = _rwkv_bwd(
        decay, kkn, b, k2, fp, v3, saved, _split3(_to_heads(dyn)), name="rwkv_scan_bwd", carry=carry)
    pre_params = [p["w0"], p["w_up"], p["a0"], p["a_up"], p["g_up"], p["k_k"], p["k_a"], p["j"]]
    (dk, dgl, dwl, dal), (dw0, dw_up, da0, da_up, dg_up, dk_k, dk_a) = _rw_bwd(
        _rwkv_pre, pre_rows, pre_params, [ddecay, dkkn, db, [dk2a, dk2b], dg], name="rwkv_pre_bwd", param_grads=list(range(7)))
    z = lambda n: jnp.zeros((s, n), F32)
    g1 = jnp.concatenate([dr1, dk, dv1, dgl, dwl, dal], axis=1)
    g2 = jnp.concatenate([dr2, z(1024), _from_heads(dv3), z(512)], axis=1)
    dfp, (dmu,) = _ct_bwd(_tshift, proj, _O_XMAP, 28, [(p["mu"], 0)], [g1, g2], name="rwkv_shift_bwd")
    dproj = jnp.concatenate([dfp[:, :3072], dxl, dgl2, dfp[:, 3072:]], axis=1)
    grads = dict(
        rwkv_mu=_mu_cols_t(dmu), rwkv_w0=dw0, rwkv_w_up=dw_up[:96][None], rwkv_a0=da0, rwkv_a_up=da_up[:96][None], rwkv_g_up=dg_up[None],
        rwkv_k_k=dk_k, rwkv_k_a=dk_a, rwkv_r_k=dr_k.reshape(1, 16, 64), rwkv_ln_g=dln_g, rwkv_ln_b=dln_b,
        lru_conv_w=dlcw[None], lru_conv_b=dlcb, lru_w_a=_blockdiag_t(dwax[:, :1024], 16)[None], lru_w_x=_blockdiag_t(dwax[:, 1024:], 16)[None],
        lru_b_a=dbax[:, :1024].reshape(1, 16, 64), lru_b_x=dbax[:, 1024:].reshape(1, 16, 64), lru_lam=dlam.reshape(1, 16, 64))
    return dproj, grads, (carried[0] if carried else None)


def _my_index():
    return 4 * lax.axis_index("x") + 2 * lax.axis_index("y") + lax.axis_index("c")


def _peer(k):
    x, y, c = lax.axis_index("x"), lax.axis_index("y"), lax.axis_index("c")
    return (1 - x if k & 4 else x, 1 - y if k & 2 else y, 1 - c if k & 1 else c)


class _Exchange:
    SCRATCH = (pltpu.SemaphoreType.DMA((N_DEV - 1,)), pltpu.SemaphoreType.DMA((N_DEV - 1,)), pltpu.SemaphoreType.DMA)

    def __init__(self, src_ref, out_ref, send_sems, recv_sems, local_sem, gather):
        me = _my_index()
        mine = src_ref if gather else src_ref.at[me]
        self.local = pltpu.make_async_copy(mine, out_ref.at[me], local_sem)
        rdma = lambda src, dst, k: pltpu.make_async_remote_copy(
            src_ref=src, dst_ref=dst, send_sem=send_sems.at[k - 1], recv_sem=recv_sems.at[k - 1],
            device_id=_peer(k), device_id_type=pl.DeviceIdType.MESH)
        ks = range(1, N_DEV)
        self.sends = [rdma(src_ref if gather else src_ref.at[jnp.bitwise_xor(me, k)], out_ref.at[me], k) for k in ks]
        self.arrivals = [rdma(mine, out_ref.at[jnp.bitwise_xor(me, k)], k) for k in ks]

    def start(self):
        self.local.start()
        for cp in self.sends:
            cp.start()

    def wait(self):
        for cp in self.arrivals:
            cp.wait_recv()
        for cp in self.sends:
            cp.wait_send()
        self.local.wait()


def _exchange(src, *, gather, name):
    def body(src_ref, out_ref, *sems):
        ex = _Exchange(src_ref, out_ref, *sems, gather)
        ex.start()
        ex.wait()

    return pl.pallas_call(
        body, out_shape=jax.ShapeDtypeStruct((N_DEV, src.shape[-2], LANES), src.dtype),
        in_specs=[pl.BlockSpec(memory_space=pl.ANY)], out_specs=pl.BlockSpec(memory_space=pl.ANY),
        scratch_shapes=list(_Exchange.SCRATCH), name=name)(src)


PACK_ALIGN = 16 * LANES
PACK_ROWS = 512


def _pack(arrs, dtype, lead=False):
    parts, rows = [], 0
    for a in arrs:
        n_lead = a.shape[0] if lead else 1
        n = a.size // n_lead
        a = a.astype(dtype)
        if n % PACK_ALIGN:
            a = jnp.pad(a.reshape(n_lead, n), ((0, 0), (0, -n % PACK_ALIGN)))
        parts.append(a.reshape(n_lead, -1, LANES))
        rows += parts[-1].shape[1]
    if rows % PACK_ROWS:
        parts.append(jnp.zeros((parts[0].shape[0], -rows % PACK_ROWS, LANES), dtype))
    buf = jnp.concatenate(parts, axis=1)
    return buf if lead else buf[0]


def _unpack(buf, shapes, lead=False):
    buf = buf if lead else buf[None]
    out, off = [], 0
    for shp in shapes:
        n = math.prod(shp)
        rows = (n + (-n % PACK_ALIGN)) // LANES
        piece = buf[:, off:off + rows]
        if n % PACK_ALIGN:
            piece = piece.reshape(buf.shape[0], rows * LANES)[:, :n]
        out.append(piece.reshape(((buf.shape[0],) if lead else ()) + tuple(shp)))
        off += rows
    return out


def _unshard(parts, axis):
    moved = jnp.moveaxis(parts, 0, axis)
    shp = list(moved.shape)
    return moved.reshape(shp[:axis] + [shp[axis] * shp[axis + 1]] + shp[axis + 2:])


def _to_parts(full, axis):
    shp = list(full.shape)
    split = full.reshape(shp[:axis] + [N_DEV, shp[axis] // N_DEV] + shp[axis + 1:])
    return jnp.moveaxis(split, axis, 0)


def _adamw(gparts, w, m, v, *, name):
    r = w.shape[0]
    tr = _pick(r, (PACK_ROWS,))

    def body(g_ref, w_ref, m_ref, v_ref, go, do, mo, vo):
        g = g_ref[0].astype(F32)
        for d in range(1, N_DEV):
            g = g + g_ref[d].astype(F32)
        m1 = ADAM_B1 * m_ref[...] + (1.0 - ADAM_B1) * g
        v1 = ADAM_B2 * v_ref[...] + (1.0 - ADAM_B2) * jnp.square(g)
        m_hat = m1 / (1.0 - ADAM_B1 ** ADAM_STEP)
        v_hat = v1 / (1.0 - ADAM_B2 ** ADAM_STEP)
        go[...] = g
        do[...] = -ADAM_LR * (m_hat / (jnp.sqrt(v_hat) + ADAM_EPS) + ADAM_WD * w_ref[...])
        mo[...] = m1
        vo[...] = v1

    blk = pl.BlockSpec((tr, LANES), lambda i: (i, 0))
    return pl.pallas_call(
        body, grid=(r // tr,), in_specs=[pl.BlockSpec((N_DEV, tr, LANES), lambda i: (0, i, 0)), blk, blk, blk], out_specs=[blk] * 4,
        out_shape=[jax.ShapeDtypeStruct((r, LANES), F32)] * 4, compiler_params=_cparams("parallel"), name=name)(gparts, w, m, v)


def _loss_and_grad(h, g, tgt, *, name, ts=256):
    s, d = h.shape
    ts = min(ts, s)

    def tile_loss(hv, gv, tv):
        (y,) = _rmsnorm_tile(0, hv, gv)
        return 0.5 * jnp.sum(jnp.mean(jnp.square(y - tv), axis=-1))

    def body(h_ref, g_ref, t_ref, l_ref, dh_ref, dg_ref):
        @pl.when(pl.program_id(0) == 0)
        def _():
            l_ref[...] = jnp.zeros_like(l_ref)
            dg_ref[...] = jnp.zeros_like(dg_ref)

        tv = t_ref[...]
        loss, vjp = jax.vjp(lambda hv, gv: tile_loss(hv, gv, tv), h_ref[...], g_ref[...])
        dh, dg = vjp(jnp.ones((), F32))
        l_ref[...] += loss
        dh_ref[...] = dh
        dg_ref[...] += dg

    row = pl.BlockSpec((ts, d), lambda i: (i, 0))
    return pl.pallas_call(
        body, grid=(s // ts,), in_specs=[row, pl.BlockSpec((1, d), lambda i: (0, 0)), row],
        out_specs=[pl.BlockSpec((8, LANES), lambda i: (0, 0)), row, pl.BlockSpec((1, d), lambda i: (0, 0))],
        out_shape=[jax.ShapeDtypeStruct((8, LANES), F32), jax.ShapeDtypeStruct((s, d), F32), jax.ShapeDtypeStruct((1, d), F32)],
        compiler_params=_cparams("arbitrary"), name=name)(h, g, tgt)


def _norm(h, g, name):
    return _rw_fwd(_rmsnorm_tile, [_row(h)], [g], [(h.shape[1], F32)], name=name)[0]


def _norm_bwd(h, g, dhn, dres, name):
    (dh,), (dg,) = _rw_bwd(_rmsnorm_tile, [_row(h)], [g], [dhn], add_rows={0: dres}, name=name)
    return dh, dg


def _layer_fwd(h, p_i, lw, mixer_fwd, mp, tag, carry=None, late=None):
    hn = _norm(h, lw["norm_mix"], f"{tag}_norm_mix")
    proj = _matmul(hn, lw["w_in"], "nn", name=f"{tag}_in_proj")
    y, mres, carried = mixer_fwd(proj, mp, carry)
    if late is not None:
        lw = {**lw, **late(carried)}
    h1 = _matmul(y, lw["w_out"], "nn", name=f"{tag}_out_proj", add=h)
    hn2 = _norm(h1, lw["norm_ffn"], f"{tag}_norm_ffn")
    act = _matmul(hn2, lw["w1"], "nn", name=f"{tag}_mlp_up", post=_relu2_of)
    h2 = _matmul(act, lw["w2"], "nn", name=f"{tag}_mlp_down", add=h1)
    hn3 = _norm(h2, lw["norm_pl"], f"{tag}_norm_pl")
    gl = _matmul(hn3, lw["w_gate"], "nn", name=f"{tag}_pl_gate")
    e = _matmul(p_i, lw["w_pl"], "nn", name=f"{tag}_pl_proj")
    (h3,) = _rw_fwd(_pl_gate, [_row(h2), _row(gl), _row(e)], [], [(h.shape[1], F32)], name=f"{tag}_pl_mix")
    return h3, (h, hn, proj, y, mres, h1, hn2, act, h2, hn3, gl, e), lw


def _layer_bwd(dh3, p_i, lw, mixer_bwd, mp, saved, tag, early=None):
    h, hn, proj, y, mres, h1, hn2, act, h2, hn3, gl, e = saved
    (dgl, de), _ = _rw_bwd(_pl_gate, [_row(h2), _row(gl), _row(e)], [], [dh3], row_grads=[1, 2], name=f"{tag}_pl_mix_bwd")
    g = dict(w_pl=_matmul(p_i, de, "tn", name=f"{tag}_pl_proj_dw"), w_gate=_matmul(hn3, dgl, "tn", name=f"{tag}_pl_gate_dw"))
    dhn3 = _matmul(dgl, lw["w_gate"], "nt", name=f"{tag}_pl_gate_dx")
    dh2, g["norm_pl"] = _norm_bwd(h2, lw["norm_pl"], dhn3, dh3, f"{tag}_norm_pl_bwd")
    da1 = _matmul(dh2, lw["w2"], "nt", name=f"{tag}_mlp_down_dx", post=_relu2_grad, aux=act)
    g["w2"] = _matmul(act, dh2, "tn", name=f"{tag}_mlp_down_dw")
    g["w1"] = _matmul(hn2, da1, "tn", name=f"{tag}_mlp_up_dw")
    dhn2 = _matmul(da1, lw["w1"], "nt", name=f"{tag}_mlp_up_dx")
    dh1, g["norm_ffn"] = _norm_bwd(h1, lw["norm_ffn"], dhn2, dh2, f"{tag}_norm_ffn_bwd")
    dy = _matmul(dh1, lw["w_out"], "nt", name=f"{tag}_out_proj_dx")
    g["w_out"] = _matmul(y, dh1, "tn", name=f"{tag}_out_proj_dw")
    dproj, mg, carried = mixer_bwd(proj, mp, mres, dy, None if early is None else early(g))
    g["w_in"] = _matmul(hn, dproj, "tn", name=f"{tag}_in_proj_dw")
    dhn = _matmul(dproj, lw["w_in"], "nt", name=f"{tag}_in_proj_dx")
    dh, g["norm_mix"] = _norm_bwd(h, lw["norm_mix"], dhn, dh1, f"{tag}_norm_mix_bwd")
    return dh, g, mg, carried


WEIGHTS = (
    ("norm_mix", None), ("norm_ffn", None), ("norm_pl", None), ("mlp_w1", 2), ("mlp_w2", 1), ("pl_proj", 2), ("pl_gate", 1),
    ("e_in_proj", 2), ("e_out_proj", 1), ("s5_lam_re", None), ("s5_lam_im", None), ("s5_log_step", None), ("s5_b_re", None),
    ("s5_b_im", None), ("s5_c_re", None), ("s5_c_im", None), ("s5_d", None), ("s5_glu_w", 1), ("s5_glu_b", None),
    ("ssd_conv_w", 2), ("ssd_conv_b", None), ("ssd_dt_bias", None), ("ssd_a_log", None), ("ssd_d", None), ("ssd_norm", None),
    ("o_in_proj", 2), ("o_out_proj", 1), ("rwkv_mu", 1), ("rwkv_w0", 1), ("rwkv_w_up", 2), ("rwkv_a0", 1), ("rwkv_a_up", 2),
    ("rwkv_g_up", 2), ("rwkv_k_k", 1), ("rwkv_k_a", 1), ("rwkv_r_k", None), ("rwkv_ln_g", 1), ("rwkv_ln_b", 1),
    ("lru_conv_w", 2), ("lru_conv_b", 1), ("lru_w_a", None), ("lru_b_a", None), ("lru_w_x", None), ("lru_b_x", None),
    ("lru_lam", None), ("norm_final", None))
MATMUL_WEIGHTS = ("mlp_w1", "mlp_w2", "pl_proj", "pl_gate", "e_in_proj", "e_out_proj", "s5_glu_w", "o_in_proj", "o_out_proj",
                  "rwkv_w_up", "rwkv_a_up", "rwkv_g_up")


LATE_WEIGHTS = (("mlp_w1", 1), ("mlp_w2", 1), ("pl_gate", 1), ("pl_proj", 1))
EARLY_GRADS = LATE_WEIGHTS + (("o_out_proj", 0),)


def kernel(x, p, norm_mix, norm_ffn, norm_pl, mlp_w1, mlp_w2, pl_proj, pl_gate, e_in_proj, e_out_proj, s5_lam_re, s5_lam_im, s5_log_step, s5_b_re, s5_b_im, s5_c_re, s5_c_im, s5_d, s5_glu_w, s5_glu_b, ssd_conv_w, ssd_conv_b, ssd_dt_bias, ssd_a_log, ssd_d, ssd_norm, o_in_proj, o_out_proj, rwkv_mu, rwkv_w0, rwkv_w_up, rwkv_a0, rwkv_a_up, rwkv_g_up, rwkv_k_k, rwkv_k_a, rwkv_r_k, rwkv_ln_g, rwkv_ln_b, lru_conv_w, lru_conv_b, lru_w_a, lru_b_a, lru_w_x, lru_b_x, lru_lam, norm_final, loss_target, m_norm_mix, m_norm_ffn, m_norm_pl, m_mlp_w1, m_mlp_w2, m_pl_proj, m_pl_gate, m_e_in_proj, m_e_out_proj, m_s5_lam_re, m_s5_lam_im, m_s5_log_step, m_s5_b_re, m_s5_b_im, m_s5_c_re, m_s5_c_im, m_s5_d, m_s5_glu_w, m_s5_glu_b, m_ssd_conv_w, m_ssd_conv_b, m_ssd_dt_bias, m_ssd_a_log, m_ssd_d, m_ssd_norm, m_o_in_proj, m_o_out_proj, m_rwkv_mu, m_rwkv_w0, m_rwkv_w_up, m_rwkv_a0, m_rwkv_a_up, m_rwkv_g_up, m_rwkv_k_k, m_rwkv_k_a, m_rwkv_r_k, m_rwkv_ln_g, m_rwkv_ln_b, m_lru_conv_w, m_lru_conv_b, m_lru_w_a, m_lru_b_a, m_lru_w_x, m_lru_b_x, m_lru_lam, m_norm_final, v_norm_mix, v_norm_ffn, v_norm_pl, v_mlp_w1, v_mlp_w2, v_pl_proj, v_pl_gate, v_e_in_proj, v_e_out_proj, v_s5_lam_re, v_s5_lam_im, v_s5_log_step, v_s5_b_re, v_s5_b_im, v_s5_c_re, v_s5_c_im, v_s5_d, v_s5_glu_w, v_s5_glu_b, v_ssd_conv_w, v_ssd_conv_b, v_ssd_dt_bias, v_ssd_a_log, v_ssd_d, v_ssd_norm, v_o_in_proj, v_o_out_proj, v_rwkv_mu, v_rwkv_w0, v_rwkv_w_up, v_rwkv_a0, v_rwkv_a_up, v_rwkv_g_up, v_rwkv_k_k, v_rwkv_k_a, v_rwkv_r_k, v_rwkv_ln_g, v_rwkv_ln_b, v_lru_conv_w, v_lru_conv_b, v_lru_w_a, v_lru_b_a, v_lru_w_x, v_lru_b_x, v_lru_lam, v_norm_final):
    a = dict(locals())
    d_model = x.shape[-1]
    row = lambda v: v.reshape(1, d_model)
    axis = dict(WEIGHTS)
    keys = [(n, i) for n, ax in WEIGHTS if ax is not None for i in range(a[n].shape[0])]
    shard = lambda pre, key: a[pre + key[0]][key[1]]
    piece_axis = lambda key: axis[key[0]] - 1

    def gathered(got, ks):
        parts = _unpack(got, [shard("", k).shape for k in ks], lead=True)
        return {k: _unshard(pt, piece_axis(k)) for k, pt in zip(ks, parts)}

    in_layer1 = lambda k: k[1] == 1 or k[0].startswith(("o_", "rwkv_", "lru_"))
    first_keys = [k for k in keys if k[0] in MATMUL_WEIGHTS and not in_layer1(k)]
    mix1_keys = [k for k in keys if k[0] in MATMUL_WEIGHTS and in_layer1(k) and k not in LATE_WEIGHTS]
    f32_keys = [k for k in keys if k[0] not in MATMUL_WEIGHTS]
    full = gathered(_exchange(_pack([shard("", k) for k in first_keys], BF16), gather=True, name="gather_weights_bf16"), first_keys)
    full.update(gathered(_exchange(_pack([shard("", k) for k in f32_keys], F32), gather=True, name="gather_weights_f32"), f32_keys))
    by_name = lambda: {n: (a[n] if ax is None else full.get((n, 0))) for n, ax in WEIGHTS}

    lw0 = dict(norm_mix=row(norm_mix[0]), norm_ffn=row(norm_ffn[0]), norm_pl=row(norm_pl[0]), w_in=_even_in_cols(full["e_in_proj", 0]),
               w_out=full["e_out_proj", 0], w1=full["mlp_w1", 0], w2=full["mlp_w2", 0], w_gate=full["pl_gate", 0], w_pl=full["pl_proj", 0])
    mp0 = _even_prep(by_name())
    mix1_src = _pack([shard("", k) for k in mix1_keys], BF16)

    def take_mix1(got):
        full.update(gathered(got, mix1_keys))
        return {}

    h1, saved0, lw0 = _layer_fwd(x[0], p[0, 0], lw0, _even_fwd, mp0, "l0", carry=(mix1_src, True), late=take_mix1)

    def late(got):
        fl = gathered(got, LATE_WEIGHTS)
        return dict(w1=fl["mlp_w1", 1], w2=fl["mlp_w2", 1], w_gate=fl["pl_gate", 1], w_pl=fl["pl_proj", 1])

    lw1 = dict(norm_mix=row(norm_mix[1]), norm_ffn=row(norm_ffn[1]), norm_pl=row(norm_pl[1]), w_in=_odd_in_cols(full["o_in_proj", 0]),
               w_out=full["o_out_proj", 0])
    mp1 = _odd_prep(by_name())
    late_src = _pack([shard("", k) for k in LATE_WEIGHTS], BF16)
    h2, saved1, lw1 = _layer_fwd(h1, p[1, 0], lw1, _odd_fwd, mp1, "l1", carry=(late_src, True), late=late)
    loss_blk, dh, dg_final = _loss_and_grad(h2, row(norm_final), loss_target[0], name="loss")
    loss = lax.psum(loss_blk[0, 0], ("x", "y", "c"))

    def slabs(grad_of, ks):
        return _pack([_to_parts(grad_of[k], piece_axis(k)) for k in ks], BF16, lead=True)

    def global_shape(k):
        return tuple((N_DEV if d == piece_axis(k) else 1) * n for d, n in enumerate(shard("", k).shape))

    early_grads = lambda g: (slabs({("mlp_w1", 1): g["w1"], ("mlp_w2", 1): g["w2"], ("pl_gate", 1): g["w_gate"],
                                    ("pl_proj", 1): g["w_pl"], ("o_out_proj", 0): g["w_out"]}, EARLY_GRADS), False)
    dh, g1, mg1, early_got = _layer_bwd(dh, p[1, 0], lw1, _odd_bwd, mp1, saved1, "l1", early=early_grads)
    rest1_keys = [k for k in keys if in_layer1(k) and k not in EARLY_GRADS]
    rest1_grad = {k: (_odd_in_cols_t(g1["w_in"]) if k[0] == "o_in_proj" else mg1[k[0]].reshape(global_shape(k))) for k in rest1_keys}
    l0a_keys = [("mlp_w1", 0), ("pl_proj", 0)]
    l0_early = lambda g: ((slabs(rest1_grad, rest1_keys), False), (slabs({("mlp_w1", 0): g["w1"], ("pl_proj", 0): g["w_pl"]}, l0a_keys), False))
    dh, g0, mg0, (rest1_got, l0a_got) = _layer_bwd(dh, p[0, 0], lw0, _even_bwd, mp0, saved0, "l0", early=l0_early)

    main_keys = [k for k in keys if not in_layer1(k) and k not in l0a_keys]
    piece_grad = {("mlp_w2", 0): g0["w2"], ("pl_gate", 0): g0["w_gate"], ("e_in_proj", 0): _even_in_cols_t(g0["w_in"]), ("e_out_proj", 0): g0["w_out"]}
    mixer_grads = {**mg0, **mg1}
    for k in main_keys:
        if k not in piece_grad:
            piece_grad[k] = mg0[k[0]].reshape(global_shape(k))
    main_got = _exchange(slabs(piece_grad, main_keys), gather=False, name="scatter_grads")
    piece_out = {}
    for got, ks, tag in ((early_got, EARLY_GRADS, "early"), (rest1_got, rest1_keys, "rest1"), (l0a_got, l0a_keys, "l0a"), (main_got, main_keys, "main")):
        res = _adamw(got, *[_pack([shard(pre, k) for k in ks], F32) for pre in ("", "m_", "v_")], name=f"adamw_sharded_{tag}")
        for j in range(4):
            piece_out.update({(j, k): v for k, v in zip(ks, _unpack(res[j], [shard("", k).shape for k in ks]))})
    rp = [n for n, ax in WEIGHTS if ax is None]
    rp_grads = {**mixer_grads, "norm_final": dg_final,
                **{n: jnp.stack([g0[n], g1[n]]) for n in ("norm_mix", "norm_ffn", "norm_pl")}}
    parts = _exchange(_pack([rp_grads[n].reshape(a[n].shape) for n in rp], F32), gather=True, name="gather_small_grads")
    rp_res = _adamw(parts, *[_pack([a[pre + n] for n in rp], F32) for pre in ("", "m_", "v_")], name="adamw_replicated")

    outs = []
    for j in range(4):
        rp_out = dict(zip(rp, _unpack(rp_res[j], [a[n].shape for n in rp])))
        outs.extend(rp_out[n] if ax is None else jnp.stack([piece_out[j, (n, i)] for i in range(a[n].shape[0])]) for n, ax in WEIGHTS)
    return (loss, dh[None], *outs)
```

```python
import functools
import math

import jax
import jax.numpy as jnp
from jax import lax
from jax.experimental import pallas as pl
from jax.experimental.pallas import tpu as pltpu

F32 = jnp.float32
BF16 = jnp.bfloat16
N_DEV = 8
LANES = 128
VMEM_LIMIT = 56 * 1024 * 1024
MATMUL_VMEM = 46 * 1024 * 1024
NORM_EPS = 1e-6
RWKV_GN_EPS = 64e-5
LRU_C = 8.0
ADAM_LR, ADAM_B1, ADAM_B2, ADAM_EPS, ADAM_WD, ADAM_STEP = 0.001, 0.9, 0.999, 1e-08, 0.01, 10
SSD_CHUNK = 128
RWKV_CHUNK = 32
HEAD = 64


def _cparams(*sem):
    return pltpu.CompilerParams(dimension_semantics=sem, vmem_limit_bytes=VMEM_LIMIT)


def _pick(n, prefs):
    for t in prefs:
        if n % t == 0:
            return t
    return n


def _relu2_of(val, _):
    r = jnp.maximum(val, 0.0)
    return r * r


def _relu2_grad(dact, act):
    return dact * (2.0 * jnp.sqrt(act))


def _matmul(a, b, mode, *, name, add=None, out_dtype=F32, post=None, aux=None):
    if mode == "nn":
        (m, k), (k2, n) = a.shape, b.shape
    elif mode == "nt":
        (m, k), (n, k2) = a.shape, b.shape
    else:
        (k, m), (k2, n) = a.shape, b.shape
    assert k == k2, (a.shape, b.shape, mode)
    tm, tn = _pick(m, (1024, 512, 256, 128)), _pick(n, (1024, 512, 256, 128))
    extras = [x for x in (add, aux) if x is not None]
    fits = lambda t: 2 * t * (tm * a.dtype.itemsize + tn * b.dtype.itemsize) + (3 + 2 * len(extras)) * tm * tn * 4 <= MATMUL_VMEM
    tk = next((t for t in (2048, 1024, 512, 256, 128) if k % t == 0 and fits(t)), k)
    nk = k // tk
    dn = {"nn": (((1,), (0,)), ((), ())), "nt": (((1,), (1,)), ((), ())), "tn": (((0,), (0,)), ((), ()))}[mode]
    a_spec = pl.BlockSpec((tk, tm), lambda i, j, kk: (kk, i)) if mode == "tn" else pl.BlockSpec((tm, tk), lambda i, j, kk: (i, kk))
    b_spec = pl.BlockSpec((tn, tk), lambda i, j, kk: (j, kk)) if mode == "nt" else pl.BlockSpec((tk, tn), lambda i, j, kk: (kk, j))
    o_spec = pl.BlockSpec((tm, tn), lambda i, j, kk: (i, j))

    def body(a_ref, b_ref, *rest):
        extra_refs, (o_ref, acc) = list(rest[:len(extras)]), rest[len(extras):]
        add_ref = extra_refs.pop(0) if add is not None else None
        aux_ref = extra_refs.pop(0) if aux is not None else None
        kk = pl.program_id(2)
        prod = lambda: lax.dot_general(a_ref[...].astype(BF16), b_ref[...].astype(BF16), dn, preferred_element_type=F32)
        first = lambda: prod() if add is None else prod() + add_ref[...].astype(F32)

        def write(val):
            if post is not None:
                val = post(val, None if aux is None else aux_ref[...])
            o_ref[...] = val.astype(o_ref.dtype)

        if nk == 1:
            write(first())
            return

        @pl.when(kk == 0)
        def _():
            acc[...] = first()

        @pl.when((kk > 0) & (kk < nk - 1))
        def _():
            acc[...] += prod()

        @pl.when(kk == nk - 1)
        def _():
            write(acc[...] + prod())

    ins, specs = [a, b] + extras, [a_spec, b_spec] + [o_spec] * len(extras)
    return pl.pallas_call(
        body, grid=(m // tm, n // tn, nk), in_specs=specs, out_specs=o_spec,
        out_shape=jax.ShapeDtypeStruct((m, n), out_dtype), scratch_shapes=[pltpu.VMEM((tm, tn), F32)],
        compiler_params=_cparams("parallel", "parallel", "arbitrary"), name=name)(*ins)


def _row(x, width=None, block=0):
    return (x, x.shape[1] if width is None else width, block)


def _rw_specs(rows, params, ts):
    specs = [pl.BlockSpec((ts, w), functools.partial(lambda i, b: (i, b), b=bi)) for (_, w, bi) in rows]
    specs += [pl.BlockSpec(p.shape, functools.partial(lambda i, nd: (0,) * nd, nd=p.ndim)) for p in params]
    return specs


def _rw_fwd(f, rows, params, outs, *, name, ts=256):
    s = rows[0][0].shape[0]
    ts = min(ts, s)
    nr, npar = len(rows), len(params)

    def body(*refs):
        row0 = pl.program_id(0) * ts
        res = f(row0, *[r[...] for r in refs[:nr + npar]])
        for o_ref, val in zip(refs[nr + npar:], res, strict=True):
            o_ref[...] = val.astype(o_ref.dtype)

    out = pl.pallas_call(
        body, grid=(s // ts,), in_specs=_rw_specs(rows, params, ts),
        out_specs=[pl.BlockSpec((ts, w), lambda i: (i, 0)) for (w, _) in outs],
        out_shape=[jax.ShapeDtypeStruct((s, w), dt) for (w, dt) in outs],
        compiler_params=_cparams("parallel"), name=name)(*[r[0] for r in rows], *params)
    return tuple(out)


def _rw_bwd(f, rows, params, cts, *, name, ts=256, row_grads=None, param_grads=None, add_rows=None):
    s = rows[0][0].shape[0]
    ts = min(ts, s)
    ct_groups = [list(c) if isinstance(c, (list, tuple)) else [c] for c in cts]
    cts = [c for grp in ct_groups for c in grp]
    nr, npar, nct = len(rows), len(params), len(cts)
    row_grads = list(range(nr)) if row_grads is None else list(row_grads)
    param_grads = list(range(npar)) if param_grads is None else list(param_grads)
    add_rows = add_rows or {}
    add_keys = sorted(add_rows)

    def body(*refs):
        i = pl.program_id(0)
        row0 = i * ts
        vals = [r[...] for r in refs[:nr + npar]]
        for pi in param_grads:
            vals[nr + pi] = vals[nr + pi].astype(F32)
        ct_refs = list(refs[nr + npar:nr + npar + nct])
        add_refs = dict(zip(add_keys, refs[nr + npar + nct:nr + npar + nct + len(add_keys)]))
        o_refs = refs[nr + npar + nct + len(add_keys):]
        res, vjp = jax.vjp(functools.partial(f, row0), *vals)
        ct_vals = []
        for grp, r in zip(ct_groups, res, strict=True):
            ct_vals.append(sum(ct_refs.pop(0)[...].astype(r.dtype) for _ in grp))
        grads = vjp(tuple(ct_vals))
        for o_ref, ri in zip(o_refs[:len(row_grads)], row_grads):
            g = grads[ri]
            if ri in add_refs:
                g = g + add_refs[ri][...]
            o_ref[...] = g.astype(o_ref.dtype)

        @pl.when(i == 0)
        def _():
            for o_ref in o_refs[len(row_grads):]:
                o_ref[...] = jnp.zeros_like(o_ref)

        for o_ref, pi in zip(o_refs[len(row_grads):], param_grads):
            o_ref[...] += grads[nr + pi].astype(F32)

    in_specs = _rw_specs(rows, params, ts)
    in_specs += [pl.BlockSpec((ts, c.shape[1]), lambda i: (i, 0)) for c in cts]
    in_specs += [pl.BlockSpec((ts, add_rows[k].shape[1]), lambda i: (i, 0)) for k in add_keys]
    out_specs = [pl.BlockSpec((ts, rows[ri][1]), lambda i: (i, 0)) for ri in row_grads]
    out_specs += [pl.BlockSpec(params[pi].shape, functools.partial(lambda i, nd: (0,) * nd, nd=params[pi].ndim)) for pi in param_grads]
    out_shape = [jax.ShapeDtypeStruct((s, rows[ri][1]), F32) for ri in row_grads]
    out_shape += [jax.ShapeDtypeStruct(params[pi].shape, F32) for pi in param_grads]
    out = pl.pallas_call(
        body, grid=(s // ts,), in_specs=in_specs, out_specs=out_specs, out_shape=out_shape,
        compiler_params=_cparams("arbitrary"), name=name)(*[r[0] for r in rows], *params, *cts, *[add_rows[k] for k in add_keys])
    return tuple(out[:len(row_grads)]), tuple(out[len(row_grads):])


@functools.partial(jax.custom_vjp, nondiff_argnums=(1,))
def _shift_down(x, k):
    rows = lax.broadcasted_iota(jnp.int32, x.shape, 0)
    return jnp.where(rows < k, 0.0, pltpu.roll(x, k, 0))


def _shift_down_fwd(x, k):
    return _shift_down(x, k), None


def _shift_down_bwd(k, _, g):
    n = g.shape[0]
    rows = lax.broadcasted_iota(jnp.int32, g.shape, 0)
    return (jnp.where(rows >= n - k, 0.0, pltpu.roll(g, n - k, 0)),)


_shift_down.defvjp(_shift_down_fwd, _shift_down_bwd)


def _ct_fwd(f, x, xmap, ntiles, params, out_width, *, name, ct=LANES):
    s = x.shape[0]

    def body(*refs):
        refs[-1][...] = f(*[r[...] for r in refs[:-1]])

    in_specs = [pl.BlockSpec((s, ct), lambda j: (0, xmap(j)))]
    in_specs += [pl.BlockSpec((p.shape[0], ct), functools.partial(lambda j, o: (0, o + j), o=o)) for (p, o) in params]
    return pl.pallas_call(
        body, grid=(ntiles,), in_specs=in_specs, out_specs=pl.BlockSpec((s, ct), lambda j: (0, j)),
        out_shape=jax.ShapeDtypeStruct((s, out_width), F32), compiler_params=_cparams("parallel"), name=name)(x, *[p for p, _ in params])


def _ct_bwd(f, x, xmap, ntiles, params, g, *, name, ct=LANES):
    s = x.shape[0]
    npar, ng = len(params), len(g)

    def body(*refs):
        vals = [r[...] for r in refs[:1 + npar]]
        _, vjp = jax.vjp(f, *vals)
        grads = vjp(sum(r[...] for r in refs[1 + npar:1 + npar + ng]))
        for o_ref, gr in zip(refs[1 + npar + ng:], grads, strict=True):
            o_ref[...] = gr

    in_specs = [pl.BlockSpec((s, ct), lambda j: (0, xmap(j)))]
    pspecs = [pl.BlockSpec((p.shape[0], ct), functools.partial(lambda j, o: (0, o + j), o=o)) for (p, o) in params]
    in_specs += pspecs + [pl.BlockSpec((s, ct), lambda j: (0, j))] * ng
    out = pl.pallas_call(
        body, grid=(ntiles,), in_specs=in_specs, out_specs=[pl.BlockSpec((s, ct), lambda j: (0, j))] + pspecs,
        out_shape=[jax.ShapeDtypeStruct((s, ntiles * ct), F32)] + [jax.ShapeDtypeStruct(p.shape, F32) for p, _ in params],
        compiler_params=_cparams("parallel"), name=name)(x, *[p for p, _ in params], *g)
    return out[0], tuple(out[1:])


def _block_vjp(f, args, cts, *, name):
    outs = jax.eval_shape(f, *args)
    if cts is None:
        def body(*refs):
            for o_ref, v in zip(refs[len(args):], f(*[r[...] for r in refs[:len(args)]]), strict=True):
                o_ref[...] = v
        return tuple(pl.pallas_call(body, out_shape=[jax.ShapeDtypeStruct(o.shape, o.dtype) for o in outs], name=name)(*args))

    def body(*refs):
        n = len(args)
        _, vjp = jax.vjp(f, *[r[...] for r in refs[:n]])
        for o_ref, gr in zip(refs[n + len(cts):], vjp(tuple(r[...] for r in refs[n:n + len(cts)])), strict=True):
            o_ref[...] = gr
    return tuple(pl.pallas_call(body, out_shape=[jax.ShapeDtypeStruct(a.shape, a.dtype) for a in args], name=name)(*args, *cts))


def _softplus(x):
    return jnp.maximum(x, 0.0) + jnp.log(1.0 + jnp.exp(-jnp.abs(x)))


def _dot_bf16(a, b):
    return jnp.dot(a.astype(BF16), b.astype(BF16), preferred_element_type=F32)


def _dot3(x, m):
    hi = x.astype(BF16)
    r1 = x - hi.astype(F32)
    mid = r1.astype(BF16)
    lo = (r1 - mid.astype(F32)).astype(BF16)
    return (jnp.dot(hi, m, preferred_element_type=F32) + jnp.dot(mid, m, preferred_element_type=F32)
            + jnp.dot(lo, m, preferred_element_type=F32))


@jax.custom_vjp
def _lin(x, m, mt):
    return _dot3(x, m)


def _lin_fwd(x, m, mt):
    return _dot3(x, m), (m, mt)


def _lin_bwd(res, g):
    m, mt = res
    return _dot3(g, mt), jnp.zeros_like(m), jnp.zeros_like(mt)


_lin.defvjp(_lin_fwd, _lin_bwd)


def _head_ones(n, head=HEAD):
    i = jnp.arange(n) // head
    return (i[:, None] == i[None, :]).astype(BF16)


def _rmsnorm_tile(_, h, g):
    return (h * lax.rsqrt(jnp.mean(h * h, axis=-1, keepdims=True) + NORM_EPS) * g,)


ROWS = 8


def _shift_rows(x, k, fill, up=False):
    rows = lax.broadcasted_iota(jnp.int32, x.shape, 0)
    if up:
        return jnp.where(rows >= ROWS - k, fill, pltpu.roll(x, ROWS - k, 0))
    return jnp.where(rows < k, fill, pltpu.roll(x, k, 0))


def _cmul(pr, pi, qr, qi):
    return pr * qr - pi * qi, pr * qi + pi * qr


def _power_rows(ar, ai, width, descending):
    pows = [(ar, ai)]
    for _ in range(ROWS - 1):
        pows.append(_cmul(*pows[-1], ar, ai))
    rows = lax.broadcasted_iota(jnp.int32, (ROWS, width), 0)
    pr = jnp.zeros((ROWS, width), F32)
    pi = jnp.zeros((ROWS, width), F32)
    for j in range(ROWS):
        qr, qi = pows[ROWS - 1 - j] if descending else pows[j]
        pr, pi = jnp.where(rows == j, qr, pr), jnp.where(rows == j, qi, pi)
    return pr, pi, ((pows[0], 1), (pows[1], 2), (pows[3], 4))


def _prev_rows(ref, tile, r0):
    before = ref[pl.ds(jnp.maximum(r0 - 1, 0), 1), :] * (r0 > 0).astype(F32)
    rows = lax.broadcasted_iota(jnp.int32, tile.shape, 0)
    return jnp.where(rows == 0, before, pltpu.roll(tile, 1, 0))


def _cscan_fwd(bu, a, *, name, ct=256, carry=None):
    s, c2 = bu.shape
    c = c2 // 2
    nt = c // ct

    def body(br_ref, bi_ref, ar_ref, ai_ref, xr_ref, xi_ref):
        pr, pi, doubling = _power_rows(ar_ref[...], ai_ref[...], ct, False)

        def tile(i, h):
            rows = pl.ds(pl.multiple_of(i * ROWS, ROWS), ROWS)
            sr, si = br_ref[rows, :], bi_ref[rows, :]
            for (qr, qi), k in doubling:
                mr, mi = _cmul(qr, qi, _shift_rows(sr, k, 0.0), _shift_rows(si, k, 0.0))
                sr, si = sr + mr, si + mi
            cr, ci = _cmul(pr, pi, *h)
            sr, si = sr + cr, si + ci
            xr_ref[rows, :] = sr
            xi_ref[rows, :] = si
            return sr[ROWS - 1:], si[ROWS - 1:]

        z = jnp.zeros((1, ct), F32)
        lax.fori_loop(0, s // ROWS, tile, (z, z))

    re = lambda j: (0, j)
    im = lambda j: (0, nt + j)
    return _call_with_exchange(
        body, nt, carry, [bu, bu, a, a],
        [pl.BlockSpec((s, ct), re), pl.BlockSpec((s, ct), im), pl.BlockSpec((1, ct), re), pl.BlockSpec((1, ct), im)],
        [pl.BlockSpec((s, ct), re), pl.BlockSpec((s, ct), re)], [jax.ShapeDtypeStruct((s, c), F32)] * 2, [], name)


def _cscan_bwd(gr, gi, xr, xi, a, *, name, ct=256, carry=None):
    s, c = gr.shape
    nt = c // ct
    n_tiles = s // ROWS

    def body(gr_ref, gi_ref, xr_ref, xi_ref, ar_ref, ai_ref, dr_ref, di_ref, dar_ref, dai_ref):
        pr, pi, doubling = _power_rows(ar_ref[...], -ai_ref[...], ct, True)

        def tile(i, state):
            dnr, dni, accr, acci = state
            r0 = pl.multiple_of((n_tiles - 1 - i) * ROWS, ROWS)
            rows = pl.ds(r0, ROWS)
            sr, si = gr_ref[rows, :], gi_ref[rows, :]
            for (qr, qi), k in doubling:
                mr, mi = _cmul(qr, qi, _shift_rows(sr, k, 0.0, up=True), _shift_rows(si, k, 0.0, up=True))
                sr, si = sr + mr, si + mi
            cr, ci = _cmul(pr, pi, dnr, dni)
            sr, si = sr + cr, si + ci
            dr_ref[rows, :] = sr
            di_ref[rows, :] = si
            xpr, xpi = _prev_rows(xr_ref, xr_ref[rows, :], r0), _prev_rows(xi_ref, xi_ref[rows, :], r0)
            return sr[:1], si[:1], accr + sr * xpr + si * xpi, acci + si * xpr - sr * xpi

        z, z8 = jnp.zeros((1, ct), F32), jnp.zeros((ROWS, ct), F32)
        _, _, accr, acci = lax.fori_loop(0, n_tiles, tile, (z, z, z8, z8))
        dar_ref[...] = jnp.sum(accr, axis=0, keepdims=True)
        dai_ref[...] = jnp.sum(acci, axis=0, keepdims=True)

    re = lambda j: (0, j)
    im = lambda j: (0, nt + j)
    blk = pl.BlockSpec((s, ct), re)
    return _call_with_exchange(
        body, nt, carry, [gr, gi, xr, xi, a, a],
        [blk, blk, blk, blk, pl.BlockSpec((1, ct), re), pl.BlockSpec((1, ct), im)],
        [blk, blk, pl.BlockSpec((1, ct), re), pl.BlockSpec((1, ct), re)],
        [jax.ShapeDtypeStruct((s, c), F32)] * 2 + [jax.ShapeDtypeStruct((1, c), F32)] * 2, [], name)


def _rscan_fwd(a, b, *, name, ct=256):
    s, c = a.shape

    def body(a_ref, b_ref, h_ref):
        def tile(i, h):
            rows = pl.ds(pl.multiple_of(i * ROWS, ROWS), ROWS)
            ca, cb = a_ref[rows, :], b_ref[rows, :]
            for k in (1, 2, 4):
                cb = cb + ca * _shift_rows(cb, k, 0.0)
                ca = ca * _shift_rows(ca, k, 1.0)
            out = cb + ca * h
            h_ref[rows, :] = out
            return out[ROWS - 1:]
        lax.fori_loop(0, s // ROWS, tile, jnp.zeros((1, ct), F32))

    blk = pl.BlockSpec((s, ct), lambda j: (0, j))
    return pl.pallas_call(body, grid=(c // ct,), in_specs=[blk, blk], out_specs=blk,
                          out_shape=jax.ShapeDtypeStruct((s, c), F32), compiler_params=_cparams("parallel"), name=name)(a, b)


def _rscan_bwd(g, a, h, *, name, ct=256):
    s, c = a.shape
    n_tiles = s // ROWS

    def body(g_ref, a_ref, h_ref, db_ref, da_ref):
        def tile(i, dn):
            r0 = pl.multiple_of((n_tiles - 1 - i) * ROWS, ROWS)
            rows = pl.ds(r0, ROWS)
            after = a_ref[pl.ds(jnp.minimum(r0 + ROWS, s - 1), 1), :] * (r0 + ROWS < s).astype(F32)
            ca = _shift_rows(a_ref[rows, :], 1, after, up=True)
            cb = g_ref[rows, :]
            for k in (1, 2, 4):
                cb = cb + ca * _shift_rows(cb, k, 0.0, up=True)
                ca = ca * _shift_rows(ca, k, 1.0, up=True)
            out = cb + ca * dn
            db_ref[rows, :] = out
            da_ref[rows, :] = out * _prev_rows(h_ref, h_ref[rows, :], r0)
            return out[:1]
        lax.fori_loop(0, n_tiles, tile, jnp.zeros((1, ct), F32))

    blk = pl.BlockSpec((s, ct), lambda j: (0, j))
    db, da = pl.pallas_call(body, grid=(c // ct,), in_specs=[blk, blk, blk], out_specs=[blk, blk],
                            out_shape=[jax.ShapeDtypeStruct((s, c), F32)] * 2, compiler_params=_cparams("parallel"), name=name)(g, a, h)
    return db, da


def _dot3l(m, x):
    hi = x.astype(BF16)
    r1 = x - hi.astype(F32)
    mid = r1.astype(BF16)
    lo = (r1 - mid.astype(F32)).astype(BF16)
    return (jnp.dot(m, hi, preferred_element_type=F32) + jnp.dot(m, mid, preferred_element_type=F32)
            + jnp.dot(m, lo, preferred_element_type=F32))


@jax.custom_vjp
def _linl(x, m, mt):
    return _dot3l(m, x)


def _linl_fwd(x, m, mt):
    return _dot3l(m, x), (m, mt)


def _linl_bwd(res, g):
    m, mt = res
    return _dot3l(mt, g), jnp.zeros_like(m), jnp.zeros_like(mt)


_linl.defvjp(_linl_fwd, _linl_bwd)


def _ssd_chunk(g, xs, bm, cm, z, dtraw, dt_bias, a_log, dskip, ng, st0, st1, st2):
    n = xs.shape[0]
    lane = lax.broadcasted_iota(jnp.int32, (1, LANES), 1)
    sub = lax.broadcasted_iota(jnp.int32, (LANES, 1), 0)
    row = lax.broadcasted_iota(jnp.int32, (n, n), 0)
    col = lax.broadcasted_iota(jnp.int32, (n, n), 1)
    tril = row >= col
    tril_m = tril.astype(BF16)
    triu_m = (row <= col).astype(BF16)
    lane_lo = lane < HEAD
    sub_lo = sub < HEAD
    dt = _softplus(dtraw + dt_bias)
    da = dt * (-jnp.exp(a_log))
    acum = _linl(da, tril_m, triu_m)
    acum_t = acum.T
    scores = lax.dot_general(cm.astype(BF16), bm.astype(BF16), (((1,), (1,)), ((), ())), preferred_element_type=F32)

    def head(h):
        sel = lane == h
        acol = jnp.sum(jnp.where(sel, acum, 0.0), axis=1, keepdims=True)
        arow = jnp.sum(jnp.where(sub == h, acum_t, 0.0), axis=0, keepdims=True)
        dtcol = jnp.sum(jnp.where(sel, dt, 0.0), axis=1, keepdims=True)
        dsk = jnp.sum(jnp.where(sel, dskip, 0.0), axis=1, keepdims=True)
        decay = jnp.exp(jnp.where(tril, acol - arow, -jnp.inf))
        alast = acol[n - 1:n, :]
        return acol, dtcol, dsk, decay, alast

    ys, new = [], []
    for q, st in enumerate((st0, st1, st2)):
        a_acol, a_dt, a_dsk, a_decay, a_last = head(g * 6 + 2 * q)
        b_acol, b_dt, b_dsk, b_decay, b_last = head(g * 6 + 2 * q + 1)
        xp = xs[:, q * LANES:(q + 1) * LANES]
        xdt = xp * jnp.where(lane_lo, a_dt, b_dt)
        yd = jnp.where(lane_lo, _dot_bf16(scores * a_decay, xdt), _dot_bf16(scores * b_decay, xdt))
        xw = xdt * jnp.where(lane_lo, jnp.exp(a_last - a_acol), jnp.exp(b_last - b_acol))
        states = lax.dot_general(xw.astype(BF16), bm.astype(BF16), (((0,), (0,)), ((), ())), preferred_element_type=F32)
        yo = lax.dot_general(cm.astype(BF16), st.astype(BF16), (((1,), (1,)), ((), ())), preferred_element_type=F32)
        yo = yo * jnp.where(lane_lo, jnp.exp(a_acol), jnp.exp(b_acol))
        new.append(st * jnp.where(sub_lo, jnp.exp(a_last), jnp.exp(b_last)) + states)
        ys.append(yd + yo + xp * jnp.where(lane_lo, a_dsk, b_dsk))
    y = jnp.concatenate(ys, axis=1)
    y = y * (z * jax.nn.sigmoid(z))
    y = y * lax.rsqrt(jnp.mean(y * y, axis=-1, keepdims=True) + NORM_EPS) * ng
    return y, new[0], new[1], new[2]


def _ssd_specs(nc, rev):
    cidx = (lambda c: nc - 1 - c) if rev else (lambda c: c)
    gw = 3 * LANES
    return [
        pl.BlockSpec((SSD_CHUNK, gw), lambda c, g: (cidx(c), g)),
        pl.BlockSpec((SSD_CHUNK, LANES), lambda c, g: (cidx(c), 12 + g)),
        pl.BlockSpec((SSD_CHUNK, LANES), lambda c, g: (cidx(c), 16 + g)),
        pl.BlockSpec((SSD_CHUNK, gw), lambda c, g: (cidx(c), g)),
        pl.BlockSpec((SSD_CHUNK, LANES), lambda c, g: (cidx(c), 36)),
        pl.BlockSpec((1, LANES), lambda c, g: (0, 0)),
        pl.BlockSpec((1, LANES), lambda c, g: (0, 0)),
        pl.BlockSpec((1, LANES), lambda c, g: (0, 0)),
        pl.BlockSpec((1, gw), lambda c, g: (0, g)),
    ], cidx


def _ssd_fwd(conv, proj, dt_bias, a_log, dskip, norm_g, *, name):
    s = conv.shape[0]
    nc = s // SSD_CHUNK
    in_specs, _ = _ssd_specs(nc, False)

    def body(xs, bm, cm, z, dtr, dtb, alog, dsk, ng, y_ref, sv_ref, st):
        c, g = pl.program_id(0), pl.program_id(1)

        @pl.when(c == 0)
        def _():
            for q in range(3):
                st[g * 3 + q] = jnp.zeros((LANES, LANES), F32)

        olds = [st[g * 3 + q] for q in range(3)]
        for q in range(3):
            sv_ref[0, 0, q] = olds[q]
        y, n0, n1, n2 = _ssd_chunk(g, xs[...], bm[...], cm[...], z[...], dtr[...], dtb[...], alog[...], dsk[...], ng[...], *olds)
        y_ref[...] = y
        for q, v in enumerate((n0, n1, n2)):
            st[g * 3 + q] = v

    return pl.pallas_call(
        body, grid=(nc, 4), in_specs=in_specs,
        out_specs=[pl.BlockSpec((SSD_CHUNK, 3 * LANES), lambda c, g: (c, g)),
                   pl.BlockSpec((1, 1, 3, LANES, LANES), lambda c, g: (c, g, 0, 0, 0))],
        out_shape=[jax.ShapeDtypeStruct((s, 12 * LANES), F32), jax.ShapeDtypeStruct((nc, 4, 3, LANES, LANES), F32)],
        scratch_shapes=[pltpu.VMEM((12, LANES, LANES), F32)],
        compiler_params=_cparams("arbitrary", "arbitrary"), name=name)(conv, conv, conv, proj, proj, dt_bias, a_log, dskip, norm_g)


def _ssd_bwd(conv, proj, dt_bias, a_log, dskip, norm_g, saved, dy, *, name, carry=None):
    s = conv.shape[0]
    nc = s // SSD_CHUNK
    in_specs, cidx = _ssd_specs(nc, True)
    gw = 3 * LANES
    in_specs += [pl.BlockSpec((1, 1, 3, LANES, LANES), lambda c, g: (cidx(c), g, 0, 0, 0)),
                 pl.BlockSpec((SSD_CHUNK, gw), lambda c, g: (cidx(c), g))]

    def body(xs, bm, cm, z, dtr, dtb, alog, dsk, ng, sv, dy_ref, dxs, dbm, dcm, dz, ddt, ddtb, dalog, ddsk, dng, dst):
        c, g = pl.program_id(0), pl.program_id(1)

        @pl.when(c == 0)
        def _():
            for q in range(3):
                dst[g * 3 + q] = jnp.zeros((LANES, LANES), F32)

        @pl.when((c == 0) & (g == 0))
        def _():
            ddtb[...] = jnp.zeros_like(ddtb)
            dalog[...] = jnp.zeros_like(dalog)
            ddsk[...] = jnp.zeros_like(ddsk)
            dng[...] = jnp.zeros_like(dng)

        @pl.when(g == 0)
        def _():
            ddt[...] = jnp.zeros_like(ddt)

        olds = [sv[0, 0, q] for q in range(3)]
        _, vjp = jax.vjp(functools.partial(_ssd_chunk, g), xs[...], bm[...], cm[...], z[...], dtr[...], dtb[...], alog[...],
                         dsk[...], ng[...], *olds)
        gr = vjp((dy_ref[...], dst[g * 3], dst[g * 3 + 1], dst[g * 3 + 2]))
        dxs[...], dbm[...], dcm[...], dz[...] = gr[0], gr[1], gr[2], gr[3]
        ddt[...] += gr[4]
        ddtb[...] += gr[5]
        dalog[...] += gr[6]
        ddsk[...] += gr[7]
        dng[g] += gr[8]
        for q in range(3):
            dst[g * 3 + q] = gr[9 + q]

    const = lambda shape: pl.BlockSpec(shape, lambda c, g: (0,) * len(shape))
    return _call_with_exchange(
        body, (nc, 4), carry, [conv, conv, conv, proj, proj, dt_bias, a_log, dskip, norm_g, saved, dy], in_specs,
        [pl.BlockSpec((SSD_CHUNK, gw), lambda c, g: (cidx(c), g)),
         pl.BlockSpec((SSD_CHUNK, LANES), lambda c, g: (cidx(c), g)),
         pl.BlockSpec((SSD_CHUNK, LANES), lambda c, g: (cidx(c), g)),
         pl.BlockSpec((SSD_CHUNK, gw), lambda c, g: (cidx(c), g)),
         pl.BlockSpec((SSD_CHUNK, LANES), lambda c, g: (cidx(c), 0)),
         const((1, LANES)), const((1, LANES)), const((1, LANES)), const((4, 1, gw))],
        [jax.ShapeDtypeStruct((s, 12 * LANES), F32), jax.ShapeDtypeStruct((s, 4 * LANES), F32),
         jax.ShapeDtypeStruct((s, 4 * LANES), F32), jax.ShapeDtypeStruct((s, 12 * LANES), F32),
         jax.ShapeDtypeStruct((s, LANES), F32), jax.ShapeDtypeStruct((1, LANES), F32),
         jax.ShapeDtypeStruct((1, LANES), F32), jax.ShapeDtypeStruct((1, LANES), F32),
         jax.ShapeDtypeStruct((4, 1, gw), F32)],
        [pltpu.VMEM((12, LANES, LANES), F32)], name)


MXU_TILE = 256
RWKV_GROUP = 8


def _rwkv_consts():
    lanes = 16 * HEAD
    hl = jnp.arange(lanes) // HEAD
    e = (jnp.arange(16)[:, None] == hl[None, :]).astype(BF16)
    return jnp.tile(e, (3, 1)), e.T, _head_ones(MXU_TILE)


def _head_sums(x, j):
    n = x.shape[0]
    x4 = jnp.concatenate([x[:, i * MXU_TILE:(i + 1) * MXU_TILE] for i in range(4)], axis=0)
    hi = x4.astype(BF16)
    mid = (x4 - hi.astype(F32)).astype(BF16)
    out = jnp.dot(jnp.concatenate([hi, mid], axis=0), j, preferred_element_type=F32)
    s4 = out[:4 * n] + out[4 * n:]
    return jnp.concatenate([s4[i * n:(i + 1) * n] for i in range(4)], axis=1)


def _fold8(x):
    return jnp.sum(x.reshape(x.shape[0] // 8, 8, x.shape[1]), axis=0)


def _split3(x):
    hi = x.astype(BF16)
    r1 = x - hi.astype(F32)
    mid = r1.astype(BF16)
    return jnp.concatenate([hi, mid, (r1 - mid.astype(F32)).astype(BF16)], axis=-1)


def _rwkv_expand(src3, dst, e3, t_):
    for g0 in range(0, t_, RWKV_GROUP):
        n = min(RWKV_GROUP, t_ - g0)
        flat = src3[g0:g0 + n].reshape(n * HEAD, e3.shape[0])
        dst[g0:g0 + n] = jnp.dot(flat, e3, preferred_element_type=F32).reshape(n, HEAD, e3.shape[1])


def _rwkv_reduce(src, dst3, et, t_):
    for g0 in range(0, t_, RWKV_GROUP):
        n = min(RWKV_GROUP, t_ - g0)
        x = src[g0:g0 + n].reshape(n * HEAD, et.shape[0])
        dst3[g0:g0 + n] = jnp.dot(x.astype(BF16), et, preferred_element_type=F32).reshape(n, HEAD, 16)


def _rwkv_fwd(w, kk, b, k, fp, v3, *, name, carry=None):
    s, lanes = w.shape
    t_ = min(RWKV_CHUNK, s)
    nc = s // t_
    e, et, j = _rwkv_consts()
    rowspec = pl.BlockSpec((t_, lanes), lambda c: (c, 0))
    cspec = lambda a: pl.BlockSpec(a.shape, lambda c: (0, 0))

    def body(w_ref, kk_ref, b_ref, k_ref, r_ref, v_ref, e_ref, et_ref, j_ref, y_ref, sv_ref, st, vm, zz):
        c = pl.program_id(0)

        @pl.when(c == 0)
        def _():
            st[...] = jnp.zeros_like(st)

        sv_ref[0] = st[...]
        jv = j_ref[...]
        _rwkv_expand(v_ref, vm, e_ref[...], t_)

        def step(t, sm):
            row = lambda ref: ref[pl.ds(t, 1), :]
            sa = _head_sums(sm * (-row(kk_ref)), jv)
            sn = sm * row(w_ref) + sa * row(b_ref) + vm[t] * row(k_ref)
            zz[t] = sn * row(r_ref)
            return sn

        st[...] = lax.fori_loop(0, t_, step, st[...])
        _rwkv_reduce(zz, y_ref, et_ref[...], t_)

    return _call_with_exchange(
        body, nc, carry, [w, kk, b, k, fp, v3, e, et, j],
        [rowspec] * 5 + [pl.BlockSpec((t_, HEAD, 48), lambda c: (c, 0, 0)), cspec(e), cspec(et), cspec(j)],
        [pl.BlockSpec((t_, HEAD, 16), lambda c: (c, 0, 0)), pl.BlockSpec((1, HEAD, lanes), lambda c: (c, 0, 0))],
        [jax.ShapeDtypeStruct((s, HEAD, 16), F32), jax.ShapeDtypeStruct((nc, HEAD, lanes), F32)],
        [pltpu.VMEM((HEAD, lanes), F32), pltpu.VMEM((t_, HEAD, lanes), F32), pltpu.VMEM((t_, HEAD, lanes), F32)], name)


def _call_with_exchange(body, grid, carry, ins, in_specs, out_specs, out_shape, scratch, name):
    grid = (grid,) if isinstance(grid, int) else tuple(grid)
    if carry is not None:
        src, gather = carry
        n_in, n_out, inner = len(ins), len(out_shape), body

        def body(*refs):
            ex = _Exchange(refs[n_in], refs[n_in + 1 + n_out], *refs[-3:], gather)
            steps = [pl.program_id(d) for d in range(len(grid))]

            @pl.when(functools.reduce(jnp.logical_and, [s == 0 for s in steps]))
            def _():
                ex.start()

            inner(*refs[:n_in], *refs[n_in + 1:n_in + 1 + n_out], *refs[n_in + 2 + n_out:-3])

            @pl.when(functools.reduce(jnp.logical_and, [s == n - 1 for s, n in zip(steps, grid)]))
            def _():
                ex.wait()

        hbm = pl.BlockSpec(memory_space=pl.ANY)
        ins, in_specs, out_specs = list(ins) + [src], list(in_specs) + [hbm], list(out_specs) + [hbm]
        out_shape = list(out_shape) + [jax.ShapeDtypeStruct((N_DEV, src.shape[-2], LANES), src.dtype)]
        scratch = list(scratch) + list(_Exchange.SCRATCH)
    return pl.pallas_call(body, grid=grid, in_specs=in_specs, out_specs=out_specs, out_shape=out_shape,
                          scratch_shapes=scratch, compiler_params=_cparams(*["arbitrary"] * len(grid)), name=name)(*ins)


def _rwkv_bwd(w, kk, b, k, fp, v3, saved, dy3, *, name, carry=None):
    s, lanes = w.shape
    t_ = min(RWKV_CHUNK, s)
    nc = s // t_
    e, et, j = _rwkv_consts()
    rev = lambda c: nc - 1 - c
    rowspec = pl.BlockSpec((t_, lanes), lambda c: (rev(c), 0))
    v3spec = pl.BlockSpec((t_, HEAD, 16), lambda c: (rev(c), 0, 0))
    s3spec = pl.BlockSpec((t_, HEAD, 48), lambda c: (rev(c), 0, 0))
    cspec = lambda a: pl.BlockSpec(a.shape, lambda c: (0, 0))

    def body(w_ref, kk_ref, b_ref, k_ref, r_ref, v_ref, sv_ref, dy_ref, e_ref, et_ref, j_ref,
             dw_ref, dkk_ref, db_ref, dk_ref, dr_ref, dv_ref, dst, h_sm, h_sa, vm, dz, pw, pkk, pb, pk, pr):
        c = pl.program_id(0)

        @pl.when(c == 0)
        def _():
            dst[...] = jnp.zeros_like(dst)

        jv = j_ref[...]
        row = lambda ref, t: ref[pl.ds(t, 1), :]
        _rwkv_expand(v_ref, vm, e_ref[...], t_)
        _rwkv_expand(dy_ref, dz, e_ref[...], t_)

        def replay(t, sm):
            h_sm[t] = sm
            sa = _head_sums(sm * (-row(kk_ref, t)), jv)
            h_sa[t] = sa
            return sm * row(w_ref, t) + sa * row(b_ref, t) + vm[t] * row(k_ref, t)

        h_sm[t_] = lax.fori_loop(0, t_, replay, sv_ref[0])

        def back(i, dcarry):
            t = t_ - 1 - i
            sm, sa, dzt = h_sm[t], h_sa[t], dz[t]
            dsn = dcarry + dzt * row(r_ref, t)
            pr[t] = _fold8(dzt * h_sm[t + 1])
            pw[t] = _fold8(dsn * sm)
            pb[t] = _fold8(dsn * sa)
            pk[t] = _fold8(dsn * vm[t])
            vm[t] = dsn * row(k_ref, t)
            dx = _head_sums(dsn * row(b_ref, t), jv)
            pkk[t] = _fold8(dx * sm)
            return dsn * row(w_ref, t) - dx * row(kk_ref, t)

        dst[...] = lax.fori_loop(0, t_, back, dst[...])
        _rwkv_reduce(vm, dv_ref, et_ref[...], t_)
        dw_ref[...] = jnp.sum(pw[...], axis=1)
        dkk_ref[...] = -jnp.sum(pkk[...], axis=1)
        db_ref[...] = jnp.sum(pb[...], axis=1)
        dk_ref[...] = jnp.sum(pk[...], axis=1)
        dr_ref[...] = jnp.sum(pr[...], axis=1)

    big = lambda n: pltpu.VMEM((n, HEAD, lanes), F32)
    part = pltpu.VMEM((t_, 8, lanes), F32)
    return _call_with_exchange(
        body, nc, carry, [w, kk, b, k, fp, v3, saved, dy3, e, et, j],
        [rowspec] * 5 + [s3spec, pl.BlockSpec((1, HEAD, lanes), lambda c: (rev(c), 0, 0)), s3spec, cspec(e), cspec(et), cspec(j)],
        [rowspec] * 5 + [v3spec],
        [jax.ShapeDtypeStruct((s, lanes), F32)] * 5 + [jax.ShapeDtypeStruct((s, HEAD, 16), F32)],
        [pltpu.VMEM((HEAD, lanes), F32), big(t_ + 1), big(t_), big(t_), big(t_)] + [part] * 5, name)


def _blockdiag(blocks):
    g, a, b = blocks.shape
    on_diag = (jnp.arange(g)[:, None, None, None] == jnp.arange(g)[None, None, :, None])
    return jnp.where(on_diag, blocks[:, :, None, :], 0).reshape(g * a, g * b)


def _blockdiag_t(dense, g):
    a, b = dense.shape[0] // g, dense.shape[1] // g
    on_diag = (jnp.arange(g)[:, None, None, None] == jnp.arange(g)[None, None, :, None])
    return jnp.sum(jnp.where(on_diag, dense.reshape(g, a, g, b), 0), axis=2)


def _pad_cols(x, n):
    return jnp.pad(x, ((0, 0), (0, n - x.shape[1])))


def _pad_rows(x, n):
    return jnp.pad(x, ((0, n - x.shape[0]), (0, 0)))


E_PROJ = 5120


def _even_in_cols(w):
    return jnp.concatenate([w[:, 512:2048], w[:, 0:512], w[:, 2048:4632], jnp.zeros((w.shape[0], E_PROJ - 4632), w.dtype)], axis=1)


def _even_in_cols_t(dw):
    return jnp.concatenate([dw[:, 1536:2048], dw[:, 0:1536], dw[:, 2048:4632]], axis=1)


O_PROJ = 5632


def _odd_in_cols(w):
    z32 = jnp.zeros((w.shape[0], 32), w.dtype)
    return jnp.concatenate([w[:, 0:3072], w[:, 3520:5568], w[:, 3264:3520], w[:, 3072:3168], z32, w[:, 3168:3264], z32], axis=1)


def _odd_in_cols_t(dw):
    return jnp.concatenate([dw[:, 0:3072], dw[:, 5376:5472], dw[:, 5504:5600], dw[:, 5120:5376], dw[:, 3072:5120]], axis=1)


def _mu_cols(mu):
    z32 = jnp.zeros((1, 32), mu.dtype)
    return jnp.concatenate([mu[:, 0:3072], mu[:, 3264:3520], mu[:, 3072:3168], z32, mu[:, 3168:3264], z32], axis=1)


def _mu_cols_t(d):
    return jnp.concatenate([d[:, 0:3072], d[:, 3328:3424], d[:, 3456:3552], d[:, 3072:3328]], axis=1)


def _conv_taps(x, w, b):
    y = b + w[3:4] * x
    for k in range(3):
        y = y + w[k:k + 1] * _shift_down(x, 3 - k)
    return y


def _conv_silu(x, w, b):
    y = _conv_taps(x, w, b)
    return y * jax.nn.sigmoid(y)


def _tshift(x, mu):
    return x + (_shift_down(x, 1) - x) * mu


def _pl_gate(_, h, gl, e):
    return (h + jax.nn.sigmoid(gl) * e,)


def _s5_param(lr, li, ls, br, bi):
    step = jnp.exp(ls)
    mag = jnp.exp(lr * step)
    ar, ai = mag * jnp.cos(li * step), mag * jnp.sin(li * step)
    den = lr * lr + li * li
    nr = ar - 1.0
    cr = (nr * lr + ai * li) / den
    ci = (ai * lr - nr * li) / den
    return ar, ai, cr * br - ci * bi, cr * bi + ci * br


def _s5_post(_, ylin, u, d, gw, gb):
    act = jax.nn.gelu(ylin + d * u)
    return (act * jax.nn.sigmoid(_dot_bf16(act, gw) + gb),)


def _rwkv_pre(_, k, gl, wl, al, w0, w_up, a0, a_up, g_up, k_k, k_a, j):
    w = -_softplus(-(w0 + _dot_bf16(jnp.tanh(wl), w_up))) - 0.5
    decay = jnp.exp(-jnp.exp(w))
    a = jax.nn.sigmoid(a0 + _dot_bf16(al, a_up))
    g = _dot_bf16(jax.nn.sigmoid(gl), g_up)
    kk = k * k_k
    k2 = k * (1.0 + (a - 1.0) * k_a)
    kkn = kk * lax.rsqrt(jnp.maximum(_lin(kk * kk, j, j), 1e-24))
    return decay, kkn, kkn * a, k2, g


def _rwkv_post(_, y, r, k2, v, g, r_k, ln_g, ln_b, j):
    mean = _lin(y, j, j) * (1.0 / HEAD)
    yc = y - mean
    var = _lin(yc * yc, j, j) * (1.0 / HEAD)
    yn = yc * lax.rsqrt(var + RWKV_GN_EPS) * ln_g + ln_b
    return ((yn + _lin(r * k2 * r_k, j, j) * v) * g,)


def _lru_pre(row0, pre, xc, bax, lam):
    n = xc.shape[1]
    gr = jax.nn.sigmoid(pre[:, :n] + bax[:, :n])
    gi = jax.nn.sigmoid(pre[:, n:] + bax[:, n:])
    log_a = -LRU_C * gr * _softplus(-lam)
    m2 = -jnp.tanh(log_a) * (jnp.exp(2.0 * log_a) + 1.0)
    mult = jnp.sqrt(jnp.maximum(m2, 0.0))
    rowid = row0 + lax.broadcasted_iota(jnp.int32, (xc.shape[0], 1), 0)
    mult = jnp.where(rowid == 0, 1.0, mult)
    return jnp.exp(log_a), xc * gi * mult


def _lru_post(_, h, gl2):
    return (h * jax.nn.gelu(gl2),)


def _even_prep(w):
    sp = (w["s5_lam_re"].reshape(32, 64), w["s5_lam_im"].reshape(32, 64), w["s5_log_step"].reshape(32, 1),
          w["s5_b_re"].reshape(32, 64, 16).transpose(2, 0, 1), w["s5_b_im"].reshape(32, 64, 16).transpose(2, 0, 1))
    ar, ai, bbr, bbi = _block_vjp(_s5_param, sp, None, name="s5_param")
    bblk = lambda bb: _blockdiag(bb.transpose(1, 0, 2))
    cblk = lambda c: _blockdiag(c.reshape(32, 16, 64).transpose(0, 2, 1))
    pad = lambda x: _pad_cols(x.reshape(1, 24), LANES)
    return dict(
        sp=sp, a_row=jnp.concatenate([ar.reshape(1, 2048), ai.reshape(1, 2048)], axis=1),
        b_re=bblk(bbr), b_im=bblk(bbi), c_re=cblk(w["s5_c_re"]), c_imn=-cblk(w["s5_c_im"]),
        d=w["s5_d"].reshape(1, 512), gw=w["s5_glu_w"].reshape(512, 512), gb=w["s5_glu_b"].reshape(1, 512),
        conv_w=w["ssd_conv_w"].reshape(4, 2560), conv_b=w["ssd_conv_b"].reshape(1, 2560),
        dt_bias=pad(w["ssd_dt_bias"]), a_log=pad(w["ssd_a_log"]), dskip=pad(w["ssd_d"]), norm=w["ssd_norm"].reshape(1, 1536))


_E_XMAP = lambda j: 16 + j


def _even_fwd(proj, p, carry=None):
    u = proj[:, 1536:2048]
    bur = _matmul(u, p["b_re"], "nn", name="s5_bu_re")
    bui = _matmul(u, p["b_im"], "nn", name="s5_bu_im")
    xr, xi, *carried = _cscan_fwd(jnp.concatenate([bur, bui], axis=1), p["a_row"], name="s5_scan", carry=carry)
    ylin = _matmul(xi, p["c_imn"], "nn", name="s5_y_im", add=_matmul(xr, p["c_re"], "nn", name="s5_y_re"))
    (ya,) = _rw_fwd(_s5_post, [_row(ylin), _row(u)], [p["d"], p["gw"], p["gb"]], [(512, F32)], name="s5_post")
    conv = _ct_fwd(_conv_silu, proj, _E_XMAP, 20, [(p["conv_w"], 0), (p["conv_b"], 0)], 2560, name="ssd_conv")
    yb, saved = _ssd_fwd(conv, proj, p["dt_bias"], p["a_log"], p["dskip"], p["norm"], name="ssd_scan")
    return jnp.concatenate([ya, yb], axis=1), (u, xr, xi, ylin, conv, saved), (carried[0] if carried else None)


def _even_bwd(proj, p, res, dy, carry=None):
    carry_ssd, carry_s5 = carry if carry is not None else (None, None)
    u, xr, xi, ylin, conv, saved = res
    s = proj.shape[0]
    dxs, dbm, dcm, dz, ddt, ddtb, dalog, ddsk, dng, *got_ssd = _ssd_bwd(
        conv, proj, p["dt_bias"], p["a_log"], p["dskip"], p["norm"], saved, dy[:, 512:], name="ssd_scan_bwd", carry=carry_ssd)
    dxbc, (dcw, dcb) = _ct_bwd(_conv_silu, proj, _E_XMAP, 20, [(p["conv_w"], 0), (p["conv_b"], 0)],
                               [jnp.concatenate([dxs, dbm, dcm], axis=1)], name="ssd_conv_bwd")
    (dylin, du), (dd, dgw, dgb) = _rw_bwd(_s5_post, [_row(ylin), _row(u)], [p["d"], p["gw"], p["gb"]], [dy[:, :512]], name="s5_post_bwd")
    dxr = _matmul(dylin, p["c_re"], "nt", name="s5_dxr")
    dxi = _matmul(dylin, p["c_imn"], "nt", name="s5_dxi")
    dc_re = _matmul(xr, dylin, "tn", name="s5_dc_re")
    dc_imn = _matmul(xi, dylin, "tn", name="s5_dc_im")
    dbr, dbi, dar, dai, *got_s5 = _cscan_bwd(dxr, dxi, xr, xi, p["a_row"], name="s5_scan_bwd", carry=carry_s5)
    du = _matmul(dbr, p["b_re"], "nt", name="s5_du_re", add=du)
    du = _matmul(dbi, p["b_im"], "nt", name="s5_du_im", add=du)
    db_re = _matmul(u, dbr, "tn", name="s5_db_re")
    db_im = _matmul(u, dbi, "tn", name="s5_db_im")
    unblk = lambda d: _blockdiag_t(d, 32).transpose(1, 0, 2)
    g_sp = _block_vjp(_s5_param, p["sp"], (dar.reshape(32, 64), dai.reshape(32, 64), unblk(db_re), unblk(db_im)), name="s5_param_bwd")
    dproj = jnp.concatenate([dz, du, dxbc, ddt, jnp.zeros((s, E_PROJ - 4736), F32)], axis=1)
    uncblk = lambda d: _blockdiag_t(d, 32).transpose(0, 2, 1)
    grads = dict(
        s5_lam_re=g_sp[0].reshape(1, 32, 64), s5_lam_im=g_sp[1].reshape(1, 32, 64), s5_log_step=g_sp[2].reshape(1, 32),
        s5_b_re=g_sp[3].transpose(1, 2, 0)[None], s5_b_im=g_sp[4].transpose(1, 2, 0)[None],
        s5_c_re=uncblk(dc_re)[None], s5_c_im=-uncblk(dc_imn)[None], s5_d=dd, s5_glu_w=dgw[None], s5_glu_b=dgb,
        ssd_conv_w=dcw[None], ssd_conv_b=dcb, ssd_dt_bias=ddtb[:, :24], ssd_a_log=dalog[:, :24], ssd_d=ddsk[:, :24],
        ssd_norm=dng.reshape(1, 1536))
    return dproj, grads, (got_ssd[0] if got_ssd else None, got_s5[0] if got_s5 else None)


def _odd_prep(w):
    pad128 = lambda x: _pad_rows(x, LANES)
    return dict(
        mu=_mu_cols(w["rwkv_mu"].reshape(1, 3520)), w0=w["rwkv_w0"].reshape(1, 1024), w_up=pad128(w["rwkv_w_up"].reshape(96, 1024)),
        a0=w["rwkv_a0"].reshape(1, 1024), a_up=pad128(w["rwkv_a_up"].reshape(96, 1024)), g_up=w["rwkv_g_up"].reshape(256, 1024),
        k_k=w["rwkv_k_k"].reshape(1, 1024), k_a=w["rwkv_k_a"].reshape(1, 1024), r_k=w["rwkv_r_k"].reshape(1, 1024),
        ln_g=w["rwkv_ln_g"].reshape(1, 1024), ln_b=w["rwkv_ln_b"].reshape(1, 1024), j=_head_ones(1024),
        conv_w=w["lru_conv_w"].reshape(4, 1024), conv_b=w["lru_conv_b"].reshape(1, 1024),
        wax=jnp.concatenate([_blockdiag(w["lru_w_a"].reshape(16, 64, 64)), _blockdiag(w["lru_w_x"].reshape(16, 64, 64))], axis=1),
        bax=jnp.concatenate([w["lru_b_a"].reshape(1, 1024), w["lru_b_x"].reshape(1, 1024)], axis=1), lam=w["lru_lam"].reshape(1, 1024))


_O_XMAP = lambda j: jnp.where(j < 24, j, j + 16)
_O_LMAP = lambda j: 24 + j


def _to_heads(x):
    return x.reshape(x.shape[0], 16, HEAD).transpose(0, 2, 1)


def _from_heads(x3):
    return x3.transpose(0, 2, 1).reshape(x3.shape[0], 16 * HEAD)


def _odd_rows(fp, y, k2, g):
    pre = [_row(fp, 1024, 1), _row(fp, 256, 12), _row(fp, 128, 26), _row(fp, 128, 27)]
    post = None if y is None else [_row(y), _row(fp, 1024, 0), _row(k2), _row(fp, 1024, 2), _row(g)]
    return pre, post


def _odd_fwd(proj, p, carry=None):
    fp = _ct_fwd(_tshift, proj, _O_XMAP, 28, [(p["mu"], 0)], 3584, name="rwkv_shift")
    pre_rows, _ = _odd_rows(fp, None, None, None)
    pre_params = [p["w0"], p["w_up"], p["a0"], p["a_up"], p["g_up"], p["k_k"], p["k_a"], p["j"]]
    decay, kkn, b, k2, g = _rw_fwd(_rwkv_pre, pre_rows, pre_params, [(1024, F32)] * 5, name="rwkv_pre")
    v3 = _split3(_to_heads(fp[:, 2048:3072]))
    y3, saved, *carried = _rwkv_fwd(decay, kkn, b, k2, fp, v3, name="rwkv_scan", carry=carry)
    y = _from_heads(y3)
    _, post_rows = _odd_rows(fp, y, k2, g)
    (yc,) = _rw_fwd(_rwkv_post, post_rows, [p["r_k"], p["ln_g"], p["ln_b"], p["j"]], [(1024, F32)], name="rwkv_post")
    xc = _ct_fwd(_conv_taps, proj, _O_LMAP, 8, [(p["conv_w"], 0), (p["conv_b"], 0)], 1024, name="lru_conv")
    pre = _matmul(xc, p["wax"], "nn", name="lru_gates")
    a, bx = _rw_fwd(_lru_pre, [_row(pre), _row(xc)], [p["bax"], p["lam"]], [(1024, F32)] * 2, name="lru_pre")
    hseq = _rscan_fwd(a, bx, name="lru_scan")
    (yd,) = _rw_fwd(_lru_post, [_row(hseq), _row(proj, 1024, 4)], [], [(1024, F32)], name="lru_post")
    return jnp.concatenate([yc, yd], axis=1), (fp, decay, kkn, b, k2, g, v3, saved, y, xc, pre, a, hseq), (carried[0] if carried else None)


def _odd_bwd(proj, p, res, dy, carry=None):
    fp, decay, kkn, b, k2, g, v3, saved, y, xc, pre, a, hseq = res
    s = proj.shape[0]
    (dh, dgl2), _ = _rw_bwd(_lru_post, [_row(hseq), _row(proj, 1024, 4)], [], [dy[:, 1024:]], name="lru_post_bwd")
    dbx, da = _rscan_bwd(dh, a, hseq, name="lru_scan_bwd")
    (dpre, dxc), (dbax, dlam) = _rw_bwd(_lru_pre, [_row(pre), _row(xc)], [p["bax"], p["lam"]], [da, dbx], name="lru_pre_bwd")
    dxc = _matmul(dpre, p["wax"], "nt", name="lru_gates_dx", add=dxc)
    dwax = _matmul(xc, dpre, "tn", name="lru_gates_dw")
    dxl, (dlcw, dlcb) = _ct_bwd(_conv_taps, proj, _O_LMAP, 8, [(p["conv_w"], 0), (p["conv_b"], 0)], [dxc], name="lru_conv_bwd")
    pre_rows, post_rows = _odd_rows(fp, y, k2, g)
    (dyn, dr1, dk2a, dv1, dg), (dr_k, dln_g, dln_b) = _rw_bwd(
        _rwkv_post, post_rows, [p["r_k"], p["ln_g"], p["ln_b"], p["j"]], [dy[:, :1024]], name="rwkv_post_bwd", param_grads=[0, 1, 2])
    ddecay, dkkn, db, dk2b, dr2, dv3, *carried = _rwkv_bwd(
        decay, kkn, b, k2, fp, v3, saved, _split3(_to_heads(dyn)), name="rwkv_scan_bwd", carry=carry)
    pre_params = [p["w0"], p["w_up"], p["a0"], p["a_up"], p["g_up"], p["k_k"], p["k_a"], p["j"]]
    (dk, dgl, dwl, dal), (dw0, dw_up, da0, da_up, dg_up, dk_k, dk_a) = _rw_bwd(
        _rwkv_pre, pre_rows, pre_params, [ddecay, dkkn, db, [dk2a, dk2b], dg], name="rwkv_pre_bwd", param_grads=list(range(7)))
    z = lambda n: jnp.zeros((s, n), F32)
    g1 = jnp.concatenate([dr1, dk, dv1, dgl, dwl, dal], axis=1)
    g2 = jnp.concatenate([dr2, z(1024), _from_heads(dv3), z(512)], axis=1)
    dfp, (dmu,) = _ct_bwd(_tshift, proj, _O_XMAP, 28, [(p["mu"], 0)], [g1, g2], name="rwkv_shift_bwd")
    dproj = jnp.concatenate([dfp[:, :3072], dxl, dgl2, dfp[:, 3072:]], axis=1)
    grads = dict(
        rwkv_mu=_mu_cols_t(dmu), rwkv_w0=dw0, rwkv_w_up=dw_up[:96][None], rwkv_a0=da0, rwkv_a_up=da_up[:96][None], rwkv_g_up=dg_up[None],
        rwkv_k_k=dk_k, rwkv_k_a=dk_a, rwkv_r_k=dr_k.reshape(1, 16, 64), rwkv_ln_g=dln_g, rwkv_ln_b=dln_b,
        lru_conv_w=dlcw[None], lru_conv_b=dlcb, lru_w_a=_blockdiag_t(dwax[:, :1024], 16)[None], lru_w_x=_blockdiag_t(dwax[:, 1024:], 16)[None],
        lru_b_a=dbax[:, :1024].reshape(1, 16, 64), lru_b_x=dbax[:, 1024:].reshape(1, 16, 64), lru_lam=dlam.reshape(1, 16, 64))
    return dproj, grads, (carried[0] if carried else None)


def _my_index():
    return 4 * lax.axis_index("x") + 2 * lax.axis_index("y") + lax.axis_index("c")


def _peer(k):
    x, y, c = lax.axis_index("x"), lax.axis_index("y"), lax.axis_index("c")
    return (1 - x if k & 4 else x, 1 - y if k & 2 else y, 1 - c if k & 1 else c)


class _Exchange:
    SCRATCH = (pltpu.SemaphoreType.DMA((N_DEV - 1,)), pltpu.SemaphoreType.DMA((N_DEV - 1,)), pltpu.SemaphoreType.DMA)

    def __init__(self, src_ref, out_ref, send_sems, recv_sems, local_sem, gather):
        me = _my_index()
        mine = src_ref if gather else src_ref.at[me]
        self.local = pltpu.make_async_copy(mine, out_ref.at[me], local_sem)
        rdma = lambda src, dst, k: pltpu.make_async_remote_copy(
            src_ref=src, dst_ref=dst, send_sem=send_sems.at[k - 1], recv_sem=recv_sems.at[k - 1],
            device_id=_peer(k), device_id_type=pl.DeviceIdType.MESH)
        ks = range(1, N_DEV)
        self.sends = [rdma(src_ref if gather else src_ref.at[jnp.bitwise_xor(me, k)], out_ref.at[me], k) for k in ks]
        self.arrivals = [rdma(mine, out_ref.at[jnp.bitwise_xor(me, k)], k) for k in ks]

    def start(self):
        self.local.start()
        for cp in self.sends:
            cp.start()

    def wait(self):
        for cp in self.arrivals:
            cp.wait_recv()
        for cp in self.sends:
            cp.wait_send()
        self.local.wait()


def _gather_two_level(src, *, name):
    def body(src_ref, out_ref, send_sems, recv_sems, local_sem):
        x, y, c = lax.axis_index("x"), lax.axis_index("y"), lax.axis_index("c")
        me, sibling = (x, y, c), (x, y, 1 - c)
        chips = [(1 - x, y), (x, 1 - y), (1 - x, 1 - y)]
        slab = lambda px, py, pc: out_ref.at[4 * px + 2 * py + pc]

        def copy(k, block, to, own=False):
            return pltpu.make_async_remote_copy(
                src_ref=src_ref if own else slab(*block), dst_ref=slab(*block), send_sem=send_sems.at[k], recv_sem=recv_sems.at[k],
                device_id=to, device_id_type=pl.DeviceIdType.MESH)

        mine = pltpu.make_async_copy(src_ref, slab(*me), local_sem)
        mine.start()
        first = [copy(0, me, sibling, own=True)] + [copy(1 + j, me, (*chip, c), own=True) for j, chip in enumerate(chips)]
        for cp in first:
            cp.start()
        passed = [copy(4 + j, (*chip, c), sibling) for j, chip in enumerate(chips)]
        for j, chip in enumerate(chips):
            copy(1 + j, (*chip, c), me).wait_recv()
            passed[j].start()
        copy(0, sibling, me).wait_recv()
        for j, chip in enumerate(chips):
            copy(4 + j, (*chip, 1 - c), me).wait_recv()
        for cp in first + passed:
            cp.wait_send()
        mine.wait()

    return pl.pallas_call(
        body, out_shape=jax.ShapeDtypeStruct((N_DEV, src.shape[-2], LANES), src.dtype),
        in_specs=[pl.BlockSpec(memory_space=pl.ANY)], out_specs=pl.BlockSpec(memory_space=pl.ANY),
        scratch_shapes=list(_Exchange.SCRATCH), name=name)(src)


def _exchange(src, *, gather, name):
    def body(src_ref, out_ref, *sems):
        ex = _Exchange(src_ref, out_ref, *sems, gather)
        ex.start()
        ex.wait()

    return pl.pallas_call(
        body, out_shape=jax.ShapeDtypeStruct((N_DEV, src.shape[-2], LANES), src.dtype),
        in_specs=[pl.BlockSpec(memory_space=pl.ANY)], out_specs=pl.BlockSpec(memory_space=pl.ANY),
        scratch_shapes=list(_Exchange.SCRATCH), name=name)(src)


PACK_ALIGN = 16 * LANES
PACK_ROWS = 512


def _pack(arrs, dtype, lead=False):
    parts, rows = [], 0
    for a in arrs:
        n_lead = a.shape[0] if lead else 1
        n = a.size // n_lead
        a = a.astype(dtype)
        if n % PACK_ALIGN:
            a = jnp.pad(a.reshape(n_lead, n), ((0, 0), (0, -n % PACK_ALIGN)))
        parts.append(a.reshape(n_lead, -1, LANES))
        rows += parts[-1].shape[1]
    if rows % PACK_ROWS:
        parts.append(jnp.zeros((parts[0].shape[0], -rows % PACK_ROWS, LANES), dtype))
    buf = jnp.concatenate(parts, axis=1)
    return buf if lead else buf[0]


def _unpack(buf, shapes, lead=False):
    buf = buf if lead else buf[None]
    out, off = [], 0
    for shp in shapes:
        n = math.prod(shp)
        rows = (n + (-n % PACK_ALIGN)) // LANES
        piece = buf[:, off:off + rows]
        if n % PACK_ALIGN:
            piece = piece.reshape(buf.shape[0], rows * LANES)[:, :n]
        out.append(piece.reshape(((buf.shape[0],) if lead else ()) + tuple(shp)))
        off += rows
    return out


def _unshard(parts, axis):
    moved = jnp.moveaxis(parts, 0, axis)
    shp = list(moved.shape)
    return moved.reshape(shp[:axis] + [shp[axis] * shp[axis + 1]] + shp[axis + 2:])


def _to_parts(full, axis):
    shp = list(full.shape)
    split = full.reshape(shp[:axis] + [N_DEV, shp[axis] // N_DEV] + shp[axis + 1:])
    return jnp.moveaxis(split, axis, 0)


def _adamw(gparts, w, m, v, *, name):
    r = w.shape[0]
    tr = _pick(r, (PACK_ROWS,))

    def body(g_ref, w_ref, m_ref, v_ref, go, do, mo, vo):
        g = g_ref[0].astype(F32)
        for d in range(1, N_DEV):
            g = g + g_ref[d].astype(F32)
        m1 = ADAM_B1 * m_ref[...] + (1.0 - ADAM_B1) * g
        v1 = ADAM_B2 * v_ref[...] + (1.0 - ADAM_B2) * jnp.square(g)
        m_hat = m1 / (1.0 - ADAM_B1 ** ADAM_STEP)
        v_hat = v1 / (1.0 - ADAM_B2 ** ADAM_STEP)
        go[...] = g
        do[...] = -ADAM_LR * (m_hat / (jnp.sqrt(v_hat) + ADAM_EPS) + ADAM_WD * w_ref[...])
        mo[...] = m1
        vo[...] = v1

    blk = pl.BlockSpec((tr, LANES), lambda i: (i, 0))
    return pl.pallas_call(
        body, grid=(r // tr,), in_specs=[pl.BlockSpec((N_DEV, tr, LANES), lambda i: (0, i, 0)), blk, blk, blk], out_specs=[blk] * 4,
        out_shape=[jax.ShapeDtypeStruct((r, LANES), F32)] * 4, compiler_params=_cparams("parallel"), name=name)(gparts, w, m, v)


def _loss_and_grad(h, g, tgt, *, name, ts=256):
    s, d = h.shape
    ts = min(ts, s)

    def tile_loss(hv, gv, tv):
        (y,) = _rmsnorm_tile(0, hv, gv)
        return 0.5 * jnp.sum(jnp.mean(jnp.square(y - tv), axis=-1))

    def body(h_ref, g_ref, t_ref, l_ref, dh_ref, dg_ref):
        @pl.when(pl.program_id(0) == 0)
        def _():
            l_ref[...] = jnp.zeros_like(l_ref)
            dg_ref[...] = jnp.zeros_like(dg_ref)

        tv = t_ref[...]
        loss, vjp = jax.vjp(lambda hv, gv: tile_loss(hv, gv, tv), h_ref[...], g_ref[...])
        dh, dg = vjp(jnp.ones((), F32))
        l_ref[...] += loss
        dh_ref[...] = dh
        dg_ref[...] += dg

    row = pl.BlockSpec((ts, d), lambda i: (i, 0))
    return pl.pallas_call(
        body, grid=(s // ts,), in_specs=[row, pl.BlockSpec((1, d), lambda i: (0, 0)), row],
        out_specs=[pl.BlockSpec((8, LANES), lambda i: (0, 0)), row, pl.BlockSpec((1, d), lambda i: (0, 0))],
        out_shape=[jax.ShapeDtypeStruct((8, LANES), F32), jax.ShapeDtypeStruct((s, d), F32), jax.ShapeDtypeStruct((1, d), F32)],
        compiler_params=_cparams("arbitrary"), name=name)(h, g, tgt)


def _norm(h, g, name):
    return _rw_fwd(_rmsnorm_tile, [_row(h)], [g], [(h.shape[1], F32)], name=name)[0]


def _norm_bwd(h, g, dhn, dres, name):
    (dh,), (dg,) = _rw_bwd(_rmsnorm_tile, [_row(h)], [g], [dhn], add_rows={0: dres}, name=name)
    return dh, dg


def _layer_fwd(h, p_i, lw, mixer_fwd, mp, tag, carry=None, late=None):
    hn = _norm(h, lw["norm_mix"], f"{tag}_norm_mix")
    proj = _matmul(hn, lw["w_in"], "nn", name=f"{tag}_in_proj")
    y, mres, carried = mixer_fwd(proj, mp, carry)
    if late is not None:
        lw = {**lw, **late(carried)}
    h1 = _matmul(y, lw["w_out"], "nn", name=f"{tag}_out_proj", add=h)
    hn2 = _norm(h1, lw["norm_ffn"], f"{tag}_norm_ffn")
    act = _matmul(hn2, lw["w1"], "nn", name=f"{tag}_mlp_up", post=_relu2_of)
    h2 = _matmul(act, lw["w2"], "nn", name=f"{tag}_mlp_down", add=h1)
    hn3 = _norm(h2, lw["norm_pl"], f"{tag}_norm_pl")
    gl = _matmul(hn3, lw["w_gate"], "nn", name=f"{tag}_pl_gate")
    e = _matmul(p_i, lw["w_pl"], "nn", name=f"{tag}_pl_proj")
    (h3,) = _rw_fwd(_pl_gate, [_row(h2), _row(gl), _row(e)], [], [(h.shape[1], F32)], name=f"{tag}_pl_mix")
    return h3, (h, hn, proj, y, mres, h1, hn2, act, h2, hn3, gl, e), lw


def _layer_bwd(dh3, p_i, lw, mixer_bwd, mp, saved, tag, early=None):
    h, hn, proj, y, mres, h1, hn2, act, h2, hn3, gl, e = saved
    (dgl, de), _ = _rw_bwd(_pl_gate, [_row(h2), _row(gl), _row(e)], [], [dh3], row_grads=[1, 2], name=f"{tag}_pl_mix_bwd")
    g = dict(w_pl=_matmul(p_i, de, "tn", name=f"{tag}_pl_proj_dw"), w_gate=_matmul(hn3, dgl, "tn", name=f"{tag}_pl_gate_dw"))
    dhn3 = _matmul(dgl, lw["w_gate"], "nt", name=f"{tag}_pl_gate_dx")
    dh2, g["norm_pl"] = _norm_bwd(h2, lw["norm_pl"], dhn3, dh3, f"{tag}_norm_pl_bwd")
    da1 = _matmul(dh2, lw["w2"], "nt", name=f"{tag}_mlp_down_dx", post=_relu2_grad, aux=act)
    g["w2"] = _matmul(act, dh2, "tn", name=f"{tag}_mlp_down_dw")
    g["w1"] = _matmul(hn2, da1, "tn", name=f"{tag}_mlp_up_dw")
    dhn2 = _matmul(da1, lw["w1"], "nt", name=f"{tag}_mlp_up_dx")
    dh1, g["norm_ffn"] = _norm_bwd(h1, lw["norm_ffn"], dhn2, dh2, f"{tag}_norm_ffn_bwd")
    dy = _matmul(dh1, lw["w_out"], "nt", name=f"{tag}_out_proj_dx")
    g["w_out"] = _matmul(y, dh1, "tn", name=f"{tag}_out_proj_dw")
    dproj, mg, carried = mixer_bwd(proj, mp, mres, dy, None if early is None else early(g))
    g["w_in"] = _matmul(hn, dproj, "tn", name=f"{tag}_in_proj_dw")
    dhn = _matmul(dproj, lw["w_in"], "nt", name=f"{tag}_in_proj_dx")
    dh, g["norm_mix"] = _norm_bwd(h, lw["norm_mix"], dhn, dh1, f"{tag}_norm_mix_bwd")
    return dh, g, mg, carried


WEIGHTS = (
    ("norm_mix", None), ("norm_ffn", None), ("norm_pl", None), ("mlp_w1", 2), ("mlp_w2", 1), ("pl_proj", 2), ("pl_gate", 1),
    ("e_in_proj", 2), ("e_out_proj", 1), ("s5_lam_re", None), ("s5_lam_im", None), ("s5_log_step", None), ("s5_b_re", None),
    ("s5_b_im", None), ("s5_c_re", None), ("s5_c_im", None), ("s5_d", None), ("s5_glu_w", 1), ("s5_glu_b", None),
    ("ssd_conv_w", 2), ("ssd_conv_b", None), ("ssd_dt_bias", None), ("ssd_a_log", None), ("ssd_d", None), ("ssd_norm", None),
    ("o_in_proj", 2), ("o_out_proj", 1), ("rwkv_mu", 1), ("rwkv_w0", 1), ("rwkv_w_up", 2), ("rwkv_a0", 1), ("rwkv_a_up", 2),
    ("rwkv_g_up", 2), ("rwkv_k_k", 1), ("rwkv_k_a", 1), ("rwkv_r_k", None), ("rwkv_ln_g", 1), ("rwkv_ln_b", 1),
    ("lru_conv_w", 2), ("lru_conv_b", 1), ("lru_w_a", None), ("lru_b_a", None), ("lru_w_x", None), ("lru_b_x", None),
    ("lru_lam", None), ("norm_final", None))
MATMUL_WEIGHTS = ("mlp_w1", "mlp_w2", "pl_proj", "pl_gate", "e_in_proj", "e_out_proj", "s5_glu_w", "o_in_proj", "o_out_proj",
                  "rwkv_w_up", "rwkv_a_up", "rwkv_g_up")


LATE_WEIGHTS = (("mlp_w1", 1), ("mlp_w2", 1), ("pl_gate", 1), ("pl_proj", 1))
EARLY_GRADS = LATE_WEIGHTS + (("o_out_proj", 0),)


def kernel(x, p, norm_mix, norm_ffn, norm_pl, mlp_w1, mlp_w2, pl_proj, pl_gate, e_in_proj, e_out_proj, s5_lam_re, s5_lam_im, s5_log_step, s5_b_re, s5_b_im, s5_c_re, s5_c_im, s5_d, s5_glu_w, s5_glu_b, ssd_conv_w, ssd_conv_b, ssd_dt_bias, ssd_a_log, ssd_d, ssd_norm, o_in_proj, o_out_proj, rwkv_mu, rwkv_w0, rwkv_w_up, rwkv_a0, rwkv_a_up, rwkv_g_up, rwkv_k_k, rwkv_k_a, rwkv_r_k, rwkv_ln_g, rwkv_ln_b, lru_conv_w, lru_conv_b, lru_w_a, lru_b_a, lru_w_x, lru_b_x, lru_lam, norm_final, loss_target, m_norm_mix, m_norm_ffn, m_norm_pl, m_mlp_w1, m_mlp_w2, m_pl_proj, m_pl_gate, m_e_in_proj, m_e_out_proj, m_s5_lam_re, m_s5_lam_im, m_s5_log_step, m_s5_b_re, m_s5_b_im, m_s5_c_re, m_s5_c_im, m_s5_d, m_s5_glu_w, m_s5_glu_b, m_ssd_conv_w, m_ssd_conv_b, m_ssd_dt_bias, m_ssd_a_log, m_ssd_d, m_ssd_norm, m_o_in_proj, m_o_out_proj, m_rwkv_mu, m_rwkv_w0, m_rwkv_w_up, m_rwkv_a0, m_rwkv_a_up, m_rwkv_g_up, m_rwkv_k_k, m_rwkv_k_a, m_rwkv_r_k, m_rwkv_ln_g, m_rwkv_ln_b, m_lru_conv_w, m_lru_conv_b, m_lru_w_a, m_lru_b_a, m_lru_w_x, m_lru_b_x, m_lru_lam, m_norm_final, v_norm_mix, v_norm_ffn, v_norm_pl, v_mlp_w1, v_mlp_w2, v_pl_proj, v_pl_gate, v_e_in_proj, v_e_out_proj, v_s5_lam_re, v_s5_lam_im, v_s5_log_step, v_s5_b_re, v_s5_b_im, v_s5_c_re, v_s5_c_im, v_s5_d, v_s5_glu_w, v_s5_glu_b, v_ssd_conv_w, v_ssd_conv_b, v_ssd_dt_bias, v_ssd_a_log, v_ssd_d, v_ssd_norm, v_o_in_proj, v_o_out_proj, v_rwkv_mu, v_rwkv_w0, v_rwkv_w_up, v_rwkv_a0, v_rwkv_a_up, v_rwkv_g_up, v_rwkv_k_k, v_rwkv_k_a, v_rwkv_r_k, v_rwkv_ln_g, v_rwkv_ln_b, v_lru_conv_w, v_lru_conv_b, v_lru_w_a, v_lru_b_a, v_lru_w_x, v_lru_b_x, v_lru_lam, v_norm_final):
    a = dict(locals())
    d_model = x.shape[-1]
    row = lambda v: v.reshape(1, d_model)
    axis = dict(WEIGHTS)
    keys = [(n, i) for n, ax in WEIGHTS if ax is not None for i in range(a[n].shape[0])]
    shard = lambda pre, key: a[pre + key[0]][key[1]]
    piece_axis = lambda key: axis[key[0]] - 1

    def gathered(got, ks):
        parts = _unpack(got, [shard("", k).shape for k in ks], lead=True)
        return {k: _unshard(pt, piece_axis(k)) for k, pt in zip(ks, parts)}

    in_layer1 = lambda k: k[1] == 1 or k[0].startswith(("o_", "rwkv_", "lru_"))
    first_keys = [k for k in keys if k[0] in MATMUL_WEIGHTS and not in_layer1(k)]
    mix1_keys = [k for k in keys if k[0] in MATMUL_WEIGHTS and in_layer1(k) and k not in LATE_WEIGHTS]
    f32_keys = [k for k in keys if k[0] not in MATMUL_WEIGHTS]
    full = gathered(_gather_two_level(_pack([shard("", k) for k in first_keys], BF16), name="gather_weights_bf16"), first_keys)
    full.update(gathered(_exchange(_pack([shard("", k) for k in f32_keys], F32), gather=True, name="gather_weights_f32"), f32_keys))
    by_name = lambda: {n: (a[n] if ax is None else full.get((n, 0))) for n, ax in WEIGHTS}

    lw0 = dict(norm_mix=row(norm_mix[0]), norm_ffn=row(norm_ffn[0]), norm_pl=row(norm_pl[0]), w_in=_even_in_cols(full["e_in_proj", 0]),
               w_out=full["e_out_proj", 0], w1=full["mlp_w1", 0], w2=full["mlp_w2", 0], w_gate=full["pl_gate", 0], w_pl=full["pl_proj", 0])
    mp0 = _even_prep(by_name())
    mix1_src = _pack([shard("", k) for k in mix1_keys], BF16)

    def take_mix1(got):
        full.update(gathered(got, mix1_keys))
        return {}

    h1, saved0, lw0 = _layer_fwd(x[0], p[0, 0], lw0, _even_fwd, mp0, "l0", carry=(mix1_src, True), late=take_mix1)

    def late(got):
        fl = gathered(got, LATE_WEIGHTS)
        return dict(w1=fl["mlp_w1", 1], w2=fl["mlp_w2", 1], w_gate=fl["pl_gate", 1], w_pl=fl["pl_proj", 1])

    lw1 = dict(norm_mix=row(norm_mix[1]), norm_ffn=row(norm_ffn[1]), norm_pl=row(norm_pl[1]), w_in=_odd_in_cols(full["o_in_proj", 0]),
               w_out=full["o_out_proj", 0])
    mp1 = _odd_prep(by_name())
    late_src = _pack([shard("", k) for k in LATE_WEIGHTS], BF16)
    h2, saved1, lw1 = _layer_fwd(h1, p[1, 0], lw1, _odd_fwd, mp1, "l1", carry=(late_src, True), late=late)
    loss_blk, dh, dg_final = _loss_and_grad(h2, row(norm_final), loss_target[0], name="loss")
    loss = lax.psum(loss_blk[0, 0], ("x", "y", "c"))

    def slabs(grad_of, ks):
        return _pack([_to_parts(grad_of[k], piece_axis(k)) for k in ks], BF16, lead=True)

    def global_shape(k):
        return tuple((N_DEV if d == piece_axis(k) else 1) * n for d, n in enumerate(shard("", k).shape))

    early_grads = lambda g: (slabs({("mlp_w1", 1): g["w1"], ("mlp_w2", 1): g["w2"], ("pl_gate", 1): g["w_gate"],
                                    ("pl_proj", 1): g["w_pl"], ("o_out_proj", 0): g["w_out"]}, EARLY_GRADS), False)
    dh, g1, mg1, early_got = _layer_bwd(dh, p[1, 0], lw1, _odd_bwd, mp1, saved1, "l1", early=early_grads)
    rest1_keys = [k for k in keys if in_layer1(k) and k not in EARLY_GRADS]
    rest1_grad = {k: (_odd_in_cols_t(g1["w_in"]) if k[0] == "o_in_proj" else mg1[k[0]].reshape(global_shape(k))) for k in rest1_keys}
    l0a_keys = [("mlp_w1", 0), ("pl_proj", 0)]
    l0_early = lambda g: ((slabs(rest1_grad, rest1_keys), False), (slabs({("mlp_w1", 0): g["w1"], ("pl_proj", 0): g["w_pl"]}, l0a_keys), False))
    dh, g0, mg0, (rest1_got, l0a_got) = _layer_bwd(dh, p[0, 0], lw0, _even_bwd, mp0, saved0, "l0", early=l0_early)

    main_keys = [k for k in keys if not in_layer1(k) and k not in l0a_keys]
    piece_grad = {("mlp_w2", 0): g0["w2"], ("pl_gate", 0): g0["w_gate"], ("e_in_proj", 0): _even_in_cols_t(g0["w_in"]), ("e_out_proj", 0): g0["w_out"]}
    mixer_grads = {**mg0, **mg1}
    for k in main_keys:
        if k not in piece_grad:
            piece_grad[k] = mg0[k[0]].reshape(global_shape(k))
    main_got = _exchange(slabs(piece_grad, main_keys), gather=False, name="scatter_grads")
    piece_out = {}
    for got, ks, tag in ((early_got, EARLY_GRADS, "early"), (rest1_got, rest1_keys, "rest1"), (l0a_got, l0a_keys, "l0a"), (main_got, main_keys, "main")):
        res = _adamw(got, *[_pack([shard(pre, k) for k in ks], F32) for pre in ("", "m_", "v_")], name=f"adamw_sharded_{tag}")
        for j in range(4):
            piece_out.update({(j, k): v for k, v in zip(ks, _unpack(res[j], [shard("", k).shape for k in ks]))})
    rp = [n for n, ax in WEIGHTS if ax is None]
    rp_grads = {**mixer_grads, "norm_final": dg_final,
                **{n: jnp.stack([g0[n], g1[n]]) for n in ("norm_mix", "norm_ffn", "norm_pl")}}
    parts = _exchange(_pack([rp_grads[n].reshape(a[n].shape) for n in rp], F32), gather=True, name="gather_small_grads")
    rp_res = _adamw(parts, *[_pack([a[pre + n] for n in rp], F32) for pre in ("", "m_", "v_")], name="adamw_replicated")

    outs = []
    for j in range(4):
        rp_out = dict(zip(rp, _unpack(rp_res[j], [a[n].shape for n in rp])))
        outs.extend(rp_out[n] if ax is None else jnp.stack([piece_out[j, (n, i)] for i in range(a[n].shape[0])]) for n, ax in WEIGHTS)
    return (loss, dh[None], *outs)
```

```python
import functools
import math

import jax
import jax.numpy as jnp
from jax import lax
from jax.experimental import pallas as pl
from jax.experimental.pallas import tpu as pltpu

F32 = jnp.float32
BF16 = jnp.bfloat16
N_DEV = 8
LANES = 128
VMEM_LIMIT = 56 * 1024 * 1024
MATMUL_VMEM = 46 * 1024 * 1024
NORM_EPS = 1e-6
RWKV_GN_EPS = 64e-5
LRU_C = 8.0
ADAM_LR, ADAM_B1, ADAM_B2, ADAM_EPS, ADAM_WD, ADAM_STEP = 0.001, 0.9, 0.999, 1e-08, 0.01, 10
SSD_CHUNK = 128
RWKV_CHUNK = 32
HEAD = 64


def _cparams(*sem):
    return pltpu.CompilerParams(dimension_semantics=sem, vmem_limit_bytes=VMEM_LIMIT)


def _pick(n, prefs):
    for t in prefs:
        if n % t == 0:
            return t
    return n


def _relu2_of(val, _):
    r = jnp.maximum(val, 0.0)
    return r * r


def _relu2_grad(dact, act):
    return dact * (2.0 * jnp.sqrt(act))


def _matmul(a, b, mode, *, name, add=None, out_dtype=F32, post=None, aux=None):
    if mode == "nn":
        (m, k), (k2, n) = a.shape, b.shape
    elif mode == "nt":
        (m, k), (n, k2) = a.shape, b.shape
    else:
        (k, m), (k2, n) = a.shape, b.shape
    assert k == k2, (a.shape, b.shape, mode)
    tm, tn = _pick(m, (1024, 512, 256, 128)), _pick(n, (1024, 512, 256, 128))
    extras = [x for x in (add, aux) if x is not None]
    fits = lambda t: 2 * t * (tm * a.dtype.itemsize + tn * b.dtype.itemsize) + (3 + 2 * len(extras)) * tm * tn * 4 <= MATMUL_VMEM
    tk = next((t for t in (2048, 1024, 512, 256, 128) if k % t == 0 and fits(t)), k)
    nk = k // tk
    dn = {"nn": (((1,), (0,)), ((), ())), "nt": (((1,), (1,)), ((), ())), "tn": (((0,), (0,)), ((), ()))}[mode]
    a_spec = pl.BlockSpec((tk, tm), lambda i, j, kk: (kk, i)) if mode == "tn" else pl.BlockSpec((tm, tk), lambda i, j, kk: (i, kk))
    b_spec = pl.BlockSpec((tn, tk), lambda i, j, kk: (j, kk)) if mode == "nt" else pl.BlockSpec((tk, tn), lambda i, j, kk: (kk, j))
    o_spec = pl.BlockSpec((tm, tn), lambda i, j, kk: (i, j))

    def body(a_ref, b_ref, *rest):
        extra_refs, (o_ref, acc) = list(rest[:len(extras)]), rest[len(extras):]
        add_ref = extra_refs.pop(0) if add is not None else None
        aux_ref = extra_refs.pop(0) if aux is not None else None
        kk = pl.program_id(2)
        prod = lambda: lax.dot_general(a_ref[...].astype(BF16), b_ref[...].astype(BF16), dn, preferred_element_type=F32)
        first = lambda: prod() if add is None else prod() + add_ref[...].astype(F32)

        def write(val):
            if post is not None:
                val = post(val, None if aux is None else aux_ref[...])
            o_ref[...] = val.astype(o_ref.dtype)

        if nk == 1:
            write(first())
            return

        @pl.when(kk == 0)
        def _():
            acc[...] = first()

        @pl.when((kk > 0) & (kk < nk - 1))
        def _():
            acc[...] += prod()

        @pl.when(kk == nk - 1)
        def _():
            write(acc[...] + prod())

    ins, specs = [a, b] + extras, [a_spec, b_spec] + [o_spec] * len(extras)
    return pl.pallas_call(
        body, grid=(m // tm, n // tn, nk), in_specs=specs, out_specs=o_spec,
        out_shape=jax.ShapeDtypeStruct((m, n), out_dtype), scratch_shapes=[pltpu.VMEM((tm, tn), F32)],
        compiler_params=_cparams("parallel", "parallel", "arbitrary"), name=name)(*ins)


def _row(x, width=None, block=0):
    return (x, x.shape[1] if width is None else width, block)


def _rw_specs(rows, params, ts):
    specs = [pl.BlockSpec((ts, w), functools.partial(lambda i, b: (i, b), b=bi)) for (_, w, bi) in rows]
    specs += [pl.BlockSpec(p.shape, functools.partial(lambda i, nd: (0,) * nd, nd=p.ndim)) for p in params]
    return specs


def _rw_fwd(f, rows, params, outs, *, name, ts=256):
    s = rows[0][0].shape[0]
    ts = min(ts, s)
    nr, npar = len(rows), len(params)

    def body(*refs):
        row0 = pl.program_id(0) * ts
        res = f(row0, *[r[...] for r in refs[:nr + npar]])
        for o_ref, val in zip(refs[nr + npar:], res, strict=True):
            o_ref[...] = val.astype(o_ref.dtype)

    out = pl.pallas_call(
        body, grid=(s // ts,), in_specs=_rw_specs(rows, params, ts),
        out_specs=[pl.BlockSpec((ts, w), lambda i: (i, 0)) for (w, _) in outs],
        out_shape=[jax.ShapeDtypeStruct((s, w), dt) for (w, dt) in outs],
        compiler_params=_cparams("parallel"), name=name)(*[r[0] for r in rows], *params)
    return tuple(out)


def _rw_bwd(f, rows, params, cts, *, name, ts=256, row_grads=None, param_grads=None, add_rows=None):
    s = rows[0][0].shape[0]
    ts = min(ts, s)
    ct_groups = [list(c) if isinstance(c, (list, tuple)) else [c] for c in cts]
    cts = [c for grp in ct_groups for c in grp]
    nr, npar, nct = len(rows), len(params), len(cts)
    row_grads = list(range(nr)) if row_grads is None else list(row_grads)
    param_grads = list(range(npar)) if param_grads is None else list(param_grads)
    add_rows = add_rows or {}
    add_keys = sorted(add_rows)

    def body(*refs):
        i = pl.program_id(0)
        row0 = i * ts
        vals = [r[...] for r in refs[:nr + npar]]
        for pi in param_grads:
            vals[nr + pi] = vals[nr + pi].astype(F32)
        ct_refs = list(refs[nr + npar:nr + npar + nct])
        add_refs = dict(zip(add_keys, refs[nr + npar + nct:nr + npar + nct + len(add_keys)]))
        o_refs = refs[nr + npar + nct + len(add_keys):]
        res, vjp = jax.vjp(functools.partial(f, row0), *vals)
        ct_vals = []
        for grp, r in zip(ct_groups, res, strict=True):
            ct_vals.append(sum(ct_refs.pop(0)[...].astype(r.dtype) for _ in grp))
        grads = vjp(tuple(ct_vals))
        for o_ref, ri in zip(o_refs[:len(row_grads)], row_grads):
            g = grads[ri]
            if ri in add_refs:
                g = g + add_refs[ri][...]
            o_ref[...] = g.astype(o_ref.dtype)

        @pl.when(i == 0)
        def _():
            for o_ref in o_refs[len(row_grads):]:
                o_ref[...] = jnp.zeros_like(o_ref)

        for o_ref, pi in zip(o_refs[len(row_grads):], param_grads):
            o_ref[...] += grads[nr + pi].astype(F32)

    in_specs = _rw_specs(rows, params, ts)
    in_specs += [pl.BlockSpec((ts, c.shape[1]), lambda i: (i, 0)) for c in cts]
    in_specs += [pl.BlockSpec((ts, add_rows[k].shape[1]), lambda i: (i, 0)) for k in add_keys]
    out_specs = [pl.BlockSpec((ts, rows[ri][1]), lambda i: (i, 0)) for ri in row_grads]
    out_specs += [pl.BlockSpec(params[pi].shape, functools.partial(lambda i, nd: (0,) * nd, nd=params[pi].ndim)) for pi in param_grads]
    out_shape = [jax.ShapeDtypeStruct((s, rows[ri][1]), F32) for ri in row_grads]
    out_shape += [jax.ShapeDtypeStruct(params[pi].shape, F32) for pi in param_grads]
    out = pl.pallas_call(
        body, grid=(s // ts,), in_specs=in_specs, out_specs=out_specs, out_shape=out_shape,
        compiler_params=_cparams("arbitrary"), name=name)(*[r[0] for r in rows], *params, *cts, *[add_rows[k] for k in add_keys])
    return tuple(out[:len(row_grads)]), tuple(out[len(row_grads):])


@functools.partial(jax.custom_vjp, nondiff_argnums=(1,))
def _shift_down(x, k):
    rows = lax.broadcasted_iota(jnp.int32, x.shape, 0)
    return jnp.where(rows < k, 0.0, pltpu.roll(x, k, 0))


def _shift_down_fwd(x, k):
    return _shift_down(x, k), None


def _shift_down_bwd(k, _, g):
    n = g.shape[0]
    rows = lax.broadcasted_iota(jnp.int32, g.shape, 0)
    return (jnp.where(rows >= n - k, 0.0, pltpu.roll(g, n - k, 0)),)


_shift_down.defvjp(_shift_down_fwd, _shift_down_bwd)


def _ct_fwd(f, x, xmap, ntiles, params, out_width, *, name, ct=LANES):
    s = x.shape[0]

    def body(*refs):
        refs[-1][...] = f(*[r[...] for r in refs[:-1]])

    in_specs = [pl.BlockSpec((s, ct), lambda j: (0, xmap(j)))]
    in_specs += [pl.BlockSpec((p.shape[0], ct), functools.partial(lambda j, o: (0, o + j), o=o)) for (p, o) in params]
    return pl.pallas_call(
        body, grid=(ntiles,), in_specs=in_specs, out_specs=pl.BlockSpec((s, ct), lambda j: (0, j)),
        out_shape=jax.ShapeDtypeStruct((s, out_width), F32), compiler_params=_cparams("parallel"), name=name)(x, *[p for p, _ in params])


def _ct_bwd(f, x, xmap, ntiles, params, g, *, name, ct=LANES):
    s = x.shape[0]
    npar, ng = len(params), len(g)

    def body(*refs):
        vals = [r[...] for r in refs[:1 + npar]]
        _, vjp = jax.vjp(f, *vals)
        grads = vjp(sum(r[...] for r in refs[1 + npar:1 + npar + ng]))
        for o_ref, gr in zip(refs[1 + npar + ng:], grads, strict=True):
            o_ref[...] = gr

    in_specs = [pl.BlockSpec((s, ct), lambda j: (0, xmap(j)))]
    pspecs = [pl.BlockSpec((p.shape[0], ct), functools.partial(lambda j, o: (0, o + j), o=o)) for (p, o) in params]
    in_specs += pspecs + [pl.BlockSpec((s, ct), lambda j: (0, j))] * ng
    out = pl.pallas_call(
        body, grid=(ntiles,), in_specs=in_specs, out_specs=[pl.BlockSpec((s, ct), lambda j: (0, j))] + pspecs,
        out_shape=[jax.ShapeDtypeStruct((s, ntiles * ct), F32)] + [jax.ShapeDtypeStruct(p.shape, F32) for p, _ in params],
        compiler_params=_cparams("parallel"), name=name)(x, *[p for p, _ in params], *g)
    return out[0], tuple(out[1:])


def _block_vjp(f, args, cts, *, name):
    outs = jax.eval_shape(f, *args)
    if cts is None:
        def body(*refs):
            for o_ref, v in zip(refs[len(args):], f(*[r[...] for r in refs[:len(args)]]), strict=True):
                o_ref[...] = v
        return tuple(pl.pallas_call(body, out_shape=[jax.ShapeDtypeStruct(o.shape, o.dtype) for o in outs], name=name)(*args))

    def body(*refs):
        n = len(args)
        _, vjp = jax.vjp(f, *[r[...] for r in refs[:n]])
        for o_ref, gr in zip(refs[n + len(cts):], vjp(tuple(r[...] for r in refs[n:n + len(cts)])), strict=True):
            o_ref[...] = gr
    return tuple(pl.pallas_call(body, out_shape=[jax.ShapeDtypeStruct(a.shape, a.dtype) for a in args], name=name)(*args, *cts))


def _softplus(x):
    return jnp.maximum(x, 0.0) + jnp.log(1.0 + jnp.exp(-jnp.abs(x)))


def _dot_bf16(a, b):
    return jnp.dot(a.astype(BF16), b.astype(BF16), preferred_element_type=F32)


def _dot3(x, m):
    hi = x.astype(BF16)
    r1 = x - hi.astype(F32)
    mid = r1.astype(BF16)
    lo = (r1 - mid.astype(F32)).astype(BF16)
    return (jnp.dot(hi, m, preferred_element_type=F32) + jnp.dot(mid, m, preferred_element_type=F32)
            + jnp.dot(lo, m, preferred_element_type=F32))


@jax.custom_vjp
def _lin(x, m, mt):
    return _dot3(x, m)


def _lin_fwd(x, m, mt):
    return _dot3(x, m), (m, mt)


def _lin_bwd(res, g):
    m, mt = res
    return _dot3(g, mt), jnp.zeros_like(m), jnp.zeros_like(mt)


_lin.defvjp(_lin_fwd, _lin_bwd)


def _head_ones(n, head=HEAD):
    i = jnp.arange(n) // head
    return (i[:, None] == i[None, :]).astype(BF16)


def _rmsnorm_tile(_, h, g):
    return (h * lax.rsqrt(jnp.mean(h * h, axis=-1, keepdims=True) + NORM_EPS) * g,)


ROWS = 8


def _shift_rows(x, k, fill, up=False):
    rows = lax.broadcasted_iota(jnp.int32, x.shape, 0)
    if up:
        return jnp.where(rows >= ROWS - k, fill, pltpu.roll(x, ROWS - k, 0))
    return jnp.where(rows < k, fill, pltpu.roll(x, k, 0))


def _cmul(pr, pi, qr, qi):
    return pr * qr - pi * qi, pr * qi + pi * qr


def _power_rows(ar, ai, width, descending):
    pows = [(ar, ai)]
    for _ in range(ROWS - 1):
        pows.append(_cmul(*pows[-1], ar, ai))
    rows = lax.broadcasted_iota(jnp.int32, (ROWS, width), 0)
    pr = jnp.zeros((ROWS, width), F32)
    pi = jnp.zeros((ROWS, width), F32)
    for j in range(ROWS):
        qr, qi = pows[ROWS - 1 - j] if descending else pows[j]
        pr, pi = jnp.where(rows == j, qr, pr), jnp.where(rows == j, qi, pi)
    return pr, pi, ((pows[0], 1), (pows[1], 2), (pows[3], 4))


def _prev_rows(ref, tile, r0):
    before = ref[pl.ds(jnp.maximum(r0 - 1, 0), 1), :] * (r0 > 0).astype(F32)
    rows = lax.broadcasted_iota(jnp.int32, tile.shape, 0)
    return jnp.where(rows == 0, before, pltpu.roll(tile, 1, 0))


def _cscan_fwd(bu, a, *, name, ct=256, carry=None):
    s, c2 = bu.shape
    c = c2 // 2
    nt = c // ct

    def body(br_ref, bi_ref, ar_ref, ai_ref, xr_ref, xi_ref):
        pr, pi, doubling = _power_rows(ar_ref[...], ai_ref[...], ct, False)

        def tile(i, h):
            rows = pl.ds(pl.multiple_of(i * ROWS, ROWS), ROWS)
            sr, si = br_ref[rows, :], bi_ref[rows, :]
            for (qr, qi), k in doubling:
                mr, mi = _cmul(qr, qi, _shift_rows(sr, k, 0.0), _shift_rows(si, k, 0.0))
                sr, si = sr + mr, si + mi
            cr, ci = _cmul(pr, pi, *h)
            sr, si = sr + cr, si + ci
            xr_ref[rows, :] = sr
            xi_ref[rows, :] = si
            return sr[ROWS - 1:], si[ROWS - 1:]

        z = jnp.zeros((1, ct), F32)
        lax.fori_loop(0, s // ROWS, tile, (z, z))

    re = lambda j: (0, j)
    im = lambda j: (0, nt + j)
    return _call_with_exchange(
        body, nt, carry, [bu, bu, a, a],
        [pl.BlockSpec((s, ct), re), pl.BlockSpec((s, ct), im), pl.BlockSpec((1, ct), re), pl.BlockSpec((1, ct), im)],
        [pl.BlockSpec((s, ct), re), pl.BlockSpec((s, ct), re)], [jax.ShapeDtypeStruct((s, c), F32)] * 2, [], name)


def _cscan_bwd(gr, gi, xr, xi, a, *, name, ct=256, carry=None):
    s, c = gr.shape
    nt = c // ct
    n_tiles = s // ROWS

    def body(gr_ref, gi_ref, xr_ref, xi_ref, ar_ref, ai_ref, dr_ref, di_ref, dar_ref, dai_ref):
        pr, pi, doubling = _power_rows(ar_ref[...], -ai_ref[...], ct, True)

        def tile(i, state):
            dnr, dni, accr, acci = state
            r0 = pl.multiple_of((n_tiles - 1 - i) * ROWS, ROWS)
            rows = pl.ds(r0, ROWS)
            sr, si = gr_ref[rows, :], gi_ref[rows, :]
            for (qr, qi), k in doubling:
                mr, mi = _cmul(qr, qi, _shift_rows(sr, k, 0.0, up=True), _shift_rows(si, k, 0.0, up=True))
                sr, si = sr + mr, si + mi
            cr, ci = _cmul(pr, pi, dnr, dni)
            sr, si = sr + cr, si + ci
            dr_ref[rows, :] = sr
            di_ref[rows, :] = si
            xpr, xpi = _prev_rows(xr_ref, xr_ref[rows, :], r0), _prev_rows(xi_ref, xi_ref[rows, :], r0)
            return sr[:1], si[:1], accr + sr * xpr + si * xpi, acci + si * xpr - sr * xpi

        z, z8 = jnp.zeros((1, ct), F32), jnp.zeros((ROWS, ct), F32)
        _, _, accr, acci = lax.fori_loop(0, n_tiles, tile, (z, z, z8, z8))
        dar_ref[...] = jnp.sum(accr, axis=0, keepdims=True)
        dai_ref[...] = jnp.sum(acci, axis=0, keepdims=True)

    re = lambda j: (0, j)
    im = lambda j: (0, nt + j)
    blk = pl.BlockSpec((s, ct), re)
    return _call_with_exchange(
        body, nt, carry, [gr, gi, xr, xi, a, a],
        [blk, blk, blk, blk, pl.BlockSpec((1, ct), re), pl.BlockSpec((1, ct), im)],
        [blk, blk, pl.BlockSpec((1, ct), re), pl.BlockSpec((1, ct), re)],
        [jax.ShapeDtypeStruct((s, c), F32)] * 2 + [jax.ShapeDtypeStruct((1, c), F32)] * 2, [], name)


def _rscan_fwd(a, b, *, name, ct=256):
    s, c = a.shape

    def body(a_ref, b_ref, h_ref):
        def tile(i, h):
            rows = pl.ds(pl.multiple_of(i * ROWS, ROWS), ROWS)
            ca, cb = a_ref[rows, :], b_ref[rows, :]
            for k in (1, 2, 4):
                cb = cb + ca * _shift_rows(cb, k, 0.0)
                ca = ca * _shift_rows(ca, k, 1.0)
            out = cb + ca * h
            h_ref[rows, :] = out
            return out[ROWS - 1:]
        lax.fori_loop(0, s // ROWS, tile, jnp.zeros((1, ct), F32))

    blk = pl.BlockSpec((s, ct), lambda j: (0, j))
    return pl.pallas_call(body, grid=(c // ct,), in_specs=[blk, blk], out_specs=blk,
                          out_shape=jax.ShapeDtypeStruct((s, c), F32), compiler_params=_cparams("parallel"), name=name)(a, b)


def _rscan_bwd(g, a, h, *, name, ct=256):
    s, c = a.shape
    n_tiles = s // ROWS

    def body(g_ref, a_ref, h_ref, db_ref, da_ref):
        def tile(i, dn):
            r0 = pl.multiple_of((n_tiles - 1 - i) * ROWS, ROWS)
            rows = pl.ds(r0, ROWS)
            after = a_ref[pl.ds(jnp.minimum(r0 + ROWS, s - 1), 1), :] * (r0 + ROWS < s).astype(F32)
            ca = _shift_rows(a_ref[rows, :], 1, after, up=True)
            cb = g_ref[rows, :]
            for k in (1, 2, 4):
                cb = cb + ca * _shift_rows(cb, k, 0.0, up=True)
                ca = ca * _shift_rows(ca, k, 1.0, up=True)
            out = cb + ca * dn
            db_ref[rows, :] = out
            da_ref[rows, :] = out * _prev_rows(h_ref, h_ref[rows, :], r0)
            return out[:1]
        lax.fori_loop(0, n_tiles, tile, jnp.zeros((1, ct), F32))

    blk = pl.BlockSpec((s, ct), lambda j: (0, j))
    db, da = pl.pallas_call(body, grid=(c // ct,), in_specs=[blk, blk, blk], out_specs=[blk, blk],
                            out_shape=[jax.ShapeDtypeStruct((s, c), F32)] * 2, compiler_params=_cparams("parallel"), name=name)(g, a, h)
    return db, da


def _dot3l(m, x):
    hi = x.astype(BF16)
    r1 = x - hi.astype(F32)
    mid = r1.astype(BF16)
    lo = (r1 - mid.astype(F32)).astype(BF16)
    return (jnp.dot(m, hi, preferred_element_type=F32) + jnp.dot(m, mid, preferred_element_type=F32)
            + jnp.dot(m, lo, preferred_element_type=F32))


@jax.custom_vjp
def _linl(x, m, mt):
    return _dot3l(m, x)


def _linl_fwd(x, m, mt):
    return _dot3l(m, x), (m, mt)


def _linl_bwd(res, g):
    m, mt = res
    return _dot3l(mt, g), jnp.zeros_like(m), jnp.zeros_like(mt)


_linl.defvjp(_linl_fwd, _linl_bwd)


def _ssd_chunk(g, xs, bm, cm, z, dtraw, dt_bias, a_log, dskip, ng, st0, st1, st2):
    n = xs.shape[0]
    lane = lax.broadcasted_iota(jnp.int32, (1, LANES), 1)
    sub = lax.broadcasted_iota(jnp.int32, (LANES, 1), 0)
    row = lax.broadcasted_iota(jnp.int32, (n, n), 0)
    col = lax.broadcasted_iota(jnp.int32, (n, n), 1)
    tril = row >= col
    tril_m = tril.astype(BF16)
    triu_m = (row <= col).astype(BF16)
    lane_lo = lane < HEAD
    sub_lo = sub < HEAD
    dt = _softplus(dtraw + dt_bias)
    da = dt * (-jnp.exp(a_log))
    acum = _linl(da, tril_m, triu_m)
    acum_t = acum.T
    scores = lax.dot_general(cm.astype(BF16), bm.astype(BF16), (((1,), (1,)), ((), ())), preferred_element_type=F32)

    def head(h):
        sel = lane == h
        acol = jnp.sum(jnp.where(sel, acum, 0.0), axis=1, keepdims=True)
        arow = jnp.sum(jnp.where(sub == h, acum_t, 0.0), axis=0, keepdims=True)
        dtcol = jnp.sum(jnp.where(sel, dt, 0.0), axis=1, keepdims=True)
        dsk = jnp.sum(jnp.where(sel, dskip, 0.0), axis=1, keepdims=True)
        decay = jnp.exp(jnp.where(tril, acol - arow, -jnp.inf))
        alast = acol[n - 1:n, :]
        return acol, dtcol, dsk, decay, alast

    ys, new = [], []
    for q, st in enumerate((st0, st1, st2)):
        a_acol, a_dt, a_dsk, a_decay, a_last = head(g * 6 + 2 * q)
        b_acol, b_dt, b_dsk, b_decay, b_last = head(g * 6 + 2 * q + 1)
        xp = xs[:, q * LANES:(q + 1) * LANES]
        xdt = xp * jnp.where(lane_lo, a_dt, b_dt)
        yd = jnp.where(lane_lo, _dot_bf16(scores * a_decay, xdt), _dot_bf16(scores * b_decay, xdt))
        xw = xdt * jnp.where(lane_lo, jnp.exp(a_last - a_acol), jnp.exp(b_last - b_acol))
        states = lax.dot_general(xw.astype(BF16), bm.astype(BF16), (((0,), (0,)), ((), ())), preferred_element_type=F32)
        yo = lax.dot_general(cm.astype(BF16), st.astype(BF16), (((1,), (1,)), ((), ())), preferred_element_type=F32)
        yo = yo * jnp.where(lane_lo, jnp.exp(a_acol), jnp.exp(b_acol))
        new.append(st * jnp.where(sub_lo, jnp.exp(a_last), jnp.exp(b_last)) + states)
        ys.append(yd + yo + xp * jnp.where(lane_lo, a_dsk, b_dsk))
    y = jnp.concatenate(ys, axis=1)
    y = y * (z * jax.nn.sigmoid(z))
    y = y * lax.rsqrt(jnp.mean(y * y, axis=-1, keepdims=True) + NORM_EPS) * ng
    return y, new[0], new[1], new[2]


def _ssd_specs(nc, rev):
    cidx = (lambda c: nc - 1 - c) if rev else (lambda c: c)
    gw = 3 * LANES
    return [
        pl.BlockSpec((SSD_CHUNK, gw), lambda c, g: (cidx(c), g)),
        pl.BlockSpec((SSD_CHUNK, LANES), lambda c, g: (cidx(c), 12 + g)),
        pl.BlockSpec((SSD_CHUNK, LANES), lambda c, g: (cidx(c), 16 + g)),
        pl.BlockSpec((SSD_CHUNK, gw), lambda c, g: (cidx(c), g)),
        pl.BlockSpec((SSD_CHUNK, LANES), lambda c, g: (cidx(c), 36)),
        pl.BlockSpec((1, LANES), lambda c, g: (0, 0)),
        pl.BlockSpec((1, LANES), lambda c, g: (0, 0)),
        pl.BlockSpec((1, LANES), lambda c, g: (0, 0)),
        pl.BlockSpec((1, gw), lambda c, g: (0, g)),
    ], cidx


def _ssd_fwd(conv, proj, dt_bias, a_log, dskip, norm_g, *, name, carry=None):
    s = conv.shape[0]
    nc = s // SSD_CHUNK
    in_specs, _ = _ssd_specs(nc, False)

    def body(xs, bm, cm, z, dtr, dtb, alog, dsk, ng, y_ref, sv_ref, st):
        c, g = pl.program_id(0), pl.program_id(1)

        @pl.when(c == 0)
        def _():
            for q in range(3):
                st[g * 3 + q] = jnp.zeros((LANES, LANES), F32)

        olds = [st[g * 3 + q] for q in range(3)]
        for q in range(3):
            sv_ref[0, 0, q] = olds[q]
        y, n0, n1, n2 = _ssd_chunk(g, xs[...], bm[...], cm[...], z[...], dtr[...], dtb[...], alog[...], dsk[...], ng[...], *olds)
        y_ref[...] = y
        for q, v in enumerate((n0, n1, n2)):
            st[g * 3 + q] = v

    return _call_with_exchange(
        body, (nc, 4), carry, [conv, conv, conv, proj, proj, dt_bias, a_log, dskip, norm_g], in_specs,
        [pl.BlockSpec((SSD_CHUNK, 3 * LANES), lambda c, g: (c, g)), pl.BlockSpec((1, 1, 3, LANES, LANES), lambda c, g: (c, g, 0, 0, 0))],
        [jax.ShapeDtypeStruct((s, 12 * LANES), F32), jax.ShapeDtypeStruct((nc, 4, 3, LANES, LANES), F32)],
        [pltpu.VMEM((12, LANES, LANES), F32)], name)


def _ssd_bwd(conv, proj, dt_bias, a_log, dskip, norm_g, saved, dy, *, name, carry=None):
    s = conv.shape[0]
    nc = s // SSD_CHUNK
    in_specs, cidx = _ssd_specs(nc, True)
    gw = 3 * LANES
    in_specs += [pl.BlockSpec((1, 1, 3, LANES, LANES), lambda c, g: (cidx(c), g, 0, 0, 0)),
                 pl.BlockSpec((SSD_CHUNK, gw), lambda c, g: (cidx(c), g))]

    def body(xs, bm, cm, z, dtr, dtb, alog, dsk, ng, sv, dy_ref, dxs, dbm, dcm, dz, ddt, ddtb, dalog, ddsk, dng, dst):
        c, g = pl.program_id(0), pl.program_id(1)

        @pl.when(c == 0)
        def _():
            for q in range(3):
                dst[g * 3 + q] = jnp.zeros((LANES, LANES), F32)

        @pl.when((c == 0) & (g == 0))
        def _():
            ddtb[...] = jnp.zeros_like(ddtb)
            dalog[...] = jnp.zeros_like(dalog)
            ddsk[...] = jnp.zeros_like(ddsk)
            dng[...] = jnp.zeros_like(dng)

        @pl.when(g == 0)
        def _():
            ddt[...] = jnp.zeros_like(ddt)

        olds = [sv[0, 0, q] for q in range(3)]
        _, vjp = jax.vjp(functools.partial(_ssd_chunk, g), xs[...], bm[...], cm[...], z[...], dtr[...], dtb[...], alog[...],
                         dsk[...], ng[...], *olds)
        gr = vjp((dy_ref[...], dst[g * 3], dst[g * 3 + 1], dst[g * 3 + 2]))
        dxs[...], dbm[...], dcm[...], dz[...] = gr[0], gr[1], gr[2], gr[3]
        ddt[...] += gr[4]
        ddtb[...] += gr[5]
        dalog[...] += gr[6]
        ddsk[...] += gr[7]
        dng[g] += gr[8]
        for q in range(3):
            dst[g * 3 + q] = gr[9 + q]

    const = lambda shape: pl.BlockSpec(shape, lambda c, g: (0,) * len(shape))
    return _call_with_exchange(
        body, (nc, 4), carry, [conv, conv, conv, proj, proj, dt_bias, a_log, dskip, norm_g, saved, dy], in_specs,
        [pl.BlockSpec((SSD_CHUNK, gw), lambda c, g: (cidx(c), g)),
         pl.BlockSpec((SSD_CHUNK, LANES), lambda c, g: (cidx(c), g)),
         pl.BlockSpec((SSD_CHUNK, LANES), lambda c, g: (cidx(c), g)),
         pl.BlockSpec((SSD_CHUNK, gw), lambda c, g: (cidx(c), g)),
         pl.BlockSpec((SSD_CHUNK, LANES), lambda c, g: (cidx(c), 0)),
         const((1, LANES)), const((1, LANES)), const((1, LANES)), const((4, 1, gw))],
        [jax.ShapeDtypeStruct((s, 12 * LANES), F32), jax.ShapeDtypeStruct((s, 4 * LANES), F32),
         jax.ShapeDtypeStruct((s, 4 * LANES), F32), jax.ShapeDtypeStruct((s, 12 * LANES), F32),
         jax.ShapeDtypeStruct((s, LANES), F32), jax.ShapeDtypeStruct((1, LANES), F32),
         jax.ShapeDtypeStruct((1, LANES), F32), jax.ShapeDtypeStruct((1, LANES), F32),
         jax.ShapeDtypeStruct((4, 1, gw), F32)],
        [pltpu.VMEM((12, LANES, LANES), F32)], name)


MXU_TILE = 256
RWKV_GROUP = 8


def _rwkv_consts():
    lanes = 16 * HEAD
    hl = jnp.arange(lanes) // HEAD
    e = (jnp.arange(16)[:, None] == hl[None, :]).astype(BF16)
    return e, e.T, _head_ones(MXU_TILE)


def _head_sums(x, j):
    n = x.shape[0]
    x4 = jnp.concatenate([x[:, i * MXU_TILE:(i + 1) * MXU_TILE] for i in range(4)], axis=0)
    s4 = jnp.dot(x4.astype(BF16), j, preferred_element_type=F32)
    return jnp.concatenate([s4[i * n:(i + 1) * n] for i in range(4)], axis=1)


def _fold8(x):
    return jnp.sum(x.reshape(x.shape[0] // 8, 8, x.shape[1]), axis=0)


def _rwkv_expand(src3, dst, e, t_):
    for g0 in range(0, t_, RWKV_GROUP):
        n = min(RWKV_GROUP, t_ - g0)
        flat = src3[g0:g0 + n].reshape(n * HEAD, e.shape[0])
        dst[g0:g0 + n] = jnp.dot(flat, e, preferred_element_type=F32).reshape(n, HEAD, e.shape[1])


def _rwkv_reduce(src, dst3, et, t_):
    for g0 in range(0, t_, RWKV_GROUP):
        n = min(RWKV_GROUP, t_ - g0)
        x = src[g0:g0 + n].reshape(n * HEAD, et.shape[0])
        dst3[g0:g0 + n] = jnp.dot(x.astype(BF16), et, preferred_element_type=F32).reshape(n, HEAD, 16)


def _rwkv_fwd(w, kk, b, k, fp, v3, *, name, carry=None):
    s, lanes = w.shape
    t_ = min(RWKV_CHUNK, s)
    nc = s // t_
    e, et, j = _rwkv_consts()
    rowspec = pl.BlockSpec((t_, lanes), lambda c: (c, 0))
    cspec = lambda a: pl.BlockSpec(a.shape, lambda c: (0, 0))

    def body(w_ref, kk_ref, b_ref, k_ref, r_ref, v_ref, e_ref, et_ref, j_ref, y_ref, sv_ref, st, vm, zz):
        c = pl.program_id(0)

        @pl.when(c == 0)
        def _():
            st[...] = jnp.zeros_like(st)

        sv_ref[0] = st[...]
        jv = j_ref[...]
        _rwkv_expand(v_ref, vm, e_ref[...], t_)

        def step(t, sm):
            row = lambda ref: ref[pl.ds(t, 1), :]
            sa = _head_sums(sm * (-row(kk_ref)), jv)
            sn = sm * row(w_ref) + sa * row(b_ref) + vm[t] * row(k_ref)
            zz[t] = sn * row(r_ref)
            return sn

        st[...] = lax.fori_loop(0, t_, step, st[...])
        _rwkv_reduce(zz, y_ref, et_ref[...], t_)

    return _call_with_exchange(
        body, nc, carry, [w, kk, b, k, fp, v3, e, et, j],
        [rowspec] * 5 + [pl.BlockSpec((t_, HEAD, 16), lambda c: (c, 0, 0)), cspec(e), cspec(et), cspec(j)],
        [pl.BlockSpec((t_, HEAD, 16), lambda c: (c, 0, 0)), pl.BlockSpec((1, HEAD, lanes), lambda c: (c, 0, 0))],
        [jax.ShapeDtypeStruct((s, HEAD, 16), F32), jax.ShapeDtypeStruct((nc, HEAD, lanes), F32)],
        [pltpu.VMEM((HEAD, lanes), F32), pltpu.VMEM((t_, HEAD, lanes), F32), pltpu.VMEM((t_, HEAD, lanes), F32)], name)


def _call_with_exchange(body, grid, carry, ins, in_specs, out_specs, out_shape, scratch, name):
    grid = (grid,) if isinstance(grid, int) else tuple(grid)
    if carry is not None:
        src, gather = carry
        n_in, n_out, inner = len(ins), len(out_shape), body

        def body(*refs):
            ex = _make_exchange(refs[n_in], refs[n_in + 1 + n_out], refs[-3:], gather)
            steps = [pl.program_id(d) for d in range(len(grid))]

            @pl.when(functools.reduce(jnp.logical_and, [s == 0 for s in steps]))
            def _():
                ex.start()

            inner(*refs[:n_in], *refs[n_in + 1:n_in + 1 + n_out], *refs[n_in + 2 + n_out:-3])

            @pl.when(functools.reduce(jnp.logical_and, [s == n - 1 for s, n in zip(steps, grid)]))
            def _():
                ex.wait()

        hbm = pl.BlockSpec(memory_space=pl.ANY)
        ins, in_specs, out_specs = list(ins) + [src], list(in_specs) + [hbm], list(out_specs) + [hbm]
        out_shape = list(out_shape) + [jax.ShapeDtypeStruct((N_DEV, src.shape[-2], LANES), src.dtype)]
        scratch = list(scratch) + list(_Exchange.SCRATCH)
    return pl.pallas_call(body, grid=grid, in_specs=in_specs, out_specs=out_specs, out_shape=out_shape,
                          scratch_shapes=scratch, compiler_params=_cparams(*["arbitrary"] * len(grid)), name=name)(*ins)


def _rwkv_bwd(w, kk, b, k, fp, v3, saved, dy3, *, name, carry=None):
    s, lanes = w.shape
    t_ = min(RWKV_CHUNK, s)
    nc = s // t_
    e, et, j = _rwkv_consts()
    rev = lambda c: nc - 1 - c
    rowspec = pl.BlockSpec((t_, lanes), lambda c: (rev(c), 0))
    v3spec = pl.BlockSpec((t_, HEAD, 16), lambda c: (rev(c), 0, 0))
    s3spec = v3spec
    cspec = lambda a: pl.BlockSpec(a.shape, lambda c: (0, 0))

    def body(w_ref, kk_ref, b_ref, k_ref, r_ref, v_ref, sv_ref, dy_ref, e_ref, et_ref, j_ref,
             dw_ref, dkk_ref, db_ref, dk_ref, dr_ref, dv_ref, dst, h_sm, h_sa, vm, dz, pw, pkk, pb, pk, pr):
        c = pl.program_id(0)

        @pl.when(c == 0)
        def _():
            dst[...] = jnp.zeros_like(dst)

        jv = j_ref[...]
        row = lambda ref, t: ref[pl.ds(t, 1), :]
        _rwkv_expand(v_ref, vm, e_ref[...], t_)
        _rwkv_expand(dy_ref, dz, e_ref[...], t_)

        def replay(t, sm):
            h_sm[t] = sm
            sa = _head_sums(sm * (-row(kk_ref, t)), jv)
            h_sa[t] = sa
            return sm * row(w_ref, t) + sa * row(b_ref, t) + vm[t] * row(k_ref, t)

        h_sm[t_] = lax.fori_loop(0, t_, replay, sv_ref[0])

        def back(i, dcarry):
            t = t_ - 1 - i
            sm, sa, dzt = h_sm[t], h_sa[t], dz[t]
            dsn = dcarry + dzt * row(r_ref, t)
            pr[t] = _fold8(dzt * h_sm[t + 1])
            pw[t] = _fold8(dsn * sm)
            pb[t] = _fold8(dsn * sa)
            pk[t] = _fold8(dsn * vm[t])
            vm[t] = dsn * row(k_ref, t)
            dx = _head_sums(dsn * row(b_ref, t), jv)
            pkk[t] = _fold8(dx * sm)
            return dsn * row(w_ref, t) - dx * row(kk_ref, t)

        dst[...] = lax.fori_loop(0, t_, back, dst[...])
        _rwkv_reduce(vm, dv_ref, et_ref[...], t_)
        dw_ref[...] = jnp.sum(pw[...], axis=1)
        dkk_ref[...] = -jnp.sum(pkk[...], axis=1)
        db_ref[...] = jnp.sum(pb[...], axis=1)
        dk_ref[...] = jnp.sum(pk[...], axis=1)
        dr_ref[...] = jnp.sum(pr[...], axis=1)

    big = lambda n: pltpu.VMEM((n, HEAD, lanes), F32)
    part = pltpu.VMEM((t_, 8, lanes), F32)
    return _call_with_exchange(
        body, nc, carry, [w, kk, b, k, fp, v3, saved, dy3, e, et, j],
        [rowspec] * 5 + [s3spec, pl.BlockSpec((1, HEAD, lanes), lambda c: (rev(c), 0, 0)), s3spec, cspec(e), cspec(et), cspec(j)],
        [rowspec] * 5 + [v3spec],
        [jax.ShapeDtypeStruct((s, lanes), F32)] * 5 + [jax.ShapeDtypeStruct((s, HEAD, 16), F32)],
        [pltpu.VMEM((HEAD, lanes), F32), big(t_ + 1), big(t_), big(t_), big(t_)] + [part] * 5, name)


def _blockdiag(blocks):
    g, a, b = blocks.shape
    on_diag = (jnp.arange(g)[:, None, None, None] == jnp.arange(g)[None, None, :, None])
    return jnp.where(on_diag, blocks[:, :, None, :], 0).reshape(g * a, g * b)


def _blockdiag_t(dense, g):
    a, b = dense.shape[0] // g, dense.shape[1] // g
    on_diag = (jnp.arange(g)[:, None, None, None] == jnp.arange(g)[None, None, :, None])
    return jnp.sum(jnp.where(on_diag, dense.reshape(g, a, g, b), 0), axis=2)


def _pad_cols(x, n):
    return jnp.pad(x, ((0, 0), (0, n - x.shape[1])))


def _pad_rows(x, n):
    return jnp.pad(x, ((0, n - x.shape[0]), (0, 0)))


E_PROJ = 5120


def _even_in_cols(w):
    return jnp.concatenate([w[:, 512:2048], w[:, 0:512], w[:, 2048:4632], jnp.zeros((w.shape[0], E_PROJ - 4632), w.dtype)], axis=1)


def _even_in_cols_t(dw):
    return jnp.concatenate([dw[:, 1536:2048], dw[:, 0:1536], dw[:, 2048:4632]], axis=1)


O_PROJ = 5632


def _odd_in_cols(w):
    z32 = jnp.zeros((w.shape[0], 32), w.dtype)
    return jnp.concatenate([w[:, 0:3072], w[:, 3520:5568], w[:, 3264:3520], w[:, 3072:3168], z32, w[:, 3168:3264], z32], axis=1)


def _odd_in_cols_t(dw):
    return jnp.concatenate([dw[:, 0:3072], dw[:, 5376:5472], dw[:, 5504:5600], dw[:, 5120:5376], dw[:, 3072:5120]], axis=1)


def _mu_cols(mu):
    z32 = jnp.zeros((1, 32), mu.dtype)
    return jnp.concatenate([mu[:, 0:3072], mu[:, 3264:3520], mu[:, 3072:3168], z32, mu[:, 3168:3264], z32], axis=1)


def _mu_cols_t(d):
    return jnp.concatenate([d[:, 0:3072], d[:, 3328:3424], d[:, 3456:3552], d[:, 3072:3328]], axis=1)


def _conv_taps(x, w, b):
    y = b + w[3:4] * x
    for k in range(3):
        y = y + w[k:k + 1] * _shift_down(x, 3 - k)
    return y


def _conv_silu(x, w, b):
    y = _conv_taps(x, w, b)
    return y * jax.nn.sigmoid(y)


def _tshift(x, mu):
    return x + (_shift_down(x, 1) - x) * mu


def _pl_gate(_, h, gl, e):
    return (h + jax.nn.sigmoid(gl) * e,)


def _s5_param(lr, li, ls, br, bi):
    step = jnp.exp(ls)
    mag = jnp.exp(lr * step)
    ar, ai = mag * jnp.cos(li * step), mag * jnp.sin(li * step)
    den = lr * lr + li * li
    nr = ar - 1.0
    cr = (nr * lr + ai * li) / den
    ci = (ai * lr - nr * li) / den
    return ar, ai, cr * br - ci * bi, cr * bi + ci * br


def _s5_post(_, ylin, u, d, gw, gb):
    act = jax.nn.gelu(ylin + d * u)
    return (act * jax.nn.sigmoid(_dot_bf16(act, gw) + gb),)


def _rwkv_pre(_, k, gl, wl, al, w0, w_up, a0, a_up, g_up, k_k, k_a, j):
    w = -_softplus(-(w0 + _dot_bf16(jnp.tanh(wl), w_up))) - 0.5
    decay = jnp.exp(-jnp.exp(w))
    a = jax.nn.sigmoid(a0 + _dot_bf16(al, a_up))
    g = _dot_bf16(jax.nn.sigmoid(gl), g_up)
    kk = k * k_k
    k2 = k * (1.0 + (a - 1.0) * k_a)
    kkn = kk * lax.rsqrt(jnp.maximum(_lin(kk * kk, j, j), 1e-24))
    return decay, kkn, kkn * a, k2, g


def _rwkv_post(_, y, r, k2, v, g, r_k, ln_g, ln_b, j):
    mean = _lin(y, j, j) * (1.0 / HEAD)
    yc = y - mean
    var = _lin(yc * yc, j, j) * (1.0 / HEAD)
    yn = yc * lax.rsqrt(var + RWKV_GN_EPS) * ln_g + ln_b
    return ((yn + _lin(r * k2 * r_k, j, j) * v) * g,)


def _lru_pre(row0, pre, xc, bax, lam):
    n = xc.shape[1]
    gr = jax.nn.sigmoid(pre[:, :n] + bax[:, :n])
    gi = jax.nn.sigmoid(pre[:, n:] + bax[:, n:])
    log_a = -LRU_C * gr * _softplus(-lam)
    m2 = -jnp.tanh(log_a) * (jnp.exp(2.0 * log_a) + 1.0)
    mult = jnp.sqrt(jnp.maximum(m2, 0.0))
    rowid = row0 + lax.broadcasted_iota(jnp.int32, (xc.shape[0], 1), 0)
    mult = jnp.where(rowid == 0, 1.0, mult)
    return jnp.exp(log_a), xc * gi * mult


def _lru_post(_, h, gl2):
    return (h * jax.nn.gelu(gl2),)


def _even_prep(w):
    sp = (w["s5_lam_re"].reshape(32, 64), w["s5_lam_im"].reshape(32, 64), w["s5_log_step"].reshape(32, 1),
          w["s5_b_re"].reshape(32, 64, 16).transpose(2, 0, 1), w["s5_b_im"].reshape(32, 64, 16).transpose(2, 0, 1))
    ar, ai, bbr, bbi = _block_vjp(_s5_param, sp, None, name="s5_param")
    bblk = lambda bb: _blockdiag(bb.transpose(1, 0, 2))
    cblk = lambda c: _blockdiag(c.reshape(32, 16, 64).transpose(0, 2, 1))
    pad = lambda x: _pad_cols(x.reshape(1, 24), LANES)
    return dict(
        sp=sp, a_row=jnp.concatenate([ar.reshape(1, 2048), ai.reshape(1, 2048)], axis=1),
        b_re=bblk(bbr), b_im=bblk(bbi), c_re=cblk(w["s5_c_re"]), c_imn=-cblk(w["s5_c_im"]),
        d=w["s5_d"].reshape(1, 512), gw=w["s5_glu_w"].reshape(512, 512), gb=w["s5_glu_b"].reshape(1, 512),
        conv_w=w["ssd_conv_w"].reshape(4, 2560), conv_b=w["ssd_conv_b"].reshape(1, 2560),
        dt_bias=pad(w["ssd_dt_bias"]), a_log=pad(w["ssd_a_log"]), dskip=pad(w["ssd_d"]), norm=w["ssd_norm"].reshape(1, 1536))


_E_XMAP = lambda j: 16 + j


def _even_fwd(proj, p, carry=None):
    u = proj[:, 1536:2048]
    bur = _matmul(u, p["b_re"], "nn", name="s5_bu_re")
    bui = _matmul(u, p["b_im"], "nn", name="s5_bu_im")
    xr, xi = _cscan_fwd(jnp.concatenate([bur, bui], axis=1), p["a_row"], name="s5_scan")
    ylin = _matmul(xi, p["c_imn"], "nn", name="s5_y_im", add=_matmul(xr, p["c_re"], "nn", name="s5_y_re"))
    (ya,) = _rw_fwd(_s5_post, [_row(ylin), _row(u)], [p["d"], p["gw"], p["gb"]], [(512, F32)], name="s5_post")
    conv = _ct_fwd(_conv_silu, proj, _E_XMAP, 20, [(p["conv_w"], 0), (p["conv_b"], 0)], 2560, name="ssd_conv")
    yb, saved, *carried = _ssd_fwd(conv, proj, p["dt_bias"], p["a_log"], p["dskip"], p["norm"], name="ssd_scan", carry=carry)
    return jnp.concatenate([ya, yb], axis=1), (u, xr, xi, ylin, conv, saved), (carried[0] if carried else None)


def _even_bwd(proj, p, res, dy, carry=None):
    u, xr, xi, ylin, conv, saved = res
    s = proj.shape[0]
    dxs, dbm, dcm, dz, ddt, ddtb, dalog, ddsk, dng, *carried = _ssd_bwd(
        conv, proj, p["dt_bias"], p["a_log"], p["dskip"], p["norm"], saved, dy[:, 512:], name="ssd_scan_bwd", carry=carry)
    dxbc, (dcw, dcb) = _ct_bwd(_conv_silu, proj, _E_XMAP, 20, [(p["conv_w"], 0), (p["conv_b"], 0)],
                               [jnp.concatenate([dxs, dbm, dcm], axis=1)], name="ssd_conv_bwd")
    (dylin, du), (dd, dgw, dgb) = _rw_bwd(_s5_post, [_row(ylin), _row(u)], [p["d"], p["gw"], p["gb"]], [dy[:, :512]], name="s5_post_bwd")
    dxr = _matmul(dylin, p["c_re"], "nt", name="s5_dxr")
    dxi = _matmul(dylin, p["c_imn"], "nt", name="s5_dxi")
    dc_re = _matmul(xr, dylin, "tn", name="s5_dc_re")
    dc_imn = _matmul(xi, dylin, "tn", name="s5_dc_im")
    dbr, dbi, dar, dai = _cscan_bwd(dxr, dxi, xr, xi, p["a_row"], name="s5_scan_bwd")
    du = _matmul(dbr, p["b_re"], "nt", name="s5_du_re", add=du)
    du = _matmul(dbi, p["b_im"], "nt", name="s5_du_im", add=du)
    db_re = _matmul(u, dbr, "tn", name="s5_db_re")
    db_im = _matmul(u, dbi, "tn", name="s5_db_im")
    unblk = lambda d: _blockdiag_t(d, 32).transpose(1, 0, 2)
    g_sp = _block_vjp(_s5_param, p["sp"], (dar.reshape(32, 64), dai.reshape(32, 64), unblk(db_re), unblk(db_im)), name="s5_param_bwd")
    dproj = jnp.concatenate([dz, du, dxbc, ddt, jnp.zeros((s, E_PROJ - 4736), F32)], axis=1)
    uncblk = lambda d: _blockdiag_t(d, 32).transpose(0, 2, 1)
    grads = dict(
        s5_lam_re=g_sp[0].reshape(1, 32, 64), s5_lam_im=g_sp[1].reshape(1, 32, 64), s5_log_step=g_sp[2].reshape(1, 32),
        s5_b_re=g_sp[3].transpose(1, 2, 0)[None], s5_b_im=g_sp[4].transpose(1, 2, 0)[None],
        s5_c_re=uncblk(dc_re)[None], s5_c_im=-uncblk(dc_imn)[None], s5_d=dd, s5_glu_w=dgw[None], s5_glu_b=dgb,
        ssd_conv_w=dcw[None], ssd_conv_b=dcb, ssd_dt_bias=ddtb[:, :24], ssd_a_log=dalog[:, :24], ssd_d=ddsk[:, :24],
        ssd_norm=dng.reshape(1, 1536))
    return dproj, grads, (carried[0] if carried else None)


def _odd_prep(w):
    pad128 = lambda x: _pad_rows(x, LANES)
    return dict(
        mu=_mu_cols(w["rwkv_mu"].reshape(1, 3520)), w0=w["rwkv_w0"].reshape(1, 1024), w_up=pad128(w["rwkv_w_up"].reshape(96, 1024)),
        a0=w["rwkv_a0"].reshape(1, 1024), a_up=pad128(w["rwkv_a_up"].reshape(96, 1024)), g_up=w["rwkv_g_up"].reshape(256, 1024),
        k_k=w["rwkv_k_k"].reshape(1, 1024), k_a=w["rwkv_k_a"].reshape(1, 1024), r_k=w["rwkv_r_k"].reshape(1, 1024),
        ln_g=w["rwkv_ln_g"].reshape(1, 1024), ln_b=w["rwkv_ln_b"].reshape(1, 1024), j=_head_ones(1024),
        conv_w=w["lru_conv_w"].reshape(4, 1024), conv_b=w["lru_conv_b"].reshape(1, 1024),
        wax=jnp.concatenate([_blockdiag(w["lru_w_a"].reshape(16, 64, 64)), _blockdiag(w["lru_w_x"].reshape(16, 64, 64))], axis=1),
        bax=jnp.concatenate([w["lru_b_a"].reshape(1, 1024), w["lru_b_x"].reshape(1, 1024)], axis=1), lam=w["lru_lam"].reshape(1, 1024))


_O_XMAP = lambda j: jnp.where(j < 24, j, j + 16)
_O_LMAP = lambda j: 24 + j


def _to_heads(x):
    return x.reshape(x.shape[0], 16, HEAD).transpose(0, 2, 1)


def _from_heads(x3):
    return x3.transpose(0, 2, 1).reshape(x3.shape[0], 16 * HEAD)


def _odd_rows(fp, y, k2, g):
    pre = [_row(fp, 1024, 1), _row(fp, 256, 12), _row(fp, 128, 26), _row(fp, 128, 27)]
    post = None if y is None else [_row(y), _row(fp, 1024, 0), _row(k2), _row(fp, 1024, 2), _row(g)]
    return pre, post


def _odd_fwd(proj, p, carry=None):
    fp = _ct_fwd(_tshift, proj, _O_XMAP, 28, [(p["mu"], 0)], 3584, name="rwkv_shift")
    pre_rows, _ = _odd_rows(fp, None, None, None)
    pre_params = [p["w0"], p["w_up"], p["a0"], p["a_up"], p["g_up"], p["k_k"], p["k_a"], p["j"]]
    decay, kkn, b, k2, g = _rw_fwd(_rwkv_pre, pre_rows, pre_params, [(1024, F32)] * 5, name="rwkv_pre")
    v3 = _to_heads(fp[:, 2048:3072]).astype(BF16)
    y3, saved, *carried = _rwkv_fwd(decay, kkn, b, k2, fp, v3, name="rwkv_scan", carry=carry)
    y = _from_heads(y3)
    _, post_rows = _odd_rows(fp, y, k2, g)
    (yc,) = _rw_fwd(_rwkv_post, post_rows, [p["r_k"], p["ln_g"], p["ln_b"], p["j"]], [(1024, F32)], name="rwkv_post")
    xc = _ct_fwd(_conv_taps, proj, _O_LMAP, 8, [(p["conv_w"], 0), (p["conv_b"], 0)], 1024, name="lru_conv")
    pre = _matmul(xc, p["wax"], "nn", name="lru_gates")
    a, bx = _rw_fwd(_lru_pre, [_row(pre), _row(xc)], [p["bax"], p["lam"]], [(1024, F32)] * 2, name="lru_pre")
    hseq = _rscan_fwd(a, bx, name="lru_scan")
    (yd,) = _rw_fwd(_lru_post, [_row(hseq), _row(proj, 1024, 4)], [], [(1024, F32)], name="lru_post")
    return jnp.concatenate([yc, yd], axis=1), (fp, decay, kkn, b, k2, g, v3, saved, y, xc, pre, a, hseq), (carried[0] if carried else None)


def _odd_bwd(proj, p, res, dy, carry=None):
    fp, decay, kkn, b, k2, g, v3, saved, y, xc, pre, a, hseq = res
    s = proj.shape[0]
    (dh, dgl2), _ = _rw_bwd(_lru_post, [_row(hseq), _row(proj, 1024, 4)], [], [dy[:, 1024:]], name="lru_post_bwd")
    dbx, da = _rscan_bwd(dh, a, hseq, name="lru_scan_bwd")
    (dpre, dxc), (dbax, dlam) = _rw_bwd(_lru_pre, [_row(pre), _row(xc)], [p["bax"], p["lam"]], [da, dbx], name="lru_pre_bwd")
    dxc = _matmul(dpre, p["wax"], "nt", name="lru_gates_dx", add=dxc)
    dwax = _matmul(xc, dpre, "tn", name="lru_gates_dw")
    dxl, (dlcw, dlcb) = _ct_bwd(_conv_taps, proj, _O_LMAP, 8, [(p["conv_w"], 0), (p["conv_b"], 0)], [dxc], name="lru_conv_bwd")
    pre_rows, post_rows = _odd_rows(fp, y, k2, g)
    (dyn, dr1, dk2a, dv1, dg), (dr_k, dln_g, dln_b) = _rw_bwd(
        _rwkv_post, post_rows, [p["r_k"], p["ln_g"], p["ln_b"], p["j"]], [dy[:, :1024]], name="rwkv_post_bwd", param_grads=[0, 1, 2])
    ddecay, dkkn, db, dk2b, dr2, dv3, *carried = _rwkv_bwd(
        decay, kkn, b, k2, fp, v3, saved, _to_heads(dyn).astype(BF16), name="rwkv_scan_bwd", carry=carry)
    pre_params = [p["w0"], p["w_up"], p["a0"], p["a_up"], p["g_up"], p["k_k"], p["k_a"], p["j"]]
    (dk, dgl, dwl, dal), (dw0, dw_up, da0, da_up, dg_up, dk_k, dk_a) = _rw_bwd(
        _rwkv_pre, pre_rows, pre_params, [ddecay, dkkn, db, [dk2a, dk2b], dg], name="rwkv_pre_bwd", param_grads=list(range(7)))
    z = lambda n: jnp.zeros((s, n), F32)
    g1 = jnp.concatenate([dr1, dk, dv1, dgl, dwl, dal], axis=1)
    g2 = jnp.concatenate([dr2, z(1024), _from_heads(dv3), z(512)], axis=1)
    dfp, (dmu,) = _ct_bwd(_tshift, proj, _O_XMAP, 28, [(p["mu"], 0)], [g1, g2], name="rwkv_shift_bwd")
    dproj = jnp.concatenate([dfp[:, :3072], dxl, dgl2, dfp[:, 3072:]], axis=1)
    grads = dict(
        rwkv_mu=_mu_cols_t(dmu), rwkv_w0=dw0, rwkv_w_up=dw_up[:96][None], rwkv_a0=da0, rwkv_a_up=da_up[:96][None], rwkv_g_up=dg_up[None],
        rwkv_k_k=dk_k, rwkv_k_a=dk_a, rwkv_r_k=dr_k.reshape(1, 16, 64), rwkv_ln_g=dln_g, rwkv_ln_b=dln_b,
        lru_conv_w=dlcw[None], lru_conv_b=dlcb, lru_w_a=_blockdiag_t(dwax[:, :1024], 16)[None], lru_w_x=_blockdiag_t(dwax[:, 1024:], 16)[None],
        lru_b_a=dbax[:, :1024].reshape(1, 16, 64), lru_b_x=dbax[:, 1024:].reshape(1, 16, 64), lru_lam=dlam.reshape(1, 16, 64))
    return dproj, grads, (carried[0] if carried else None)


def _my_index():
    return 4 * lax.axis_index("x") + 2 * lax.axis_index("y") + lax.axis_index("c")


def _peer(k):
    x, y, c = lax.axis_index("x"), lax.axis_index("y"), lax.axis_index("c")
    return (1 - x if k & 4 else x, 1 - y if k & 2 else y, 1 - c if k & 1 else c)


class _Exchange:
    SCRATCH = (pltpu.SemaphoreType.DMA((N_DEV - 1,)), pltpu.SemaphoreType.DMA((N_DEV - 1,)), pltpu.SemaphoreType.DMA)

    def __init__(self, src_ref, out_ref, send_sems, recv_sems, local_sem, gather):
        me = _my_index()
        mine = src_ref if gather else src_ref.at[me]
        self.local = pltpu.make_async_copy(mine, out_ref.at[me], local_sem)
        rdma = lambda src, dst, k: pltpu.make_async_remote_copy(
            src_ref=src, dst_ref=dst, send_sem=send_sems.at[k - 1], recv_sem=recv_sems.at[k - 1],
            device_id=_peer(k), device_id_type=pl.DeviceIdType.MESH)
        ks = range(1, N_DEV)
        self.sends = [rdma(src_ref if gather else src_ref.at[jnp.bitwise_xor(me, k)], out_ref.at[me], k) for k in ks]
        self.arrivals = [rdma(mine, out_ref.at[jnp.bitwise_xor(me, k)], k) for k in ks]

    def start(self):
        self.local.start()
        for cp in self.sends:
            cp.start()

    def wait(self):
        for cp in self.arrivals:
            cp.wait_recv()
        for cp in self.sends:
            cp.wait_send()
        self.local.wait()


class _TwoLevelGather:
    def __init__(self, src_ref, out_ref, send_sems, recv_sems, local_sem):
        x, y, c = lax.axis_index("x"), lax.axis_index("y"), lax.axis_index("c")
        me, self.sibling, self.c = (x, y, c), (x, y, 1 - c), c
        self.me = me
        self.chips = [(1 - x, y), (x, 1 - y), (1 - x, 1 - y)]
        slab = lambda px, py, pc: out_ref.at[4 * px + 2 * py + pc]

        def copy(k, block, to, own=False):
            return pltpu.make_async_remote_copy(
                src_ref=src_ref if own else slab(*block), dst_ref=slab(*block), send_sem=send_sems.at[k], recv_sem=recv_sems.at[k],
                device_id=to, device_id_type=pl.DeviceIdType.MESH)

        self.copy = copy
        self.mine = pltpu.make_async_copy(src_ref, slab(*me), local_sem)
        self.first = [copy(0, me, self.sibling, own=True)] + [copy(1 + j, me, (*chip, c), own=True) for j, chip in enumerate(self.chips)]

    def start(self):
        self.mine.start()
        for cp in self.first:
            cp.start()

    def wait(self):
        passed = [self.copy(4 + j, (*chip, self.c), self.sibling) for j, chip in enumerate(self.chips)]
        for j, chip in enumerate(self.chips):
            self.copy(1 + j, (*chip, self.c), self.me).wait_recv()
            passed[j].start()
        self.copy(0, self.sibling, self.me).wait_recv()
        for j, chip in enumerate(self.chips):
            self.copy(4 + j, (*chip, 1 - self.c), self.me).wait_recv()
        for cp in self.first + passed:
            cp.wait_send()
        self.mine.wait()


TWO_LEVEL = "two-level gather"


def _make_exchange(src_ref, out_ref, sems, kind):
    return _TwoLevelGather(src_ref, out_ref, *sems) if kind == TWO_LEVEL else _Exchange(src_ref, out_ref, *sems, kind)


def _exchange(src, *, gather, name):
    def body(src_ref, out_ref, *sems):
        ex = _make_exchange(src_ref, out_ref, sems, gather)
        ex.start()
        ex.wait()

    return pl.pallas_call(
        body, out_shape=jax.ShapeDtypeStruct((N_DEV, src.shape[-2], LANES), src.dtype),
        in_specs=[pl.BlockSpec(memory_space=pl.ANY)], out_specs=pl.BlockSpec(memory_space=pl.ANY),
        scratch_shapes=list(_Exchange.SCRATCH), name=name)(src)


PACK_ALIGN = 16 * LANES
PACK_ROWS = 512


def _pack(arrs, dtype, lead=False):
    parts, rows = [], 0
    for a in arrs:
        n_lead = a.shape[0] if lead else 1
        n = a.size // n_lead
        a = a.astype(dtype)
        if n % PACK_ALIGN:
            a = jnp.pad(a.reshape(n_lead, n), ((0, 0), (0, -n % PACK_ALIGN)))
        parts.append(a.reshape(n_lead, -1, LANES))
        rows += parts[-1].shape[1]
    if rows % PACK_ROWS:
        parts.append(jnp.zeros((parts[0].shape[0], -rows % PACK_ROWS, LANES), dtype))
    buf = jnp.concatenate(parts, axis=1)
    return buf if lead else buf[0]


def _unpack(buf, shapes, lead=False):
    buf = buf if lead else buf[None]
    out, off = [], 0
    for shp in shapes:
        n = math.prod(shp)
        rows = (n + (-n % PACK_ALIGN)) // LANES
        piece = buf[:, off:off + rows]
        if n % PACK_ALIGN:
            piece = piece.reshape(buf.shape[0], rows * LANES)[:, :n]
        out.append(piece.reshape(((buf.shape[0],) if lead else ()) + tuple(shp)))
        off += rows
    return out


def _unshard(parts, axis):
    moved = jnp.moveaxis(parts, 0, axis)
    shp = list(moved.shape)
    return moved.reshape(shp[:axis] + [shp[axis] * shp[axis + 1]] + shp[axis + 2:])


def _to_parts(full, axis):
    shp = list(full.shape)
    split = full.reshape(shp[:axis] + [N_DEV, shp[axis] // N_DEV] + shp[axis + 1:])
    return jnp.moveaxis(split, axis, 0)


def _adamw(gparts, w, m, v, *, name):
    r = w.shape[0]
    tr = _pick(r, (PACK_ROWS,))

    def body(g_ref, w_ref, m_ref, v_ref, go, do, mo, vo):
        g = g_ref[0].astype(F32)
        for d in range(1, N_DEV):
            g = g + g_ref[d].astype(F32)
        m1 = ADAM_B1 * m_ref[...] + (1.0 - ADAM_B1) * g
        v1 = ADAM_B2 * v_ref[...] + (1.0 - ADAM_B2) * jnp.square(g)
        m_hat = m1 / (1.0 - ADAM_B1 ** ADAM_STEP)
        v_hat = v1 / (1.0 - ADAM_B2 ** ADAM_STEP)
        go[...] = g
        do[...] = -ADAM_LR * (m_hat / (jnp.sqrt(v_hat) + ADAM_EPS) + ADAM_WD * w_ref[...])
        mo[...] = m1
        vo[...] = v1

    blk = pl.BlockSpec((tr, LANES), lambda i: (i, 0))
    return pl.pallas_call(
        body, grid=(r // tr,), in_specs=[pl.BlockSpec((N_DEV, tr, LANES), lambda i: (0, i, 0)), blk, blk, blk], out_specs=[blk] * 4,
        out_shape=[jax.ShapeDtypeStruct((r, LANES), F32)] * 4, compiler_params=_cparams("parallel"), name=name)(gparts, w, m, v)


def _loss_and_grad(h, g, tgt, *, name, ts=256):
    s, d = h.shape
    ts = min(ts, s)

    def tile_loss(hv, gv, tv):
        (y,) = _rmsnorm_tile(0, hv, gv)
        return 0.5 * jnp.sum(jnp.mean(jnp.square(y - tv), axis=-1))

    def body(h_ref, g_ref, t_ref, l_ref, dh_ref, dg_ref):
        @pl.when(pl.program_id(0) == 0)
        def _():
            l_ref[...] = jnp.zeros_like(l_ref)
            dg_ref[...] = jnp.zeros_like(dg_ref)

        tv = t_ref[...]
        loss, vjp = jax.vjp(lambda hv, gv: tile_loss(hv, gv, tv), h_ref[...], g_ref[...])
        dh, dg = vjp(jnp.ones((), F32))
        l_ref[...] += loss
        dh_ref[...] = dh
        dg_ref[...] += dg

    row = pl.BlockSpec((ts, d), lambda i: (i, 0))
    return pl.pallas_call(
        body, grid=(s // ts,), in_specs=[row, pl.BlockSpec((1, d), lambda i: (0, 0)), row],
        out_specs=[pl.BlockSpec((8, LANES), lambda i: (0, 0)), row, pl.BlockSpec((1, d), lambda i: (0, 0))],
        out_shape=[jax.ShapeDtypeStruct((8, LANES), F32), jax.ShapeDtypeStruct((s, d), F32), jax.ShapeDtypeStruct((1, d), F32)],
        compiler_params=_cparams("arbitrary"), name=name)(h, g, tgt)


def _norm(h, g, name):
    return _rw_fwd(_rmsnorm_tile, [_row(h)], [g], [(h.shape[1], F32)], name=name)[0]


def _norm_bwd(h, g, dhn, dres, name):
    (dh,), (dg,) = _rw_bwd(_rmsnorm_tile, [_row(h)], [g], [dhn], add_rows={0: dres}, name=name)
    return dh, dg


def _layer_fwd(h, p_i, lw, mixer_fwd, mp, tag, carry=None, late=None):
    hn = _norm(h, lw["norm_mix"], f"{tag}_norm_mix")
    proj = _matmul(hn, lw["w_in"], "nn", name=f"{tag}_in_proj")
    y, mres, carried = mixer_fwd(proj, mp, carry)
    if late is not None:
        lw = {**lw, **late(carried)}
    h1 = _matmul(y, lw["w_out"], "nn", name=f"{tag}_out_proj", add=h)
    hn2 = _norm(h1, lw["norm_ffn"], f"{tag}_norm_ffn")
    act = _matmul(hn2, lw["w1"], "nn", name=f"{tag}_mlp_up", post=_relu2_of)
    h2 = _matmul(act, lw["w2"], "nn", name=f"{tag}_mlp_down", add=h1)
    hn3 = _norm(h2, lw["norm_pl"], f"{tag}_norm_pl")
    gl = _matmul(hn3, lw["w_gate"], "nn", name=f"{tag}_pl_gate")
    e = _matmul(p_i, lw["w_pl"], "nn", name=f"{tag}_pl_proj")
    (h3,) = _rw_fwd(_pl_gate, [_row(h2), _row(gl), _row(e)], [], [(h.shape[1], F32)], name=f"{tag}_pl_mix")
    return h3, (h, hn, proj, y, mres, h1, hn2, act, h2, hn3, gl, e), lw


def _layer_bwd(dh3, p_i, lw, mixer_bwd, mp, saved, tag, early=None):
    h, hn, proj, y, mres, h1, hn2, act, h2, hn3, gl, e = saved
    (dgl, de), _ = _rw_bwd(_pl_gate, [_row(h2), _row(gl), _row(e)], [], [dh3], row_grads=[1, 2], name=f"{tag}_pl_mix_bwd")
    g = dict(w_pl=_matmul(p_i, de, "tn", name=f"{tag}_pl_proj_dw"), w_gate=_matmul(hn3, dgl, "tn", name=f"{tag}_pl_gate_dw"))
    dhn3 = _matmul(dgl, lw["w_gate"], "nt", name=f"{tag}_pl_gate_dx")
    dh2, g["norm_pl"] = _norm_bwd(h2, lw["norm_pl"], dhn3, dh3, f"{tag}_norm_pl_bwd")
    da1 = _matmul(dh2, lw["w2"], "nt", name=f"{tag}_mlp_down_dx", post=_relu2_grad, aux=act)
    g["w2"] = _matmul(act, dh2, "tn", name=f"{tag}_mlp_down_dw")
    g["w1"] = _matmul(hn2, da1, "tn", name=f"{tag}_mlp_up_dw")
    dhn2 = _matmul(da1, lw["w1"], "nt", name=f"{tag}_mlp_up_dx")
    dh1, g["norm_ffn"] = _norm_bwd(h1, lw["norm_ffn"], dhn2, dh2, f"{tag}_norm_ffn_bwd")
    dy = _matmul(dh1, lw["w_out"], "nt", name=f"{tag}_out_proj_dx")
    g["w_out"] = _matmul(y, dh1, "tn", name=f"{tag}_out_proj_dw")
    dproj, mg, carried = mixer_bwd(proj, mp, mres, dy, None if early is None else early(g))
    g["w_in"] = _matmul(hn, dproj, "tn", name=f"{tag}_in_proj_dw")
    dhn = _matmul(dproj, lw["w_in"], "nt", name=f"{tag}_in_proj_dx")
    dh, g["norm_mix"] = _norm_bwd(h, lw["norm_mix"], dhn, dh1, f"{tag}_norm_mix_bwd")
    return dh, g, mg, carried


WEIGHTS = (
    ("norm_mix", None), ("norm_ffn", None), ("norm_pl", None), ("mlp_w1", 2), ("mlp_w2", 1), ("pl_proj", 2), ("pl_gate", 1),
    ("e_in_proj", 2), ("e_out_proj", 1), ("s5_lam_re", None), ("s5_lam_im", None), ("s5_log_step", None), ("s5_b_re", None),
    ("s5_b_im", None), ("s5_c_re", None), ("s5_c_im", None), ("s5_d", None), ("s5_glu_w", 1), ("s5_glu_b", None),
    ("ssd_conv_w", 2), ("ssd_conv_b", None), ("ssd_dt_bias", None), ("ssd_a_log", None), ("ssd_d", None), ("ssd_norm", None),
    ("o_in_proj", 2), ("o_out_proj", 1), ("rwkv_mu", 1), ("rwkv_w0", 1), ("rwkv_w_up", 2), ("rwkv_a0", 1), ("rwkv_a_up", 2),
    ("rwkv_g_up", 2), ("rwkv_k_k", 1), ("rwkv_k_a", 1), ("rwkv_r_k", None), ("rwkv_ln_g", 1), ("rwkv_ln_b", 1),
    ("lru_conv_w", 2), ("lru_conv_b", 1), ("lru_w_a", None), ("lru_b_a", None), ("lru_w_x", None), ("lru_b_x", None),
    ("lru_lam", None), ("norm_final", None))
MATMUL_WEIGHTS = ("mlp_w1", "mlp_w2", "pl_proj", "pl_gate", "e_in_proj", "e_out_proj", "s5_glu_w", "o_in_proj", "o_out_proj",
                  "rwkv_w_up", "rwkv_a_up", "rwkv_g_up")


LATE_WEIGHTS = (("mlp_w1", 1), ("mlp_w2", 1), ("pl_gate", 1), ("pl_proj", 1))
EARLY_GRADS = LATE_WEIGHTS + (("o_out_proj", 0),)


def kernel(x, p, norm_mix, norm_ffn, norm_pl, mlp_w1, mlp_w2, pl_proj, pl_gate, e_in_proj, e_out_proj, s5_lam_re, s5_lam_im, s5_log_step, s5_b_re, s5_b_im, s5_c_re, s5_c_im, s5_d, s5_glu_w, s5_glu_b, ssd_conv_w, ssd_conv_b, ssd_dt_bias, ssd_a_log, ssd_d, ssd_norm, o_in_proj, o_out_proj, rwkv_mu, rwkv_w0, rwkv_w_up, rwkv_a0, rwkv_a_up, rwkv_g_up, rwkv_k_k, rwkv_k_a, rwkv_r_k, rwkv_ln_g, rwkv_ln_b, lru_conv_w, lru_conv_b, lru_w_a, lru_b_a, lru_w_x, lru_b_x, lru_lam, norm_final, loss_target, m_norm_mix, m_norm_ffn, m_norm_pl, m_mlp_w1, m_mlp_w2, m_pl_proj, m_pl_gate, m_e_in_proj, m_e_out_proj, m_s5_lam_re, m_s5_lam_im, m_s5_log_step, m_s5_b_re, m_s5_b_im, m_s5_c_re, m_s5_c_im, m_s5_d, m_s5_glu_w, m_s5_glu_b, m_ssd_conv_w, m_ssd_conv_b, m_ssd_dt_bias, m_ssd_a_log, m_ssd_d, m_ssd_norm, m_o_in_proj, m_o_out_proj, m_rwkv_mu, m_rwkv_w0, m_rwkv_w_up, m_rwkv_a0, m_rwkv_a_up, m_rwkv_g_up, m_rwkv_k_k, m_rwkv_k_a, m_rwkv_r_k, m_rwkv_ln_g, m_rwkv_ln_b, m_lru_conv_w, m_lru_conv_b, m_lru_w_a, m_lru_b_a, m_lru_w_x, m_lru_b_x, m_lru_lam, m_norm_final, v_norm_mix, v_norm_ffn, v_norm_pl, v_mlp_w1, v_mlp_w2, v_pl_proj, v_pl_gate, v_e_in_proj, v_e_out_proj, v_s5_lam_re, v_s5_lam_im, v_s5_log_step, v_s5_b_re, v_s5_b_im, v_s5_c_re, v_s5_c_im, v_s5_d, v_s5_glu_w, v_s5_glu_b, v_ssd_conv_w, v_ssd_conv_b, v_ssd_dt_bias, v_ssd_a_log, v_ssd_d, v_ssd_norm, v_o_in_proj, v_o_out_proj, v_rwkv_mu, v_rwkv_w0, v_rwkv_w_up, v_rwkv_a0, v_rwkv_a_up, v_rwkv_g_up, v_rwkv_k_k, v_rwkv_k_a, v_rwkv_r_k, v_rwkv_ln_g, v_rwkv_ln_b, v_lru_conv_w, v_lru_conv_b, v_lru_w_a, v_lru_b_a, v_lru_w_x, v_lru_b_x, v_lru_lam, v_norm_final):
    a = dict(locals())
    d_model = x.shape[-1]
    row = lambda v: v.reshape(1, d_model)
    axis = dict(WEIGHTS)
    keys = [(n, i) for n, ax in WEIGHTS if ax is not None for i in range(a[n].shape[0])]
    shard = lambda pre, key: a[pre + key[0]][key[1]]
    piece_axis = lambda key: axis[key[0]] - 1

    def gathered(got, ks):
        parts = _unpack(got, [shard("", k).shape for k in ks], lead=True)
        return {k: _unshard(pt, piece_axis(k)) for k, pt in zip(ks, parts)}

    in_layer1 = lambda k: k[1] == 1 or k[0].startswith(("o_", "rwkv_", "lru_"))
    first_keys = [k for k in keys if k[0] in MATMUL_WEIGHTS and not in_layer1(k)]
    mix1_keys = [k for k in keys if k[0] in MATMUL_WEIGHTS and in_layer1(k) and k not in LATE_WEIGHTS]
    f32_keys = [k for k in keys if k[0] not in MATMUL_WEIGHTS]
    full = gathered(_exchange(_pack([shard("", k) for k in first_keys], BF16), gather=TWO_LEVEL, name="gather_weights_bf16"), first_keys)
    full.update(gathered(_exchange(_pack([shard("", k) for k in f32_keys], F32), gather=True, name="gather_weights_f32"), f32_keys))
    by_name = lambda: {n: (a[n] if ax is None else full.get((n, 0))) for n, ax in WEIGHTS}

    lw0 = dict(norm_mix=row(norm_mix[0]), norm_ffn=row(norm_ffn[0]), norm_pl=row(norm_pl[0]), w_in=_even_in_cols(full["e_in_proj", 0]),
               w_out=full["e_out_proj", 0], w1=full["mlp_w1", 0], w2=full["mlp_w2", 0], w_gate=full["pl_gate", 0], w_pl=full["pl_proj", 0])
    mp0 = _even_prep(by_name())
    mix1_src = _pack([shard("", k) for k in mix1_keys], BF16)

    def take_mix1(got):
        full.update(gathered(got, mix1_keys))
        return {}

    h1, saved0, lw0 = _layer_fwd(x[0], p[0, 0], lw0, _even_fwd, mp0, "l0", carry=(mix1_src, TWO_LEVEL), late=take_mix1)

    def late(got):
        fl = gathered(got, LATE_WEIGHTS)
        return dict(w1=fl["mlp_w1", 1], w2=fl["mlp_w2", 1], w_gate=fl["pl_gate", 1], w_pl=fl["pl_proj", 1])

    lw1 = dict(norm_mix=row(norm_mix[1]), norm_ffn=row(norm_ffn[1]), norm_pl=row(norm_pl[1]), w_in=_odd_in_cols(full["o_in_proj", 0]),
               w_out=full["o_out_proj", 0])
    mp1 = _odd_prep(by_name())
    late_src = _pack([shard("", k) for k in LATE_WEIGHTS], BF16)
    h2, saved1, lw1 = _layer_fwd(h1, p[1, 0], lw1, _odd_fwd, mp1, "l1", carry=(late_src, True), late=late)
    loss_blk, dh, dg_final = _loss_and_grad(h2, row(norm_final), loss_target[0], name="loss")
    loss = lax.psum(loss_blk[0, 0], ("x", "y", "c"))

    def slabs(grad_of, ks):
        return _pack([_to_parts(grad_of[k], piece_axis(k)) for k in ks], BF16, lead=True)

    def global_shape(k):
        return tuple((N_DEV if d == piece_axis(k) else 1) * n for d, n in enumerate(shard("", k).shape))

    early_grads = lambda g: (slabs({("mlp_w1", 1): g["w1"], ("mlp_w2", 1): g["w2"], ("pl_gate", 1): g["w_gate"],
                                    ("pl_proj", 1): g["w_pl"], ("o_out_proj", 0): g["w_out"]}, EARLY_GRADS), False)
    dh, g1, mg1, early_got = _layer_bwd(dh, p[1, 0], lw1, _odd_bwd, mp1, saved1, "l1", early=early_grads)
    l0a_keys = [("mlp_w1", 0), ("pl_proj", 0)]
    mid_keys = [k for k in keys if in_layer1(k) and k not in EARLY_GRADS] + l0a_keys
    mid_grad = {k: (_odd_in_cols_t(g1["w_in"]) if k[0] == "o_in_proj" else mg1[k[0]].reshape(global_shape(k))) for k in mid_keys[:-2]}
    l0_early = lambda g: (slabs({**mid_grad, ("mlp_w1", 0): g["w1"], ("pl_proj", 0): g["w_pl"]}, mid_keys), False)
    dh, g0, mg0, mid_got = _layer_bwd(dh, p[0, 0], lw0, _even_bwd, mp0, saved0, "l0", early=l0_early)

    main_keys = [k for k in keys if not in_layer1(k) and k not in l0a_keys]
    piece_grad = {("mlp_w2", 0): g0["w2"], ("pl_gate", 0): g0["w_gate"], ("e_in_proj", 0): _even_in_cols_t(g0["w_in"]), ("e_out_proj", 0): g0["w_out"]}
    mixer_grads = {**mg0, **mg1}
    for k in main_keys:
        if k not in piece_grad:
            piece_grad[k] = mg0[k[0]].reshape(global_shape(k))
    main_got = _exchange(slabs(piece_grad, main_keys), gather=False, name="scatter_grads")
    piece_out = {}
    for got, ks, tag in ((early_got, EARLY_GRADS, "early"), (mid_got, mid_keys, "mid"), (main_got, main_keys, "main")):
        res = _adamw(got, *[_pack([shard(pre, k) for k in ks], F32) for pre in ("", "m_", "v_")], name=f"adamw_sharded_{tag}")
        for j in range(4):
            piece_out.update({(j, k): v for k, v in zip(ks, _unpack(res[j], [shard("", k).shape for k in ks]))})
    rp = [n for n, ax in WEIGHTS if ax is None]
    rp_grads = {**mixer_grads, "norm_final": dg_final,
                **{n: jnp.stack([g0[n], g1[n]]) for n in ("norm_mix", "norm_ffn", "norm_pl")}}
    parts = _exchange(_pack([rp_grads[n].reshape(a[n].shape) for n in rp], F32), gather=True, name="gather_small_grads")
    rp_res = _adamw(parts, *[_pack([a[pre + n] for n in rp], F32) for pre in ("", "m_", "v_")], name="adamw_replicated")

    outs = []
    for j in range(4):
        rp_out = dict(zip(rp, _unpack(rp_res[j], [a[n].shape for n in rp])))
        outs.extend(rp_out[n] if ax is None else jnp.stack([piece_out[j, (n, i)] for i in range(a[n].shape[0])]) for n, ax in WEIGHTS)
    return (loss, dh[None], *outs)
```

```python
import functools
import math

import jax
import jax.numpy as jnp
from jax import lax
from jax.experimental import pallas as pl
from jax.experimental.pallas import tpu as pltpu

F32 = jnp.float32
BF16 = jnp.bfloat16
N_DEV = 8
LANES = 128
VMEM_LIMIT = 56 * 1024 * 1024
MATMUL_VMEM = 46 * 1024 * 1024
NORM_EPS = 1e-6
RWKV_GN_EPS = 64e-5
LRU_C = 8.0
ADAM_LR, ADAM_B1, ADAM_B2, ADAM_EPS, ADAM_WD, ADAM_STEP = 0.001, 0.9, 0.999, 1e-08, 0.01, 10
SSD_CHUNK = 128
RWKV_CHUNK = 32
HEAD = 64


def _cparams(*sem):
    return pltpu.CompilerParams(dimension_semantics=sem, vmem_limit_bytes=VMEM_LIMIT)


def _pick(n, prefs):
    for t in prefs:
        if n % t == 0:
            return t
    return n


def _relu2_of(val, _):
    r = jnp.maximum(val, 0.0)
    return r * r


def _relu2_grad(dact, act):
    return dact * (2.0 * jnp.sqrt(act))


def _matmul(a, b, mode, *, name, add=None, out_dtype=F32, post=None, aux=None):
    if mode == "nn":
        (m, k), (k2, n) = a.shape, b.shape
    elif mode == "nt":
        (m, k), (n, k2) = a.shape, b.shape
    else:
        (k, m), (k2, n) = a.shape, b.shape
    assert k == k2, (a.shape, b.shape, mode)
    tm, tn = _pick(m, (1024, 512, 256, 128)), _pick(n, (1024, 512, 256, 128))
    extras = [x for x in (add, aux) if x is not None]
    fits = lambda t: 2 * t * (tm * a.dtype.itemsize + tn * b.dtype.itemsize) + (3 + 2 * len(extras)) * tm * tn * 4 <= MATMUL_VMEM
    tk = next((t for t in (2048, 1024, 512, 256, 128) if k % t == 0 and fits(t)), k)
    nk = k // tk
    dn = {"nn": (((1,), (0,)), ((), ())), "nt": (((1,), (1,)), ((), ())), "tn": (((0,), (0,)), ((), ()))}[mode]
    a_spec = pl.BlockSpec((tk, tm), lambda i, j, kk: (kk, i)) if mode == "tn" else pl.BlockSpec((tm, tk), lambda i, j, kk: (i, kk))
    b_spec = pl.BlockSpec((tn, tk), lambda i, j, kk: (j, kk)) if mode == "nt" else pl.BlockSpec((tk, tn), lambda i, j, kk: (kk, j))
    o_spec = pl.BlockSpec((tm, tn), lambda i, j, kk: (i, j))

    def body(a_ref, b_ref, *rest):
        extra_refs, (o_ref, acc) = list(rest[:len(extras)]), rest[len(extras):]
        add_ref = extra_refs.pop(0) if add is not None else None
        aux_ref = extra_refs.pop(0) if aux is not None else None
        kk = pl.program_id(2)
        prod = lambda: lax.dot_general(a_ref[...].astype(BF16), b_ref[...].astype(BF16), dn, preferred_element_type=F32)
        first = lambda: prod() if add is None else prod() + add_ref[...].astype(F32)

        def write(val):
            if post is not None:
                val = post(val, None if aux is None else aux_ref[...])
            o_ref[...] = val.astype(o_ref.dtype)

        if nk == 1:
            write(first())
            return

        @pl.when(kk == 0)
        def _():
            acc[...] = first()

        @pl.when((kk > 0) & (kk < nk - 1))
        def _():
            acc[...] += prod()

        @pl.when(kk == nk - 1)
        def _():
            write(acc[...] + prod())

    ins, specs = [a, b] + extras, [a_spec, b_spec] + [o_spec] * len(extras)
    return pl.pallas_call(
        body, grid=(m // tm, n // tn, nk), in_specs=specs, out_specs=o_spec,
        out_shape=jax.ShapeDtypeStruct((m, n), out_dtype), scratch_shapes=[pltpu.VMEM((tm, tn), F32)],
        compiler_params=_cparams("parallel", "parallel", "arbitrary"), name=name)(*ins)


def _row(x, width=None, block=0):
    return (x, x.shape[1] if width is None else width, block)


def _rw_specs(rows, params, ts):
    specs = [pl.BlockSpec((ts, w), functools.partial(lambda i, b: (i, b), b=bi)) for (_, w, bi) in rows]
    specs += [pl.BlockSpec(p.shape, functools.partial(lambda i, nd: (0,) * nd, nd=p.ndim)) for p in params]
    return specs


def _rw_fwd(f, rows, params, outs, *, name, ts=256):
    s = rows[0][0].shape[0]
    ts = min(ts, s)
    nr, npar = len(rows), len(params)

    def body(*refs):
        row0 = pl.program_id(0) * ts
        res = f(row0, *[r[...] for r in refs[:nr + npar]])
        for o_ref, val in zip(refs[nr + npar:], res, strict=True):
            o_ref[...] = val.astype(o_ref.dtype)

    out = pl.pallas_call(
        body, grid=(s // ts,), in_specs=_rw_specs(rows, params, ts),
        out_specs=[pl.BlockSpec((ts, w), lambda i: (i, 0)) for (w, _) in outs],
        out_shape=[jax.ShapeDtypeStruct((s, w), dt) for (w, dt) in outs],
        compiler_params=_cparams("parallel"), name=name)(*[r[0] for r in rows], *params)
    return tuple(out)


def _rw_bwd(f, rows, params, cts, *, name, ts=256, row_grads=None, param_grads=None, add_rows=None):
    s = rows[0][0].shape[0]
    ts = min(ts, s)
    ct_groups = [list(c) if isinstance(c, (list, tuple)) else [c] for c in cts]
    cts = [c for grp in ct_groups for c in grp]
    nr, npar, nct = len(rows), len(params), len(cts)
    row_grads = list(range(nr)) if row_grads is None else list(row_grads)
    param_grads = list(range(npar)) if param_grads is None else list(param_grads)
    add_rows = add_rows or {}
    add_keys = sorted(add_rows)

    def body(*refs):
        i = pl.program_id(0)
        row0 = i * ts
        vals = [r[...] for r in refs[:nr + npar]]
        for pi in param_grads:
            vals[nr + pi] = vals[nr + pi].astype(F32)
        ct_refs = list(refs[nr + npar:nr + npar + nct])
        add_refs = dict(zip(add_keys, refs[nr + npar + nct:nr + npar + nct + len(add_keys)]))
        o_refs = refs[nr + npar + nct + len(add_keys):]
        res, vjp = jax.vjp(functools.partial(f, row0), *vals)
        ct_vals = []
        for grp, r in zip(ct_groups, res, strict=True):
            ct_vals.append(sum(ct_refs.pop(0)[...].astype(r.dtype) for _ in grp))
        grads = vjp(tuple(ct_vals))
        for o_ref, ri in zip(o_refs[:len(row_grads)], row_grads):
            g = grads[ri]
            if ri in add_refs:
                g = g + add_refs[ri][...]
            o_ref[...] = g.astype(o_ref.dtype)

        @pl.when(i == 0)
        def _():
            for o_ref in o_refs[len(row_grads):]:
                o_ref[...] = jnp.zeros_like(o_ref)

        for o_ref, pi in zip(o_refs[len(row_grads):], param_grads):
            o_ref[...] += grads[nr + pi].astype(F32)

    in_specs = _rw_specs(rows, params, ts)
    in_specs += [pl.BlockSpec((ts, c.shape[1]), lambda i: (i, 0)) for c in cts]
    in_specs += [pl.BlockSpec((ts, add_rows[k].shape[1]), lambda i: (i, 0)) for k in add_keys]
    out_specs = [pl.BlockSpec((ts, rows[ri][1]), lambda i: (i, 0)) for ri in row_grads]
    out_specs += [pl.BlockSpec(params[pi].shape, functools.partial(lambda i, nd: (0,) * nd, nd=params[pi].ndim)) for pi in param_grads]
    out_shape = [jax.ShapeDtypeStruct((s, rows[ri][1]), F32) for ri in row_grads]
    out_shape += [jax.ShapeDtypeStruct(params[pi].shape, F32) for pi in param_grads]
    out = pl.pallas_call(
        body, grid=(s // ts,), in_specs=in_specs, out_specs=out_specs, out_shape=out_shape,
        compiler_params=_cparams("arbitrary"), name=name)(*[r[0] for r in rows], *params, *cts, *[add_rows[k] for k in add_keys])
    return tuple(out[:len(row_grads)]), tuple(out[len(row_grads):])


@functools.partial(jax.custom_vjp, nondiff_argnums=(1,))
def _shift_down(x, k):
    rows = lax.broadcasted_iota(jnp.int32, x.shape, 0)
    return jnp.where(rows < k, 0.0, pltpu.roll(x, k, 0))


def _shift_down_fwd(x, k):
    return _shift_down(x, k), None


def _shift_down_bwd(k, _, g):
    n = g.shape[0]
    rows = lax.broadcasted_iota(jnp.int32, g.shape, 0)
    return (jnp.where(rows >= n - k, 0.0, pltpu.roll(g, n - k, 0)),)


_shift_down.defvjp(_shift_down_fwd, _shift_down_bwd)


def _ct_fwd(f, x, xmap, ntiles, params, out_width, *, name, ct=LANES):
    s = x.shape[0]

    def body(*refs):
        refs[-1][...] = f(*[r[...] for r in refs[:-1]])

    in_specs = [pl.BlockSpec((s, ct), lambda j: (0, xmap(j)))]
    in_specs += [pl.BlockSpec((p.shape[0], ct), functools.partial(lambda j, o: (0, o + j), o=o)) for (p, o) in params]
    return pl.pallas_call(
        body, grid=(ntiles,), in_specs=in_specs, out_specs=pl.BlockSpec((s, ct), lambda j: (0, j)),
        out_shape=jax.ShapeDtypeStruct((s, out_width), F32), compiler_params=_cparams("parallel"), name=name)(x, *[p for p, _ in params])


def _ct_bwd(f, x, xmap, ntiles, params, g, *, name, ct=LANES):
    s = x.shape[0]
    npar, ng = len(params), len(g)

    def body(*refs):
        vals = [r[...] for r in refs[:1 + npar]]
        _, vjp = jax.vjp(f, *vals)
        grads = vjp(sum(r[...] for r in refs[1 + npar:1 + npar + ng]))
        for o_ref, gr in zip(refs[1 + npar + ng:], grads, strict=True):
            o_ref[...] = gr

    in_specs = [pl.BlockSpec((s, ct), lambda j: (0, xmap(j)))]
    pspecs = [pl.BlockSpec((p.shape[0], ct), functools.partial(lambda j, o: (0, o + j), o=o)) for (p, o) in params]
    in_specs += pspecs + [pl.BlockSpec((s, ct), lambda j: (0, j))] * ng
    out = pl.pallas_call(
        body, grid=(ntiles,), in_specs=in_specs, out_specs=[pl.BlockSpec((s, ct), lambda j: (0, j))] + pspecs,
        out_shape=[jax.ShapeDtypeStruct((s, ntiles * ct), F32)] + [jax.ShapeDtypeStruct(p.shape, F32) for p, _ in params],
        compiler_params=_cparams("parallel"), name=name)(x, *[p for p, _ in params], *g)
    return out[0], tuple(out[1:])


def _block_vjp(f, args, cts, *, name):
    outs = jax.eval_shape(f, *args)
    if cts is None:
        def body(*refs):
            for o_ref, v in zip(refs[len(args):], f(*[r[...] for r in refs[:len(args)]]), strict=True):
                o_ref[...] = v
        return tuple(pl.pallas_call(body, out_shape=[jax.ShapeDtypeStruct(o.shape, o.dtype) for o in outs], name=name)(*args))

    def body(*refs):
        n = len(args)
        _, vjp = jax.vjp(f, *[r[...] for r in refs[:n]])
        for o_ref, gr in zip(refs[n + len(cts):], vjp(tuple(r[...] for r in refs[n:n + len(cts)])), strict=True):
            o_ref[...] = gr
    return tuple(pl.pallas_call(body, out_shape=[jax.ShapeDtypeStruct(a.shape, a.dtype) for a in args], name=name)(*args, *cts))


def _softplus(x):
    return jnp.maximum(x, 0.0) + jnp.log(1.0 + jnp.exp(-jnp.abs(x)))


def _dot_bf16(a, b):
    return jnp.dot(a.astype(BF16), b.astype(BF16), preferred_element_type=F32)


def _dot3(x, m):
    hi = x.astype(BF16)
    r1 = x - hi.astype(F32)
    mid = r1.astype(BF16)
    lo = (r1 - mid.astype(F32)).astype(BF16)
    return (jnp.dot(hi, m, preferred_element_type=F32) + jnp.dot(mid, m, preferred_element_type=F32)
            + jnp.dot(lo, m, preferred_element_type=F32))


@jax.custom_vjp
def _lin(x, m, mt):
    return _dot3(x, m)


def _lin_fwd(x, m, mt):
    return _dot3(x, m), (m, mt)


def _lin_bwd(res, g):
    m, mt = res
    return _dot3(g, mt), jnp.zeros_like(m), jnp.zeros_like(mt)


_lin.defvjp(_lin_fwd, _lin_bwd)


def _head_ones(n, head=HEAD):
    i = jnp.arange(n) // head
    return (i[:, None] == i[None, :]).astype(BF16)


def _rmsnorm_tile(_, h, g):
    return (h * lax.rsqrt(jnp.mean(h * h, axis=-1, keepdims=True) + NORM_EPS) * g,)


ROWS = 8


def _shift_rows(x, k, fill, up=False):
    rows = lax.broadcasted_iota(jnp.int32, x.shape, 0)
    if up:
        return jnp.where(rows >= ROWS - k, fill, pltpu.roll(x, ROWS - k, 0))
    return jnp.where(rows < k, fill, pltpu.roll(x, k, 0))


def _cmul(pr, pi, qr, qi):
    return pr * qr - pi * qi, pr * qi + pi * qr


def _power_rows(ar, ai, width, descending):
    pows = [(ar, ai)]
    for _ in range(ROWS - 1):
        pows.append(_cmul(*pows[-1], ar, ai))
    rows = lax.broadcasted_iota(jnp.int32, (ROWS, width), 0)
    pr = jnp.zeros((ROWS, width), F32)
    pi = jnp.zeros((ROWS, width), F32)
    for j in range(ROWS):
        qr, qi = pows[ROWS - 1 - j] if descending else pows[j]
        pr, pi = jnp.where(rows == j, qr, pr), jnp.where(rows == j, qi, pi)
    return pr, pi, ((pows[0], 1), (pows[1], 2), (pows[3], 4))


def _prev_rows(ref, tile, r0):
    before = ref[pl.ds(jnp.maximum(r0 - 1, 0), 1), :] * (r0 > 0).astype(F32)
    rows = lax.broadcasted_iota(jnp.int32, tile.shape, 0)
    return jnp.where(rows == 0, before, pltpu.roll(tile, 1, 0))


def _cscan_fwd(bu, a, *, name, ct=256, carry=None):
    s, c2 = bu.shape
    c = c2 // 2
    nt = c // ct

    def body(br_ref, bi_ref, ar_ref, ai_ref, xr_ref, xi_ref):
        pr, pi, doubling = _power_rows(ar_ref[...], ai_ref[...], ct, False)

        def tile(i, h):
            rows = pl.ds(pl.multiple_of(i * ROWS, ROWS), ROWS)
            sr, si = br_ref[rows, :], bi_ref[rows, :]
            for (qr, qi), k in doubling:
                mr, mi = _cmul(qr, qi, _shift_rows(sr, k, 0.0), _shift_rows(si, k, 0.0))
                sr, si = sr + mr, si + mi
            cr, ci = _cmul(pr, pi, *h)
            sr, si = sr + cr, si + ci
            xr_ref[rows, :] = sr
            xi_ref[rows, :] = si
            return sr[ROWS - 1:], si[ROWS - 1:]

        z = jnp.zeros((1, ct), F32)
        lax.fori_loop(0, s // ROWS, tile, (z, z))

    re = lambda j: (0, j)
    im = lambda j: (0, nt + j)
    return _call_with_exchange(
        body, nt, carry, [bu, bu, a, a],
        [pl.BlockSpec((s, ct), re), pl.BlockSpec((s, ct), im), pl.BlockSpec((1, ct), re), pl.BlockSpec((1, ct), im)],
        [pl.BlockSpec((s, ct), re), pl.BlockSpec((s, ct), re)], [jax.ShapeDtypeStruct((s, c), F32)] * 2, [], name)


def _cscan_bwd(gr, gi, xr, xi, a, *, name, ct=256, carry=None):
    s, c = gr.shape
    nt = c // ct
    n_tiles = s // ROWS

    def body(gr_ref, gi_ref, xr_ref, xi_ref, ar_ref, ai_ref, dr_ref, di_ref, dar_ref, dai_ref):
        pr, pi, doubling = _power_rows(ar_ref[...], -ai_ref[...], ct, True)

        def tile(i, state):
            dnr, dni, accr, acci = state
            r0 = pl.multiple_of((n_tiles - 1 - i) * ROWS, ROWS)
            rows = pl.ds(r0, ROWS)
            sr, si = gr_ref[rows, :], gi_ref[rows, :]
            for (qr, qi), k in doubling:
                mr, mi = _cmul(qr, qi, _shift_rows(sr, k, 0.0, up=True), _shift_rows(si, k, 0.0, up=True))
                sr, si = sr + mr, si + mi
            cr, ci = _cmul(pr, pi, dnr, dni)
            sr, si = sr + cr, si + ci
            dr_ref[rows, :] = sr
            di_ref[rows, :] = si
            xpr, xpi = _prev_rows(xr_ref, xr_ref[rows, :], r0), _prev_rows(xi_ref, xi_ref[rows, :], r0)
            return sr[:1], si[:1], accr + sr * xpr + si * xpi, acci + si * xpr - sr * xpi

        z, z8 = jnp.zeros((1, ct), F32), jnp.zeros((ROWS, ct), F32)
        _, _, accr, acci = lax.fori_loop(0, n_tiles, tile, (z, z, z8, z8))
        dar_ref[...] = jnp.sum(accr, axis=0, keepdims=True)
        dai_ref[...] = jnp.sum(acci, axis=0, keepdims=True)

    re = lambda j: (0, j)
    im = lambda j: (0, nt + j)
    blk = pl.BlockSpec((s, ct), re)
    return _call_with_exchange(
        body, nt, carry, [gr, gi, xr, xi, a, a],
        [blk, blk, blk, blk, pl.BlockSpec((1, ct), re), pl.BlockSpec((1, ct), im)],
        [blk, blk, pl.BlockSpec((1, ct), re), pl.BlockSpec((1, ct), re)],
        [jax.ShapeDtypeStruct((s, c), F32)] * 2 + [jax.ShapeDtypeStruct((1, c), F32)] * 2, [], name)


def _rscan_fwd(a, b, *, name, ct=256):
    s, c = a.shape

    def body(a_ref, b_ref, h_ref):
        def tile(i, h):
            rows = pl.ds(pl.multiple_of(i * ROWS, ROWS), ROWS)
            ca, cb = a_ref[rows, :], b_ref[rows, :]
            for k in (1, 2, 4):
                cb = cb + ca * _shift_rows(cb, k, 0.0)
                ca = ca * _shift_rows(ca, k, 1.0)
            out = cb + ca * h
            h_ref[rows, :] = out
            return out[ROWS - 1:]
        lax.fori_loop(0, s // ROWS, tile, jnp.zeros((1, ct), F32))

    blk = pl.BlockSpec((s, ct), lambda j: (0, j))
    return pl.pallas_call(body, grid=(c // ct,), in_specs=[blk, blk], out_specs=blk,
                          out_shape=jax.ShapeDtypeStruct((s, c), F32), compiler_params=_cparams("parallel"), name=name)(a, b)


def _rscan_bwd(g, a, h, *, name, ct=256):
    s, c = a.shape
    n_tiles = s // ROWS

    def body(g_ref, a_ref, h_ref, db_ref, da_ref):
        def tile(i, dn):
            r0 = pl.multiple_of((n_tiles - 1 - i) * ROWS, ROWS)
            rows = pl.ds(r0, ROWS)
            after = a_ref[pl.ds(jnp.minimum(r0 + ROWS, s - 1), 1), :] * (r0 + ROWS < s).astype(F32)
            ca = _shift_rows(a_ref[rows, :], 1, after, up=True)
            cb = g_ref[rows, :]
            for k in (1, 2, 4):
                cb = cb + ca * _shift_rows(cb, k, 0.0, up=True)
                ca = ca * _shift_rows(ca, k, 1.0, up=True)
            out = cb + ca * dn
            db_ref[rows, :] = out
            da_ref[rows, :] = out * _prev_rows(h_ref, h_ref[rows, :], r0)
            return out[:1]
        lax.fori_loop(0, n_tiles, tile, jnp.zeros((1, ct), F32))

    blk = pl.BlockSpec((s, ct), lambda j: (0, j))
    db, da = pl.pallas_call(body, grid=(c // ct,), in_specs=[blk, blk, blk], out_specs=[blk, blk],
                            out_shape=[jax.ShapeDtypeStruct((s, c), F32)] * 2, compiler_params=_cparams("parallel"), name=name)(g, a, h)
    return db, da


def _dot3l(m, x):
    hi = x.astype(BF16)
    r1 = x - hi.astype(F32)
    mid = r1.astype(BF16)
    lo = (r1 - mid.astype(F32)).astype(BF16)
    return (jnp.dot(m, hi, preferred_element_type=F32) + jnp.dot(m, mid, preferred_element_type=F32)
            + jnp.dot(m, lo, preferred_element_type=F32))


@jax.custom_vjp
def _linl(x, m, mt):
    return _dot3l(m, x)


def _linl_fwd(x, m, mt):
    return _dot3l(m, x), (m, mt)


def _linl_bwd(res, g):
    m, mt = res
    return _dot3l(mt, g), jnp.zeros_like(m), jnp.zeros_like(mt)


_linl.defvjp(_linl_fwd, _linl_bwd)


def _ssd_chunk(g, xs, bm, cm, z, dtraw, dt_bias, a_log, dskip, ng, st0, st1, st2):
    n = xs.shape[0]
    lane = lax.broadcasted_iota(jnp.int32, (1, LANES), 1)
    sub = lax.broadcasted_iota(jnp.int32, (LANES, 1), 0)
    row = lax.broadcasted_iota(jnp.int32, (n, n), 0)
    col = lax.broadcasted_iota(jnp.int32, (n, n), 1)
    tril = row >= col
    tril_m = tril.astype(BF16)
    triu_m = (row <= col).astype(BF16)
    lane_lo = lane < HEAD
    sub_lo = sub < HEAD
    dt = _softplus(dtraw + dt_bias)
    da = dt * (-jnp.exp(a_log))
    acum = _linl(da, tril_m, triu_m)
    acum_t = acum.T
    scores = lax.dot_general(cm.astype(BF16), bm.astype(BF16), (((1,), (1,)), ((), ())), preferred_element_type=F32)

    def head(h):
        sel = lane == h
        acol = jnp.sum(jnp.where(sel, acum, 0.0), axis=1, keepdims=True)
        arow = jnp.sum(jnp.where(sub == h, acum_t, 0.0), axis=0, keepdims=True)
        dtcol = jnp.sum(jnp.where(sel, dt, 0.0), axis=1, keepdims=True)
        dsk = jnp.sum(jnp.where(sel, dskip, 0.0), axis=1, keepdims=True)
        decay = jnp.exp(jnp.where(tril, acol - arow, -jnp.inf))
        alast = acol[n - 1:n, :]
        return acol, dtcol, dsk, decay, alast

    ys, new = [], []
    for q, st in enumerate((st0, st1, st2)):
        a_acol, a_dt, a_dsk, a_decay, a_last = head(g * 6 + 2 * q)
        b_acol, b_dt, b_dsk, b_decay, b_last = head(g * 6 + 2 * q + 1)
        xp = xs[:, q * LANES:(q + 1) * LANES]
        xdt = xp * jnp.where(lane_lo, a_dt, b_dt)
        yd = jnp.where(lane_lo, _dot_bf16(scores * a_decay, xdt), _dot_bf16(scores * b_decay, xdt))
        xw = xdt * jnp.where(lane_lo, jnp.exp(a_last - a_acol), jnp.exp(b_last - b_acol))
        states = lax.dot_general(xw.astype(BF16), bm.astype(BF16), (((0,), (0,)), ((), ())), preferred_element_type=F32)
        yo = lax.dot_general(cm.astype(BF16), st.astype(BF16), (((1,), (1,)), ((), ())), preferred_element_type=F32)
        yo = yo * jnp.where(lane_lo, jnp.exp(a_acol), jnp.exp(b_acol))
        new.append(st * jnp.where(sub_lo, jnp.exp(a_last), jnp.exp(b_last)) + states)
        ys.append(yd + yo + xp * jnp.where(lane_lo, a_dsk, b_dsk))
    y = jnp.concatenate(ys, axis=1)
    y = y * (z * jax.nn.sigmoid(z))
    y = y * lax.rsqrt(jnp.mean(y * y, axis=-1, keepdims=True) + NORM_EPS) * ng
    return y, new[0], new[1], new[2]


def _ssd_specs(nc, rev):
    cidx = (lambda c: nc - 1 - c) if rev else (lambda c: c)
    gw = 3 * LANES
    return [
        pl.BlockSpec((SSD_CHUNK, gw), lambda c, g: (cidx(c), g)),
        pl.BlockSpec((SSD_CHUNK, LANES), lambda c, g: (cidx(c), 12 + g)),
        pl.BlockSpec((SSD_CHUNK, LANES), lambda c, g: (cidx(c), 16 + g)),
        pl.BlockSpec((SSD_CHUNK, gw), lambda c, g: (cidx(c), g)),
        pl.BlockSpec((SSD_CHUNK, LANES), lambda c, g: (cidx(c), 36)),
        pl.BlockSpec((1, LANES), lambda c, g: (0, 0)),
        pl.BlockSpec((1, LANES), lambda c, g: (0, 0)),
        pl.BlockSpec((1, LANES), lambda c, g: (0, 0)),
        pl.BlockSpec((1, gw), lambda c, g: (0, g)),
    ], cidx


def _ssd_fwd(conv, proj, dt_bias, a_log, dskip, norm_g, *, name, carry=None):
    s = conv.shape[0]
    nc = s // SSD_CHUNK
    in_specs, _ = _ssd_specs(nc, False)

    def body(xs, bm, cm, z, dtr, dtb, alog, dsk, ng, y_ref, sv_ref, st):
        c, g = pl.program_id(0), pl.program_id(1)

        @pl.when(c == 0)
        def _():
            for q in range(3):
                st[g * 3 + q] = jnp.zeros((LANES, LANES), F32)

        olds = [st[g * 3 + q] for q in range(3)]
        for q in range(3):
            sv_ref[0, 0, q] = olds[q]
        y, n0, n1, n2 = _ssd_chunk(g, xs[...], bm[...], cm[...], z[...], dtr[...], dtb[...], alog[...], dsk[...], ng[...], *olds)
        y_ref[...] = y
        for q, v in enumerate((n0, n1, n2)):
            st[g * 3 + q] = v

    return _call_with_exchange(
        body, (nc, 4), carry, [conv, conv, conv, proj, proj, dt_bias, a_log, dskip, norm_g], in_specs,
        [pl.BlockSpec((SSD_CHUNK, 3 * LANES), lambda c, g: (c, g)), pl.BlockSpec((1, 1, 3, LANES, LANES), lambda c, g: (c, g, 0, 0, 0))],
        [jax.ShapeDtypeStruct((s, 12 * LANES), F32), jax.ShapeDtypeStruct((nc, 4, 3, LANES, LANES), F32)],
        [pltpu.VMEM((12, LANES, LANES), F32)], name)


def _ssd_bwd(conv, proj, dt_bias, a_log, dskip, norm_g, saved, dy, *, name, carry=None):
    s = conv.shape[0]
    nc = s // SSD_CHUNK
    in_specs, cidx = _ssd_specs(nc, True)
    gw = 3 * LANES
    in_specs += [pl.BlockSpec((1, 1, 3, LANES, LANES), lambda c, g: (cidx(c), g, 0, 0, 0)),
                 pl.BlockSpec((SSD_CHUNK, gw), lambda c, g: (cidx(c), g))]

    def body(xs, bm, cm, z, dtr, dtb, alog, dsk, ng, sv, dy_ref, dxs, dbm, dcm, dz, ddt, ddtb, dalog, ddsk, dng, dst):
        c, g = pl.program_id(0), pl.program_id(1)

        @pl.when(c == 0)
        def _():
            for q in range(3):
                dst[g * 3 + q] = jnp.zeros((LANES, LANES), F32)

        @pl.when((c == 0) & (g == 0))
        def _():
            ddtb[...] = jnp.zeros_like(ddtb)
            dalog[...] = jnp.zeros_like(dalog)
            ddsk[...] = jnp.zeros_like(ddsk)
            dng[...] = jnp.zeros_like(dng)

        @pl.when(g == 0)
        def _():
            ddt[...] = jnp.zeros_like(ddt)

        olds = [sv[0, 0, q] for q in range(3)]
        _, vjp = jax.vjp(functools.partial(_ssd_chunk, g), xs[...], bm[...], cm[...], z[...], dtr[...], dtb[...], alog[...],
                         dsk[...], ng[...], *olds)
        gr = vjp((dy_ref[...], dst[g * 3], dst[g * 3 + 1], dst[g * 3 + 2]))
        dxs[...], dbm[...], dcm[...], dz[...] = gr[0], gr[1], gr[2], gr[3]
        ddt[...] += gr[4]
        ddtb[...] += gr[5]
        dalog[...] += gr[6]
        ddsk[...] += gr[7]
        dng[g] += gr[8]
        for q in range(3):
            dst[g * 3 + q] = gr[9 + q]

    const = lambda shape: pl.BlockSpec(shape, lambda c, g: (0,) * len(shape))
    return _call_with_exchange(
        body, (nc, 4), carry, [conv, conv, conv, proj, proj, dt_bias, a_log, dskip, norm_g, saved, dy], in_specs,
        [pl.BlockSpec((SSD_CHUNK, gw), lambda c, g: (cidx(c), g)),
         pl.BlockSpec((SSD_CHUNK, LANES), lambda c, g: (cidx(c), g)),
         pl.BlockSpec((SSD_CHUNK, LANES), lambda c, g: (cidx(c), g)),
         pl.BlockSpec((SSD_CHUNK, gw), lambda c, g: (cidx(c), g)),
         pl.BlockSpec((SSD_CHUNK, LANES), lambda c, g: (cidx(c), 0)),
         const((1, LANES)), const((1, LANES)), const((1, LANES)), const((4, 1, gw))],
        [jax.ShapeDtypeStruct((s, 12 * LANES), F32), jax.ShapeDtypeStruct((s, 4 * LANES), F32),
         jax.ShapeDtypeStruct((s, 4 * LANES), F32), jax.ShapeDtypeStruct((s, 12 * LANES), F32),
         jax.ShapeDtypeStruct((s, LANES), F32), jax.ShapeDtypeStruct((1, LANES), F32),
         jax.ShapeDtypeStruct((1, LANES), F32), jax.ShapeDtypeStruct((1, LANES), F32),
         jax.ShapeDtypeStruct((4, 1, gw), F32)],
        [pltpu.VMEM((12, LANES, LANES), F32)], name)


MXU_TILE = 256
RWKV_GROUP = 8
RWKV_UNROLL = 8


def _rwkv_consts():
    lanes = 16 * HEAD
    hl = jnp.arange(lanes) // HEAD
    e = (jnp.arange(16)[:, None] == hl[None, :]).astype(BF16)
    return e, e.T, _head_ones(MXU_TILE)


_RWKV_LANE_GROUPS = tuple(slice(i * MXU_TILE, (i + 1) * MXU_TILE) for i in range(16 * HEAD // MXU_TILE))


def _head_sums(x, j):
    return jnp.dot(x.astype(BF16), j, preferred_element_type=F32)


def _fold8(x):
    return jnp.sum(x.reshape(x.shape[0] // 8, 8, x.shape[1]), axis=0)


def _rwkv_expand(src3, dst, e, t_):
    for g0 in range(0, t_, RWKV_GROUP):
        n = min(RWKV_GROUP, t_ - g0)
        flat = src3[g0:g0 + n].reshape(n * HEAD, e.shape[0])
        dst[g0:g0 + n] = jnp.dot(flat, e, preferred_element_type=F32).reshape(n, HEAD, e.shape[1])


def _rwkv_reduce(src, dst3, et, t_):
    for g0 in range(0, t_, RWKV_GROUP):
        n = min(RWKV_GROUP, t_ - g0)
        x = src[g0:g0 + n].reshape(n * HEAD, et.shape[0])
        dst3[g0:g0 + n] = jnp.dot(x.astype(BF16), et, preferred_element_type=F32).reshape(n, HEAD, 16)


def _rwkv_fwd(w, kk, b, k, fp, v3, *, name, carry=None):
    s, lanes = w.shape
    t_ = min(RWKV_CHUNK, s)
    nc = s // t_
    e, et, j = _rwkv_consts()
    rowspec = pl.BlockSpec((t_, lanes), lambda c: (c, 0))
    cspec = lambda a: pl.BlockSpec(a.shape, lambda c: (0, 0))

    def body(w_ref, kk_ref, b_ref, k_ref, r_ref, v_ref, e_ref, et_ref, j_ref, y_ref, sv_ref, st, vm, zz):
        c = pl.program_id(0)

        @pl.when(c == 0)
        def _():
            st[...] = jnp.zeros_like(st)

        sv_ref[0] = st[...]
        jv = j_ref[...]
        _rwkv_expand(v_ref, vm, e_ref[...], t_)

        def step(t, carry):
            for grp in _RWKV_LANE_GROUPS:
                row = lambda ref: ref[pl.ds(t, 1), grp]
                sm = st[:, grp]
                sa = _head_sums(sm * (-row(kk_ref)), jv)
                sn = sm * row(w_ref) + sa * row(b_ref) + vm[t, :, grp] * row(k_ref)
                st[:, grp] = sn
                zz[t, :, grp] = sn * row(r_ref)
            return carry

        lax.fori_loop(0, t_, step, 0, unroll=RWKV_UNROLL)
        _rwkv_reduce(zz, y_ref, et_ref[...], t_)

    return _call_with_exchange(
        body, nc, carry, [w, kk, b, k, fp, v3, e, et, j],
        [rowspec] * 5 + [pl.BlockSpec((t_, HEAD, 16), lambda c: (c, 0, 0)), cspec(e), cspec(et), cspec(j)],
        [pl.BlockSpec((t_, HEAD, 16), lambda c: (c, 0, 0)), pl.BlockSpec((1, HEAD, lanes), lambda c: (c, 0, 0))],
        [jax.ShapeDtypeStruct((s, HEAD, 16), F32), jax.ShapeDtypeStruct((nc, HEAD, lanes), F32)],
        [pltpu.VMEM((HEAD, lanes), F32), pltpu.VMEM((t_, HEAD, lanes), F32), pltpu.VMEM((t_, HEAD, lanes), F32)], name)


def _call_with_exchange(body, grid, carry, ins, in_specs, out_specs, out_shape, scratch, name):
    grid = (grid,) if isinstance(grid, int) else tuple(grid)
    if carry is not None:
        src, gather = carry
        n_in, n_out, inner = len(ins), len(out_shape), body

        def body(*refs):
            ex = _make_exchange(refs[n_in], refs[n_in + 1 + n_out], refs[-3:], gather)
            steps = [pl.program_id(d) for d in range(len(grid))]

            @pl.when(functools.reduce(jnp.logical_and, [s == 0 for s in steps]))
            def _():
                ex.start()

            inner(*refs[:n_in], *refs[n_in + 1:n_in + 1 + n_out], *refs[n_in + 2 + n_out:-3])

            @pl.when(functools.reduce(jnp.logical_and, [s == n - 1 for s, n in zip(steps, grid)]))
            def _():
                ex.wait()

        hbm = pl.BlockSpec(memory_space=pl.ANY)
        ins, in_specs, out_specs = list(ins) + [src], list(in_specs) + [hbm], list(out_specs) + [hbm]
        out_shape = list(out_shape) + [jax.ShapeDtypeStruct((N_DEV, src.shape[-2], LANES), src.dtype)]
        scratch = list(scratch) + list(_Exchange.SCRATCH)
    return pl.pallas_call(body, grid=grid, in_specs=in_specs, out_specs=out_specs, out_shape=out_shape,
                          scratch_shapes=scratch, compiler_params=_cparams(*["arbitrary"] * len(grid)), name=name)(*ins)


def _rwkv_bwd(w, kk, b, k, fp, v3, saved, dy3, *, name, carry=None):
    s, lanes = w.shape
    t_ = min(RWKV_CHUNK, s)
    nc = s // t_
    e, et, j = _rwkv_consts()
    rev = lambda c: nc - 1 - c
    rowspec = pl.BlockSpec((t_, lanes), lambda c: (rev(c), 0))
    v3spec = pl.BlockSpec((t_, HEAD, 16), lambda c: (rev(c), 0, 0))
    s3spec = v3spec
    cspec = lambda a: pl.BlockSpec(a.shape, lambda c: (0, 0))

    def body(w_ref, kk_ref, b_ref, k_ref, r_ref, v_ref, sv_ref, dy_ref, e_ref, et_ref, j_ref,
             dw_ref, dkk_ref, db_ref, dk_ref, dr_ref, dv_ref, dst, h_sm, h_sa, vm, dz, pw, pkk, pb, pk, pr):
        c = pl.program_id(0)

        @pl.when(c == 0)
        def _():
            dst[...] = jnp.zeros_like(dst)

        jv = j_ref[...]
        _rwkv_expand(v_ref, vm, e_ref[...], t_)
        _rwkv_expand(dy_ref, dz, e_ref[...], t_)
        h_sm[0] = sv_ref[0]

        def replay(t, carry):
            for grp in _RWKV_LANE_GROUPS:
                row = lambda ref: ref[pl.ds(t, 1), grp]
                sm = h_sm[t, :, grp]
                sa = _head_sums(sm * (-row(kk_ref)), jv)
                h_sa[t, :, grp] = sa
                h_sm[t + 1, :, grp] = sm * row(w_ref) + sa * row(b_ref) + vm[t, :, grp] * row(k_ref)
            return carry

        lax.fori_loop(0, t_, replay, 0, unroll=RWKV_UNROLL)

        def back(i, carry):
            t = t_ - 1 - i
            for grp in _RWKV_LANE_GROUPS:
                row = lambda ref: ref[pl.ds(t, 1), grp]
                sm, sa, dzt = h_sm[t, :, grp], h_sa[t, :, grp], dz[t, :, grp]
                dsn = dst[:, grp] + dzt * row(r_ref)
                pr[t, :, grp] = _fold8(dzt * h_sm[t + 1, :, grp])
                pw[t, :, grp] = _fold8(dsn * sm)
                pb[t, :, grp] = _fold8(dsn * sa)
                pk[t, :, grp] = _fold8(dsn * vm[t, :, grp])
                vm[t, :, grp] = dsn * row(k_ref)
                dx = _head_sums(dsn * row(b_ref), jv)
                pkk[t, :, grp] = _fold8(dx * sm)
                dst[:, grp] = dsn * row(w_ref) - dx * row(kk_ref)
            return carry

        lax.fori_loop(0, t_, back, 0, unroll=RWKV_UNROLL)
        _rwkv_reduce(vm, dv_ref, et_ref[...], t_)
        dw_ref[...] = jnp.sum(pw[...], axis=1)
        dkk_ref[...] = -jnp.sum(pkk[...], axis=1)
        db_ref[...] = jnp.sum(pb[...], axis=1)
        dk_ref[...] = jnp.sum(pk[...], axis=1)
        dr_ref[...] = jnp.sum(pr[...], axis=1)

    big = lambda n: pltpu.VMEM((n, HEAD, lanes), F32)
    part = pltpu.VMEM((t_, 8, lanes), F32)
    return _call_with_exchange(
        body, nc, carry, [w, kk, b, k, fp, v3, saved, dy3, e, et, j],
        [rowspec] * 5 + [s3spec, pl.BlockSpec((1, HEAD, lanes), lambda c: (rev(c), 0, 0)), s3spec, cspec(e), cspec(et), cspec(j)],
        [rowspec] * 5 + [v3spec],
        [jax.ShapeDtypeStruct((s, lanes), F32)] * 5 + [jax.ShapeDtypeStruct((s, HEAD, 16), F32)],
        [pltpu.VMEM((HEAD, lanes), F32), big(t_ + 1), big(t_), big(t_), big(t_)] + [part] * 5, name)


def _blockdiag(blocks):
    g, a, b = blocks.shape
    on_diag = (jnp.arange(g)[:, None, None, None] == jnp.arange(g)[None, None, :, None])
    return jnp.where(on_diag, blocks[:, :, None, :], 0).reshape(g * a, g * b)


def _blockdiag_t(dense, g):
    a, b = dense.shape[0] // g, dense.shape[1] // g
    on_diag = (jnp.arange(g)[:, None, None, None] == jnp.arange(g)[None, None, :, None])
    return jnp.sum(jnp.where(on_diag, dense.reshape(g, a, g, b), 0), axis=2)


def _pad_cols(x, n):
    return jnp.pad(x, ((0, 0), (0, n - x.shape[1])))


def _pad_rows(x, n):
    return jnp.pad(x, ((0, n - x.shape[0]), (0, 0)))


E_PROJ = 5120


def _even_in_cols(w):
    return jnp.concatenate([w[:, 512:2048], w[:, 0:512], w[:, 2048:4632], jnp.zeros((w.shape[0], E_PROJ - 4632), w.dtype)], axis=1)


def _even_in_cols_t(dw):
    return jnp.concatenate([dw[:, 1536:2048], dw[:, 0:1536], dw[:, 2048:4632]], axis=1)


O_PROJ = 5632


def _odd_in_cols(w):
    z32 = jnp.zeros((w.shape[0], 32), w.dtype)
    return jnp.concatenate([w[:, 0:3072], w[:, 3520:5568], w[:, 3264:3520], w[:, 3072:3168], z32, w[:, 3168:3264], z32], axis=1)


def _odd_in_cols_t(dw):
    return jnp.concatenate([dw[:, 0:3072], dw[:, 5376:5472], dw[:, 5504:5600], dw[:, 5120:5376], dw[:, 3072:5120]], axis=1)


def _mu_cols(mu):
    z32 = jnp.zeros((1, 32), mu.dtype)
    return jnp.concatenate([mu[:, 0:3072], mu[:, 3264:3520], mu[:, 3072:3168], z32, mu[:, 3168:3264], z32], axis=1)


def _mu_cols_t(d):
    return jnp.concatenate([d[:, 0:3072], d[:, 3328:3424], d[:, 3456:3552], d[:, 3072:3328]], axis=1)


def _conv_taps(x, w, b):
    y = b + w[3:4] * x
    for k in range(3):
        y = y + w[k:k + 1] * _shift_down(x, 3 - k)
    return y


def _conv_silu(x, w, b):
    y = _conv_taps(x, w, b)
    return y * jax.nn.sigmoid(y)


def _tshift(x, mu):
    return x + (_shift_down(x, 1) - x) * mu


def _pl_gate(_, h, gl, e):
    return (h + jax.nn.sigmoid(gl) * e,)


def _s5_param(lr, li, ls, br, bi):
    step = jnp.exp(ls)
    mag = jnp.exp(lr * step)
    ar, ai = mag * jnp.cos(li * step), mag * jnp.sin(li * step)
    den = lr * lr + li * li
    nr = ar - 1.0
    cr = (nr * lr + ai * li) / den
    ci = (ai * lr - nr * li) / den
    return ar, ai, cr * br - ci * bi, cr * bi + ci * br


def _s5_post(_, ylin, u, d, gw, gb):
    act = jax.nn.gelu(ylin + d * u)
    return (act * jax.nn.sigmoid(_dot_bf16(act, gw) + gb),)


def _rwkv_pre(_, k, gl, wl, al, w0, w_up, a0, a_up, g_up, k_k, k_a, j):
    w = -_softplus(-(w0 + _dot_bf16(jnp.tanh(wl), w_up))) - 0.5
    decay = jnp.exp(-jnp.exp(w))
    a = jax.nn.sigmoid(a0 + _dot_bf16(al, a_up))
    g = _dot_bf16(jax.nn.sigmoid(gl), g_up)
    kk = k * k_k
    k2 = k * (1.0 + (a - 1.0) * k_a)
    kkn = kk * lax.rsqrt(jnp.maximum(_lin(kk * kk, j, j), 1e-24))
    return decay, kkn, kkn * a, k2, g


def _rwkv_post(_, y, r, k2, v, g, r_k, ln_g, ln_b, j):
    mean = _lin(y, j, j) * (1.0 / HEAD)
    yc = y - mean
    var = _lin(yc * yc, j, j) * (1.0 / HEAD)
    yn = yc * lax.rsqrt(var + RWKV_GN_EPS) * ln_g + ln_b
    return ((yn + _lin(r * k2 * r_k, j, j) * v) * g,)


def _lru_pre(row0, pre, xc, bax, lam):
    n = xc.shape[1]
    gr = jax.nn.sigmoid(pre[:, :n] + bax[:, :n])
    gi = jax.nn.sigmoid(pre[:, n:] + bax[:, n:])
    log_a = -LRU_C * gr * _softplus(-lam)
    m2 = -jnp.tanh(log_a) * (jnp.exp(2.0 * log_a) + 1.0)
    mult = jnp.sqrt(jnp.maximum(m2, 0.0))
    rowid = row0 + lax.broadcasted_iota(jnp.int32, (xc.shape[0], 1), 0)
    mult = jnp.where(rowid == 0, 1.0, mult)
    return jnp.exp(log_a), xc * gi * mult


def _lru_post(_, h, gl2):
    return (h * jax.nn.gelu(gl2),)


def _even_prep(w):
    sp = (w["s5_lam_re"].reshape(32, 64), w["s5_lam_im"].reshape(32, 64), w["s5_log_step"].reshape(32, 1),
          w["s5_b_re"].reshape(32, 64, 16).transpose(2, 0, 1), w["s5_b_im"].reshape(32, 64, 16).transpose(2, 0, 1))
    ar, ai, bbr, bbi = _block_vjp(_s5_param, sp, None, name="s5_param")
    bblk = lambda bb: _blockdiag(bb.transpose(1, 0, 2))
    cblk = lambda c: _blockdiag(c.reshape(32, 16, 64).transpose(0, 2, 1))
    pad = lambda x: _pad_cols(x.reshape(1, 24), LANES)
    return dict(
        sp=sp, a_row=jnp.concatenate([ar.reshape(1, 2048), ai.reshape(1, 2048)], axis=1),
        b_re=bblk(bbr), b_im=bblk(bbi), c_re=cblk(w["s5_c_re"]), c_imn=-cblk(w["s5_c_im"]),
        d=w["s5_d"].reshape(1, 512), gw=w["s5_glu_w"].reshape(512, 512), gb=w["s5_glu_b"].reshape(1, 512),
        conv_w=w["ssd_conv_w"].reshape(4, 2560), conv_b=w["ssd_conv_b"].reshape(1, 2560),
        dt_bias=pad(w["ssd_dt_bias"]), a_log=pad(w["ssd_a_log"]), dskip=pad(w["ssd_d"]), norm=w["ssd_norm"].reshape(1, 1536))


_E_XMAP = lambda j: 16 + j


def _even_fwd(proj, p, carry=None):
    u = proj[:, 1536:2048]
    bur = _matmul(u, p["b_re"], "nn", name="s5_bu_re")
    bui = _matmul(u, p["b_im"], "nn", name="s5_bu_im")
    xr, xi = _cscan_fwd(jnp.concatenate([bur, bui], axis=1), p["a_row"], name="s5_scan")
    ylin = _matmul(xi, p["c_imn"], "nn", name="s5_y_im", add=_matmul(xr, p["c_re"], "nn", name="s5_y_re"))
    (ya,) = _rw_fwd(_s5_post, [_row(ylin), _row(u)], [p["d"], p["gw"], p["gb"]], [(512, F32)], name="s5_post")
    conv = _ct_fwd(_conv_silu, proj, _E_XMAP, 20, [(p["conv_w"], 0), (p["conv_b"], 0)], 2560, name="ssd_conv")
    yb, saved, *carried = _ssd_fwd(conv, proj, p["dt_bias"], p["a_log"], p["dskip"], p["norm"], name="ssd_scan", carry=carry)
    return jnp.concatenate([ya, yb], axis=1), (u, xr, xi, ylin, conv, saved), (carried[0] if carried else None)


def _even_bwd(proj, p, res, dy, carry=None):
    u, xr, xi, ylin, conv, saved = res
    s = proj.shape[0]
    dxs, dbm, dcm, dz, ddt, ddtb, dalog, ddsk, dng, *carried = _ssd_bwd(
        conv, proj, p["dt_bias"], p["a_log"], p["dskip"], p["norm"], saved, dy[:, 512:], name="ssd_scan_bwd", carry=carry)
    dxbc, (dcw, dcb) = _ct_bwd(_conv_silu, proj, _E_XMAP, 20, [(p["conv_w"], 0), (p["conv_b"], 0)],
                               [jnp.concatenate([dxs, dbm, dcm], axis=1)], name="ssd_conv_bwd")
    (dylin, du), (dd, dgw, dgb) = _rw_bwd(_s5_post, [_row(ylin), _row(u)], [p["d"], p["gw"], p["gb"]], [dy[:, :512]], name="s5_post_bwd")
    dxr = _matmul(dylin, p["c_re"], "nt", name="s5_dxr")
    dxi = _matmul(dylin, p["c_imn"], "nt", name="s5_dxi")
    dc_re = _matmul(xr, dylin, "tn", name="s5_dc_re")
    dc_imn = _matmul(xi, dylin, "tn", name="s5_dc_im")
    dbr, dbi, dar, dai = _cscan_bwd(dxr, dxi, xr, xi, p["a_row"], name="s5_scan_bwd")
    du = _matmul(dbr, p["b_re"], "nt", name="s5_du_re", add=du)
    du = _matmul(dbi, p["b_im"], "nt", name="s5_du_im", add=du)
    db_re = _matmul(u, dbr, "tn", name="s5_db_re")
    db_im = _matmul(u, dbi, "tn", name="s5_db_im")
    unblk = lambda d: _blockdiag_t(d, 32).transpose(1, 0, 2)
    g_sp = _block_vjp(_s5_param, p["sp"], (dar.reshape(32, 64), dai.reshape(32, 64), unblk(db_re), unblk(db_im)), name="s5_param_bwd")
    dproj = jnp.concatenate([dz, du, dxbc, ddt, jnp.zeros((s, E_PROJ - 4736), F32)], axis=1)
    uncblk = lambda d: _blockdiag_t(d, 32).transpose(0, 2, 1)
    grads = dict(
        s5_lam_re=g_sp[0].reshape(1, 32, 64), s5_lam_im=g_sp[1].reshape(1, 32, 64), s5_log_step=g_sp[2].reshape(1, 32),
        s5_b_re=g_sp[3].transpose(1, 2, 0)[None], s5_b_im=g_sp[4].transpose(1, 2, 0)[None],
        s5_c_re=uncblk(dc_re)[None], s5_c_im=-uncblk(dc_imn)[None], s5_d=dd, s5_glu_w=dgw[None], s5_glu_b=dgb,
        ssd_conv_w=dcw[None], ssd_conv_b=dcb, ssd_dt_bias=ddtb[:, :24], ssd_a_log=dalog[:, :24], ssd_d=ddsk[:, :24],
        ssd_norm=dng.reshape(1, 1536))
    return dproj, grads, (carried[0] if carried else None)


def _odd_prep(w):
    pad128 = lambda x: _pad_rows(x, LANES)
    return dict(
        mu=_mu_cols(w["rwkv_mu"].reshape(1, 3520)), w0=w["rwkv_w0"].reshape(1, 1024), w_up=pad128(w["rwkv_w_up"].reshape(96, 1024)),
        a0=w["rwkv_a0"].reshape(1, 1024), a_up=pad128(w["rwkv_a_up"].reshape(96, 1024)), g_up=w["rwkv_g_up"].reshape(256, 1024),
        k_k=w["rwkv_k_k"].reshape(1, 1024), k_a=w["rwkv_k_a"].reshape(1, 1024), r_k=w["rwkv_r_k"].reshape(1, 1024),
        ln_g=w["rwkv_ln_g"].reshape(1, 1024), ln_b=w["rwkv_ln_b"].reshape(1, 1024), j=_head_ones(1024),
        conv_w=w["lru_conv_w"].reshape(4, 1024), conv_b=w["lru_conv_b"].reshape(1, 1024),
        wax=jnp.concatenate([_blockdiag(w["lru_w_a"].reshape(16, 64, 64)), _blockdiag(w["lru_w_x"].reshape(16, 64, 64))], axis=1),
        bax=jnp.concatenate([w["lru_b_a"].reshape(1, 1024), w["lru_b_x"].reshape(1, 1024)], axis=1), lam=w["lru_lam"].reshape(1, 1024))


_O_XMAP = lambda j: jnp.where(j < 24, j, j + 16)
_O_LMAP = lambda j: 24 + j


def _to_heads(x):
    return x.reshape(x.shape[0], 16, HEAD).transpose(0, 2, 1)


def _from_heads(x3):
    return x3.transpose(0, 2, 1).reshape(x3.shape[0], 16 * HEAD)


def _odd_rows(fp, y, k2, g):
    pre = [_row(fp, 1024, 1), _row(fp, 256, 12), _row(fp, 128, 26), _row(fp, 128, 27)]
    post = None if y is None else [_row(y), _row(fp, 1024, 0), _row(k2), _row(fp, 1024, 2), _row(g)]
    return pre, post


def _odd_fwd(proj, p, carry=None):
    fp = _ct_fwd(_tshift, proj, _O_XMAP, 28, [(p["mu"], 0)], 3584, name="rwkv_shift")
    pre_rows, _ = _odd_rows(fp, None, None, None)
    pre_params = [p["w0"], p["w_up"], p["a0"], p["a_up"], p["g_up"], p["k_k"], p["k_a"], p["j"]]
    decay, kkn, b, k2, g = _rw_fwd(_rwkv_pre, pre_rows, pre_params, [(1024, F32)] * 5, name="rwkv_pre")
    v3 = _to_heads(fp[:, 2048:3072]).astype(BF16)
    y3, saved, *carried = _rwkv_fwd(decay, kkn, b, k2, fp, v3, name="rwkv_scan", carry=carry)
    y = _from_heads(y3)
    _, post_rows = _odd_rows(fp, y, k2, g)
    (yc,) = _rw_fwd(_rwkv_post, post_rows, [p["r_k"], p["ln_g"], p["ln_b"], p["j"]], [(1024, F32)], name="rwkv_post")
    xc = _ct_fwd(_conv_taps, proj, _O_LMAP, 8, [(p["conv_w"], 0), (p["conv_b"], 0)], 1024, name="lru_conv")
    pre = _matmul(xc, p["wax"], "nn", name="lru_gates")
    a, bx = _rw_fwd(_lru_pre, [_row(pre), _row(xc)], [p["bax"], p["lam"]], [(1024, F32)] * 2, name="lru_pre")
    hseq = _rscan_fwd(a, bx, name="lru_scan")
    (yd,) = _rw_fwd(_lru_post, [_row(hseq), _row(proj, 1024, 4)], [], [(1024, F32)], name="lru_post")
    return jnp.concatenate([yc, yd], axis=1), (fp, decay, kkn, b, k2, g, v3, saved, y, xc, pre, a, hseq), (carried[0] if carried else None)


def _odd_bwd(proj, p, res, dy, carry=None):
    fp, decay, kkn, b, k2, g, v3, saved, y, xc, pre, a, hseq = res
    s = proj.shape[0]
    (dh, dgl2), _ = _rw_bwd(_lru_post, [_row(hseq), _row(proj, 1024, 4)], [], [dy[:, 1024:]], name="lru_post_bwd")
    dbx, da = _rscan_bwd(dh, a, hseq, name="lru_scan_bwd")
    (dpre, dxc), (dbax, dlam) = _rw_bwd(_lru_pre, [_row(pre), _row(xc)], [p["bax"], p["lam"]], [da, dbx], name="lru_pre_bwd")
    dxc = _matmul(dpre, p["wax"], "nt", name="lru_gates_dx", add=dxc)
    dwax = _matmul(xc, dpre, "tn", name="lru_gates_dw")
    dxl, (dlcw, dlcb) = _ct_bwd(_conv_taps, proj, _O_LMAP, 8, [(p["conv_w"], 0), (p["conv_b"], 0)], [dxc], name="lru_conv_bwd")
    pre_rows, post_rows = _odd_rows(fp, y, k2, g)
    (dyn, dr1, dk2a, dv1, dg), (dr_k, dln_g, dln_b) = _rw_bwd(
        _rwkv_post, post_rows, [p["r_k"], p["ln_g"], p["ln_b"], p["j"]], [dy[:, :1024]], name="rwkv_post_bwd", param_grads=[0, 1, 2])
    ddecay, dkkn, db, dk2b, dr2, dv3, *carried = _rwkv_bwd(
        decay, kkn, b, k2, fp, v3, saved, _to_heads(dyn).astype(BF16), name="rwkv_scan_bwd", carry=carry)
    pre_params = [p["w0"], p["w_up"], p["a0"], p["a_up"], p["g_up"], p["k_k"], p["k_a"], p["j"]]
    (dk, dgl, dwl, dal), (dw0, dw_up, da0, da_up, dg_up, dk_k, dk_a) = _rw_bwd(
        _rwkv_pre, pre_rows, pre_params, [ddecay, dkkn, db, [dk2a, dk2b], dg], name="rwkv_pre_bwd", param_grads=list(range(7)))
    z = lambda n: jnp.zeros((s, n), F32)
    g1 = jnp.concatenate([dr1, dk, dv1, dgl, dwl, dal], axis=1)
    g2 = jnp.concatenate([dr2, z(1024), _from_heads(dv3), z(512)], axis=1)
    dfp, (dmu,) = _ct_bwd(_tshift, proj, _O_XMAP, 28, [(p["mu"], 0)], [g1, g2], name="rwkv_shift_bwd")
    dproj = jnp.concatenate([dfp[:, :3072], dxl, dgl2, dfp[:, 3072:]], axis=1)
    grads = dict(
        rwkv_mu=_mu_cols_t(dmu), rwkv_w0=dw0, rwkv_w_up=dw_up[:96][None], rwkv_a0=da0, rwkv_a_up=da_up[:96][None], rwkv_g_up=dg_up[None],
        rwkv_k_k=dk_k, rwkv_k_a=dk_a, rwkv_r_k=dr_k.reshape(1, 16, 64), rwkv_ln_g=dln_g, rwkv_ln_b=dln_b,
        lru_conv_w=dlcw[None], lru_conv_b=dlcb, lru_w_a=_blockdiag_t(dwax[:, :1024], 16)[None], lru_w_x=_blockdiag_t(dwax[:, 1024:], 16)[None],
        lru_b_a=dbax[:, :1024].reshape(1, 16, 64), lru_b_x=dbax[:, 1024:].reshape(1, 16, 64), lru_lam=dlam.reshape(1, 16, 64))
    return dproj, grads, (carried[0] if carried else None)


def _my_index():
    return 4 * lax.axis_index("x") + 2 * lax.axis_index("y") + lax.axis_index("c")


def _peer(k):
    x, y, c = lax.axis_index("x"), lax.axis_index("y"), lax.axis_index("c")
    return (1 - x if k & 4 else x, 1 - y if k & 2 else y, 1 - c if k & 1 else c)


class _Exchange:
    SCRATCH = (pltpu.SemaphoreType.DMA((N_DEV - 1,)), pltpu.SemaphoreType.DMA((N_DEV - 1,)), pltpu.SemaphoreType.DMA)

    def __init__(self, src_ref, out_ref, send_sems, recv_sems, local_sem, gather):
        me = _my_index()
        mine = src_ref if gather else src_ref.at[me]
        self.local = pltpu.make_async_copy(mine, out_ref.at[me], local_sem)
        rdma = lambda src, dst, k: pltpu.make_async_remote_copy(
            src_ref=src, dst_ref=dst, send_sem=send_sems.at[k - 1], recv_sem=recv_sems.at[k - 1],
            device_id=_peer(k), device_id_type=pl.DeviceIdType.MESH)
        ks = range(1, N_DEV)
        self.sends = [rdma(src_ref if gather else src_ref.at[jnp.bitwise_xor(me, k)], out_ref.at[me], k) for k in ks]
        self.arrivals = [rdma(mine, out_ref.at[jnp.bitwise_xor(me, k)], k) for k in ks]

    def start(self):
        self.local.start()
        for cp in self.sends:
            cp.start()

    def wait(self):
        for cp in self.arrivals:
            cp.wait_recv()
        for cp in self.sends:
            cp.wait_send()
        self.local.wait()


class _TwoLevelGather:
    def __init__(self, src_ref, out_ref, send_sems, recv_sems, local_sem):
        x, y, c = lax.axis_index("x"), lax.axis_index("y"), lax.axis_index("c")
        me, self.sibling, self.c = (x, y, c), (x, y, 1 - c), c
        self.me = me
        self.chips = [(1 - x, y), (x, 1 - y), (1 - x, 1 - y)]
        slab = lambda px, py, pc: out_ref.at[4 * px + 2 * py + pc]

        def copy(k, block, to, own=False):
            return pltpu.make_async_remote_copy(
                src_ref=src_ref if own else slab(*block), dst_ref=slab(*block), send_sem=send_sems.at[k], recv_sem=recv_sems.at[k],
                device_id=to, device_id_type=pl.DeviceIdType.MESH)

        self.copy = copy
        self.mine = pltpu.make_async_copy(src_ref, slab(*me), local_sem)
        self.first = [copy(0, me, self.sibling, own=True)] + [copy(1 + j, me, (*chip, c), own=True) for j, chip in enumerate(self.chips)]

    def start(self):
        self.mine.start()
        for cp in self.first:
            cp.start()

    def wait(self):
        passed = [self.copy(4 + j, (*chip, self.c), self.sibling) for j, chip in enumerate(self.chips)]
        for j, chip in enumerate(self.chips):
            self.copy(1 + j, (*chip, self.c), self.me).wait_recv()
            passed[j].start()
        self.copy(0, self.sibling, self.me).wait_recv()
        for j, chip in enumerate(self.chips):
            self.copy(4 + j, (*chip, 1 - self.c), self.me).wait_recv()
        for cp in self.first + passed:
            cp.wait_send()
        self.mine.wait()


TWO_LEVEL = "two-level gather"


def _make_exchange(src_ref, out_ref, sems, kind):
    return _TwoLevelGather(src_ref, out_ref, *sems) if kind == TWO_LEVEL else _Exchange(src_ref, out_ref, *sems, kind)


def _exchange(src, *, gather, name):
    def body(src_ref, out_ref, *sems):
        ex = _make_exchange(src_ref, out_ref, sems, gather)
        ex.start()
        ex.wait()

    return pl.pallas_call(
        body, out_shape=jax.ShapeDtypeStruct((N_DEV, src.shape[-2], LANES), src.dtype),
        in_specs=[pl.BlockSpec(memory_space=pl.ANY)], out_specs=pl.BlockSpec(memory_space=pl.ANY),
        scratch_shapes=list(_Exchange.SCRATCH), name=name)(src)


PACK_ALIGN = 16 * LANES
PACK_ROWS = 512


def _pack(arrs, dtype, lead=False):
    parts, rows = [], 0
    for a in arrs:
        n_lead = a.shape[0] if lead else 1
        n = a.size // n_lead
        a = a.astype(dtype)
        if n % PACK_ALIGN:
            a = jnp.pad(a.reshape(n_lead, n), ((0, 0), (0, -n % PACK_ALIGN)))
        parts.append(a.reshape(n_lead, -1, LANES))
        rows += parts[-1].shape[1]
    if rows % PACK_ROWS:
        parts.append(jnp.zeros((parts[0].shape[0], -rows % PACK_ROWS, LANES), dtype))
    buf = jnp.concatenate(parts, axis=1)
    return buf if lead else buf[0]


def _unpack(buf, shapes, lead=False):
    buf = buf if lead else buf[None]
    out, off = [], 0
    for shp in shapes:
        n = math.prod(shp)
        rows = (n + (-n % PACK_ALIGN)) // LANES
        piece = buf[:, off:off + rows]
        if n % PACK_ALIGN:
            piece = piece.reshape(buf.shape[0], rows * LANES)[:, :n]
        out.append(piece.reshape(((buf.shape[0],) if lead else ()) + tuple(shp)))
        off += rows
    return out


def _unshard(parts, axis):
    moved = jnp.moveaxis(parts, 0, axis)
    shp = list(moved.shape)
    return moved.reshape(shp[:axis] + [shp[axis] * shp[axis + 1]] + shp[axis + 2:])


def _to_parts(full, axis):
    shp = list(full.shape)
    split = full.reshape(shp[:axis] + [N_DEV, shp[axis] // N_DEV] + shp[axis + 1:])
    return jnp.moveaxis(split, axis, 0)


def _adamw(gparts, w, m, v, *, name):
    r = w.shape[0]
    tr = _pick(r, (PACK_ROWS,))

    def body(g_ref, w_ref, m_ref, v_ref, go, do, mo, vo):
        g = g_ref[0].astype(F32)
        for d in range(1, N_DEV):
            g = g + g_ref[d].astype(F32)
        m1 = ADAM_B1 * m_ref[...] + (1.0 - ADAM_B1) * g
        v1 = ADAM_B2 * v_ref[...] + (1.0 - ADAM_B2) * jnp.square(g)
        m_hat = m1 / (1.0 - ADAM_B1 ** ADAM_STEP)
        v_hat = v1 / (1.0 - ADAM_B2 ** ADAM_STEP)
        go[...] = g
        do[...] = -ADAM_LR * (m_hat / (jnp.sqrt(v_hat) + ADAM_EPS) + ADAM_WD * w_ref[...])
        mo[...] = m1
        vo[...] = v1

    blk = pl.BlockSpec((tr, LANES), lambda i: (i, 0))
    return pl.pallas_call(
        body, grid=(r // tr,), in_specs=[pl.BlockSpec((N_DEV, tr, LANES), lambda i: (0, i, 0)), blk, blk, blk], out_specs=[blk] * 4,
        out_shape=[jax.ShapeDtypeStruct((r, LANES), F32)] * 4, compiler_params=_cparams("parallel"), name=name)(gparts, w, m, v)


def _loss_and_grad(h, g, tgt, *, name, ts=256):
    s, d = h.shape
    ts = min(ts, s)

    def tile_loss(hv, gv, tv):
        (y,) = _rmsnorm_tile(0, hv, gv)
        return 0.5 * jnp.sum(jnp.mean(jnp.square(y - tv), axis=-1))

    def body(h_ref, g_ref, t_ref, l_ref, dh_ref, dg_ref):
        @pl.when(pl.program_id(0) == 0)
        def _():
            l_ref[...] = jnp.zeros_like(l_ref)
            dg_ref[...] = jnp.zeros_like(dg_ref)

        tv = t_ref[...]
        loss, vjp = jax.vjp(lambda hv, gv: tile_loss(hv, gv, tv), h_ref[...], g_ref[...])
        dh, dg = vjp(jnp.ones((), F32))
        l_ref[...] += loss
        dh_ref[...] = dh
        dg_ref[...] += dg

    row = pl.BlockSpec((ts, d), lambda i: (i, 0))
    return pl.pallas_call(
        body, grid=(s // ts,), in_specs=[row, pl.BlockSpec((1, d), lambda i: (0, 0)), row],
        out_specs=[pl.BlockSpec((8, LANES), lambda i: (0, 0)), row, pl.BlockSpec((1, d), lambda i: (0, 0))],
        out_shape=[jax.ShapeDtypeStruct((8, LANES), F32), jax.ShapeDtypeStruct((s, d), F32), jax.ShapeDtypeStruct((1, d), F32)],
        compiler_params=_cparams("arbitrary"), name=name)(h, g, tgt)


def _norm(h, g, name):
    return _rw_fwd(_rmsnorm_tile, [_row(h)], [g], [(h.shape[1], F32)], name=name)[0]


def _norm_bwd(h, g, dhn, dres, name):
    (dh,), (dg,) = _rw_bwd(_rmsnorm_tile, [_row(h)], [g], [dhn], add_rows={0: dres}, name=name)
    return dh, dg


def _layer_fwd(h, p_i, lw, mixer_fwd, mp, tag, carry=None, late=None):
    hn = _norm(h, lw["norm_mix"], f"{tag}_norm_mix")
    proj = _matmul(hn, lw["w_in"], "nn", name=f"{tag}_in_proj")
    y, mres, carried = mixer_fwd(proj, mp, carry)
    if late is not None:
        lw = {**lw, **late(carried)}
    h1 = _matmul(y, lw["w_out"], "nn", name=f"{tag}_out_proj", add=h)
    hn2 = _norm(h1, lw["norm_ffn"], f"{tag}_norm_ffn")
    act = _matmul(hn2, lw["w1"], "nn", name=f"{tag}_mlp_up", post=_relu2_of)
    h2 = _matmul(act, lw["w2"], "nn", name=f"{tag}_mlp_down", add=h1)
    hn3 = _norm(h2, lw["norm_pl"], f"{tag}_norm_pl")
    gl = _matmul(hn3, lw["w_gate"], "nn", name=f"{tag}_pl_gate")
    e = _matmul(p_i, lw["w_pl"], "nn", name=f"{tag}_pl_proj")
    (h3,) = _rw_fwd(_pl_gate, [_row(h2), _row(gl), _row(e)], [], [(h.shape[1], F32)], name=f"{tag}_pl_mix")
    return h3, (h, hn, proj, y, mres, h1, hn2, act, h2, hn3, gl, e), lw


def _layer_bwd(dh3, p_i, lw, mixer_bwd, mp, saved, tag, early=None):
    h, hn, proj, y, mres, h1, hn2, act, h2, hn3, gl, e = saved
    (dgl, de), _ = _rw_bwd(_pl_gate, [_row(h2), _row(gl), _row(e)], [], [dh3], row_grads=[1, 2], name=f"{tag}_pl_mix_bwd")
    g = dict(w_pl=_matmul(p_i, de, "tn", name=f"{tag}_pl_proj_dw"), w_gate=_matmul(hn3, dgl, "tn", name=f"{tag}_pl_gate_dw"))
    dhn3 = _matmul(dgl, lw["w_gate"], "nt", name=f"{tag}_pl_gate_dx")
    dh2, g["norm_pl"] = _norm_bwd(h2, lw["norm_pl"], dhn3, dh3, f"{tag}_norm_pl_bwd")
    da1 = _matmul(dh2, lw["w2"], "nt", name=f"{tag}_mlp_down_dx", post=_relu2_grad, aux=act)
    g["w2"] = _matmul(act, dh2, "tn", name=f"{tag}_mlp_down_dw")
    g["w1"] = _matmul(hn2, da1, "tn", name=f"{tag}_mlp_up_dw")
    dhn2 = _matmul(da1, lw["w1"], "nt", name=f"{tag}_mlp_up_dx")
    dh1, g["norm_ffn"] = _norm_bwd(h1, lw["norm_ffn"], dhn2, dh2, f"{tag}_norm_ffn_bwd")
    dy = _matmul(dh1, lw["w_out"], "nt", name=f"{tag}_out_proj_dx")
    g["w_out"] = _matmul(y, dh1, "tn", name=f"{tag}_out_proj_dw")
    dproj, mg, carried = mixer_bwd(proj, mp, mres, dy, None if early is None else early(g))
    g["w_in"] = _matmul(hn, dproj, "tn", name=f"{tag}_in_proj_dw")
    dhn = _matmul(dproj, lw["w_in"], "nt", name=f"{tag}_in_proj_dx")
    dh, g["norm_mix"] = _norm_bwd(h, lw["norm_mix"], dhn, dh1, f"{tag}_norm_mix_bwd")
    return dh, g, mg, carried


WEIGHTS = (
    ("norm_mix", None), ("norm_ffn", None), ("norm_pl", None), ("mlp_w1", 2), ("mlp_w2", 1), ("pl_proj", 2), ("pl_gate", 1),
    ("e_in_proj", 2), ("e_out_proj", 1), ("s5_lam_re", None), ("s5_lam_im", None), ("s5_log_step", None), ("s5_b_re", None),
    ("s5_b_im", None), ("s5_c_re", None), ("s5_c_im", None), ("s5_d", None), ("s5_glu_w", 1), ("s5_glu_b", None),
    ("ssd_conv_w", 2), ("ssd_conv_b", None), ("ssd_dt_bias", None), ("ssd_a_log", None), ("ssd_d", None), ("ssd_norm", None),
    ("o_in_proj", 2), ("o_out_proj", 1), ("rwkv_mu", 1), ("rwkv_w0", 1), ("rwkv_w_up", 2), ("rwkv_a0", 1), ("rwkv_a_up", 2),
    ("rwkv_g_up", 2), ("rwkv_k_k", 1), ("rwkv_k_a", 1), ("rwkv_r_k", None), ("rwkv_ln_g", 1), ("rwkv_ln_b", 1),
    ("lru_conv_w", 2), ("lru_conv_b", 1), ("lru_w_a", None), ("lru_b_a", None), ("lru_w_x", None), ("lru_b_x", None),
    ("lru_lam", None), ("norm_final", None))
MATMUL_WEIGHTS = ("mlp_w1", "mlp_w2", "pl_proj", "pl_gate", "e_in_proj", "e_out_proj", "s5_glu_w", "o_in_proj", "o_out_proj",
                  "rwkv_w_up", "rwkv_a_up", "rwkv_g_up")


LATE_WEIGHTS = (("mlp_w1", 1), ("mlp_w2", 1), ("pl_gate", 1), ("pl_proj", 1))
EARLY_GRADS = LATE_WEIGHTS + (("o_out_proj", 0),)


def kernel(x, p, norm_mix, norm_ffn, norm_pl, mlp_w1, mlp_w2, pl_proj, pl_gate, e_in_proj, e_out_proj, s5_lam_re, s5_lam_im, s5_log_step, s5_b_re, s5_b_im, s5_c_re, s5_c_im, s5_d, s5_glu_w, s5_glu_b, ssd_conv_w, ssd_conv_b, ssd_dt_bias, ssd_a_log, ssd_d, ssd_norm, o_in_proj, o_out_proj, rwkv_mu, rwkv_w0, rwkv_w_up, rwkv_a0, rwkv_a_up, rwkv_g_up, rwkv_k_k, rwkv_k_a, rwkv_r_k, rwkv_ln_g, rwkv_ln_b, lru_conv_w, lru_conv_b, lru_w_a, lru_b_a, lru_w_x, lru_b_x, lru_lam, norm_final, loss_target, m_norm_mix, m_norm_ffn, m_norm_pl, m_mlp_w1, m_mlp_w2, m_pl_proj, m_pl_gate, m_e_in_proj, m_e_out_proj, m_s5_lam_re, m_s5_lam_im, m_s5_log_step, m_s5_b_re, m_s5_b_im, m_s5_c_re, m_s5_c_im, m_s5_d, m_s5_glu_w, m_s5_glu_b, m_ssd_conv_w, m_ssd_conv_b, m_ssd_dt_bias, m_ssd_a_log, m_ssd_d, m_ssd_norm, m_o_in_proj, m_o_out_proj, m_rwkv_mu, m_rwkv_w0, m_rwkv_w_up, m_rwkv_a0, m_rwkv_a_up, m_rwkv_g_up, m_rwkv_k_k, m_rwkv_k_a, m_rwkv_r_k, m_rwkv_ln_g, m_rwkv_ln_b, m_lru_conv_w, m_lru_conv_b, m_lru_w_a, m_lru_b_a, m_lru_w_x, m_lru_b_x, m_lru_lam, m_norm_final, v_norm_mix, v_norm_ffn, v_norm_pl, v_mlp_w1, v_mlp_w2, v_pl_proj, v_pl_gate, v_e_in_proj, v_e_out_proj, v_s5_lam_re, v_s5_lam_im, v_s5_log_step, v_s5_b_re, v_s5_b_im, v_s5_c_re, v_s5_c_im, v_s5_d, v_s5_glu_w, v_s5_glu_b, v_ssd_conv_w, v_ssd_conv_b, v_ssd_dt_bias, v_ssd_a_log, v_ssd_d, v_ssd_norm, v_o_in_proj, v_o_out_proj, v_rwkv_mu, v_rwkv_w0, v_rwkv_w_up, v_rwkv_a0, v_rwkv_a_up, v_rwkv_g_up, v_rwkv_k_k, v_rwkv_k_a, v_rwkv_r_k, v_rwkv_ln_g, v_rwkv_ln_b, v_lru_conv_w, v_lru_conv_b, v_lru_w_a, v_lru_b_a, v_lru_w_x, v_lru_b_x, v_lru_lam, v_norm_final):
    a = dict(locals())
    d_model = x.shape[-1]
    row = lambda v: v.reshape(1, d_model)
    axis = dict(WEIGHTS)
    keys = [(n, i) for n, ax in WEIGHTS if ax is not None for i in range(a[n].shape[0])]
    shard = lambda pre, key: a[pre + key[0]][key[1]]
    piece_axis = lambda key: axis[key[0]] - 1

    def gathered(got, ks):
        parts = _unpack(got, [shard("", k).shape for k in ks], lead=True)
        return {k: _unshard(pt, piece_axis(k)) for k, pt in zip(ks, parts)}

    in_layer1 = lambda k: k[1] == 1 or k[0].startswith(("o_", "rwkv_", "lru_"))
    first_keys = [k for k in keys if k[0] in MATMUL_WEIGHTS and not in_layer1(k)]
    mix1_keys = [k for k in keys if k[0] in MATMUL_WEIGHTS and in_layer1(k) and k not in LATE_WEIGHTS]
    f32_keys = [k for k in keys if k[0] not in MATMUL_WEIGHTS]
    full = gathered(_exchange(_pack([shard("", k) for k in first_keys], BF16), gather=TWO_LEVEL, name="gather_weights_bf16"), first_keys)
    full.update(gathered(_exchange(_pack([shard("", k) for k in f32_keys], F32), gather=True, name="gather_weights_f32"), f32_keys))
    by_name = lambda: {n: (a[n] if ax is None else full.get((n, 0))) for n, ax in WEIGHTS}

    lw0 = dict(norm_mix=row(norm_mix[0]), norm_ffn=row(norm_ffn[0]), norm_pl=row(norm_pl[0]), w_in=_even_in_cols(full["e_in_proj", 0]),
               w_out=full["e_out_proj", 0], w1=full["mlp_w1", 0], w2=full["mlp_w2", 0], w_gate=full["pl_gate", 0], w_pl=full["pl_proj", 0])
    mp0 = _even_prep(by_name())
    mix1_src = _pack([shard("", k) for k in mix1_keys], BF16)

    def take_mix1(got):
        full.update(gathered(got, mix1_keys))
        return {}

    h1, saved0, lw0 = _layer_fwd(x[0], p[0, 0], lw0, _even_fwd, mp0, "l0", carry=(mix1_src, TWO_LEVEL), late=take_mix1)

    def late(got):
        fl = gathered(got, LATE_WEIGHTS)
        return dict(w1=fl["mlp_w1", 1], w2=fl["mlp_w2", 1], w_gate=fl["pl_gate", 1], w_pl=fl["pl_proj", 1])

    lw1 = dict(norm_mix=row(norm_mix[1]), norm_ffn=row(norm_ffn[1]), norm_pl=row(norm_pl[1]), w_in=_odd_in_cols(full["o_in_proj", 0]),
               w_out=full["o_out_proj", 0])
    mp1 = _odd_prep(by_name())
    late_src = _pack([shard("", k) for k in LATE_WEIGHTS], BF16)
    h2, saved1, lw1 = _layer_fwd(h1, p[1, 0], lw1, _odd_fwd, mp1, "l1", carry=(late_src, True), late=late)
    loss_blk, dh, dg_final = _loss_and_grad(h2, row(norm_final), loss_target[0], name="loss")
    loss = lax.psum(loss_blk[0, 0], ("x", "y", "c"))

    def slabs(grad_of, ks):
        return _pack([_to_parts(grad_of[k], piece_axis(k)) for k in ks], BF16, lead=True)

    def global_shape(k):
        return tuple((N_DEV if d == piece_axis(k) else 1) * n for d, n in enumerate(shard("", k).shape))

    early_grads = lambda g: (slabs({("mlp_w1", 1): g["w1"], ("mlp_w2", 1): g["w2"], ("pl_gate", 1): g["w_gate"],
                                    ("pl_proj", 1): g["w_pl"], ("o_out_proj", 0): g["w_out"]}, EARLY_GRADS), False)
    dh, g1, mg1, early_got = _layer_bwd(dh, p[1, 0], lw1, _odd_bwd, mp1, saved1, "l1", early=early_grads)
    l0a_keys = [("mlp_w1", 0), ("pl_proj", 0)]
    mid_keys = [k for k in keys if in_layer1(k) and k not in EARLY_GRADS] + l0a_keys
    mid_grad = {k: (_odd_in_cols_t(g1["w_in"]) if k[0] == "o_in_proj" else mg1[k[0]].reshape(global_shape(k))) for k in mid_keys[:-2]}
    l0_early = lambda g: (slabs({**mid_grad, ("mlp_w1", 0): g["w1"], ("pl_proj", 0): g["w_pl"]}, mid_keys), False)
    dh, g0, mg0, mid_got = _layer_bwd(dh, p[0, 0], lw0, _even_bwd, mp0, saved0, "l0", early=l0_early)

    main_keys = [k for k in keys if not in_layer1(k) and k not in l0a_keys]
    piece_grad = {("mlp_w2", 0): g0["w2"], ("pl_gate", 0): g0["w_gate"], ("e_in_proj", 0): _even_in_cols_t(g0["w_in"]), ("e_out_proj", 0): g0["w_out"]}
    mixer_grads = {**mg0, **mg1}
    for k in main_keys:
        if k not in piece_grad:
            piece_grad[k] = mg0[k[0]].reshape(global_shape(k))
    main_got = _exchange(slabs(piece_grad, main_keys), gather=False, name="scatter_grads")
    piece_out = {}
    for got, ks, tag in ((early_got, EARLY_GRADS, "early"), (mid_got, mid_keys, "mid"), (main_got, main_keys, "main")):
        res = _adamw(got, *[_pack([shard(pre, k) for k in ks], F32) for pre in ("", "m_", "v_")], name=f"adamw_sharded_{tag}")
        for j in range(4):
            piece_out.update({(j, k): v for k, v in zip(ks, _unpack(res[j], [shard("", k).shape for k in ks]))})
    rp = [n for n, ax in WEIGHTS if ax is None]
    rp_grads = {**mixer_grads, "norm_final": dg_final,
                **{n: jnp.stack([g0[n], g1[n]]) for n in ("norm_mix", "norm_ffn", "norm_pl")}}
    parts = _exchange(_pack([rp_grads[n].reshape(a[n].shape) for n in rp], F32), gather=True, name="gather_small_grads")
    rp_res = _adamw(parts, *[_pack([a[pre + n] for n in rp], F32) for pre in ("", "m_", "v_")], name="adamw_replicated")

    outs = []
    for j in range(4):
        rp_out = dict(zip(rp, _unpack(rp_res[j], [a[n].shape for n in rp])))
        outs.extend(rp_out[n] if ax is None else jnp.stack([piece_out[j, (n, i)] for i in range(a[n].shape[0])]) for n, ax in WEIGHTS)
    return (loss, dh[None], *outs)
```

```python
import functools
import math

import jax
import jax.numpy as jnp
from jax import lax
from jax.experimental import pallas as pl
from jax.experimental.pallas import tpu as pltpu

F32 = jnp.float32
BF16 = jnp.bfloat16
N_DEV = 8
LANES = 128
VMEM_LIMIT = 56 * 1024 * 1024
MATMUL_VMEM = 46 * 1024 * 1024
NORM_EPS = 1e-6
RWKV_GN_EPS = 64e-5
LRU_C = 8.0
ADAM_LR, ADAM_B1, ADAM_B2, ADAM_EPS, ADAM_WD, ADAM_STEP = 0.001, 0.9, 0.999, 1e-08, 0.01, 10
SSD_CHUNK = 128
RWKV_CHUNK = 32
HEAD = 64


def _cparams(*sem):
    return pltpu.CompilerParams(dimension_semantics=sem, vmem_limit_bytes=VMEM_LIMIT)


def _pick(n, prefs):
    for t in prefs:
        if n % t == 0:
            return t
    return n


def _relu2_of(val, _):
    r = jnp.maximum(val, 0.0)
    return r * r


def _relu2_grad(dact, act):
    return dact * (2.0 * jnp.sqrt(act.astype(F32)))


def _matmul(a, b, mode, *, name, add=None, out_dtype=F32, post=None, aux=None):
    if mode == "nn":
        (m, k), (k2, n) = a.shape, b.shape
    elif mode == "nt":
        (m, k), (n, k2) = a.shape, b.shape
    else:
        (k, m), (k2, n) = a.shape, b.shape
    assert k == k2, (a.shape, b.shape, mode)
    tm, tn = _pick(m, (1024, 512, 256, 128)), _pick(n, (1024, 512, 256, 128))
    extras = [x for x in (add, aux) if x is not None]
    fits = lambda t: 2 * t * (tm * a.dtype.itemsize + tn * b.dtype.itemsize) + (3 + 2 * len(extras)) * tm * tn * 4 <= MATMUL_VMEM
    tk = next((t for t in (2048, 1024, 512, 256, 128) if k % t == 0 and fits(t)), k)
    nk = k // tk
    dn = {"nn": (((1,), (0,)), ((), ())), "nt": (((1,), (1,)), ((), ())), "tn": (((0,), (0,)), ((), ()))}[mode]
    a_spec = pl.BlockSpec((tk, tm), lambda i, j, kk: (kk, i)) if mode == "tn" else pl.BlockSpec((tm, tk), lambda i, j, kk: (i, kk))
    b_spec = pl.BlockSpec((tn, tk), lambda i, j, kk: (j, kk)) if mode == "nt" else pl.BlockSpec((tk, tn), lambda i, j, kk: (kk, j))
    o_spec = pl.BlockSpec((tm, tn), lambda i, j, kk: (i, j))

    def body(a_ref, b_ref, *rest):
        extra_refs, (o_ref, acc) = list(rest[:len(extras)]), rest[len(extras):]
        add_ref = extra_refs.pop(0) if add is not None else None
        aux_ref = extra_refs.pop(0) if aux is not None else None
        kk = pl.program_id(2)
        prod = lambda: lax.dot_general(a_ref[...].astype(BF16), b_ref[...].astype(BF16), dn, preferred_element_type=F32)
        first = lambda: prod() if add is None else prod() + add_ref[...].astype(F32)

        def write(val):
            if post is not None:
                val = post(val, None if aux is None else aux_ref[...])
            o_ref[...] = val.astype(o_ref.dtype)

        if nk == 1:
            write(first())
            return

        @pl.when(kk == 0)
        def _():
            acc[...] = first()

        @pl.when((kk > 0) & (kk < nk - 1))
        def _():
            acc[...] += prod()

        @pl.when(kk == nk - 1)
        def _():
            write(acc[...] + prod())

    ins, specs = [a, b] + extras, [a_spec, b_spec] + [o_spec] * len(extras)
    return pl.pallas_call(
        body, grid=(m // tm, n // tn, nk), in_specs=specs, out_specs=o_spec,
        out_shape=jax.ShapeDtypeStruct((m, n), out_dtype), scratch_shapes=[pltpu.VMEM((tm, tn), F32)],
        compiler_params=_cparams("parallel", "parallel", "arbitrary"), name=name)(*ins)


def _row(x, width=None, block=0):
    return (x, x.shape[1] if width is None else width, block)


def _rw_specs(rows, params, ts):
    specs = [pl.BlockSpec((ts, w), functools.partial(lambda i, b: (i, b), b=bi)) for (_, w, bi) in rows]
    specs += [pl.BlockSpec(p.shape, functools.partial(lambda i, nd: (0,) * nd, nd=p.ndim)) for p in params]
    return specs


def _rw_fwd(f, rows, params, outs, *, name, ts=256):
    s = rows[0][0].shape[0]
    ts = min(ts, s)
    nr, npar = len(rows), len(params)

    def body(*refs):
        row0 = pl.program_id(0) * ts
        res = f(row0, *[r[...] for r in refs[:nr + npar]])
        for o_ref, val in zip(refs[nr + npar:], res, strict=True):
            o_ref[...] = val.astype(o_ref.dtype)

    out = pl.pallas_call(
        body, grid=(s // ts,), in_specs=_rw_specs(rows, params, ts),
        out_specs=[pl.BlockSpec((ts, w), lambda i: (i, 0)) for (w, _) in outs],
        out_shape=[jax.ShapeDtypeStruct((s, w), dt) for (w, dt) in outs],
        compiler_params=_cparams("parallel"), name=name)(*[r[0] for r in rows], *params)
    return tuple(out)


def _rw_bwd(f, rows, params, cts, *, name, ts=256, row_grads=None, param_grads=None, add_rows=None):
    s = rows[0][0].shape[0]
    ts = min(ts, s)
    ct_groups = [list(c) if isinstance(c, (list, tuple)) else [c] for c in cts]
    cts = [c for grp in ct_groups for c in grp]
    nr, npar, nct = len(rows), len(params), len(cts)
    row_grads = list(range(nr)) if row_grads is None else list(row_grads)
    param_grads = list(range(npar)) if param_grads is None else list(param_grads)
    add_rows = add_rows or {}
    add_keys = sorted(add_rows)

    def body(*refs):
        i = pl.program_id(0)
        row0 = i * ts
        vals = [r[...] for r in refs[:nr + npar]]
        for pi in param_grads:
            vals[nr + pi] = vals[nr + pi].astype(F32)
        ct_refs = list(refs[nr + npar:nr + npar + nct])
        add_refs = dict(zip(add_keys, refs[nr + npar + nct:nr + npar + nct + len(add_keys)]))
        o_refs = refs[nr + npar + nct + len(add_keys):]
        res, vjp = jax.vjp(functools.partial(f, row0), *vals)
        ct_vals = []
        for grp, r in zip(ct_groups, res, strict=True):
            ct_vals.append(sum(ct_refs.pop(0)[...].astype(r.dtype) for _ in grp))
        grads = vjp(tuple(ct_vals))
        for o_ref, ri in zip(o_refs[:len(row_grads)], row_grads):
            g = grads[ri]
            if ri in add_refs:
                g = g + add_refs[ri][...]
            o_ref[...] = g.astype(o_ref.dtype)

        @pl.when(i == 0)
        def _():
            for o_ref in o_refs[len(row_grads):]:
                o_ref[...] = jnp.zeros_like(o_ref)

        for o_ref, pi in zip(o_refs[len(row_grads):], param_grads):
            o_ref[...] += grads[nr + pi].astype(F32)

    in_specs = _rw_specs(rows, params, ts)
    in_specs += [pl.BlockSpec((ts, c.shape[1]), lambda i: (i, 0)) for c in cts]
    in_specs += [pl.BlockSpec((ts, add_rows[k].shape[1]), lambda i: (i, 0)) for k in add_keys]
    out_specs = [pl.BlockSpec((ts, rows[ri][1]), lambda i: (i, 0)) for ri in row_grads]
    out_specs += [pl.BlockSpec(params[pi].shape, functools.partial(lambda i, nd: (0,) * nd, nd=params[pi].ndim)) for pi in param_grads]
    out_shape = [jax.ShapeDtypeStruct((s, rows[ri][1]), F32) for ri in row_grads]
    out_shape += [jax.ShapeDtypeStruct(params[pi].shape, F32) for pi in param_grads]
    out = pl.pallas_call(
        body, grid=(s // ts,), in_specs=in_specs, out_specs=out_specs, out_shape=out_shape,
        compiler_params=_cparams("arbitrary"), name=name)(*[r[0] for r in rows], *params, *cts, *[add_rows[k] for k in add_keys])
    return tuple(out[:len(row_grads)]), tuple(out[len(row_grads):])


@functools.partial(jax.custom_vjp, nondiff_argnums=(1,))
def _shift_down(x, k):
    rows = lax.broadcasted_iota(jnp.int32, x.shape, 0)
    return jnp.where(rows < k, 0.0, pltpu.roll(x, k, 0))


def _shift_down_fwd(x, k):
    return _shift_down(x, k), None


def _shift_down_bwd(k, _, g):
    n = g.shape[0]
    rows = lax.broadcasted_iota(jnp.int32, g.shape, 0)
    return (jnp.where(rows >= n - k, 0.0, pltpu.roll(g, n - k, 0)),)


_shift_down.defvjp(_shift_down_fwd, _shift_down_bwd)


def _ct_fwd(f, x, xmap, ntiles, params, out_width, *, name, ct=LANES):
    s = x.shape[0]

    def body(*refs):
        refs[-1][...] = f(*[r[...] for r in refs[:-1]])

    in_specs = [pl.BlockSpec((s, ct), lambda j: (0, xmap(j)))]
    in_specs += [pl.BlockSpec((p.shape[0], ct), functools.partial(lambda j, o: (0, o + j), o=o)) for (p, o) in params]
    return pl.pallas_call(
        body, grid=(ntiles,), in_specs=in_specs, out_specs=pl.BlockSpec((s, ct), lambda j: (0, j)),
        out_shape=jax.ShapeDtypeStruct((s, out_width), F32), compiler_params=_cparams("parallel"), name=name)(x, *[p for p, _ in params])


def _ct_bwd(f, x, xmap, ntiles, params, g, *, name, ct=LANES):
    s = x.shape[0]
    npar, ng = len(params), len(g)

    def body(*refs):
        vals = [r[...] for r in refs[:1 + npar]]
        _, vjp = jax.vjp(f, *vals)
        grads = vjp(sum(r[...] for r in refs[1 + npar:1 + npar + ng]))
        for o_ref, gr in zip(refs[1 + npar + ng:], grads, strict=True):
            o_ref[...] = gr

    in_specs = [pl.BlockSpec((s, ct), lambda j: (0, xmap(j)))]
    pspecs = [pl.BlockSpec((p.shape[0], ct), functools.partial(lambda j, o: (0, o + j), o=o)) for (p, o) in params]
    in_specs += pspecs + [pl.BlockSpec((s, ct), lambda j: (0, j))] * ng
    out = pl.pallas_call(
        body, grid=(ntiles,), in_specs=in_specs, out_specs=[pl.BlockSpec((s, ct), lambda j: (0, j))] + pspecs,
        out_shape=[jax.ShapeDtypeStruct((s, ntiles * ct), F32)] + [jax.ShapeDtypeStruct(p.shape, F32) for p, _ in params],
        compiler_params=_cparams("parallel"), name=name)(x, *[p for p, _ in params], *g)
    return out[0], tuple(out[1:])


def _block_vjp(f, args, cts, *, name):
    outs = jax.eval_shape(f, *args)
    if cts is None:
        def body(*refs):
            for o_ref, v in zip(refs[len(args):], f(*[r[...] for r in refs[:len(args)]]), strict=True):
                o_ref[...] = v
        return tuple(pl.pallas_call(body, out_shape=[jax.ShapeDtypeStruct(o.shape, o.dtype) for o in outs], name=name)(*args))

    def body(*refs):
        n = len(args)
        _, vjp = jax.vjp(f, *[r[...] for r in refs[:n]])
        for o_ref, gr in zip(refs[n + len(cts):], vjp(tuple(r[...] for r in refs[n:n + len(cts)])), strict=True):
            o_ref[...] = gr
    return tuple(pl.pallas_call(body, out_shape=[jax.ShapeDtypeStruct(a.shape, a.dtype) for a in args], name=name)(*args, *cts))


def _softplus(x):
    return jnp.maximum(x, 0.0) + jnp.log(1.0 + jnp.exp(-jnp.abs(x)))


def _dot_bf16(a, b):
    return jnp.dot(a.astype(BF16), b.astype(BF16), preferred_element_type=F32)


def _dot3(x, m):
    hi = x.astype(BF16)
    r1 = x - hi.astype(F32)
    mid = r1.astype(BF16)
    lo = (r1 - mid.astype(F32)).astype(BF16)
    return (jnp.dot(hi, m, preferred_element_type=F32) + jnp.dot(mid, m, preferred_element_type=F32)
            + jnp.dot(lo, m, preferred_element_type=F32))


@jax.custom_vjp
def _lin(x, m, mt):
    return _dot3(x, m)


def _lin_fwd(x, m, mt):
    return _dot3(x, m), (m, mt)


def _lin_bwd(res, g):
    m, mt = res
    return _dot3(g, mt), jnp.zeros_like(m), jnp.zeros_like(mt)


_lin.defvjp(_lin_fwd, _lin_bwd)


def _head_ones(n, head=HEAD):
    i = jnp.arange(n) // head
    return (i[:, None] == i[None, :]).astype(BF16)


def _rmsnorm_tile(_, h, g):
    return (h * lax.rsqrt(jnp.mean(h * h, axis=-1, keepdims=True) + NORM_EPS) * g,)


ROWS = 8


def _shift_rows(x, k, fill, up=False):
    rows = lax.broadcasted_iota(jnp.int32, x.shape, 0)
    if up:
        return jnp.where(rows >= ROWS - k, fill, pltpu.roll(x, ROWS - k, 0))
    return jnp.where(rows < k, fill, pltpu.roll(x, k, 0))


def _cmul(pr, pi, qr, qi):
    return pr * qr - pi * qi, pr * qi + pi * qr


def _power_rows(ar, ai, width, descending):
    pows = [(ar, ai)]
    for _ in range(ROWS - 1):
        pows.append(_cmul(*pows[-1], ar, ai))
    rows = lax.broadcasted_iota(jnp.int32, (ROWS, width), 0)
    pr = jnp.zeros((ROWS, width), F32)
    pi = jnp.zeros((ROWS, width), F32)
    for j in range(ROWS):
        qr, qi = pows[ROWS - 1 - j] if descending else pows[j]
        pr, pi = jnp.where(rows == j, qr, pr), jnp.where(rows == j, qi, pi)
    return pr, pi, ((pows[0], 1), (pows[1], 2), (pows[3], 4))


def _prev_rows(ref, tile, r0):
    before = ref[pl.ds(jnp.maximum(r0 - 1, 0), 1), :] * (r0 > 0).astype(F32)
    rows = lax.broadcasted_iota(jnp.int32, tile.shape, 0)
    return jnp.where(rows == 0, before, pltpu.roll(tile, 1, 0))


def _cscan_fwd(bu, a, *, name, ct=256, carry=None):
    s, c2 = bu.shape
    c = c2 // 2
    nt = c // ct

    def body(br_ref, bi_ref, ar_ref, ai_ref, xr_ref, xi_ref):
        pr, pi, doubling = _power_rows(ar_ref[...], ai_ref[...], ct, False)

        def tile(i, h):
            rows = pl.ds(pl.multiple_of(i * ROWS, ROWS), ROWS)
            sr, si = br_ref[rows, :], bi_ref[rows, :]
            for (qr, qi), k in doubling:
                mr, mi = _cmul(qr, qi, _shift_rows(sr, k, 0.0), _shift_rows(si, k, 0.0))
                sr, si = sr + mr, si + mi
            cr, ci = _cmul(pr, pi, *h)
            sr, si = sr + cr, si + ci
            xr_ref[rows, :] = sr
            xi_ref[rows, :] = si
            return sr[ROWS - 1:], si[ROWS - 1:]

        z = jnp.zeros((1, ct), F32)
        lax.fori_loop(0, s // ROWS, tile, (z, z))

    re = lambda j: (0, j)
    im = lambda j: (0, nt + j)
    return _call_with_exchange(
        body, nt, carry, [bu, bu, a, a],
        [pl.BlockSpec((s, ct), re), pl.BlockSpec((s, ct), im), pl.BlockSpec((1, ct), re), pl.BlockSpec((1, ct), im)],
        [pl.BlockSpec((s, ct), re), pl.BlockSpec((s, ct), re)], [jax.ShapeDtypeStruct((s, c), F32)] * 2, [], name)


def _cscan_bwd(gr, gi, xr, xi, a, *, name, ct=256, carry=None):
    s, c = gr.shape
    nt = c // ct
    n_tiles = s // ROWS

    def body(gr_ref, gi_ref, xr_ref, xi_ref, ar_ref, ai_ref, dr_ref, di_ref, dar_ref, dai_ref):
        pr, pi, doubling = _power_rows(ar_ref[...], -ai_ref[...], ct, True)

        def tile(i, state):
            dnr, dni, accr, acci = state
            r0 = pl.multiple_of((n_tiles - 1 - i) * ROWS, ROWS)
            rows = pl.ds(r0, ROWS)
            sr, si = gr_ref[rows, :], gi_ref[rows, :]
            for (qr, qi), k in doubling:
                mr, mi = _cmul(qr, qi, _shift_rows(sr, k, 0.0, up=True), _shift_rows(si, k, 0.0, up=True))
                sr, si = sr + mr, si + mi
            cr, ci = _cmul(pr, pi, dnr, dni)
            sr, si = sr + cr, si + ci
            dr_ref[rows, :] = sr
            di_ref[rows, :] = si
            xpr, xpi = _prev_rows(xr_ref, xr_ref[rows, :], r0), _prev_rows(xi_ref, xi_ref[rows, :], r0)
            return sr[:1], si[:1], accr + sr * xpr + si * xpi, acci + si * xpr - sr * xpi

        z, z8 = jnp.zeros((1, ct), F32), jnp.zeros((ROWS, ct), F32)
        _, _, accr, acci = lax.fori_loop(0, n_tiles, tile, (z, z, z8, z8))
        dar_ref[...] = jnp.sum(accr, axis=0, keepdims=True)
        dai_ref[...] = jnp.sum(acci, axis=0, keepdims=True)

    re = lambda j: (0, j)
    im = lambda j: (0, nt + j)
    blk = pl.BlockSpec((s, ct), re)
    return _call_with_exchange(
        body, nt, carry, [gr, gi, xr, xi, a, a],
        [blk, blk, blk, blk, pl.BlockSpec((1, ct), re), pl.BlockSpec((1, ct), im)],
        [blk, blk, pl.BlockSpec((1, ct), re), pl.BlockSpec((1, ct), re)],
        [jax.ShapeDtypeStruct((s, c), F32)] * 2 + [jax.ShapeDtypeStruct((1, c), F32)] * 2, [], name)


def _rscan_fwd(a, b, *, name, ct=256):
    s, c = a.shape

    def body(a_ref, b_ref, h_ref):
        def tile(i, h):
            rows = pl.ds(pl.multiple_of(i * ROWS, ROWS), ROWS)
            ca, cb = a_ref[rows, :], b_ref[rows, :]
            for k in (1, 2, 4):
                cb = cb + ca * _shift_rows(cb, k, 0.0)
                ca = ca * _shift_rows(ca, k, 1.0)
            out = cb + ca * h
            h_ref[rows, :] = out
            return out[ROWS - 1:]
        lax.fori_loop(0, s // ROWS, tile, jnp.zeros((1, ct), F32))

    blk = pl.BlockSpec((s, ct), lambda j: (0, j))
    return pl.pallas_call(body, grid=(c // ct,), in_specs=[blk, blk], out_specs=blk,
                          out_shape=jax.ShapeDtypeStruct((s, c), F32), compiler_params=_cparams("parallel"), name=name)(a, b)


def _rscan_bwd(g, a, h, *, name, ct=256):
    s, c = a.shape
    n_tiles = s // ROWS

    def body(g_ref, a_ref, h_ref, db_ref, da_ref):
        def tile(i, dn):
            r0 = pl.multiple_of((n_tiles - 1 - i) * ROWS, ROWS)
            rows = pl.ds(r0, ROWS)
            after = a_ref[pl.ds(jnp.minimum(r0 + ROWS, s - 1), 1), :] * (r0 + ROWS < s).astype(F32)
            ca = _shift_rows(a_ref[rows, :], 1, after, up=True)
            cb = g_ref[rows, :]
            for k in (1, 2, 4):
                cb = cb + ca * _shift_rows(cb, k, 0.0, up=True)
                ca = ca * _shift_rows(ca, k, 1.0, up=True)
            out = cb + ca * dn
            db_ref[rows, :] = out
            da_ref[rows, :] = out * _prev_rows(h_ref, h_ref[rows, :], r0)
            return out[:1]
        lax.fori_loop(0, n_tiles, tile, jnp.zeros((1, ct), F32))

    blk = pl.BlockSpec((s, ct), lambda j: (0, j))
    db, da = pl.pallas_call(body, grid=(c // ct,), in_specs=[blk, blk, blk], out_specs=[blk, blk],
                            out_shape=[jax.ShapeDtypeStruct((s, c), F32)] * 2, compiler_params=_cparams("parallel"), name=name)(g, a, h)
    return db, da


def _dot3l(m, x):
    hi = x.astype(BF16)
    r1 = x - hi.astype(F32)
    mid = r1.astype(BF16)
    lo = (r1 - mid.astype(F32)).astype(BF16)
    return (jnp.dot(m, hi, preferred_element_type=F32) + jnp.dot(m, mid, preferred_element_type=F32)
            + jnp.dot(m, lo, preferred_element_type=F32))


@jax.custom_vjp
def _linl(x, m, mt):
    return _dot3l(m, x)


def _linl_fwd(x, m, mt):
    return _dot3l(m, x), (m, mt)


def _linl_bwd(res, g):
    m, mt = res
    return _dot3l(mt, g), jnp.zeros_like(m), jnp.zeros_like(mt)


_linl.defvjp(_linl_fwd, _linl_bwd)


def _ssd_chunk(g, xs, bm, cm, z, dtraw, dt_bias, a_log, dskip, ng, st0, st1, st2):
    n = xs.shape[0]
    lane = lax.broadcasted_iota(jnp.int32, (1, LANES), 1)
    sub = lax.broadcasted_iota(jnp.int32, (LANES, 1), 0)
    row = lax.broadcasted_iota(jnp.int32, (n, n), 0)
    col = lax.broadcasted_iota(jnp.int32, (n, n), 1)
    tril = row >= col
    tril_m = tril.astype(BF16)
    triu_m = (row <= col).astype(BF16)
    lane_lo = lane < HEAD
    sub_lo = sub < HEAD
    dt = _softplus(dtraw + dt_bias)
    da = dt * (-jnp.exp(a_log))
    acum = _linl(da, tril_m, triu_m)
    acum_t = acum.T
    scores = lax.dot_general(cm.astype(BF16), bm.astype(BF16), (((1,), (1,)), ((), ())), preferred_element_type=F32)

    def head(h):
        sel = lane == h
        acol = jnp.sum(jnp.where(sel, acum, 0.0), axis=1, keepdims=True)
        arow = jnp.sum(jnp.where(sub == h, acum_t, 0.0), axis=0, keepdims=True)
        dtcol = jnp.sum(jnp.where(sel, dt, 0.0), axis=1, keepdims=True)
        dsk = jnp.sum(jnp.where(sel, dskip, 0.0), axis=1, keepdims=True)
        decay = jnp.exp(jnp.where(tril, acol - arow, -jnp.inf))
        alast = acol[n - 1:n, :]
        return acol, dtcol, dsk, decay, alast

    ys, new = [], []
    for q, st in enumerate((st0, st1, st2)):
        a_acol, a_dt, a_dsk, a_decay, a_last = head(g * 6 + 2 * q)
        b_acol, b_dt, b_dsk, b_decay, b_last = head(g * 6 + 2 * q + 1)
        xp = xs[:, q * LANES:(q + 1) * LANES]
        xdt = xp * jnp.where(lane_lo, a_dt, b_dt)
        yd = jnp.where(lane_lo, _dot_bf16(scores * a_decay, xdt), _dot_bf16(scores * b_decay, xdt))
        xw = xdt * jnp.where(lane_lo, jnp.exp(a_last - a_acol), jnp.exp(b_last - b_acol))
        states = lax.dot_general(xw.astype(BF16), bm.astype(BF16), (((0,), (0,)), ((), ())), preferred_element_type=F32)
        yo = lax.dot_general(cm.astype(BF16), st.astype(BF16), (((1,), (1,)), ((), ())), preferred_element_type=F32)
        yo = yo * jnp.where(lane_lo, jnp.exp(a_acol), jnp.exp(b_acol))
        new.append(st * jnp.where(sub_lo, jnp.exp(a_last), jnp.exp(b_last)) + states)
        ys.append(yd + yo + xp * jnp.where(lane_lo, a_dsk, b_dsk))
    y = jnp.concatenate(ys, axis=1)
    y = y * (z * jax.nn.sigmoid(z))
    y = y * lax.rsqrt(jnp.mean(y * y, axis=-1, keepdims=True) + NORM_EPS) * ng
    return y, new[0], new[1], new[2]


def _ssd_specs(nc, rev):
    cidx = (lambda c: nc - 1 - c) if rev else (lambda c: c)
    gw = 3 * LANES
    return [
        pl.BlockSpec((SSD_CHUNK, gw), lambda c, g: (cidx(c), g)),
        pl.BlockSpec((SSD_CHUNK, LANES), lambda c, g: (cidx(c), 12 + g)),
        pl.BlockSpec((SSD_CHUNK, LANES), lambda c, g: (cidx(c), 16 + g)),
        pl.BlockSpec((SSD_CHUNK, gw), lambda c, g: (cidx(c), g)),
        pl.BlockSpec((SSD_CHUNK, LANES), lambda c, g: (cidx(c), 36)),
        pl.BlockSpec((1, LANES), lambda c, g: (0, 0)),
        pl.BlockSpec((1, LANES), lambda c, g: (0, 0)),
        pl.BlockSpec((1, LANES), lambda c, g: (0, 0)),
        pl.BlockSpec((1, gw), lambda c, g: (0, g)),
    ], cidx


def _ssd_fwd(conv, proj, dt_bias, a_log, dskip, norm_g, *, name, carry=None):
    s = conv.shape[0]
    nc = s // SSD_CHUNK
    in_specs, _ = _ssd_specs(nc, False)

    def body(xs, bm, cm, z, dtr, dtb, alog, dsk, ng, y_ref, sv_ref, st):
        c, g = pl.program_id(0), pl.program_id(1)

        @pl.when(c == 0)
        def _():
            for q in range(3):
                st[g * 3 + q] = jnp.zeros((LANES, LANES), F32)

        olds = [st[g * 3 + q] for q in range(3)]
        for q in range(3):
            sv_ref[0, 0, q] = olds[q]
        y, n0, n1, n2 = _ssd_chunk(g, xs[...], bm[...], cm[...], z[...], dtr[...], dtb[...], alog[...], dsk[...], ng[...], *olds)
        y_ref[...] = y
        for q, v in enumerate((n0, n1, n2)):
            st[g * 3 + q] = v

    return _call_with_exchange(
        body, (nc, 4), carry, [conv, conv, conv, proj, proj, dt_bias, a_log, dskip, norm_g], in_specs,
        [pl.BlockSpec((SSD_CHUNK, 3 * LANES), lambda c, g: (c, g)), pl.BlockSpec((1, 1, 3, LANES, LANES), lambda c, g: (c, g, 0, 0, 0))],
        [jax.ShapeDtypeStruct((s, 12 * LANES), F32), jax.ShapeDtypeStruct((nc, 4, 3, LANES, LANES), F32)],
        [pltpu.VMEM((12, LANES, LANES), F32)], name)


def _ssd_bwd(conv, proj, dt_bias, a_log, dskip, norm_g, saved, dy, *, name, carry=None):
    s = conv.shape[0]
    nc = s // SSD_CHUNK
    in_specs, cidx = _ssd_specs(nc, True)
    gw = 3 * LANES
    in_specs += [pl.BlockSpec((1, 1, 3, LANES, LANES), lambda c, g: (cidx(c), g, 0, 0, 0)),
                 pl.BlockSpec((SSD_CHUNK, gw), lambda c, g: (cidx(c), g))]

    def body(xs, bm, cm, z, dtr, dtb, alog, dsk, ng, sv, dy_ref, dxs, dbm, dcm, dz, ddt, ddtb, dalog, ddsk, dng, dst):
        c, g = pl.program_id(0), pl.program_id(1)

        @pl.when(c == 0)
        def _():
            for q in range(3):
                dst[g * 3 + q] = jnp.zeros((LANES, LANES), F32)

        @pl.when((c == 0) & (g == 0))
        def _():
            ddtb[...] = jnp.zeros_like(ddtb)
            dalog[...] = jnp.zeros_like(dalog)
            ddsk[...] = jnp.zeros_like(ddsk)
            dng[...] = jnp.zeros_like(dng)

        @pl.when(g == 0)
        def _():
            ddt[...] = jnp.zeros_like(ddt)

        olds = [sv[0, 0, q] for q in range(3)]
        _, vjp = jax.vjp(functools.partial(_ssd_chunk, g), xs[...], bm[...], cm[...], z[...], dtr[...], dtb[...], alog[...],
                         dsk[...], ng[...], *olds)
        gr = vjp((dy_ref[...], dst[g * 3], dst[g * 3 + 1], dst[g * 3 + 2]))
        dxs[...], dbm[...], dcm[...], dz[...] = gr[0], gr[1], gr[2], gr[3]
        ddt[...] += gr[4]
        ddtb[...] += gr[5]
        dalog[...] += gr[6]
        ddsk[...] += gr[7]
        dng[g] += gr[8]
        for q in range(3):
            dst[g * 3 + q] = gr[9 + q]

    const = lambda shape: pl.BlockSpec(shape, lambda c, g: (0,) * len(shape))
    return _call_with_exchange(
        body, (nc, 4), carry, [conv, conv, conv, proj, proj, dt_bias, a_log, dskip, norm_g, saved, dy], in_specs,
        [pl.BlockSpec((SSD_CHUNK, gw), lambda c, g: (cidx(c), g)),
         pl.BlockSpec((SSD_CHUNK, LANES), lambda c, g: (cidx(c), g)),
         pl.BlockSpec((SSD_CHUNK, LANES), lambda c, g: (cidx(c), g)),
         pl.BlockSpec((SSD_CHUNK, gw), lambda c, g: (cidx(c), g)),
         pl.BlockSpec((SSD_CHUNK, LANES), lambda c, g: (cidx(c), 0)),
         const((1, LANES)), const((1, LANES)), const((1, LANES)), const((4, 1, gw))],
        [jax.ShapeDtypeStruct((s, 12 * LANES), F32), jax.ShapeDtypeStruct((s, 4 * LANES), F32),
         jax.ShapeDtypeStruct((s, 4 * LANES), F32), jax.ShapeDtypeStruct((s, 12 * LANES), F32),
         jax.ShapeDtypeStruct((s, LANES), F32), jax.ShapeDtypeStruct((1, LANES), F32),
         jax.ShapeDtypeStruct((1, LANES), F32), jax.ShapeDtypeStruct((1, LANES), F32),
         jax.ShapeDtypeStruct((4, 1, gw), F32)],
        [pltpu.VMEM((12, LANES, LANES), F32)], name)


MXU_TILE = 256
RWKV_GROUP = 8
RWKV_UNROLL = 8


def _rwkv_consts():
    lanes = 16 * HEAD
    hl = jnp.arange(lanes) // HEAD
    e = (jnp.arange(16)[:, None] == hl[None, :]).astype(BF16)
    return e, e.T, _head_ones(MXU_TILE)


_RWKV_LANE_GROUPS = tuple(slice(i * MXU_TILE, (i + 1) * MXU_TILE) for i in range(16 * HEAD // MXU_TILE))


def _head_sums(x, j):
    return jnp.dot(x.astype(BF16), j, preferred_element_type=F32)


def _fold8(x):
    return jnp.sum(x.reshape(x.shape[0] // 8, 8, x.shape[1]), axis=0)


def _rwkv_expand(src3, dst, e, t_):
    for g0 in range(0, t_, RWKV_GROUP):
        n = min(RWKV_GROUP, t_ - g0)
        flat = src3[g0:g0 + n].reshape(n * HEAD, e.shape[0])
        dst[g0:g0 + n] = jnp.dot(flat, e, preferred_element_type=F32).reshape(n, HEAD, e.shape[1])


def _rwkv_reduce(src, dst3, et, t_):
    for g0 in range(0, t_, RWKV_GROUP):
        n = min(RWKV_GROUP, t_ - g0)
        x = src[g0:g0 + n].reshape(n * HEAD, et.shape[0])
        dst3[g0:g0 + n] = jnp.dot(x.astype(BF16), et, preferred_element_type=F32).reshape(n, HEAD, 16)


def _rwkv_fwd(w, kk, b, k, fp, v3, *, name, carry=None):
    s, lanes = w.shape
    t_ = min(RWKV_CHUNK, s)
    nc = s // t_
    e, et, j = _rwkv_consts()
    rowspec = pl.BlockSpec((t_, lanes), lambda c: (c, 0))
    cspec = lambda a: pl.BlockSpec(a.shape, lambda c: (0, 0))

    def body(w_ref, kk_ref, b_ref, k_ref, r_ref, v_ref, e_ref, et_ref, j_ref, y_ref, sv_ref, st, vm, zz):
        c = pl.program_id(0)

        @pl.when(c == 0)
        def _():
            st[...] = jnp.zeros_like(st)

        sv_ref[0] = st[...]
        jv = j_ref[...]
        _rwkv_expand(v_ref, vm, e_ref[...], t_)

        def step(t, carry):
            for grp in _RWKV_LANE_GROUPS:
                row = lambda ref: ref[pl.ds(t, 1), grp]
                sm = st[:, grp]
                sa = _head_sums(sm * (-row(kk_ref)), jv)
                sn = sm * row(w_ref) + sa * row(b_ref) + vm[t, :, grp] * row(k_ref)
                st[:, grp] = sn
                zz[t, :, grp] = sn * row(r_ref)
            return carry

        lax.fori_loop(0, t_, step, 0, unroll=RWKV_UNROLL)
        _rwkv_reduce(zz, y_ref, et_ref[...], t_)

    return _call_with_exchange(
        body, nc, carry, [w, kk, b, k, fp, v3, e, et, j],
        [rowspec] * 5 + [pl.BlockSpec((t_, HEAD, 16), lambda c: (c, 0, 0)), cspec(e), cspec(et), cspec(j)],
        [pl.BlockSpec((t_, HEAD, 16), lambda c: (c, 0, 0)), pl.BlockSpec((1, HEAD, lanes), lambda c: (c, 0, 0))],
        [jax.ShapeDtypeStruct((s, HEAD, 16), F32), jax.ShapeDtypeStruct((nc, HEAD, lanes), F32)],
        [pltpu.VMEM((HEAD, lanes), F32), pltpu.VMEM((t_, HEAD, lanes), F32), pltpu.VMEM((t_, HEAD, lanes), F32)], name)


def _call_with_exchange(body, grid, carry, ins, in_specs, out_specs, out_shape, scratch, name):
    grid = (grid,) if isinstance(grid, int) else tuple(grid)
    if carry is not None:
        src, gather = carry
        n_in, n_out, inner = len(ins), len(out_shape), body

        def body(*refs):
            ex = _make_exchange(refs[n_in], refs[n_in + 1 + n_out], refs[-3:], gather)
            steps = [pl.program_id(d) for d in range(len(grid))]

            @pl.when(functools.reduce(jnp.logical_and, [s == 0 for s in steps]))
            def _():
                ex.start()

            inner(*refs[:n_in], *refs[n_in + 1:n_in + 1 + n_out], *refs[n_in + 2 + n_out:-3])

            @pl.when(functools.reduce(jnp.logical_and, [s == n - 1 for s, n in zip(steps, grid)]))
            def _():
                ex.wait()

        hbm = pl.BlockSpec(memory_space=pl.ANY)
        ins, in_specs, out_specs = list(ins) + [src], list(in_specs) + [hbm], list(out_specs) + [hbm]
        out_shape = list(out_shape) + [jax.ShapeDtypeStruct((N_DEV, src.shape[-2], LANES), src.dtype)]
        scratch = list(scratch) + list(_Exchange.SCRATCH)
    return pl.pallas_call(body, grid=grid, in_specs=in_specs, out_specs=out_specs, out_shape=out_shape,
                          scratch_shapes=scratch, compiler_params=_cparams(*["arbitrary"] * len(grid)), name=name)(*ins)


def _rwkv_bwd(w, kk, b, k, fp, v3, saved, dy3, *, name, carry=None):
    s, lanes = w.shape
    t_ = min(RWKV_CHUNK, s)
    nc = s // t_
    e, et, j = _rwkv_consts()
    rev = lambda c: nc - 1 - c
    rowspec = pl.BlockSpec((t_, lanes), lambda c: (rev(c), 0))
    v3spec = pl.BlockSpec((t_, HEAD, 16), lambda c: (rev(c), 0, 0))
    s3spec = v3spec
    cspec = lambda a: pl.BlockSpec(a.shape, lambda c: (0, 0))

    def body(w_ref, kk_ref, b_ref, k_ref, r_ref, v_ref, sv_ref, dy_ref, e_ref, et_ref, j_ref,
             dw_ref, dkk_ref, db_ref, dk_ref, dr_ref, dv_ref, dst, h_sm, h_sa, vm, dz, pw, pkk, pb, pk, pr):
        c = pl.program_id(0)

        @pl.when(c == 0)
        def _():
            dst[...] = jnp.zeros_like(dst)

        jv = j_ref[...]
        _rwkv_expand(v_ref, vm, e_ref[...], t_)
        _rwkv_expand(dy_ref, dz, e_ref[...], t_)
        h_sm[0] = sv_ref[0]

        def replay(t, carry):
            for grp in _RWKV_LANE_GROUPS:
                row = lambda ref: ref[pl.ds(t, 1), grp]
                sm = h_sm[t, :, grp]
                sa = _head_sums(sm * (-row(kk_ref)), jv)
                h_sa[t, :, grp] = sa
                h_sm[t + 1, :, grp] = sm * row(w_ref) + sa * row(b_ref) + vm[t, :, grp] * row(k_ref)
            return carry

        lax.fori_loop(0, t_, replay, 0, unroll=RWKV_UNROLL)

        def back(i, carry):
            t = t_ - 1 - i
            for grp in _RWKV_LANE_GROUPS:
                row = lambda ref: ref[pl.ds(t, 1), grp]
                sm, sa, dzt = h_sm[t, :, grp], h_sa[t, :, grp], dz[t, :, grp]
                dsn = dst[:, grp] + dzt * row(r_ref)
                pr[t, :, grp] = _fold8(dzt * h_sm[t + 1, :, grp])
                pw[t, :, grp] = _fold8(dsn * sm)
                pb[t, :, grp] = _fold8(dsn * sa)
                pk[t, :, grp] = _fold8(dsn * vm[t, :, grp])
                vm[t, :, grp] = dsn * row(k_ref)
                dx = _head_sums(dsn * row(b_ref), jv)
                pkk[t, :, grp] = _fold8(dx * sm)
                dst[:, grp] = dsn * row(w_ref) - dx * row(kk_ref)
            return carry

        lax.fori_loop(0, t_, back, 0, unroll=RWKV_UNROLL)
        _rwkv_reduce(vm, dv_ref, et_ref[...], t_)
        dw_ref[...] = jnp.sum(pw[...], axis=1)
        dkk_ref[...] = -jnp.sum(pkk[...], axis=1)
        db_ref[...] = jnp.sum(pb[...], axis=1)
        dk_ref[...] = jnp.sum(pk[...], axis=1)
        dr_ref[...] = jnp.sum(pr[...], axis=1)

    big = lambda n: pltpu.VMEM((n, HEAD, lanes), F32)
    part = pltpu.VMEM((t_, 8, lanes), F32)
    return _call_with_exchange(
        body, nc, carry, [w, kk, b, k, fp, v3, saved, dy3, e, et, j],
        [rowspec] * 5 + [s3spec, pl.BlockSpec((1, HEAD, lanes), lambda c: (rev(c), 0, 0)), s3spec, cspec(e), cspec(et), cspec(j)],
        [rowspec] * 5 + [v3spec],
        [jax.ShapeDtypeStruct((s, lanes), F32)] * 5 + [jax.ShapeDtypeStruct((s, HEAD, 16), F32)],
        [pltpu.VMEM((HEAD, lanes), F32), big(t_ + 1), big(t_), big(t_), big(t_)] + [part] * 5, name)


def _blockdiag(blocks):
    g, a, b = blocks.shape
    on_diag = (jnp.arange(g)[:, None, None, None] == jnp.arange(g)[None, None, :, None])
    return jnp.where(on_diag, blocks[:, :, None, :], 0).reshape(g * a, g * b)


def _blockdiag_t(dense, g):
    a, b = dense.shape[0] // g, dense.shape[1] // g
    on_diag = (jnp.arange(g)[:, None, None, None] == jnp.arange(g)[None, None, :, None])
    return jnp.sum(jnp.where(on_diag, dense.reshape(g, a, g, b), 0), axis=2)


def _pad_cols(x, n):
    return jnp.pad(x, ((0, 0), (0, n - x.shape[1])))


def _pad_rows(x, n):
    return jnp.pad(x, ((0, n - x.shape[0]), (0, 0)))


E_PROJ = 5120


def _even_in_cols(w):
    return jnp.concatenate([w[:, 512:2048], w[:, 0:512], w[:, 2048:4632], jnp.zeros((w.shape[0], E_PROJ - 4632), w.dtype)], axis=1)


def _even_in_cols_t(dw):
    return jnp.concatenate([dw[:, 1536:2048], dw[:, 0:1536], dw[:, 2048:4632]], axis=1)


O_PROJ = 5632


def _odd_in_cols(w):
    z32 = jnp.zeros((w.shape[0], 32), w.dtype)
    return jnp.concatenate([w[:, 0:3072], w[:, 3520:5568], w[:, 3264:3520], w[:, 3072:3168], z32, w[:, 3168:3264], z32], axis=1)


def _odd_in_cols_t(dw):
    return jnp.concatenate([dw[:, 0:3072], dw[:, 5376:5472], dw[:, 5504:5600], dw[:, 5120:5376], dw[:, 3072:5120]], axis=1)


def _mu_cols(mu):
    z32 = jnp.zeros((1, 32), mu.dtype)
    return jnp.concatenate([mu[:, 0:3072], mu[:, 3264:3520], mu[:, 3072:3168], z32, mu[:, 3168:3264], z32], axis=1)


def _mu_cols_t(d):
    return jnp.concatenate([d[:, 0:3072], d[:, 3328:3424], d[:, 3456:3552], d[:, 3072:3328]], axis=1)


def _conv_taps(x, w, b):
    y = b + w[3:4] * x
    for k in range(3):
        y = y + w[k:k + 1] * _shift_down(x, 3 - k)
    return y


def _conv_silu(x, w, b):
    y = _conv_taps(x, w, b)
    return y * jax.nn.sigmoid(y)


def _tshift(x, mu):
    return x + (_shift_down(x, 1) - x) * mu


def _pl_gate(_, h, gl, e):
    return (h + jax.nn.sigmoid(gl) * e,)


def _s5_param(lr, li, ls, br, bi):
    step = jnp.exp(ls)
    mag = jnp.exp(lr * step)
    ar, ai = mag * jnp.cos(li * step), mag * jnp.sin(li * step)
    den = lr * lr + li * li
    nr = ar - 1.0
    cr = (nr * lr + ai * li) / den
    ci = (ai * lr - nr * li) / den
    return ar, ai, cr * br - ci * bi, cr * bi + ci * br


def _s5_post(_, ylin, u, d, gw, gb):
    act = jax.nn.gelu(ylin + d * u)
    return (act * jax.nn.sigmoid(_dot_bf16(act, gw) + gb),)


def _rwkv_pre(_, k, gl, wl, al, w0, w_up, a0, a_up, g_up, k_k, k_a, j):
    w = -_softplus(-(w0 + _dot_bf16(jnp.tanh(wl), w_up))) - 0.5
    decay = jnp.exp(-jnp.exp(w))
    a = jax.nn.sigmoid(a0 + _dot_bf16(al, a_up))
    g = _dot_bf16(jax.nn.sigmoid(gl), g_up)
    kk = k * k_k
    k2 = k * (1.0 + (a - 1.0) * k_a)
    kkn = kk * lax.rsqrt(jnp.maximum(_lin(kk * kk, j, j), 1e-24))
    return decay, kkn, kkn * a, k2, g


def _rwkv_post(_, y, r, k2, v, g, r_k, ln_g, ln_b, j):
    mean = _lin(y, j, j) * (1.0 / HEAD)
    yc = y - mean
    var = _lin(yc * yc, j, j) * (1.0 / HEAD)
    yn = yc * lax.rsqrt(var + RWKV_GN_EPS) * ln_g + ln_b
    return ((yn + _lin(r * k2 * r_k, j, j) * v) * g,)


def _lru_pre(row0, pre, xc, bax, lam):
    n = xc.shape[1]
    gr = jax.nn.sigmoid(pre[:, :n] + bax[:, :n])
    gi = jax.nn.sigmoid(pre[:, n:] + bax[:, n:])
    log_a = -LRU_C * gr * _softplus(-lam)
    m2 = -jnp.tanh(log_a) * (jnp.exp(2.0 * log_a) + 1.0)
    mult = jnp.sqrt(jnp.maximum(m2, 0.0))
    rowid = row0 + lax.broadcasted_iota(jnp.int32, (xc.shape[0], 1), 0)
    mult = jnp.where(rowid == 0, 1.0, mult)
    return jnp.exp(log_a), xc * gi * mult


def _lru_post(_, h, gl2):
    return (h * jax.nn.gelu(gl2),)


def _even_prep(w):
    sp = (w["s5_lam_re"].reshape(32, 64), w["s5_lam_im"].reshape(32, 64), w["s5_log_step"].reshape(32, 1),
          w["s5_b_re"].reshape(32, 64, 16).transpose(2, 0, 1), w["s5_b_im"].reshape(32, 64, 16).transpose(2, 0, 1))
    ar, ai, bbr, bbi = _block_vjp(_s5_param, sp, None, name="s5_param")
    bblk = lambda bb: _blockdiag(bb.transpose(1, 0, 2))
    cblk = lambda c: _blockdiag(c.reshape(32, 16, 64).transpose(0, 2, 1))
    pad = lambda x: _pad_cols(x.reshape(1, 24), LANES)
    return dict(
        sp=sp, a_row=jnp.concatenate([ar.reshape(1, 2048), ai.reshape(1, 2048)], axis=1),
        b_re=bblk(bbr), b_im=bblk(bbi), c_re=cblk(w["s5_c_re"]), c_imn=-cblk(w["s5_c_im"]),
        d=w["s5_d"].reshape(1, 512), gw=w["s5_glu_w"].reshape(512, 512), gb=w["s5_glu_b"].reshape(1, 512),
        conv_w=w["ssd_conv_w"].reshape(4, 2560), conv_b=w["ssd_conv_b"].reshape(1, 2560),
        dt_bias=pad(w["ssd_dt_bias"]), a_log=pad(w["ssd_a_log"]), dskip=pad(w["ssd_d"]), norm=w["ssd_norm"].reshape(1, 1536))


_E_XMAP = lambda j: 16 + j


def _even_fwd(proj, p, carry=None):
    u = proj[:, 1536:2048]
    bur = _matmul(u, p["b_re"], "nn", name="s5_bu_re")
    bui = _matmul(u, p["b_im"], "nn", name="s5_bu_im")
    xr, xi = _cscan_fwd(jnp.concatenate([bur, bui], axis=1), p["a_row"], name="s5_scan")
    ylin = _matmul(xi, p["c_imn"], "nn", name="s5_y_im", add=_matmul(xr, p["c_re"], "nn", name="s5_y_re"))
    (ya,) = _rw_fwd(_s5_post, [_row(ylin), _row(u)], [p["d"], p["gw"], p["gb"]], [(512, F32)], name="s5_post")
    conv = _ct_fwd(_conv_silu, proj, _E_XMAP, 20, [(p["conv_w"], 0), (p["conv_b"], 0)], 2560, name="ssd_conv")
    yb, saved, *carried = _ssd_fwd(conv, proj, p["dt_bias"], p["a_log"], p["dskip"], p["norm"], name="ssd_scan", carry=carry)
    return jnp.concatenate([ya, yb], axis=1), (u, xr, xi, ylin, conv, saved), (carried[0] if carried else None)


def _even_bwd(proj, p, res, dy, carry=None):
    u, xr, xi, ylin, conv, saved = res
    s = proj.shape[0]
    dxs, dbm, dcm, dz, ddt, ddtb, dalog, ddsk, dng, *carried = _ssd_bwd(
        conv, proj, p["dt_bias"], p["a_log"], p["dskip"], p["norm"], saved, dy[:, 512:], name="ssd_scan_bwd", carry=carry)
    dxbc, (dcw, dcb) = _ct_bwd(_conv_silu, proj, _E_XMAP, 20, [(p["conv_w"], 0), (p["conv_b"], 0)],
                               [jnp.concatenate([dxs, dbm, dcm], axis=1)], name="ssd_conv_bwd")
    (dylin, du), (dd, dgw, dgb) = _rw_bwd(_s5_post, [_row(ylin), _row(u)], [p["d"], p["gw"], p["gb"]], [dy[:, :512]], name="s5_post_bwd")
    dxr = _matmul(dylin, p["c_re"], "nt", name="s5_dxr")
    dxi = _matmul(dylin, p["c_imn"], "nt", name="s5_dxi")
    dc_re = _matmul(xr, dylin, "tn", name="s5_dc_re")
    dc_imn = _matmul(xi, dylin, "tn", name="s5_dc_im")
    dbr, dbi, dar, dai = _cscan_bwd(dxr, dxi, xr, xi, p["a_row"], name="s5_scan_bwd")
    du = _matmul(dbr, p["b_re"], "nt", name="s5_du_re", add=du)
    du = _matmul(dbi, p["b_im"], "nt", name="s5_du_im", add=du)
    db_re = _matmul(u, dbr, "tn", name="s5_db_re")
    db_im = _matmul(u, dbi, "tn", name="s5_db_im")
    unblk = lambda d: _blockdiag_t(d, 32).transpose(1, 0, 2)
    g_sp = _block_vjp(_s5_param, p["sp"], (dar.reshape(32, 64), dai.reshape(32, 64), unblk(db_re), unblk(db_im)), name="s5_param_bwd")
    dproj = jnp.concatenate([dz, du, dxbc, ddt, jnp.zeros((s, E_PROJ - 4736), F32)], axis=1)
    uncblk = lambda d: _blockdiag_t(d, 32).transpose(0, 2, 1)
    grads = dict(
        s5_lam_re=g_sp[0].reshape(1, 32, 64), s5_lam_im=g_sp[1].reshape(1, 32, 64), s5_log_step=g_sp[2].reshape(1, 32),
        s5_b_re=g_sp[3].transpose(1, 2, 0)[None], s5_b_im=g_sp[4].transpose(1, 2, 0)[None],
        s5_c_re=uncblk(dc_re)[None], s5_c_im=-uncblk(dc_imn)[None], s5_d=dd, s5_glu_w=dgw[None], s5_glu_b=dgb,
        ssd_conv_w=dcw[None], ssd_conv_b=dcb, ssd_dt_bias=ddtb[:, :24], ssd_a_log=dalog[:, :24], ssd_d=ddsk[:, :24],
        ssd_norm=dng.reshape(1, 1536))
    return dproj, grads, (carried[0] if carried else None)


def _odd_prep(w):
    pad128 = lambda x: _pad_rows(x, LANES)
    return dict(
        mu=_mu_cols(w["rwkv_mu"].reshape(1, 3520)), w0=w["rwkv_w0"].reshape(1, 1024), w_up=pad128(w["rwkv_w_up"].reshape(96, 1024)),
        a0=w["rwkv_a0"].reshape(1, 1024), a_up=pad128(w["rwkv_a_up"].reshape(96, 1024)), g_up=w["rwkv_g_up"].reshape(256, 1024),
        k_k=w["rwkv_k_k"].reshape(1, 1024), k_a=w["rwkv_k_a"].reshape(1, 1024), r_k=w["rwkv_r_k"].reshape(1, 1024),
        ln_g=w["rwkv_ln_g"].reshape(1, 1024), ln_b=w["rwkv_ln_b"].reshape(1, 1024), j=_head_ones(1024),
        conv_w=w["lru_conv_w"].reshape(4, 1024), conv_b=w["lru_conv_b"].reshape(1, 1024),
        wax=jnp.concatenate([_blockdiag(w["lru_w_a"].reshape(16, 64, 64)), _blockdiag(w["lru_w_x"].reshape(16, 64, 64))], axis=1),
        bax=jnp.concatenate([w["lru_b_a"].reshape(1, 1024), w["lru_b_x"].reshape(1, 1024)], axis=1), lam=w["lru_lam"].reshape(1, 1024))


_O_XMAP = lambda j: jnp.where(j < 24, j, j + 16)
_O_LMAP = lambda j: 24 + j


def _to_heads(x):
    return x.reshape(x.shape[0], 16, HEAD).transpose(0, 2, 1)


def _from_heads(x3):
    return x3.transpose(0, 2, 1).reshape(x3.shape[0], 16 * HEAD)


def _odd_rows(fp, y, k2, g):
    pre = [_row(fp, 1024, 1), _row(fp, 256, 12), _row(fp, 128, 26), _row(fp, 128, 27)]
    post = None if y is None else [_row(y), _row(fp, 1024, 0), _row(k2), _row(fp, 1024, 2), _row(g)]
    return pre, post


def _odd_fwd(proj, p, carry=None):
    fp = _ct_fwd(_tshift, proj, _O_XMAP, 28, [(p["mu"], 0)], 3584, name="rwkv_shift")
    pre_rows, _ = _odd_rows(fp, None, None, None)
    pre_params = [p["w0"], p["w_up"], p["a0"], p["a_up"], p["g_up"], p["k_k"], p["k_a"], p["j"]]
    decay, kkn, b, k2, g = _rw_fwd(_rwkv_pre, pre_rows, pre_params, [(1024, F32)] * 5, name="rwkv_pre")
    v3 = _to_heads(fp[:, 2048:3072]).astype(BF16)
    y3, saved, *carried = _rwkv_fwd(decay, kkn, b, k2, fp, v3, name="rwkv_scan", carry=carry)
    y = _from_heads(y3)
    _, post_rows = _odd_rows(fp, y, k2, g)
    (yc,) = _rw_fwd(_rwkv_post, post_rows, [p["r_k"], p["ln_g"], p["ln_b"], p["j"]], [(1024, F32)], name="rwkv_post")
    xc = _ct_fwd(_conv_taps, proj, _O_LMAP, 8, [(p["conv_w"], 0), (p["conv_b"], 0)], 1024, name="lru_conv")
    pre = _matmul(xc, p["wax"], "nn", name="lru_gates")
    a, bx = _rw_fwd(_lru_pre, [_row(pre), _row(xc)], [p["bax"], p["lam"]], [(1024, F32)] * 2, name="lru_pre")
    hseq = _rscan_fwd(a, bx, name="lru_scan")
    (yd,) = _rw_fwd(_lru_post, [_row(hseq), _row(proj, 1024, 4)], [], [(1024, F32)], name="lru_post")
    return jnp.concatenate([yc, yd], axis=1), (fp, decay, kkn, b, k2, g, v3, saved, y, xc, pre, a, hseq), (carried[0] if carried else None)


def _odd_bwd(proj, p, res, dy, carry=None):
    fp, decay, kkn, b, k2, g, v3, saved, y, xc, pre, a, hseq = res
    s = proj.shape[0]
    (dh, dgl2), _ = _rw_bwd(_lru_post, [_row(hseq), _row(proj, 1024, 4)], [], [dy[:, 1024:]], name="lru_post_bwd")
    dbx, da = _rscan_bwd(dh, a, hseq, name="lru_scan_bwd")
    (dpre, dxc), (dbax, dlam) = _rw_bwd(_lru_pre, [_row(pre), _row(xc)], [p["bax"], p["lam"]], [da, dbx], name="lru_pre_bwd")
    dxc = _matmul(dpre, p["wax"], "nt", name="lru_gates_dx", add=dxc)
    dwax = _matmul(xc, dpre, "tn", name="lru_gates_dw")
    dxl, (dlcw, dlcb) = _ct_bwd(_conv_taps, proj, _O_LMAP, 8, [(p["conv_w"], 0), (p["conv_b"], 0)], [dxc], name="lru_conv_bwd")
    pre_rows, post_rows = _odd_rows(fp, y, k2, g)
    (dyn, dr1, dk2a, dv1, dg), (dr_k, dln_g, dln_b) = _rw_bwd(
        _rwkv_post, post_rows, [p["r_k"], p["ln_g"], p["ln_b"], p["j"]], [dy[:, :1024]], name="rwkv_post_bwd", param_grads=[0, 1, 2])
    ddecay, dkkn, db, dk2b, dr2, dv3, *carried = _rwkv_bwd(
        decay, kkn, b, k2, fp, v3, saved, _to_heads(dyn).astype(BF16), name="rwkv_scan_bwd", carry=carry)
    pre_params = [p["w0"], p["w_up"], p["a0"], p["a_up"], p["g_up"], p["k_k"], p["k_a"], p["j"]]
    (dk, dgl, dwl, dal), (dw0, dw_up, da0, da_up, dg_up, dk_k, dk_a) = _rw_bwd(
        _rwkv_pre, pre_rows, pre_params, [ddecay, dkkn, db, [dk2a, dk2b], dg], name="rwkv_pre_bwd", param_grads=list(range(7)))
    z = lambda n: jnp.zeros((s, n), F32)
    g1 = jnp.concatenate([dr1, dk, dv1, dgl, dwl, dal], axis=1)
    g2 = jnp.concatenate([dr2, z(1024), _from_heads(dv3), z(512)], axis=1)
    dfp, (dmu,) = _ct_bwd(_tshift, proj, _O_XMAP, 28, [(p["mu"], 0)], [g1, g2], name="rwkv_shift_bwd")
    dproj = jnp.concatenate([dfp[:, :3072], dxl, dgl2, dfp[:, 3072:]], axis=1)
    grads = dict(
        rwkv_mu=_mu_cols_t(dmu), rwkv_w0=dw0, rwkv_w_up=dw_up[:96][None], rwkv_a0=da0, rwkv_a_up=da_up[:96][None], rwkv_g_up=dg_up[None],
        rwkv_k_k=dk_k, rwkv_k_a=dk_a, rwkv_r_k=dr_k.reshape(1, 16, 64), rwkv_ln_g=dln_g, rwkv_ln_b=dln_b,
        lru_conv_w=dlcw[None], lru_conv_b=dlcb, lru_w_a=_blockdiag_t(dwax[:, :1024], 16)[None], lru_w_x=_blockdiag_t(dwax[:, 1024:], 16)[None],
        lru_b_a=dbax[:, :1024].reshape(1, 16, 64), lru_b_x=dbax[:, 1024:].reshape(1, 16, 64), lru_lam=dlam.reshape(1, 16, 64))
    return dproj, grads, (carried[0] if carried else None)


def _my_index():
    return 4 * lax.axis_index("x") + 2 * lax.axis_index("y") + lax.axis_index("c")


def _peer(k):
    x, y, c = lax.axis_index("x"), lax.axis_index("y"), lax.axis_index("c")
    return (1 - x if k & 4 else x, 1 - y if k & 2 else y, 1 - c if k & 1 else c)


class _Exchange:
    SCRATCH = (pltpu.SemaphoreType.DMA((N_DEV - 1,)), pltpu.SemaphoreType.DMA((N_DEV - 1,)), pltpu.SemaphoreType.DMA)

    def __init__(self, src_ref, out_ref, send_sems, recv_sems, local_sem, gather):
        me = _my_index()
        mine = src_ref if gather else src_ref.at[me]
        self.local = pltpu.make_async_copy(mine, out_ref.at[me], local_sem)
        rdma = lambda src, dst, k: pltpu.make_async_remote_copy(
            src_ref=src, dst_ref=dst, send_sem=send_sems.at[k - 1], recv_sem=recv_sems.at[k - 1],
            device_id=_peer(k), device_id_type=pl.DeviceIdType.MESH)
        ks = range(1, N_DEV)
        self.sends = [rdma(src_ref if gather else src_ref.at[jnp.bitwise_xor(me, k)], out_ref.at[me], k) for k in ks]
        self.arrivals = [rdma(mine, out_ref.at[jnp.bitwise_xor(me, k)], k) for k in ks]

    def start(self):
        self.local.start()
        for cp in self.sends:
            cp.start()

    def wait(self):
        for cp in self.arrivals:
            cp.wait_recv()
        for cp in self.sends:
            cp.wait_send()
        self.local.wait()


class _TwoLevelGather:
    def __init__(self, src_ref, out_ref, send_sems, recv_sems, local_sem):
        x, y, c = lax.axis_index("x"), lax.axis_index("y"), lax.axis_index("c")
        me, self.sibling, self.c = (x, y, c), (x, y, 1 - c), c
        self.me = me
        self.chips = [(1 - x, y), (x, 1 - y), (1 - x, 1 - y)]
        slab = lambda px, py, pc: out_ref.at[4 * px + 2 * py + pc]

        def copy(k, block, to, own=False):
            return pltpu.make_async_remote_copy(
                src_ref=src_ref if own else slab(*block), dst_ref=slab(*block), send_sem=send_sems.at[k], recv_sem=recv_sems.at[k],
                device_id=to, device_id_type=pl.DeviceIdType.MESH)

        self.copy = copy
        self.mine = pltpu.make_async_copy(src_ref, slab(*me), local_sem)
        self.first = [copy(0, me, self.sibling, own=True)] + [copy(1 + j, me, (*chip, c), own=True) for j, chip in enumerate(self.chips)]

    def start(self):
        self.mine.start()
        for cp in self.first:
            cp.start()

    def wait(self):
        passed = [self.copy(4 + j, (*chip, self.c), self.sibling) for j, chip in enumerate(self.chips)]
        for j, chip in enumerate(self.chips):
            self.copy(1 + j, (*chip, self.c), self.me).wait_recv()
            passed[j].start()
        self.copy(0, self.sibling, self.me).wait_recv()
        for j, chip in enumerate(self.chips):
            self.copy(4 + j, (*chip, 1 - self.c), self.me).wait_recv()
        for cp in self.first + passed:
            cp.wait_send()
        self.mine.wait()


TWO_LEVEL = "two-level gather"


def _make_exchange(src_ref, out_ref, sems, kind):
    return _TwoLevelGather(src_ref, out_ref, *sems) if kind == TWO_LEVEL else _Exchange(src_ref, out_ref, *sems, kind)


def _exchange(src, *, gather, name):
    def body(src_ref, out_ref, *sems):
        ex = _make_exchange(src_ref, out_ref, sems, gather)
        ex.start()
        ex.wait()

    return pl.pallas_call(
        body, out_shape=jax.ShapeDtypeStruct((N_DEV, src.shape[-2], LANES), src.dtype),
        in_specs=[pl.BlockSpec(memory_space=pl.ANY)], out_specs=pl.BlockSpec(memory_space=pl.ANY),
        scratch_shapes=list(_Exchange.SCRATCH), name=name)(src)


PACK_ALIGN = 16 * LANES
PACK_ROWS = 512


def _pack(arrs, dtype, lead=False):
    parts, rows = [], 0
    for a in arrs:
        n_lead = a.shape[0] if lead else 1
        n = a.size // n_lead
        a = a.astype(dtype)
        if n % PACK_ALIGN:
            a = jnp.pad(a.reshape(n_lead, n), ((0, 0), (0, -n % PACK_ALIGN)))
        parts.append(a.reshape(n_lead, -1, LANES))
        rows += parts[-1].shape[1]
    if rows % PACK_ROWS:
        parts.append(jnp.zeros((parts[0].shape[0], -rows % PACK_ROWS, LANES), dtype))
    buf = jnp.concatenate(parts, axis=1)
    return buf if lead else buf[0]


def _unpack(buf, shapes, lead=False):
    buf = buf if lead else buf[None]
    out, off = [], 0
    for shp in shapes:
        n = math.prod(shp)
        rows = (n + (-n % PACK_ALIGN)) // LANES
        piece = buf[:, off:off + rows]
        if n % PACK_ALIGN:
            piece = piece.reshape(buf.shape[0], rows * LANES)[:, :n]
        out.append(piece.reshape(((buf.shape[0],) if lead else ()) + tuple(shp)))
        off += rows
    return out


def _unshard(parts, axis):
    moved = jnp.moveaxis(parts, 0, axis)
    shp = list(moved.shape)
    return moved.reshape(shp[:axis] + [shp[axis] * shp[axis + 1]] + shp[axis + 2:])


def _to_parts(full, axis):
    shp = list(full.shape)
    split = full.reshape(shp[:axis] + [N_DEV, shp[axis] // N_DEV] + shp[axis + 1:])
    return jnp.moveaxis(split, axis, 0)


def _adamw(gparts, w, m, v, *, name):
    r = w.shape[0]
    tr = _pick(r, (PACK_ROWS,))

    def body(g_ref, w_ref, m_ref, v_ref, go, do, mo, vo):
        g = g_ref[0].astype(F32)
        for d in range(1, N_DEV):
            g = g + g_ref[d].astype(F32)
        m1 = ADAM_B1 * m_ref[...] + (1.0 - ADAM_B1) * g
        v1 = ADAM_B2 * v_ref[...] + (1.0 - ADAM_B2) * jnp.square(g)
        m_hat = m1 / (1.0 - ADAM_B1 ** ADAM_STEP)
        v_hat = v1 / (1.0 - ADAM_B2 ** ADAM_STEP)
        go[...] = g
        do[...] = -ADAM_LR * (m_hat / (jnp.sqrt(v_hat) + ADAM_EPS) + ADAM_WD * w_ref[...])
        mo[...] = m1
        vo[...] = v1

    blk = pl.BlockSpec((tr, LANES), lambda i: (i, 0))
    return pl.pallas_call(
        body, grid=(r // tr,), in_specs=[pl.BlockSpec((N_DEV, tr, LANES), lambda i: (0, i, 0)), blk, blk, blk], out_specs=[blk] * 4,
        out_shape=[jax.ShapeDtypeStruct((r, LANES), F32)] * 4, compiler_params=_cparams("parallel"), name=name)(gparts, w, m, v)


def _loss_and_grad(h, g, tgt, *, name, ts=256):
    s, d = h.shape
    ts = min(ts, s)

    def tile_loss(hv, gv, tv):
        (y,) = _rmsnorm_tile(0, hv, gv)
        return 0.5 * jnp.sum(jnp.mean(jnp.square(y - tv), axis=-1))

    def body(h_ref, g_ref, t_ref, l_ref, dh_ref, dg_ref):
        @pl.when(pl.program_id(0) == 0)
        def _():
            l_ref[...] = jnp.zeros_like(l_ref)
            dg_ref[...] = jnp.zeros_like(dg_ref)

        tv = t_ref[...]
        loss, vjp = jax.vjp(lambda hv, gv: tile_loss(hv, gv, tv), h_ref[...], g_ref[...])
        dh, dg = vjp(jnp.ones((), F32))
        l_ref[...] += loss
        dh_ref[...] = dh
        dg_ref[...] += dg

    row = pl.BlockSpec((ts, d), lambda i: (i, 0))
    return pl.pallas_call(
        body, grid=(s // ts,), in_specs=[row, pl.BlockSpec((1, d), lambda i: (0, 0)), row],
        out_specs=[pl.BlockSpec((8, LANES), lambda i: (0, 0)), row, pl.BlockSpec((1, d), lambda i: (0, 0))],
        out_shape=[jax.ShapeDtypeStruct((8, LANES), F32), jax.ShapeDtypeStruct((s, d), F32), jax.ShapeDtypeStruct((1, d), F32)],
        compiler_params=_cparams("arbitrary"), name=name)(h, g, tgt)


def _norm(h, g, name):
    return _rw_fwd(_rmsnorm_tile, [_row(h)], [g], [(h.shape[1], BF16)], name=name)[0]


def _norm_bwd(h, g, dhn, dres, name):
    (dh,), (dg,) = _rw_bwd(_rmsnorm_tile, [_row(h)], [g], [dhn], add_rows={0: dres}, name=name)
    return dh, dg


def _layer_fwd(h, p_i, lw, mixer_fwd, mp, tag, carry=None, late=None):
    hn = _norm(h, lw["norm_mix"], f"{tag}_norm_mix")
    proj = _matmul(hn, lw["w_in"], "nn", name=f"{tag}_in_proj")
    y, mres, carried = mixer_fwd(proj, mp, carry)
    if late is not None:
        lw = {**lw, **late(carried)}
    h1 = _matmul(y, lw["w_out"], "nn", name=f"{tag}_out_proj", add=h)
    hn2 = _norm(h1, lw["norm_ffn"], f"{tag}_norm_ffn")
    act = _matmul(hn2, lw["w1"], "nn", name=f"{tag}_mlp_up", post=_relu2_of, out_dtype=BF16)
    h2 = _matmul(act, lw["w2"], "nn", name=f"{tag}_mlp_down", add=h1)
    hn3 = _norm(h2, lw["norm_pl"], f"{tag}_norm_pl")
    gl = _matmul(hn3, lw["w_gate"], "nn", name=f"{tag}_pl_gate")
    e = _matmul(p_i, lw["w_pl"], "nn", name=f"{tag}_pl_proj")
    (h3,) = _rw_fwd(_pl_gate, [_row(h2), _row(gl), _row(e)], [], [(h.shape[1], F32)], name=f"{tag}_pl_mix")
    return h3, (h, hn, proj, y, mres, h1, hn2, act, h2, hn3, gl, e), lw


def _layer_bwd(dh3, p_i, lw, mixer_bwd, mp, saved, tag, early=None):
    h, hn, proj, y, mres, h1, hn2, act, h2, hn3, gl, e = saved
    (dgl, de), _ = _rw_bwd(_pl_gate, [_row(h2), _row(gl), _row(e)], [], [dh3], row_grads=[1, 2], name=f"{tag}_pl_mix_bwd")
    g = dict(w_pl=_matmul(p_i, de, "tn", name=f"{tag}_pl_proj_dw"), w_gate=_matmul(hn3, dgl, "tn", name=f"{tag}_pl_gate_dw"))
    dhn3 = _matmul(dgl, lw["w_gate"], "nt", name=f"{tag}_pl_gate_dx")
    dh2, g["norm_pl"] = _norm_bwd(h2, lw["norm_pl"], dhn3, dh3, f"{tag}_norm_pl_bwd")
    da1 = _matmul(dh2, lw["w2"], "nt", name=f"{tag}_mlp_down_dx", post=_relu2_grad, aux=act, out_dtype=BF16)
    g["w2"] = _matmul(act, dh2, "tn", name=f"{tag}_mlp_down_dw")
    g["w1"] = _matmul(hn2, da1, "tn", name=f"{tag}_mlp_up_dw")
    dhn2 = _matmul(da1, lw["w1"], "nt", name=f"{tag}_mlp_up_dx")
    dh1, g["norm_ffn"] = _norm_bwd(h1, lw["norm_ffn"], dhn2, dh2, f"{tag}_norm_ffn_bwd")
    dy = _matmul(dh1, lw["w_out"], "nt", name=f"{tag}_out_proj_dx")
    g["w_out"] = _matmul(y, dh1, "tn", name=f"{tag}_out_proj_dw")
    dproj, mg, carried = mixer_bwd(proj, mp, mres, dy, None if early is None else early(g))
    g["w_in"] = _matmul(hn, dproj, "tn", name=f"{tag}_in_proj_dw")
    dhn = _matmul(dproj, lw["w_in"], "nt", name=f"{tag}_in_proj_dx")
    dh, g["norm_mix"] = _norm_bwd(h, lw["norm_mix"], dhn, dh1, f"{tag}_norm_mix_bwd")
    return dh, g, mg, carried


WEIGHTS = (
    ("norm_mix", None), ("norm_ffn", None), ("norm_pl", None), ("mlp_w1", 2), ("mlp_w2", 1), ("pl_proj", 2), ("pl_gate", 1),
    ("e_in_proj", 2), ("e_out_proj", 1), ("s5_lam_re", None), ("s5_lam_im", None), ("s5_log_step", None), ("s5_b_re", None),
    ("s5_b_im", None), ("s5_c_re", None), ("s5_c_im", None), ("s5_d", None), ("s5_glu_w", 1), ("s5_glu_b", None),
    ("ssd_conv_w", 2), ("ssd_conv_b", None), ("ssd_dt_bias", None), ("ssd_a_log", None), ("ssd_d", None), ("ssd_norm", None),
    ("o_in_proj", 2), ("o_out_proj", 1), ("rwkv_mu", 1), ("rwkv_w0", 1), ("rwkv_w_up", 2), ("rwkv_a0", 1), ("rwkv_a_up", 2),
    ("rwkv_g_up", 2), ("rwkv_k_k", 1), ("rwkv_k_a", 1), ("rwkv_r_k", None), ("rwkv_ln_g", 1), ("rwkv_ln_b", 1),
    ("lru_conv_w", 2), ("lru_conv_b", 1), ("lru_w_a", None), ("lru_b_a", None), ("lru_w_x", None), ("lru_b_x", None),
    ("lru_lam", None), ("norm_final", None))
MATMUL_WEIGHTS = ("mlp_w1", "mlp_w2", "pl_proj", "pl_gate", "e_in_proj", "e_out_proj", "s5_glu_w", "o_in_proj", "o_out_proj",
                  "rwkv_w_up", "rwkv_a_up", "rwkv_g_up")


LATE_WEIGHTS = (("mlp_w1", 1), ("mlp_w2", 1), ("pl_gate", 1), ("pl_proj", 1))
EARLY_GRADS = LATE_WEIGHTS + (("o_out_proj", 0),)


def kernel(x, p, norm_mix, norm_ffn, norm_pl, mlp_w1, mlp_w2, pl_proj, pl_gate, e_in_proj, e_out_proj, s5_lam_re, s5_lam_im, s5_log_step, s5_b_re, s5_b_im, s5_c_re, s5_c_im, s5_d, s5_glu_w, s5_glu_b, ssd_conv_w, ssd_conv_b, ssd_dt_bias, ssd_a_log, ssd_d, ssd_norm, o_in_proj, o_out_proj, rwkv_mu, rwkv_w0, rwkv_w_up, rwkv_a0, rwkv_a_up, rwkv_g_up, rwkv_k_k, rwkv_k_a, rwkv_r_k, rwkv_ln_g, rwkv_ln_b, lru_conv_w, lru_conv_b, lru_w_a, lru_b_a, lru_w_x, lru_b_x, lru_lam, norm_final, loss_target, m_norm_mix, m_norm_ffn, m_norm_pl, m_mlp_w1, m_mlp_w2, m_pl_proj, m_pl_gate, m_e_in_proj, m_e_out_proj, m_s5_lam_re, m_s5_lam_im, m_s5_log_step, m_s5_b_re, m_s5_b_im, m_s5_c_re, m_s5_c_im, m_s5_d, m_s5_glu_w, m_s5_glu_b, m_ssd_conv_w, m_ssd_conv_b, m_ssd_dt_bias, m_ssd_a_log, m_ssd_d, m_ssd_norm, m_o_in_proj, m_o_out_proj, m_rwkv_mu, m_rwkv_w0, m_rwkv_w_up, m_rwkv_a0, m_rwkv_a_up, m_rwkv_g_up, m_rwkv_k_k, m_rwkv_k_a, m_rwkv_r_k, m_rwkv_ln_g, m_rwkv_ln_b, m_lru_conv_w, m_lru_conv_b, m_lru_w_a, m_lru_b_a, m_lru_w_x, m_lru_b_x, m_lru_lam, m_norm_final, v_norm_mix, v_norm_ffn, v_norm_pl, v_mlp_w1, v_mlp_w2, v_pl_proj, v_pl_gate, v_e_in_proj, v_e_out_proj, v_s5_lam_re, v_s5_lam_im, v_s5_log_step, v_s5_b_re, v_s5_b_im, v_s5_c_re, v_s5_c_im, v_s5_d, v_s5_glu_w, v_s5_glu_b, v_ssd_conv_w, v_ssd_conv_b, v_ssd_dt_bias, v_ssd_a_log, v_ssd_d, v_ssd_norm, v_o_in_proj, v_o_out_proj, v_rwkv_mu, v_rwkv_w0, v_rwkv_w_up, v_rwkv_a0, v_rwkv_a_up, v_rwkv_g_up, v_rwkv_k_k, v_rwkv_k_a, v_rwkv_r_k, v_rwkv_ln_g, v_rwkv_ln_b, v_lru_conv_w, v_lru_conv_b, v_lru_w_a, v_lru_b_a, v_lru_w_x, v_lru_b_x, v_lru_lam, v_norm_final):
    a = dict(locals())
    d_model = x.shape[-1]
    row = lambda v: v.reshape(1, d_model)
    axis = dict(WEIGHTS)
    keys = [(n, i) for n, ax in WEIGHTS if ax is not None for i in range(a[n].shape[0])]
    shard = lambda pre, key: a[pre + key[0]][key[1]]
    piece_axis = lambda key: axis[key[0]] - 1

    def gathered(got, ks):
        parts = _unpack(got, [shard("", k).shape for k in ks], lead=True)
        return {k: _unshard(pt, piece_axis(k)) for k, pt in zip(ks, parts)}

    in_layer1 = lambda k: k[1] == 1 or k[0].startswith(("o_", "rwkv_", "lru_"))
    first_keys = [k for k in keys if k[0] in MATMUL_WEIGHTS and not in_layer1(k)]
    mix1_keys = [k for k in keys if k[0] in MATMUL_WEIGHTS and in_layer1(k) and k not in LATE_WEIGHTS]
    f32_keys = [k for k in keys if k[0] not in MATMUL_WEIGHTS]
    full = gathered(_exchange(_pack([shard("", k) for k in first_keys], BF16), gather=TWO_LEVEL, name="gather_weights_bf16"), first_keys)
    full.update(gathered(_exchange(_pack([shard("", k) for k in f32_keys], F32), gather=True, name="gather_weights_f32"), f32_keys))
    by_name = lambda: {n: (a[n] if ax is None else full.get((n, 0))) for n, ax in WEIGHTS}

    lw0 = dict(norm_mix=row(norm_mix[0]), norm_ffn=row(norm_ffn[0]), norm_pl=row(norm_pl[0]), w_in=_even_in_cols(full["e_in_proj", 0]),
               w_out=full["e_out_proj", 0], w1=full["mlp_w1", 0], w2=full["mlp_w2", 0], w_gate=full["pl_gate", 0], w_pl=full["pl_proj", 0])
    mp0 = _even_prep(by_name())
    mix1_src = _pack([shard("", k) for k in mix1_keys], BF16)

    def take_mix1(got):
        full.update(gathered(got, mix1_keys))
        return {}

    h1, saved0, lw0 = _layer_fwd(x[0], p[0, 0], lw0, _even_fwd, mp0, "l0", carry=(mix1_src, TWO_LEVEL), late=take_mix1)

    def late(got):
        fl = gathered(got, LATE_WEIGHTS)
        return dict(w1=fl["mlp_w1", 1], w2=fl["mlp_w2", 1], w_gate=fl["pl_gate", 1], w_pl=fl["pl_proj", 1])

    lw1 = dict(norm_mix=row(norm_mix[1]), norm_ffn=row(norm_ffn[1]), norm_pl=row(norm_pl[1]), w_in=_odd_in_cols(full["o_in_proj", 0]),
               w_out=full["o_out_proj", 0])
    mp1 = _odd_prep(by_name())
    late_src = _pack([shard("", k) for k in LATE_WEIGHTS], BF16)
    h2, saved1, lw1 = _layer_fwd(h1, p[1, 0], lw1, _odd_fwd, mp1, "l1", carry=(late_src, True), late=late)
    loss_blk, dh, dg_final = _loss_and_grad(h2, row(norm_final), loss_target[0], name="loss")
    loss = lax.psum(loss_blk[0, 0], ("x", "y", "c"))

    def slabs(grad_of, ks):
        return _pack([_to_parts(grad_of[k], piece_axis(k)) for k in ks], BF16, lead=True)

    def global_shape(k):
        return tuple((N_DEV if d == piece_axis(k) else 1) * n for d, n in enumerate(shard("", k).shape))

    early_grads = lambda g: (slabs({("mlp_w1", 1): g["w1"], ("mlp_w2", 1): g["w2"], ("pl_gate", 1): g["w_gate"],
                                    ("pl_proj", 1): g["w_pl"], ("o_out_proj", 0): g["w_out"]}, EARLY_GRADS), False)
    dh, g1, mg1, early_got = _layer_bwd(dh, p[1, 0], lw1, _odd_bwd, mp1, saved1, "l1", early=early_grads)
    l0a_keys = [("mlp_w1", 0), ("pl_proj", 0)]
    mid_keys = [k for k in keys if in_layer1(k) and k not in EARLY_GRADS] + l0a_keys
    mid_grad = {k: (_odd_in_cols_t(g1["w_in"]) if k[0] == "o_in_proj" else mg1[k[0]].reshape(global_shape(k))) for k in mid_keys[:-2]}
    l0_early = lambda g: (slabs({**mid_grad, ("mlp_w1", 0): g["w1"], ("pl_proj", 0): g["w_pl"]}, mid_keys), False)
    dh, g0, mg0, mid_got = _layer_bwd(dh, p[0, 0], lw0, _even_bwd, mp0, saved0, "l0", early=l0_early)

    main_keys = [k for k in keys if not in_layer1(k) and k not in l0a_keys]
    piece_grad = {("mlp_w2", 0): g0["w2"], ("pl_gate", 0): g0["w_gate"], ("e_in_proj", 0): _even_in_cols_t(g0["w_in"]), ("e_out_proj", 0): g0["w_out"]}
    mixer_grads = {**mg0, **mg1}
    for k in main_keys:
        if k not in piece_grad:
            piece_grad[k] = mg0[k[0]].reshape(global_shape(k))
    main_got = _exchange(slabs(piece_grad, main_keys), gather=False, name="scatter_grads")
    piece_out = {}
    for got, ks, tag in ((early_got, EARLY_GRADS, "early"), (mid_got, mid_keys, "mid"), (main_got, main_keys, "main")):
        res = _adamw(got, *[_pack([shard(pre, k) for k in ks], F32) for pre in ("", "m_", "v_")], name=f"adamw_sharded_{tag}")
        for j in range(4):
            piece_out.update({(j, k): v for k, v in zip(ks, _unpack(res[j], [shard("", k).shape for k in ks]))})
    rp = [n for n, ax in WEIGHTS if ax is None]
    rp_grads = {**mixer_grads, "norm_final": dg_final,
                **{n: jnp.stack([g0[n], g1[n]]) for n in ("norm_mix", "norm_ffn", "norm_pl")}}
    parts = _exchange(_pack([rp_grads[n].reshape(a[n].shape) for n in rp], F32), gather=True, name="gather_small_grads")
    rp_res = _adamw(parts, *[_pack([a[pre + n] for n in rp], F32) for pre in ("", "m_", "v_")], name="adamw_replicated")

    outs = []
    for j in range(4):
        rp_out = dict(zip(rp, _unpack(rp_res[j], [a[n].shape for n in rp])))
        outs.extend(rp_out[n] if ax is None else jnp.stack([piece_out[j, (n, i)] for i in range(a[n].shape[0])]) for n, ax in WEIGHTS)
    return (loss, dh[None], *outs)
```

```python
import functools
import math

import jax
import jax.numpy as jnp
from jax import lax
from jax.experimental import pallas as pl
from jax.experimental.pallas import tpu as pltpu

F32 = jnp.float32
BF16 = jnp.bfloat16
N_DEV = 8
LANES = 128
VMEM_LIMIT = 56 * 1024 * 1024
MATMUL_VMEM = 46 * 1024 * 1024
NORM_EPS = 1e-6
RWKV_GN_EPS = 64e-5
LRU_C = 8.0
ADAM_LR, ADAM_B1, ADAM_B2, ADAM_EPS, ADAM_WD, ADAM_STEP = 0.001, 0.9, 0.999, 1e-08, 0.01, 10
SSD_CHUNK = 128
RWKV_CHUNK = 32
HEAD = 64


def _cparams(*sem):
    return pltpu.CompilerParams(dimension_semantics=sem, vmem_limit_bytes=VMEM_LIMIT)


def _pick(n, prefs):
    for t in prefs:
        if n % t == 0:
            return t
    return n


def _relu2_of(val, _):
    r = jnp.maximum(val, 0.0)
    return r * r


def _relu2_grad(dact, act):
    return dact * (2.0 * jnp.sqrt(act.astype(F32)))


def _matmul(a, b, mode, *, name, add=None, out_dtype=F32, post=None, aux=None):
    if mode == "nn":
        (m, k), (k2, n) = a.shape, b.shape
    elif mode == "nt":
        (m, k), (n, k2) = a.shape, b.shape
    else:
        (k, m), (k2, n) = a.shape, b.shape
    assert k == k2, (a.shape, b.shape, mode)
    tm, tn = _pick(m, (1024, 512, 256, 128)), _pick(n, (1024, 512, 256, 128))
    extras = [x for x in (add, aux) if x is not None]
    fits = lambda t: 2 * t * (tm * a.dtype.itemsize + tn * b.dtype.itemsize) + (3 + 2 * len(extras)) * tm * tn * 4 <= MATMUL_VMEM
    tk = next((t for t in (2048, 1024, 512, 256, 128) if k % t == 0 and fits(t)), k)
    nk = k // tk
    dn = {"nn": (((1,), (0,)), ((), ())), "nt": (((1,), (1,)), ((), ())), "tn": (((0,), (0,)), ((), ()))}[mode]
    a_spec = pl.BlockSpec((tk, tm), lambda i, j, kk: (kk, i)) if mode == "tn" else pl.BlockSpec((tm, tk), lambda i, j, kk: (i, kk))
    b_spec = pl.BlockSpec((tn, tk), lambda i, j, kk: (j, kk)) if mode == "nt" else pl.BlockSpec((tk, tn), lambda i, j, kk: (kk, j))
    o_spec = pl.BlockSpec((tm, tn), lambda i, j, kk: (i, j))

    def body(a_ref, b_ref, *rest):
        extra_refs, (o_ref, acc) = list(rest[:len(extras)]), rest[len(extras):]
        add_ref = extra_refs.pop(0) if add is not None else None
        aux_ref = extra_refs.pop(0) if aux is not None else None
        kk = pl.program_id(2)
        prod = lambda: lax.dot_general(a_ref[...].astype(BF16), b_ref[...].astype(BF16), dn, preferred_element_type=F32)
        first = lambda: prod() if add is None else prod() + add_ref[...].astype(F32)

        def write(val):
            if post is not None:
                val = post(val, None if aux is None else aux_ref[...])
            o_ref[...] = val.astype(o_ref.dtype)

        if nk == 1:
            write(first())
            return

        @pl.when(kk == 0)
        def _():
            acc[...] = first()

        @pl.when((kk > 0) & (kk < nk - 1))
        def _():
            acc[...] += prod()

        @pl.when(kk == nk - 1)
        def _():
            write(acc[...] + prod())

    ins, specs = [a, b] + extras, [a_spec, b_spec] + [o_spec] * len(extras)
    return pl.pallas_call(
        body, grid=(m // tm, n // tn, nk), in_specs=specs, out_specs=o_spec,
        out_shape=jax.ShapeDtypeStruct((m, n), out_dtype), scratch_shapes=[pltpu.VMEM((tm, tn), F32)],
        compiler_params=_cparams("parallel", "parallel", "arbitrary"), name=name)(*ins)


def _row(x, width=None, block=0):
    return (x, x.shape[1] if width is None else width, block)


def _rw_specs(rows, params, ts):
    specs = [pl.BlockSpec((ts, w), functools.partial(lambda i, b: (i, b), b=bi)) for (_, w, bi) in rows]
    specs += [pl.BlockSpec(p.shape, functools.partial(lambda i, nd: (0,) * nd, nd=p.ndim)) for p in params]
    return specs


def _rw_fwd(f, rows, params, outs, *, name, ts=256):
    s = rows[0][0].shape[0]
    ts = min(ts, s)
    nr, npar = len(rows), len(params)

    def body(*refs):
        row0 = pl.program_id(0) * ts
        res = f(row0, *[r[...] for r in refs[:nr + npar]])
        for o_ref, val in zip(refs[nr + npar:], res, strict=True):
            o_ref[...] = val.astype(o_ref.dtype)

    out = pl.pallas_call(
        body, grid=(s // ts,), in_specs=_rw_specs(rows, params, ts),
        out_specs=[pl.BlockSpec((ts, w), lambda i: (i, 0)) for (w, _) in outs],
        out_shape=[jax.ShapeDtypeStruct((s, w), dt) for (w, dt) in outs],
        compiler_params=_cparams("parallel"), name=name)(*[r[0] for r in rows], *params)
    return tuple(out)


def _rw_bwd(f, rows, params, cts, *, name, ts=256, row_grads=None, param_grads=None, add_rows=None):
    s = rows[0][0].shape[0]
    ts = min(ts, s)
    ct_groups = [list(c) if isinstance(c, (list, tuple)) else [c] for c in cts]
    cts = [c for grp in ct_groups for c in grp]
    nr, npar, nct = len(rows), len(params), len(cts)
    row_grads = list(range(nr)) if row_grads is None else list(row_grads)
    param_grads = list(range(npar)) if param_grads is None else list(param_grads)
    add_rows = add_rows or {}
    add_keys = sorted(add_rows)

    def body(*refs):
        i = pl.program_id(0)
        row0 = i * ts
        vals = [r[...] for r in refs[:nr + npar]]
        for pi in param_grads:
            vals[nr + pi] = vals[nr + pi].astype(F32)
        ct_refs = list(refs[nr + npar:nr + npar + nct])
        add_refs = dict(zip(add_keys, refs[nr + npar + nct:nr + npar + nct + len(add_keys)]))
        o_refs = refs[nr + npar + nct + len(add_keys):]
        res, vjp = jax.vjp(functools.partial(f, row0), *vals)
        ct_vals = []
        for grp, r in zip(ct_groups, res, strict=True):
            ct_vals.append(sum(ct_refs.pop(0)[...].astype(r.dtype) for _ in grp))
        grads = vjp(tuple(ct_vals))
        for o_ref, ri in zip(o_refs[:len(row_grads)], row_grads):
            g = grads[ri]
            if ri in add_refs:
                g = g + add_refs[ri][...]
            o_ref[...] = g.astype(o_ref.dtype)

        @pl.when(i == 0)
        def _():
            for o_ref in o_refs[len(row_grads):]:
                o_ref[...] = jnp.zeros_like(o_ref)

        for o_ref, pi in zip(o_refs[len(row_grads):], param_grads):
            o_ref[...] += grads[nr + pi].astype(F32)

    in_specs = _rw_specs(rows, params, ts)
    in_specs += [pl.BlockSpec((ts, c.shape[1]), lambda i: (i, 0)) for c in cts]
    in_specs += [pl.BlockSpec((ts, add_rows[k].shape[1]), lambda i: (i, 0)) for k in add_keys]
    out_specs = [pl.BlockSpec((ts, rows[ri][1]), lambda i: (i, 0)) for ri in row_grads]
    out_specs += [pl.BlockSpec(params[pi].shape, functools.partial(lambda i, nd: (0,) * nd, nd=params[pi].ndim)) for pi in param_grads]
    out_shape = [jax.ShapeDtypeStruct((s, rows[ri][1]), F32) for ri in row_grads]
    out_shape += [jax.ShapeDtypeStruct(params[pi].shape, F32) for pi in param_grads]
    out = pl.pallas_call(
        body, grid=(s // ts,), in_specs=in_specs, out_specs=out_specs, out_shape=out_shape,
        compiler_params=_cparams("arbitrary"), name=name)(*[r[0] for r in rows], *params, *cts, *[add_rows[k] for k in add_keys])
    return tuple(out[:len(row_grads)]), tuple(out[len(row_grads):])


@functools.partial(jax.custom_vjp, nondiff_argnums=(1,))
def _shift_down(x, k):
    rows = lax.broadcasted_iota(jnp.int32, x.shape, 0)
    return jnp.where(rows < k, 0.0, pltpu.roll(x, k, 0))


def _shift_down_fwd(x, k):
    return _shift_down(x, k), None


def _shift_down_bwd(k, _, g):
    n = g.shape[0]
    rows = lax.broadcasted_iota(jnp.int32, g.shape, 0)
    return (jnp.where(rows >= n - k, 0.0, pltpu.roll(g, n - k, 0)),)


_shift_down.defvjp(_shift_down_fwd, _shift_down_bwd)


def _ct_fwd(f, x, xmap, ntiles, params, out_width, *, name, ct=LANES):
    s = x.shape[0]

    def body(*refs):
        refs[-1][...] = f(*[r[...] for r in refs[:-1]])

    in_specs = [pl.BlockSpec((s, ct), lambda j: (0, xmap(j)))]
    in_specs += [pl.BlockSpec((p.shape[0], ct), functools.partial(lambda j, o: (0, o + j), o=o)) for (p, o) in params]
    return pl.pallas_call(
        body, grid=(ntiles,), in_specs=in_specs, out_specs=pl.BlockSpec((s, ct), lambda j: (0, j)),
        out_shape=jax.ShapeDtypeStruct((s, out_width), F32), compiler_params=_cparams("parallel"), name=name)(x, *[p for p, _ in params])


def _ct_bwd(f, x, xmap, ntiles, params, g, *, name, ct=LANES):
    s = x.shape[0]
    npar, ng = len(params), len(g)

    def body(*refs):
        vals = [r[...] for r in refs[:1 + npar]]
        _, vjp = jax.vjp(f, *vals)
        grads = vjp(sum(r[...] for r in refs[1 + npar:1 + npar + ng]))
        for o_ref, gr in zip(refs[1 + npar + ng:], grads, strict=True):
            o_ref[...] = gr

    in_specs = [pl.BlockSpec((s, ct), lambda j: (0, xmap(j)))]
    pspecs = [pl.BlockSpec((p.shape[0], ct), functools.partial(lambda j, o: (0, o + j), o=o)) for (p, o) in params]
    in_specs += pspecs + [pl.BlockSpec((s, ct), lambda j: (0, j))] * ng
    out = pl.pallas_call(
        body, grid=(ntiles,), in_specs=in_specs, out_specs=[pl.BlockSpec((s, ct), lambda j: (0, j))] + pspecs,
        out_shape=[jax.ShapeDtypeStruct((s, ntiles * ct), F32)] + [jax.ShapeDtypeStruct(p.shape, F32) for p, _ in params],
        compiler_params=_cparams("parallel"), name=name)(x, *[p for p, _ in params], *g)
    return out[0], tuple(out[1:])


def _block_vjp(f, args, cts, *, name):
    outs = jax.eval_shape(f, *args)
    if cts is None:
        def body(*refs):
            for o_ref, v in zip(refs[len(args):], f(*[r[...] for r in refs[:len(args)]]), strict=True):
                o_ref[...] = v
        return tuple(pl.pallas_call(body, out_shape=[jax.ShapeDtypeStruct(o.shape, o.dtype) for o in outs], name=name)(*args))

    def body(*refs):
        n = len(args)
        _, vjp = jax.vjp(f, *[r[...] for r in refs[:n]])
        for o_ref, gr in zip(refs[n + len(cts):], vjp(tuple(r[...] for r in refs[n:n + len(cts)])), strict=True):
            o_ref[...] = gr
    return tuple(pl.pallas_call(body, out_shape=[jax.ShapeDtypeStruct(a.shape, a.dtype) for a in args], name=name)(*args, *cts))


def _softplus(x):
    return jnp.maximum(x, 0.0) + jnp.log(1.0 + jnp.exp(-jnp.abs(x)))


def _dot_bf16(a, b):
    return jnp.dot(a.astype(BF16), b.astype(BF16), preferred_element_type=F32)


def _dot3(x, m):
    hi = x.astype(BF16)
    r1 = x - hi.astype(F32)
    mid = r1.astype(BF16)
    lo = (r1 - mid.astype(F32)).astype(BF16)
    return (jnp.dot(hi, m, preferred_element_type=F32) + jnp.dot(mid, m, preferred_element_type=F32)
            + jnp.dot(lo, m, preferred_element_type=F32))


@jax.custom_vjp
def _lin(x, m, mt):
    return _dot3(x, m)


def _lin_fwd(x, m, mt):
    return _dot3(x, m), (m, mt)


def _lin_bwd(res, g):
    m, mt = res
    return _dot3(g, mt), jnp.zeros_like(m), jnp.zeros_like(mt)


_lin.defvjp(_lin_fwd, _lin_bwd)


def _head_ones(n, head=HEAD):
    i = jnp.arange(n) // head
    return (i[:, None] == i[None, :]).astype(BF16)


def _rmsnorm_tile(_, h, g):
    return (h * lax.rsqrt(jnp.mean(h * h, axis=-1, keepdims=True) + NORM_EPS) * g,)


ROWS = 8


def _shift_rows(x, k, fill, up=False):
    rows = lax.broadcasted_iota(jnp.int32, x.shape, 0)
    if up:
        return jnp.where(rows >= ROWS - k, fill, pltpu.roll(x, ROWS - k, 0))
    return jnp.where(rows < k, fill, pltpu.roll(x, k, 0))


def _cmul(pr, pi, qr, qi):
    return pr * qr - pi * qi, pr * qi + pi * qr


def _power_rows(ar, ai, width, descending):
    pows = [(ar, ai)]
    for _ in range(ROWS - 1):
        pows.append(_cmul(*pows[-1], ar, ai))
    rows = lax.broadcasted_iota(jnp.int32, (ROWS, width), 0)
    pr = jnp.zeros((ROWS, width), F32)
    pi = jnp.zeros((ROWS, width), F32)
    for j in range(ROWS):
        qr, qi = pows[ROWS - 1 - j] if descending else pows[j]
        pr, pi = jnp.where(rows == j, qr, pr), jnp.where(rows == j, qi, pi)
    return pr, pi, ((pows[0], 1), (pows[1], 2), (pows[3], 4))


def _prev_rows(ref, tile, r0):
    before = ref[pl.ds(jnp.maximum(r0 - 1, 0), 1), :] * (r0 > 0).astype(F32)
    rows = lax.broadcasted_iota(jnp.int32, tile.shape, 0)
    return jnp.where(rows == 0, before, pltpu.roll(tile, 1, 0))


def _cscan_fwd(bu, a, *, name, ct=256, carry=None):
    s, c2 = bu.shape
    c = c2 // 2
    nt = c // ct

    def body(br_ref, bi_ref, ar_ref, ai_ref, xr_ref, xi_ref):
        pr, pi, doubling = _power_rows(ar_ref[...], ai_ref[...], ct, False)

        def tile(i, h):
            rows = pl.ds(pl.multiple_of(i * ROWS, ROWS), ROWS)
            sr, si = br_ref[rows, :], bi_ref[rows, :]
            for (qr, qi), k in doubling:
                mr, mi = _cmul(qr, qi, _shift_rows(sr, k, 0.0), _shift_rows(si, k, 0.0))
                sr, si = sr + mr, si + mi
            cr, ci = _cmul(pr, pi, *h)
            sr, si = sr + cr, si + ci
            xr_ref[rows, :] = sr
            xi_ref[rows, :] = si
            return sr[ROWS - 1:], si[ROWS - 1:]

        z = jnp.zeros((1, ct), F32)
        lax.fori_loop(0, s // ROWS, tile, (z, z))

    re = lambda j: (0, j)
    im = lambda j: (0, nt + j)
    return _call_with_exchange(
        body, nt, carry, [bu, bu, a, a],
        [pl.BlockSpec((s, ct), re), pl.BlockSpec((s, ct), im), pl.BlockSpec((1, ct), re), pl.BlockSpec((1, ct), im)],
        [pl.BlockSpec((s, ct), re), pl.BlockSpec((s, ct), re)], [jax.ShapeDtypeStruct((s, c), F32)] * 2, [], name)


def _cscan_bwd(gr, gi, xr, xi, a, *, name, ct=256, carry=None):
    s, c = gr.shape
    nt = c // ct
    n_tiles = s // ROWS

    def body(gr_ref, gi_ref, xr_ref, xi_ref, ar_ref, ai_ref, dr_ref, di_ref, dar_ref, dai_ref):
        pr, pi, doubling = _power_rows(ar_ref[...], -ai_ref[...], ct, True)

        def tile(i, state):
            dnr, dni, accr, acci = state
            r0 = pl.multiple_of((n_tiles - 1 - i) * ROWS, ROWS)
            rows = pl.ds(r0, ROWS)
            sr, si = gr_ref[rows, :], gi_ref[rows, :]
            for (qr, qi), k in doubling:
                mr, mi = _cmul(qr, qi, _shift_rows(sr, k, 0.0, up=True), _shift_rows(si, k, 0.0, up=True))
                sr, si = sr + mr, si + mi
            cr, ci = _cmul(pr, pi, dnr, dni)
            sr, si = sr + cr, si + ci
            dr_ref[rows, :] = sr
            di_ref[rows, :] = si
            xpr, xpi = _prev_rows(xr_ref, xr_ref[rows, :], r0), _prev_rows(xi_ref, xi_ref[rows, :], r0)
            return sr[:1], si[:1], accr + sr * xpr + si * xpi, acci + si * xpr - sr * xpi

        z, z8 = jnp.zeros((1, ct), F32), jnp.zeros((ROWS, ct), F32)
        _, _, accr, acci = lax.fori_loop(0, n_tiles, tile, (z, z, z8, z8))
        dar_ref[...] = jnp.sum(accr, axis=0, keepdims=True)
        dai_ref[...] = jnp.sum(acci, axis=0, keepdims=True)

    re = lambda j: (0, j)
    im = lambda j: (0, nt + j)
    blk = pl.BlockSpec((s, ct), re)
    return _call_with_exchange(
        body, nt, carry, [gr, gi, xr, xi, a, a],
        [blk, blk, blk, blk, pl.BlockSpec((1, ct), re), pl.BlockSpec((1, ct), im)],
        [blk, blk, pl.BlockSpec((1, ct), re), pl.BlockSpec((1, ct), re)],
        [jax.ShapeDtypeStruct((s, c), F32)] * 2 + [jax.ShapeDtypeStruct((1, c), F32)] * 2, [], name)


def _rscan_fwd(a, b, *, name, ct=256):
    s, c = a.shape

    def body(a_ref, b_ref, h_ref):
        def tile(i, h):
            rows = pl.ds(pl.multiple_of(i * ROWS, ROWS), ROWS)
            ca, cb = a_ref[rows, :], b_ref[rows, :]
            for k in (1, 2, 4):
                cb = cb + ca * _shift_rows(cb, k, 0.0)
                ca = ca * _shift_rows(ca, k, 1.0)
            out = cb + ca * h
            h_ref[rows, :] = out
            return out[ROWS - 1:]
        lax.fori_loop(0, s // ROWS, tile, jnp.zeros((1, ct), F32))

    blk = pl.BlockSpec((s, ct), lambda j: (0, j))
    return pl.pallas_call(body, grid=(c // ct,), in_specs=[blk, blk], out_specs=blk,
                          out_shape=jax.ShapeDtypeStruct((s, c), F32), compiler_params=_cparams("parallel"), name=name)(a, b)


def _rscan_bwd(g, a, h, *, name, ct=256):
    s, c = a.shape
    n_tiles = s // ROWS

    def body(g_ref, a_ref, h_ref, db_ref, da_ref):
        def tile(i, dn):
            r0 = pl.multiple_of((n_tiles - 1 - i) * ROWS, ROWS)
            rows = pl.ds(r0, ROWS)
            after = a_ref[pl.ds(jnp.minimum(r0 + ROWS, s - 1), 1), :] * (r0 + ROWS < s).astype(F32)
            ca = _shift_rows(a_ref[rows, :], 1, after, up=True)
            cb = g_ref[rows, :]
            for k in (1, 2, 4):
                cb = cb + ca * _shift_rows(cb, k, 0.0, up=True)
                ca = ca * _shift_rows(ca, k, 1.0, up=True)
            out = cb + ca * dn
            db_ref[rows, :] = out
            da_ref[rows, :] = out * _prev_rows(h_ref, h_ref[rows, :], r0)
            return out[:1]
        lax.fori_loop(0, n_tiles, tile, jnp.zeros((1, ct), F32))

    blk = pl.BlockSpec((s, ct), lambda j: (0, j))
    db, da = pl.pallas_call(body, grid=(c // ct,), in_specs=[blk, blk, blk], out_specs=[blk, blk],
                            out_shape=[jax.ShapeDtypeStruct((s, c), F32)] * 2, compiler_params=_cparams("parallel"), name=name)(g, a, h)
    return db, da


def _dot3l(m, x):
    hi = x.astype(BF16)
    r1 = x - hi.astype(F32)
    mid = r1.astype(BF16)
    lo = (r1 - mid.astype(F32)).astype(BF16)
    return (jnp.dot(m, hi, preferred_element_type=F32) + jnp.dot(m, mid, preferred_element_type=F32)
            + jnp.dot(m, lo, preferred_element_type=F32))


@jax.custom_vjp
def _linl(x, m, mt):
    return _dot3l(m, x)


def _linl_fwd(x, m, mt):
    return _dot3l(m, x), (m, mt)


def _linl_bwd(res, g):
    m, mt = res
    return _dot3l(mt, g), jnp.zeros_like(m), jnp.zeros_like(mt)


_linl.defvjp(_linl_fwd, _linl_bwd)


def _ssd_chunk(g, xs, bm, cm, z, dtraw, dt_bias, a_log, dskip, ng, st0, st1, st2):
    n = xs.shape[0]
    lane = lax.broadcasted_iota(jnp.int32, (1, LANES), 1)
    sub = lax.broadcasted_iota(jnp.int32, (LANES, 1), 0)
    row = lax.broadcasted_iota(jnp.int32, (n, n), 0)
    col = lax.broadcasted_iota(jnp.int32, (n, n), 1)
    tril = row >= col
    tril_m = tril.astype(BF16)
    triu_m = (row <= col).astype(BF16)
    lane_lo = lane < HEAD
    sub_lo = sub < HEAD
    dt = _softplus(dtraw + dt_bias)
    da = dt * (-jnp.exp(a_log))
    acum = _linl(da, tril_m, triu_m)
    acum_t = acum.T
    scores = lax.dot_general(cm.astype(BF16), bm.astype(BF16), (((1,), (1,)), ((), ())), preferred_element_type=F32)

    def head(h):
        sel = lane == h
        acol = jnp.sum(jnp.where(sel, acum, 0.0), axis=1, keepdims=True)
        arow = jnp.sum(jnp.where(sub == h, acum_t, 0.0), axis=0, keepdims=True)
        dtcol = jnp.sum(jnp.where(sel, dt, 0.0), axis=1, keepdims=True)
        dsk = jnp.sum(jnp.where(sel, dskip, 0.0), axis=1, keepdims=True)
        decay = jnp.exp(jnp.where(tril, acol - arow, -jnp.inf))
        alast = acol[n - 1:n, :]
        return acol, dtcol, dsk, decay, alast

    ys, new = [], []
    for q, st in enumerate((st0, st1, st2)):
        a_acol, a_dt, a_dsk, a_decay, a_last = head(g * 6 + 2 * q)
        b_acol, b_dt, b_dsk, b_decay, b_last = head(g * 6 + 2 * q + 1)
        xp = xs[:, q * LANES:(q + 1) * LANES]
        xdt = xp * jnp.where(lane_lo, a_dt, b_dt)
        yd = jnp.where(lane_lo, _dot_bf16(scores * a_decay, xdt), _dot_bf16(scores * b_decay, xdt))
        xw = xdt * jnp.where(lane_lo, jnp.exp(a_last - a_acol), jnp.exp(b_last - b_acol))
        states = lax.dot_general(xw.astype(BF16), bm.astype(BF16), (((0,), (0,)), ((), ())), preferred_element_type=F32)
        yo = lax.dot_general(cm.astype(BF16), st.astype(BF16), (((1,), (1,)), ((), ())), preferred_element_type=F32)
        yo = yo * jnp.where(lane_lo, jnp.exp(a_acol), jnp.exp(b_acol))
        new.append(st * jnp.where(sub_lo, jnp.exp(a_last), jnp.exp(b_last)) + states)
        ys.append(yd + yo + xp * jnp.where(lane_lo, a_dsk, b_dsk))
    y = jnp.concatenate(ys, axis=1)
    y = y * (z * jax.nn.sigmoid(z))
    y = y * lax.rsqrt(jnp.mean(y * y, axis=-1, keepdims=True) + NORM_EPS) * ng
    return y, new[0], new[1], new[2]


def _ssd_specs(nc, rev):
    cidx = (lambda c: nc - 1 - c) if rev else (lambda c: c)
    gw = 3 * LANES
    return [
        pl.BlockSpec((SSD_CHUNK, gw), lambda c, g: (cidx(c), g)),
        pl.BlockSpec((SSD_CHUNK, LANES), lambda c, g: (cidx(c), 12 + g)),
        pl.BlockSpec((SSD_CHUNK, LANES), lambda c, g: (cidx(c), 16 + g)),
        pl.BlockSpec((SSD_CHUNK, gw), lambda c, g: (cidx(c), g)),
        pl.BlockSpec((SSD_CHUNK, LANES), lambda c, g: (cidx(c), 36)),
        pl.BlockSpec((1, LANES), lambda c, g: (0, 0)),
        pl.BlockSpec((1, LANES), lambda c, g: (0, 0)),
        pl.BlockSpec((1, LANES), lambda c, g: (0, 0)),
        pl.BlockSpec((1, gw), lambda c, g: (0, g)),
    ], cidx


def _ssd_fwd(conv, proj, dt_bias, a_log, dskip, norm_g, *, name, carry=None):
    s = conv.shape[0]
    nc = s // SSD_CHUNK
    in_specs, _ = _ssd_specs(nc, False)

    def body(xs, bm, cm, z, dtr, dtb, alog, dsk, ng, y_ref, sv_ref, st):
        c, g = pl.program_id(0), pl.program_id(1)

        @pl.when(c == 0)
        def _():
            for q in range(3):
                st[g * 3 + q] = jnp.zeros((LANES, LANES), F32)

        olds = [st[g * 3 + q] for q in range(3)]
        for q in range(3):
            sv_ref[0, 0, q] = olds[q]
        y, n0, n1, n2 = _ssd_chunk(g, xs[...], bm[...], cm[...], z[...], dtr[...], dtb[...], alog[...], dsk[...], ng[...], *olds)
        y_ref[...] = y
        for q, v in enumerate((n0, n1, n2)):
            st[g * 3 + q] = v

    return _call_with_exchange(
        body, (nc, 4), carry, [conv, conv, conv, proj, proj, dt_bias, a_log, dskip, norm_g], in_specs,
        [pl.BlockSpec((SSD_CHUNK, 3 * LANES), lambda c, g: (c, g)), pl.BlockSpec((1, 1, 3, LANES, LANES), lambda c, g: (c, g, 0, 0, 0))],
        [jax.ShapeDtypeStruct((s, 12 * LANES), F32), jax.ShapeDtypeStruct((nc, 4, 3, LANES, LANES), F32)],
        [pltpu.VMEM((12, LANES, LANES), F32)], name)


def _ssd_bwd(conv, proj, dt_bias, a_log, dskip, norm_g, saved, dy, *, name, carry=None):
    s = conv.shape[0]
    nc = s // SSD_CHUNK
    in_specs, cidx = _ssd_specs(nc, True)
    gw = 3 * LANES
    in_specs += [pl.BlockSpec((1, 1, 3, LANES, LANES), lambda c, g: (cidx(c), g, 0, 0, 0)),
                 pl.BlockSpec((SSD_CHUNK, gw), lambda c, g: (cidx(c), g))]

    def body(xs, bm, cm, z, dtr, dtb, alog, dsk, ng, sv, dy_ref, dxs, dbm, dcm, dz, ddt, ddtb, dalog, ddsk, dng, dst):
        c, g = pl.program_id(0), pl.program_id(1)

        @pl.when(c == 0)
        def _():
            for q in range(3):
                dst[g * 3 + q] = jnp.zeros((LANES, LANES), F32)

        @pl.when((c == 0) & (g == 0))
        def _():
            ddtb[...] = jnp.zeros_like(ddtb)
            dalog[...] = jnp.zeros_like(dalog)
            ddsk[...] = jnp.zeros_like(ddsk)
            dng[...] = jnp.zeros_like(dng)

        @pl.when(g == 0)
        def _():
            ddt[...] = jnp.zeros_like(ddt)

        olds = [sv[0, 0, q] for q in range(3)]
        _, vjp = jax.vjp(functools.partial(_ssd_chunk, g), xs[...], bm[...], cm[...], z[...], dtr[...], dtb[...], alog[...],
                         dsk[...], ng[...], *olds)
        gr = vjp((dy_ref[...], dst[g * 3], dst[g * 3 + 1], dst[g * 3 + 2]))
        dxs[...], dbm[...], dcm[...], dz[...] = gr[0], gr[1], gr[2], gr[3]
        ddt[...] += gr[4]
        ddtb[...] += gr[5]
        dalog[...] += gr[6]
        ddsk[...] += gr[7]
        dng[g] += gr[8]
        for q in range(3):
            dst[g * 3 + q] = gr[9 + q]

    const = lambda shape: pl.BlockSpec(shape, lambda c, g: (0,) * len(shape))
    return _call_with_exchange(
        body, (nc, 4), carry, [conv, conv, conv, proj, proj, dt_bias, a_log, dskip, norm_g, saved, dy], in_specs,
        [pl.BlockSpec((SSD_CHUNK, gw), lambda c, g: (cidx(c), g)),
         pl.BlockSpec((SSD_CHUNK, LANES), lambda c, g: (cidx(c), g)),
         pl.BlockSpec((SSD_CHUNK, LANES), lambda c, g: (cidx(c), g)),
         pl.BlockSpec((SSD_CHUNK, gw), lambda c, g: (cidx(c), g)),
         pl.BlockSpec((SSD_CHUNK, LANES), lambda c, g: (cidx(c), 0)),
         const((1, LANES)), const((1, LANES)), const((1, LANES)), const((4, 1, gw))],
        [jax.ShapeDtypeStruct((s, 12 * LANES), F32), jax.ShapeDtypeStruct((s, 4 * LANES), F32),
         jax.ShapeDtypeStruct((s, 4 * LANES), F32), jax.ShapeDtypeStruct((s, 12 * LANES), F32),
         jax.ShapeDtypeStruct((s, LANES), F32), jax.ShapeDtypeStruct((1, LANES), F32),
         jax.ShapeDtypeStruct((1, LANES), F32), jax.ShapeDtypeStruct((1, LANES), F32),
         jax.ShapeDtypeStruct((4, 1, gw), F32)],
        [pltpu.VMEM((12, LANES, LANES), F32)], name)


MXU_TILE = 256
RWKV_GROUP = 8
RWKV_UNROLL = 8


def _rwkv_consts():
    lanes = 16 * HEAD
    hl = jnp.arange(lanes) // HEAD
    e = (jnp.arange(16)[:, None] == hl[None, :]).astype(BF16)
    return e, e.T, _head_ones(MXU_TILE)


_RWKV_LANE_GROUPS = tuple(slice(i * MXU_TILE, (i + 1) * MXU_TILE) for i in range(16 * HEAD // MXU_TILE))


def _head_sums(x, j):
    return jnp.dot(x.astype(BF16), j, preferred_element_type=F32)


def _fold8(x):
    return jnp.sum(x.reshape(x.shape[0] // 8, 8, x.shape[1]), axis=0)


def _rwkv_expand(src3, dst, e, t_):
    for g0 in range(0, t_, RWKV_GROUP):
        n = min(RWKV_GROUP, t_ - g0)
        flat = src3[g0:g0 + n].reshape(n * HEAD, e.shape[0])
        dst[g0:g0 + n] = jnp.dot(flat, e, preferred_element_type=F32).reshape(n, HEAD, e.shape[1])


def _rwkv_reduce(src, dst3, et, t_):
    for g0 in range(0, t_, RWKV_GROUP):
        n = min(RWKV_GROUP, t_ - g0)
        x = src[g0:g0 + n].reshape(n * HEAD, et.shape[0])
        dst3[g0:g0 + n] = jnp.dot(x.astype(BF16), et, preferred_element_type=F32).reshape(n, HEAD, 16)


def _rwkv_fwd(w, kk, b, k, fp, v3, *, name, carry=None):
    s, lanes = w.shape
    t_ = min(RWKV_CHUNK, s)
    nc = s // t_
    e, et, j = _rwkv_consts()
    rowspec = pl.BlockSpec((t_, lanes), lambda c: (c, 0))
    cspec = lambda a: pl.BlockSpec(a.shape, lambda c: (0, 0))

    def body(w_ref, kk_ref, b_ref, k_ref, r_ref, v_ref, e_ref, et_ref, j_ref, y_ref, sv_ref, st, vm, zz):
        c = pl.program_id(0)

        @pl.when(c == 0)
        def _():
            st[...] = jnp.zeros_like(st)

        sv_ref[0] = st[...]
        jv = j_ref[...]
        _rwkv_expand(v_ref, vm, e_ref[...], t_)

        def step(t, carry):
            for grp in _RWKV_LANE_GROUPS:
                row = lambda ref: ref[pl.ds(t, 1), grp]
                sm = st[:, grp]
                sa = _head_sums(sm * (-row(kk_ref)), jv)
                sn = sm * row(w_ref) + sa * row(b_ref) + vm[t, :, grp] * row(k_ref)
                st[:, grp] = sn
                zz[t, :, grp] = sn * row(r_ref)
            return carry

        lax.fori_loop(0, t_, step, 0, unroll=RWKV_UNROLL)
        _rwkv_reduce(zz, y_ref, et_ref[...], t_)

    return _call_with_exchange(
        body, nc, carry, [w, kk, b, k, fp, v3, e, et, j],
        [rowspec] * 5 + [pl.BlockSpec((t_, HEAD, 16), lambda c: (c, 0, 0)), cspec(e), cspec(et), cspec(j)],
        [pl.BlockSpec((t_, HEAD, 16), lambda c: (c, 0, 0)), pl.BlockSpec((1, HEAD, lanes), lambda c: (c, 0, 0))],
        [jax.ShapeDtypeStruct((s, HEAD, 16), F32), jax.ShapeDtypeStruct((nc, HEAD, lanes), F32)],
        [pltpu.VMEM((HEAD, lanes), F32), pltpu.VMEM((t_, HEAD, lanes), F32), pltpu.VMEM((t_, HEAD, lanes), F32)], name)


def _call_with_exchange(body, grid, carry, ins, in_specs, out_specs, out_shape, scratch, name):
    grid = (grid,) if isinstance(grid, int) else tuple(grid)
    if carry is not None:
        src, gather = carry
        n_in, n_out, inner = len(ins), len(out_shape), body

        def body(*refs):
            ex = _make_exchange(refs[n_in], refs[n_in + 1 + n_out], refs[-3:], gather)
            steps = [pl.program_id(d) for d in range(len(grid))]

            @pl.when(functools.reduce(jnp.logical_and, [s == 0 for s in steps]))
            def _():
                ex.start()

            inner(*refs[:n_in], *refs[n_in + 1:n_in + 1 + n_out], *refs[n_in + 2 + n_out:-3])

            @pl.when(functools.reduce(jnp.logical_and, [s == n - 1 for s, n in zip(steps, grid)]))
            def _():
                ex.wait()

        hbm = pl.BlockSpec(memory_space=pl.ANY)
        ins, in_specs, out_specs = list(ins) + [src], list(in_specs) + [hbm], list(out_specs) + [hbm]
        out_shape = list(out_shape) + [jax.ShapeDtypeStruct((N_DEV, src.shape[-2], LANES), src.dtype)]
        scratch = list(scratch) + list(_Exchange.SCRATCH)
    return pl.pallas_call(body, grid=grid, in_specs=in_specs, out_specs=out_specs, out_shape=out_shape,
                          scratch_shapes=scratch, compiler_params=_cparams(*["arbitrary"] * len(grid)), name=name)(*ins)


def _rwkv_bwd(w, kk, b, k, fp, v3, saved, dy3, *, name, carry=None):
    s, lanes = w.shape
    t_ = min(RWKV_CHUNK, s)
    nc = s // t_
    e, et, j = _rwkv_consts()
    rev = lambda c: nc - 1 - c
    rowspec = pl.BlockSpec((t_, lanes), lambda c: (rev(c), 0))
    v3spec = pl.BlockSpec((t_, HEAD, 16), lambda c: (rev(c), 0, 0))
    s3spec = v3spec
    cspec = lambda a: pl.BlockSpec(a.shape, lambda c: (0, 0))

    def body(w_ref, kk_ref, b_ref, k_ref, r_ref, v_ref, sv_ref, dy_ref, e_ref, et_ref, j_ref,
             dw_ref, dkk_ref, db_ref, dk_ref, dr_ref, dv_ref, dst, h_sm, h_sa, vm, dz, pw, pkk, pb, pk, pr):
        c = pl.program_id(0)

        @pl.when(c == 0)
        def _():
            dst[...] = jnp.zeros_like(dst)

        jv = j_ref[...]
        _rwkv_expand(v_ref, vm, e_ref[...], t_)
        _rwkv_expand(dy_ref, dz, e_ref[...], t_)
        h_sm[0] = sv_ref[0]

        def replay(t, carry):
            for grp in _RWKV_LANE_GROUPS:
                row = lambda ref: ref[pl.ds(t, 1), grp]
                sm = h_sm[t, :, grp]
                sa = _head_sums(sm * (-row(kk_ref)), jv)
                h_sa[t, :, grp] = sa
                h_sm[t + 1, :, grp] = sm * row(w_ref) + sa * row(b_ref) + vm[t, :, grp] * row(k_ref)
            return carry

        lax.fori_loop(0, t_, replay, 0, unroll=RWKV_UNROLL)

        def back(i, carry):
            t = t_ - 1 - i
            for grp in _RWKV_LANE_GROUPS:
                row = lambda ref: ref[pl.ds(t, 1), grp]
                sm, sa, dzt = h_sm[t, :, grp], h_sa[t, :, grp], dz[t, :, grp]
                dsn = dst[:, grp] + dzt * row(r_ref)
                pr[t, :, grp] = _fold8(dzt * h_sm[t + 1, :, grp])
                pw[t, :, grp] = _fold8(dsn * sm)
                pb[t, :, grp] = _fold8(dsn * sa)
                pk[t, :, grp] = _fold8(dsn * vm[t, :, grp])
                vm[t, :, grp] = dsn * row(k_ref)
                dx = _head_sums(dsn * row(b_ref), jv)
                pkk[t, :, grp] = _fold8(dx * sm)
                dst[:, grp] = dsn * row(w_ref) - dx * row(kk_ref)
            return carry

        lax.fori_loop(0, t_, back, 0, unroll=RWKV_UNROLL)
        _rwkv_reduce(vm, dv_ref, et_ref[...], t_)
        dw_ref[...] = jnp.sum(pw[...], axis=1)
        dkk_ref[...] = -jnp.sum(pkk[...], axis=1)
        db_ref[...] = jnp.sum(pb[...], axis=1)
        dk_ref[...] = jnp.sum(pk[...], axis=1)
        dr_ref[...] = jnp.sum(pr[...], axis=1)

    big = lambda n: pltpu.VMEM((n, HEAD, lanes), F32)
    part = pltpu.VMEM((t_, 8, lanes), F32)
    return _call_with_exchange(
        body, nc, carry, [w, kk, b, k, fp, v3, saved, dy3, e, et, j],
        [rowspec] * 5 + [s3spec, pl.BlockSpec((1, HEAD, lanes), lambda c: (rev(c), 0, 0)), s3spec, cspec(e), cspec(et), cspec(j)],
        [rowspec] * 5 + [v3spec],
        [jax.ShapeDtypeStruct((s, lanes), F32)] * 5 + [jax.ShapeDtypeStruct((s, HEAD, 16), F32)],
        [pltpu.VMEM((HEAD, lanes), F32), big(t_ + 1), big(t_), big(t_), big(t_)] + [part] * 5, name)


def _blockdiag(blocks):
    g, a, b = blocks.shape
    on_diag = (jnp.arange(g)[:, None, None, None] == jnp.arange(g)[None, None, :, None])
    return jnp.where(on_diag, blocks[:, :, None, :], 0).reshape(g * a, g * b)


def _blockdiag_t(dense, g):
    a, b = dense.shape[0] // g, dense.shape[1] // g
    on_diag = (jnp.arange(g)[:, None, None, None] == jnp.arange(g)[None, None, :, None])
    return jnp.sum(jnp.where(on_diag, dense.reshape(g, a, g, b), 0), axis=2)


def _pad_cols(x, n):
    return jnp.pad(x, ((0, 0), (0, n - x.shape[1])))


def _pad_rows(x, n):
    return jnp.pad(x, ((0, n - x.shape[0]), (0, 0)))


E_PROJ = 5120


def _even_in_cols(w):
    return jnp.concatenate([w[:, 512:2048], w[:, 0:512], w[:, 2048:4632], jnp.zeros((w.shape[0], E_PROJ - 4632), w.dtype)], axis=1)


def _even_in_cols_t(dw):
    return jnp.concatenate([dw[:, 1536:2048], dw[:, 0:1536], dw[:, 2048:4632]], axis=1)


O_PROJ = 5632


def _odd_in_cols(w):
    z32 = jnp.zeros((w.shape[0], 32), w.dtype)
    return jnp.concatenate([w[:, 0:3072], w[:, 3520:5568], w[:, 3264:3520], w[:, 3072:3168], z32, w[:, 3168:3264], z32], axis=1)


def _odd_in_cols_t(dw):
    return jnp.concatenate([dw[:, 0:3072], dw[:, 5376:5472], dw[:, 5504:5600], dw[:, 5120:5376], dw[:, 3072:5120]], axis=1)


def _mu_cols(mu):
    z32 = jnp.zeros((1, 32), mu.dtype)
    return jnp.concatenate([mu[:, 0:3072], mu[:, 3264:3520], mu[:, 3072:3168], z32, mu[:, 3168:3264], z32], axis=1)


def _mu_cols_t(d):
    return jnp.concatenate([d[:, 0:3072], d[:, 3328:3424], d[:, 3456:3552], d[:, 3072:3328]], axis=1)


def _conv_taps(x, w, b):
    y = b + w[3:4] * x
    for k in range(3):
        y = y + w[k:k + 1] * _shift_down(x, 3 - k)
    return y


def _conv_silu(x, w, b):
    y = _conv_taps(x, w, b)
    return y * jax.nn.sigmoid(y)


def _tshift(x, mu):
    return x + (_shift_down(x, 1) - x) * mu


def _pl_gate(_, h, gl, e):
    return (h + jax.nn.sigmoid(gl) * e,)


def _s5_param(lr, li, ls, br, bi):
    step = jnp.exp(ls)
    mag = jnp.exp(lr * step)
    ar, ai = mag * jnp.cos(li * step), mag * jnp.sin(li * step)
    den = lr * lr + li * li
    nr = ar - 1.0
    cr = (nr * lr + ai * li) / den
    ci = (ai * lr - nr * li) / den
    return ar, ai, cr * br - ci * bi, cr * bi + ci * br


def _s5_post(_, ylin, u, d, gw, gb):
    act = jax.nn.gelu(ylin + d * u)
    return (act * jax.nn.sigmoid(_dot_bf16(act, gw) + gb),)


def _rwkv_pre(_, k, gl, wl, al, w0, w_up, a0, a_up, g_up, k_k, k_a, j):
    w = -_softplus(-(w0 + _dot_bf16(jnp.tanh(wl), w_up))) - 0.5
    decay = jnp.exp(-jnp.exp(w))
    a = jax.nn.sigmoid(a0 + _dot_bf16(al, a_up))
    g = _dot_bf16(jax.nn.sigmoid(gl), g_up)
    kk = k * k_k
    k2 = k * (1.0 + (a - 1.0) * k_a)
    kkn = kk * lax.rsqrt(jnp.maximum(_lin(kk * kk, j, j), 1e-24))
    return decay, kkn, kkn * a, k2, g


def _rwkv_post(_, y, r, k2, v, g, r_k, ln_g, ln_b, j):
    mean = _lin(y, j, j) * (1.0 / HEAD)
    yc = y - mean
    var = _lin(yc * yc, j, j) * (1.0 / HEAD)
    yn = yc * lax.rsqrt(var + RWKV_GN_EPS) * ln_g + ln_b
    return ((yn + _lin(r * k2 * r_k, j, j) * v) * g,)


def _lru_pre(row0, pre, xc, bax, lam):
    n = xc.shape[1]
    gr = jax.nn.sigmoid(pre[:, :n] + bax[:, :n])
    gi = jax.nn.sigmoid(pre[:, n:] + bax[:, n:])
    log_a = -LRU_C * gr * _softplus(-lam)
    m2 = -jnp.tanh(log_a) * (jnp.exp(2.0 * log_a) + 1.0)
    mult = jnp.sqrt(jnp.maximum(m2, 0.0))
    rowid = row0 + lax.broadcasted_iota(jnp.int32, (xc.shape[0], 1), 0)
    mult = jnp.where(rowid == 0, 1.0, mult)
    return jnp.exp(log_a), xc * gi * mult


def _lru_post(_, h, gl2):
    return (h * jax.nn.gelu(gl2),)


def _even_prep(w):
    sp = (w["s5_lam_re"].reshape(32, 64), w["s5_lam_im"].reshape(32, 64), w["s5_log_step"].reshape(32, 1),
          w["s5_b_re"].reshape(32, 64, 16).transpose(2, 0, 1), w["s5_b_im"].reshape(32, 64, 16).transpose(2, 0, 1))
    ar, ai, bbr, bbi = _block_vjp(_s5_param, sp, None, name="s5_param")
    bblk = lambda bb: _blockdiag(bb.transpose(1, 0, 2))
    cblk = lambda c: _blockdiag(c.reshape(32, 16, 64).transpose(0, 2, 1))
    pad = lambda x: _pad_cols(x.reshape(1, 24), LANES)
    return dict(
        sp=sp, a_row=jnp.concatenate([ar.reshape(1, 2048), ai.reshape(1, 2048)], axis=1),
        b_re=bblk(bbr), b_im=bblk(bbi), c_re=cblk(w["s5_c_re"]), c_imn=-cblk(w["s5_c_im"]),
        d=w["s5_d"].reshape(1, 512), gw=w["s5_glu_w"].reshape(512, 512), gb=w["s5_glu_b"].reshape(1, 512),
        conv_w=w["ssd_conv_w"].reshape(4, 2560), conv_b=w["ssd_conv_b"].reshape(1, 2560),
        dt_bias=pad(w["ssd_dt_bias"]), a_log=pad(w["ssd_a_log"]), dskip=pad(w["ssd_d"]), norm=w["ssd_norm"].reshape(1, 1536))


_E_XMAP = lambda j: 16 + j


def _even_fwd(proj, p, carry=None):
    u = proj[:, 1536:2048]
    bur = _matmul(u, p["b_re"], "nn", name="s5_bu_re")
    bui = _matmul(u, p["b_im"], "nn", name="s5_bu_im")
    xr, xi = _cscan_fwd(jnp.concatenate([bur, bui], axis=1), p["a_row"], name="s5_scan")
    ylin = _matmul(xi, p["c_imn"], "nn", name="s5_y_im", add=_matmul(xr, p["c_re"], "nn", name="s5_y_re"))
    (ya,) = _rw_fwd(_s5_post, [_row(ylin), _row(u)], [p["d"], p["gw"], p["gb"]], [(512, F32)], name="s5_post")
    conv = _ct_fwd(_conv_silu, proj, _E_XMAP, 20, [(p["conv_w"], 0), (p["conv_b"], 0)], 2560, name="ssd_conv")
    yb, saved, *carried = _ssd_fwd(conv, proj, p["dt_bias"], p["a_log"], p["dskip"], p["norm"], name="ssd_scan", carry=carry)
    y = jnp.concatenate([ya, yb], axis=1).astype(BF16)
    return y, (u, xr, xi, ylin, conv, saved), (carried[0] if carried else None)


def _even_bwd(proj, p, res, dy, carry=None):
    u, xr, xi, ylin, conv, saved = res
    s = proj.shape[0]
    dxs, dbm, dcm, dz, ddt, ddtb, dalog, ddsk, dng, *carried = _ssd_bwd(
        conv, proj, p["dt_bias"], p["a_log"], p["dskip"], p["norm"], saved, dy[:, 512:], name="ssd_scan_bwd", carry=carry)
    dxbc, (dcw, dcb) = _ct_bwd(_conv_silu, proj, _E_XMAP, 20, [(p["conv_w"], 0), (p["conv_b"], 0)],
                               [jnp.concatenate([dxs, dbm, dcm], axis=1)], name="ssd_conv_bwd")
    (dylin, du), (dd, dgw, dgb) = _rw_bwd(_s5_post, [_row(ylin), _row(u)], [p["d"], p["gw"], p["gb"]], [dy[:, :512]], name="s5_post_bwd")
    dxr = _matmul(dylin, p["c_re"], "nt", name="s5_dxr")
    dxi = _matmul(dylin, p["c_imn"], "nt", name="s5_dxi")
    dc_re = _matmul(xr, dylin, "tn", name="s5_dc_re")
    dc_imn = _matmul(xi, dylin, "tn", name="s5_dc_im")
    dbr, dbi, dar, dai = _cscan_bwd(dxr, dxi, xr, xi, p["a_row"], name="s5_scan_bwd")
    du = _matmul(dbr, p["b_re"], "nt", name="s5_du_re", add=du)
    du = _matmul(dbi, p["b_im"], "nt", name="s5_du_im", add=du)
    db_re = _matmul(u, dbr, "tn", name="s5_db_re")
    db_im = _matmul(u, dbi, "tn", name="s5_db_im")
    unblk = lambda d: _blockdiag_t(d, 32).transpose(1, 0, 2)
    g_sp = _block_vjp(_s5_param, p["sp"], (dar.reshape(32, 64), dai.reshape(32, 64), unblk(db_re), unblk(db_im)), name="s5_param_bwd")
    dproj = jnp.concatenate([dz, du, dxbc, ddt, jnp.zeros((s, E_PROJ - 4736), F32)], axis=1).astype(BF16)
    uncblk = lambda d: _blockdiag_t(d, 32).transpose(0, 2, 1)
    grads = dict(
        s5_lam_re=g_sp[0].reshape(1, 32, 64), s5_lam_im=g_sp[1].reshape(1, 32, 64), s5_log_step=g_sp[2].reshape(1, 32),
        s5_b_re=g_sp[3].transpose(1, 2, 0)[None], s5_b_im=g_sp[4].transpose(1, 2, 0)[None],
        s5_c_re=uncblk(dc_re)[None], s5_c_im=-uncblk(dc_imn)[None], s5_d=dd, s5_glu_w=dgw[None], s5_glu_b=dgb,
        ssd_conv_w=dcw[None], ssd_conv_b=dcb, ssd_dt_bias=ddtb[:, :24], ssd_a_log=dalog[:, :24], ssd_d=ddsk[:, :24],
        ssd_norm=dng.reshape(1, 1536))
    return dproj, grads, (carried[0] if carried else None)


def _odd_prep(w):
    pad128 = lambda x: _pad_rows(x, LANES)
    return dict(
        mu=_mu_cols(w["rwkv_mu"].reshape(1, 3520)), w0=w["rwkv_w0"].reshape(1, 1024), w_up=pad128(w["rwkv_w_up"].reshape(96, 1024)),
        a0=w["rwkv_a0"].reshape(1, 1024), a_up=pad128(w["rwkv_a_up"].reshape(96, 1024)), g_up=w["rwkv_g_up"].reshape(256, 1024),
        k_k=w["rwkv_k_k"].reshape(1, 1024), k_a=w["rwkv_k_a"].reshape(1, 1024), r_k=w["rwkv_r_k"].reshape(1, 1024),
        ln_g=w["rwkv_ln_g"].reshape(1, 1024), ln_b=w["rwkv_ln_b"].reshape(1, 1024), j=_head_ones(1024),
        conv_w=w["lru_conv_w"].reshape(4, 1024), conv_b=w["lru_conv_b"].reshape(1, 1024),
        wax=jnp.concatenate([_blockdiag(w["lru_w_a"].reshape(16, 64, 64)), _blockdiag(w["lru_w_x"].reshape(16, 64, 64))], axis=1),
        bax=jnp.concatenate([w["lru_b_a"].reshape(1, 1024), w["lru_b_x"].reshape(1, 1024)], axis=1), lam=w["lru_lam"].reshape(1, 1024))


_O_XMAP = lambda j: jnp.where(j < 24, j, j + 16)
_O_LMAP = lambda j: 24 + j


def _to_heads(x):
    return x.reshape(x.shape[0], 16, HEAD).transpose(0, 2, 1)


def _from_heads(x3):
    return x3.transpose(0, 2, 1).reshape(x3.shape[0], 16 * HEAD)


def _odd_rows(fp, y, k2, g):
    pre = [_row(fp, 1024, 1), _row(fp, 256, 12), _row(fp, 128, 26), _row(fp, 128, 27)]
    post = None if y is None else [_row(y), _row(fp, 1024, 0), _row(k2), _row(fp, 1024, 2), _row(g)]
    return pre, post


def _odd_fwd(proj, p, carry=None):
    fp = _ct_fwd(_tshift, proj, _O_XMAP, 28, [(p["mu"], 0)], 3584, name="rwkv_shift")
    pre_rows, _ = _odd_rows(fp, None, None, None)
    pre_params = [p["w0"], p["w_up"], p["a0"], p["a_up"], p["g_up"], p["k_k"], p["k_a"], p["j"]]
    decay, kkn, b, k2, g = _rw_fwd(_rwkv_pre, pre_rows, pre_params, [(1024, F32)] * 5, name="rwkv_pre")
    v3 = _to_heads(fp[:, 2048:3072]).astype(BF16)
    y3, saved, *carried = _rwkv_fwd(decay, kkn, b, k2, fp, v3, name="rwkv_scan", carry=carry)
    y = _from_heads(y3)
    _, post_rows = _odd_rows(fp, y, k2, g)
    (yc,) = _rw_fwd(_rwkv_post, post_rows, [p["r_k"], p["ln_g"], p["ln_b"], p["j"]], [(1024, F32)], name="rwkv_post")
    xc = _ct_fwd(_conv_taps, proj, _O_LMAP, 8, [(p["conv_w"], 0), (p["conv_b"], 0)], 1024, name="lru_conv")
    pre = _matmul(xc, p["wax"], "nn", name="lru_gates")
    a, bx = _rw_fwd(_lru_pre, [_row(pre), _row(xc)], [p["bax"], p["lam"]], [(1024, F32)] * 2, name="lru_pre")
    hseq = _rscan_fwd(a, bx, name="lru_scan")
    (yd,) = _rw_fwd(_lru_post, [_row(hseq), _row(proj, 1024, 4)], [], [(1024, F32)], name="lru_post")
    y_out = jnp.concatenate([yc, yd], axis=1).astype(BF16)
    return y_out, (fp, decay, kkn, b, k2, g, v3, saved, y, xc, pre, a, hseq), (carried[0] if carried else None)


def _odd_bwd(proj, p, res, dy, carry=None):
    fp, decay, kkn, b, k2, g, v3, saved, y, xc, pre, a, hseq = res
    s = proj.shape[0]
    (dh, dgl2), _ = _rw_bwd(_lru_post, [_row(hseq), _row(proj, 1024, 4)], [], [dy[:, 1024:]], name="lru_post_bwd")
    dbx, da = _rscan_bwd(dh, a, hseq, name="lru_scan_bwd")
    (dpre, dxc), (dbax, dlam) = _rw_bwd(_lru_pre, [_row(pre), _row(xc)], [p["bax"], p["lam"]], [da, dbx], name="lru_pre_bwd")
    dxc = _matmul(dpre, p["wax"], "nt", name="lru_gates_dx", add=dxc)
    dwax = _matmul(xc, dpre, "tn", name="lru_gates_dw")
    dxl, (dlcw, dlcb) = _ct_bwd(_conv_taps, proj, _O_LMAP, 8, [(p["conv_w"], 0), (p["conv_b"], 0)], [dxc], name="lru_conv_bwd")
    pre_rows, post_rows = _odd_rows(fp, y, k2, g)
    (dyn, dr1, dk2a, dv1, dg), (dr_k, dln_g, dln_b) = _rw_bwd(
        _rwkv_post, post_rows, [p["r_k"], p["ln_g"], p["ln_b"], p["j"]], [dy[:, :1024]], name="rwkv_post_bwd", param_grads=[0, 1, 2])
    ddecay, dkkn, db, dk2b, dr2, dv3, *carried = _rwkv_bwd(
        decay, kkn, b, k2, fp, v3, saved, _to_heads(dyn).astype(BF16), name="rwkv_scan_bwd", carry=carry)
    pre_params = [p["w0"], p["w_up"], p["a0"], p["a_up"], p["g_up"], p["k_k"], p["k_a"], p["j"]]
    (dk, dgl, dwl, dal), (dw0, dw_up, da0, da_up, dg_up, dk_k, dk_a) = _rw_bwd(
        _rwkv_pre, pre_rows, pre_params, [ddecay, dkkn, db, [dk2a, dk2b], dg], name="rwkv_pre_bwd", param_grads=list(range(7)))
    z = lambda n: jnp.zeros((s, n), F32)
    g1 = jnp.concatenate([dr1, dk, dv1, dgl, dwl, dal], axis=1)
    g2 = jnp.concatenate([dr2, z(1024), _from_heads(dv3), z(512)], axis=1)
    dfp, (dmu,) = _ct_bwd(_tshift, proj, _O_XMAP, 28, [(p["mu"], 0)], [g1, g2], name="rwkv_shift_bwd")
    dproj = jnp.concatenate([dfp[:, :3072], dxl, dgl2, dfp[:, 3072:]], axis=1).astype(BF16)
    grads = dict(
        rwkv_mu=_mu_cols_t(dmu), rwkv_w0=dw0, rwkv_w_up=dw_up[:96][None], rwkv_a0=da0, rwkv_a_up=da_up[:96][None], rwkv_g_up=dg_up[None],
        rwkv_k_k=dk_k, rwkv_k_a=dk_a, rwkv_r_k=dr_k.reshape(1, 16, 64), rwkv_ln_g=dln_g, rwkv_ln_b=dln_b,
        lru_conv_w=dlcw[None], lru_conv_b=dlcb, lru_w_a=_blockdiag_t(dwax[:, :1024], 16)[None], lru_w_x=_blockdiag_t(dwax[:, 1024:], 16)[None],
        lru_b_a=dbax[:, :1024].reshape(1, 16, 64), lru_b_x=dbax[:, 1024:].reshape(1, 16, 64), lru_lam=dlam.reshape(1, 16, 64))
    return dproj, grads, (carried[0] if carried else None)


def _my_index():
    return 4 * lax.axis_index("x") + 2 * lax.axis_index("y") + lax.axis_index("c")


def _peer(k):
    x, y, c = lax.axis_index("x"), lax.axis_index("y"), lax.axis_index("c")
    return (1 - x if k & 4 else x, 1 - y if k & 2 else y, 1 - c if k & 1 else c)


class _Exchange:
    SCRATCH = (pltpu.SemaphoreType.DMA((N_DEV - 1,)), pltpu.SemaphoreType.DMA((N_DEV - 1,)), pltpu.SemaphoreType.DMA)

    def __init__(self, src_ref, out_ref, send_sems, recv_sems, local_sem, gather):
        me = _my_index()
        mine = src_ref if gather else src_ref.at[me]
        self.local = pltpu.make_async_copy(mine, out_ref.at[me], local_sem)
        rdma = lambda src, dst, k: pltpu.make_async_remote_copy(
            src_ref=src, dst_ref=dst, send_sem=send_sems.at[k - 1], recv_sem=recv_sems.at[k - 1],
            device_id=_peer(k), device_id_type=pl.DeviceIdType.MESH)
        ks = range(1, N_DEV)
        self.sends = [rdma(src_ref if gather else src_ref.at[jnp.bitwise_xor(me, k)], out_ref.at[me], k) for k in ks]
        self.arrivals = [rdma(mine, out_ref.at[jnp.bitwise_xor(me, k)], k) for k in ks]

    def start(self):
        self.local.start()
        for cp in self.sends:
            cp.start()

    def wait(self):
        for cp in self.arrivals:
            cp.wait_recv()
        for cp in self.sends:
            cp.wait_send()
        self.local.wait()


class _TwoLevelGather:
    def __init__(self, src_ref, out_ref, send_sems, recv_sems, local_sem):
        x, y, c = lax.axis_index("x"), lax.axis_index("y"), lax.axis_index("c")
        me, self.sibling, self.c = (x, y, c), (x, y, 1 - c), c
        self.me = me
        self.chips = [(1 - x, y), (x, 1 - y), (1 - x, 1 - y)]
        slab = lambda px, py, pc: out_ref.at[4 * px + 2 * py + pc]

        def copy(k, block, to, own=False):
            return pltpu.make_async_remote_copy(
                src_ref=src_ref if own else slab(*block), dst_ref=slab(*block), send_sem=send_sems.at[k], recv_sem=recv_sems.at[k],
                device_id=to, device_id_type=pl.DeviceIdType.MESH)

        self.copy = copy
        self.mine = pltpu.make_async_copy(src_ref, slab(*me), local_sem)
        self.first = [copy(0, me, self.sibling, own=True)] + [copy(1 + j, me, (*chip, c), own=True) for j, chip in enumerate(self.chips)]

    def start(self):
        self.mine.start()
        for cp in self.first:
            cp.start()

    def wait(self):
        passed = [self.copy(4 + j, (*chip, self.c), self.sibling) for j, chip in enumerate(self.chips)]
        for j, chip in enumerate(self.chips):
            self.copy(1 + j, (*chip, self.c), self.me).wait_recv()
            passed[j].start()
        self.copy(0, self.sibling, self.me).wait_recv()
        for j, chip in enumerate(self.chips):
            self.copy(4 + j, (*chip, 1 - self.c), self.me).wait_recv()
        for cp in self.first + passed:
            cp.wait_send()
        self.mine.wait()


TWO_LEVEL = "two-level gather"


def _make_exchange(src_ref, out_ref, sems, kind):
    return _TwoLevelGather(src_ref, out_ref, *sems) if kind == TWO_LEVEL else _Exchange(src_ref, out_ref, *sems, kind)


def _exchange(src, *, gather, name):
    def body(src_ref, out_ref, *sems):
        ex = _make_exchange(src_ref, out_ref, sems, gather)
        ex.start()
        ex.wait()

    return pl.pallas_call(
        body, out_shape=jax.ShapeDtypeStruct((N_DEV, src.shape[-2], LANES), src.dtype),
        in_specs=[pl.BlockSpec(memory_space=pl.ANY)], out_specs=pl.BlockSpec(memory_space=pl.ANY),
        scratch_shapes=list(_Exchange.SCRATCH), name=name)(src)


PACK_ALIGN = 16 * LANES
PACK_ROWS = 512


def _pack(arrs, dtype, lead=False):
    parts, rows = [], 0
    for a in arrs:
        n_lead = a.shape[0] if lead else 1
        n = a.size // n_lead
        a = a.astype(dtype)
        if n % PACK_ALIGN:
            a = jnp.pad(a.reshape(n_lead, n), ((0, 0), (0, -n % PACK_ALIGN)))
        parts.append(a.reshape(n_lead, -1, LANES))
        rows += parts[-1].shape[1]
    if rows % PACK_ROWS:
        parts.append(jnp.zeros((parts[0].shape[0], -rows % PACK_ROWS, LANES), dtype))
    buf = jnp.concatenate(parts, axis=1)
    return buf if lead else buf[0]


def _unpack(buf, shapes, lead=False):
    buf = buf if lead else buf[None]
    out, off = [], 0
    for shp in shapes:
        n = math.prod(shp)
        rows = (n + (-n % PACK_ALIGN)) // LANES
        piece = buf[:, off:off + rows]
        if n % PACK_ALIGN:
            piece = piece.reshape(buf.shape[0], rows * LANES)[:, :n]
        out.append(piece.reshape(((buf.shape[0],) if lead else ()) + tuple(shp)))
        off += rows
    return out


def _unshard(parts, axis):
    moved = jnp.moveaxis(parts, 0, axis)
    shp = list(moved.shape)
    return moved.reshape(shp[:axis] + [shp[axis] * shp[axis + 1]] + shp[axis + 2:])


def _to_parts(full, axis):
    shp = list(full.shape)
    split = full.reshape(shp[:axis] + [N_DEV, shp[axis] // N_DEV] + shp[axis + 1:])
    return jnp.moveaxis(split, axis, 0)


def _adamw(gparts, w, m, v, *, name):
    r = w.shape[0]
    tr = _pick(r, (PACK_ROWS,))

    def body(g_ref, w_ref, m_ref, v_ref, go, do, mo, vo):
        g = g_ref[0].astype(F32)
        for d in range(1, N_DEV):
            g = g + g_ref[d].astype(F32)
        m1 = ADAM_B1 * m_ref[...] + (1.0 - ADAM_B1) * g
        v1 = ADAM_B2 * v_ref[...] + (1.0 - ADAM_B2) * jnp.square(g)
        m_hat = m1 / (1.0 - ADAM_B1 ** ADAM_STEP)
        v_hat = v1 / (1.0 - ADAM_B2 ** ADAM_STEP)
        go[...] = g
        do[...] = -ADAM_LR * (m_hat / (jnp.sqrt(v_hat) + ADAM_EPS) + ADAM_WD * w_ref[...])
        mo[...] = m1
        vo[...] = v1

    blk = pl.BlockSpec((tr, LANES), lambda i: (i, 0))
    return pl.pallas_call(
        body, grid=(r // tr,), in_specs=[pl.BlockSpec((N_DEV, tr, LANES), lambda i: (0, i, 0)), blk, blk, blk], out_specs=[blk] * 4,
        out_shape=[jax.ShapeDtypeStruct((r, LANES), F32)] * 4, compiler_params=_cparams("parallel"), name=name)(gparts, w, m, v)


def _loss_and_grad(h, g, tgt, *, name, ts=256):
    s, d = h.shape
    ts = min(ts, s)

    def tile_loss(hv, gv, tv):
        (y,) = _rmsnorm_tile(0, hv, gv)
        return 0.5 * jnp.sum(jnp.mean(jnp.square(y - tv), axis=-1))

    def body(h_ref, g_ref, t_ref, l_ref, dh_ref, dg_ref):
        @pl.when(pl.program_id(0) == 0)
        def _():
            l_ref[...] = jnp.zeros_like(l_ref)
            dg_ref[...] = jnp.zeros_like(dg_ref)

        tv = t_ref[...]
        loss, vjp = jax.vjp(lambda hv, gv: tile_loss(hv, gv, tv), h_ref[...], g_ref[...])
        dh, dg = vjp(jnp.ones((), F32))
        l_ref[...] += loss
        dh_ref[...] = dh
        dg_ref[...] += dg

    row = pl.BlockSpec((ts, d), lambda i: (i, 0))
    return pl.pallas_call(
        body, grid=(s // ts,), in_specs=[row, pl.BlockSpec((1, d), lambda i: (0, 0)), row],
        out_specs=[pl.BlockSpec((8, LANES), lambda i: (0, 0)), row, pl.BlockSpec((1, d), lambda i: (0, 0))],
        out_shape=[jax.ShapeDtypeStruct((8, LANES), F32), jax.ShapeDtypeStruct((s, d), F32), jax.ShapeDtypeStruct((1, d), F32)],
        compiler_params=_cparams("arbitrary"), name=name)(h, g, tgt)


def _norm(h, g, name):
    return _rw_fwd(_rmsnorm_tile, [_row(h)], [g], [(h.shape[1], BF16)], name=name)[0]


def _norm_bwd(h, g, dhn, dres, name):
    (dh,), (dg,) = _rw_bwd(_rmsnorm_tile, [_row(h)], [g], [dhn], add_rows={0: dres}, name=name)
    return dh, dg


def _layer_fwd(h, p_i, lw, mixer_fwd, mp, tag, carry=None, late=None):
    hn = _norm(h, lw["norm_mix"], f"{tag}_norm_mix")
    proj = _matmul(hn, lw["w_in"], "nn", name=f"{tag}_in_proj")
    y, mres, carried = mixer_fwd(proj, mp, carry)
    if late is not None:
        lw = {**lw, **late(carried)}
    h1 = _matmul(y, lw["w_out"], "nn", name=f"{tag}_out_proj", add=h)
    hn2 = _norm(h1, lw["norm_ffn"], f"{tag}_norm_ffn")
    act = _matmul(hn2, lw["w1"], "nn", name=f"{tag}_mlp_up", post=_relu2_of, out_dtype=BF16)
    h2 = _matmul(act, lw["w2"], "nn", name=f"{tag}_mlp_down", add=h1)
    hn3 = _norm(h2, lw["norm_pl"], f"{tag}_norm_pl")
    gl = _matmul(hn3, lw["w_gate"], "nn", name=f"{tag}_pl_gate")
    e = _matmul(p_i, lw["w_pl"], "nn", name=f"{tag}_pl_proj")
    (h3,) = _rw_fwd(_pl_gate, [_row(h2), _row(gl), _row(e)], [], [(h.shape[1], F32)], name=f"{tag}_pl_mix")
    return h3, (h, hn, proj, y, mres, h1, hn2, act, h2, hn3, gl, e), lw


def _layer_bwd(dh3, p_i, lw, mixer_bwd, mp, saved, tag, early=None):
    h, hn, proj, y, mres, h1, hn2, act, h2, hn3, gl, e = saved
    (dgl, de), _ = _rw_bwd(_pl_gate, [_row(h2), _row(gl), _row(e)], [], [dh3], row_grads=[1, 2], name=f"{tag}_pl_mix_bwd")
    g = dict(w_pl=_matmul(p_i, de, "tn", name=f"{tag}_pl_proj_dw"), w_gate=_matmul(hn3, dgl, "tn", name=f"{tag}_pl_gate_dw"))
    dhn3 = _matmul(dgl, lw["w_gate"], "nt", name=f"{tag}_pl_gate_dx")
    dh2, g["norm_pl"] = _norm_bwd(h2, lw["norm_pl"], dhn3, dh3, f"{tag}_norm_pl_bwd")
    da1 = _matmul(dh2, lw["w2"], "nt", name=f"{tag}_mlp_down_dx", post=_relu2_grad, aux=act, out_dtype=BF16)
    g["w2"] = _matmul(act, dh2, "tn", name=f"{tag}_mlp_down_dw")
    g["w1"] = _matmul(hn2, da1, "tn", name=f"{tag}_mlp_up_dw")
    dhn2 = _matmul(da1, lw["w1"], "nt", name=f"{tag}_mlp_up_dx")
    dh1, g["norm_ffn"] = _norm_bwd(h1, lw["norm_ffn"], dhn2, dh2, f"{tag}_norm_ffn_bwd")
    dy = _matmul(dh1, lw["w_out"], "nt", name=f"{tag}_out_proj_dx")
    g["w_out"] = _matmul(y, dh1, "tn", name=f"{tag}_out_proj_dw")
    dproj, mg, carried = mixer_bwd(proj, mp, mres, dy, None if early is None else early(g))
    g["w_in"] = _matmul(hn, dproj, "tn", name=f"{tag}_in_proj_dw")
    dhn = _matmul(dproj, lw["w_in"], "nt", name=f"{tag}_in_proj_dx")
    dh, g["norm_mix"] = _norm_bwd(h, lw["norm_mix"], dhn, dh1, f"{tag}_norm_mix_bwd")
    return dh, g, mg, carried


WEIGHTS = (
    ("norm_mix", None), ("norm_ffn", None), ("norm_pl", None), ("mlp_w1", 2), ("mlp_w2", 1), ("pl_proj", 2), ("pl_gate", 1),
    ("e_in_proj", 2), ("e_out_proj", 1), ("s5_lam_re", None), ("s5_lam_im", None), ("s5_log_step", None), ("s5_b_re", None),
    ("s5_b_im", None), ("s5_c_re", None), ("s5_c_im", None), ("s5_d", None), ("s5_glu_w", 1), ("s5_glu_b", None),
    ("ssd_conv_w", 2), ("ssd_conv_b", None), ("ssd_dt_bias", None), ("ssd_a_log", None), ("ssd_d", None), ("ssd_norm", None),
    ("o_in_proj", 2), ("o_out_proj", 1), ("rwkv_mu", 1), ("rwkv_w0", 1), ("rwkv_w_up", 2), ("rwkv_a0", 1), ("rwkv_a_up", 2),
    ("rwkv_g_up", 2), ("rwkv_k_k", 1), ("rwkv_k_a", 1), ("rwkv_r_k", None), ("rwkv_ln_g", 1), ("rwkv_ln_b", 1),
    ("lru_conv_w", 2), ("lru_conv_b", 1), ("lru_w_a", None), ("lru_b_a", None), ("lru_w_x", None), ("lru_b_x", None),
    ("lru_lam", None), ("norm_final", None))
MATMUL_WEIGHTS = ("mlp_w1", "mlp_w2", "pl_proj", "pl_gate", "e_in_proj", "e_out_proj", "s5_glu_w", "o_in_proj", "o_out_proj",
                  "rwkv_w_up", "rwkv_a_up", "rwkv_g_up")


LATE_WEIGHTS = (("mlp_w1", 1), ("mlp_w2", 1), ("pl_gate", 1), ("pl_proj", 1))
EARLY_GRADS = LATE_WEIGHTS + (("o_out_proj", 0),)


def kernel(x, p, norm_mix, norm_ffn, norm_pl, mlp_w1, mlp_w2, pl_proj, pl_gate, e_in_proj, e_out_proj, s5_lam_re, s5_lam_im, s5_log_step, s5_b_re, s5_b_im, s5_c_re, s5_c_im, s5_d, s5_glu_w, s5_glu_b, ssd_conv_w, ssd_conv_b, ssd_dt_bias, ssd_a_log, ssd_d, ssd_norm, o_in_proj, o_out_proj, rwkv_mu, rwkv_w0, rwkv_w_up, rwkv_a0, rwkv_a_up, rwkv_g_up, rwkv_k_k, rwkv_k_a, rwkv_r_k, rwkv_ln_g, rwkv_ln_b, lru_conv_w, lru_conv_b, lru_w_a, lru_b_a, lru_w_x, lru_b_x, lru_lam, norm_final, loss_target, m_norm_mix, m_norm_ffn, m_norm_pl, m_mlp_w1, m_mlp_w2, m_pl_proj, m_pl_gate, m_e_in_proj, m_e_out_proj, m_s5_lam_re, m_s5_lam_im, m_s5_log_step, m_s5_b_re, m_s5_b_im, m_s5_c_re, m_s5_c_im, m_s5_d, m_s5_glu_w, m_s5_glu_b, m_ssd_conv_w, m_ssd_conv_b, m_ssd_dt_bias, m_ssd_a_log, m_ssd_d, m_ssd_norm, m_o_in_proj, m_o_out_proj, m_rwkv_mu, m_rwkv_w0, m_rwkv_w_up, m_rwkv_a0, m_rwkv_a_up, m_rwkv_g_up, m_rwkv_k_k, m_rwkv_k_a, m_rwkv_r_k, m_rwkv_ln_g, m_rwkv_ln_b, m_lru_conv_w, m_lru_conv_b, m_lru_w_a, m_lru_b_a, m_lru_w_x, m_lru_b_x, m_lru_lam, m_norm_final, v_norm_mix, v_norm_ffn, v_norm_pl, v_mlp_w1, v_mlp_w2, v_pl_proj, v_pl_gate, v_e_in_proj, v_e_out_proj, v_s5_lam_re, v_s5_lam_im, v_s5_log_step, v_s5_b_re, v_s5_b_im, v_s5_c_re, v_s5_c_im, v_s5_d, v_s5_glu_w, v_s5_glu_b, v_ssd_conv_w, v_ssd_conv_b, v_ssd_dt_bias, v_ssd_a_log, v_ssd_d, v_ssd_norm, v_o_in_proj, v_o_out_proj, v_rwkv_mu, v_rwkv_w0, v_rwkv_w_up, v_rwkv_a0, v_rwkv_a_up, v_rwkv_g_up, v_rwkv_k_k, v_rwkv_k_a, v_rwkv_r_k, v_rwkv_ln_g, v_rwkv_ln_b, v_lru_conv_w, v_lru_conv_b, v_lru_w_a, v_lru_b_a, v_lru_w_x, v_lru_b_x, v_lru_lam, v_norm_final):
    a = dict(locals())
    d_model = x.shape[-1]
    row = lambda v: v.reshape(1, d_model)
    axis = dict(WEIGHTS)
    keys = [(n, i) for n, ax in WEIGHTS if ax is not None for i in range(a[n].shape[0])]
    shard = lambda pre, key: a[pre + key[0]][key[1]]
    piece_axis = lambda key: axis[key[0]] - 1

    def gathered(got, ks):
        parts = _unpack(got, [shard("", k).shape for k in ks], lead=True)
        return {k: _unshard(pt, piece_axis(k)) for k, pt in zip(ks, parts)}

    in_layer1 = lambda k: k[1] == 1 or k[0].startswith(("o_", "rwkv_", "lru_"))
    first_keys = [k for k in keys if k[0] in MATMUL_WEIGHTS and not in_layer1(k)]
    mix1_keys = [k for k in keys if k[0] in MATMUL_WEIGHTS and in_layer1(k) and k not in LATE_WEIGHTS]
    f32_keys = [k for k in keys if k[0] not in MATMUL_WEIGHTS]
    full = gathered(_exchange(_pack([shard("", k) for k in first_keys], BF16), gather=TWO_LEVEL, name="gather_weights_bf16"), first_keys)
    full.update(gathered(_exchange(_pack([shard("", k) for k in f32_keys], F32), gather=True, name="gather_weights_f32"), f32_keys))
    by_name = lambda: {n: (a[n] if ax is None else full.get((n, 0))) for n, ax in WEIGHTS}

    lw0 = dict(norm_mix=row(norm_mix[0]), norm_ffn=row(norm_ffn[0]), norm_pl=row(norm_pl[0]), w_in=_even_in_cols(full["e_in_proj", 0]),
               w_out=full["e_out_proj", 0], w1=full["mlp_w1", 0], w2=full["mlp_w2", 0], w_gate=full["pl_gate", 0], w_pl=full["pl_proj", 0])
    mp0 = _even_prep(by_name())
    mix1_src = _pack([shard("", k) for k in mix1_keys], BF16)

    def take_mix1(got):
        full.update(gathered(got, mix1_keys))
        return {}

    h1, saved0, lw0 = _layer_fwd(x[0], p[0, 0], lw0, _even_fwd, mp0, "l0", carry=(mix1_src, TWO_LEVEL), late=take_mix1)

    def late(got):
        fl = gathered(got, LATE_WEIGHTS)
        return dict(w1=fl["mlp_w1", 1], w2=fl["mlp_w2", 1], w_gate=fl["pl_gate", 1], w_pl=fl["pl_proj", 1])

    lw1 = dict(norm_mix=row(norm_mix[1]), norm_ffn=row(norm_ffn[1]), norm_pl=row(norm_pl[1]), w_in=_odd_in_cols(full["o_in_proj", 0]),
               w_out=full["o_out_proj", 0])
    mp1 = _odd_prep(by_name())
    late_src = _pack([shard("", k) for k in LATE_WEIGHTS], BF16)
    h2, saved1, lw1 = _layer_fwd(h1, p[1, 0], lw1, _odd_fwd, mp1, "l1", carry=(late_src, True), late=late)
    loss_blk, dh, dg_final = _loss_and_grad(h2, row(norm_final), loss_target[0], name="loss")
    loss = lax.psum(loss_blk[0, 0], ("x", "y", "c"))

    def slabs(grad_of, ks):
        return _pack([_to_parts(grad_of[k], piece_axis(k)) for k in ks], BF16, lead=True)

    def global_shape(k):
        return tuple((N_DEV if d == piece_axis(k) else 1) * n for d, n in enumerate(shard("", k).shape))

    early_grads = lambda g: (slabs({("mlp_w1", 1): g["w1"], ("mlp_w2", 1): g["w2"], ("pl_gate", 1): g["w_gate"],
                                    ("pl_proj", 1): g["w_pl"], ("o_out_proj", 0): g["w_out"]}, EARLY_GRADS), False)
    dh, g1, mg1, early_got = _layer_bwd(dh, p[1, 0], lw1, _odd_bwd, mp1, saved1, "l1", early=early_grads)
    l0a_keys = [("mlp_w1", 0), ("pl_proj", 0)]
    mid_keys = [k for k in keys if in_layer1(k) and k not in EARLY_GRADS] + l0a_keys
    mid_grad = {k: (_odd_in_cols_t(g1["w_in"]) if k[0] == "o_in_proj" else mg1[k[0]].reshape(global_shape(k))) for k in mid_keys[:-2]}
    l0_early = lambda g: (slabs({**mid_grad, ("mlp_w1", 0): g["w1"], ("pl_proj", 0): g["w_pl"]}, mid_keys), False)
    dh, g0, mg0, mid_got = _layer_bwd(dh, p[0, 0], lw0, _even_bwd, mp0, saved0, "l0", early=l0_early)

    main_keys = [k for k in keys if not in_layer1(k) and k not in l0a_keys]
    piece_grad = {("mlp_w2", 0): g0["w2"], ("pl_gate", 0): g0["w_gate"], ("e_in_proj", 0): _even_in_cols_t(g0["w_in"]), ("e_out_proj", 0): g0["w_out"]}
    mixer_grads = {**mg0, **mg1}
    for k in main_keys:
        if k not in piece_grad:
            piece_grad[k] = mg0[k[0]].reshape(global_shape(k))
    main_got = _exchange(slabs(piece_grad, main_keys), gather=False, name="scatter_grads")
    piece_out = {}
    for got, ks, tag in ((early_got, EARLY_GRADS, "early"), (mid_got, mid_keys, "mid"), (main_got, main_keys, "main")):
        res = _adamw(got, *[_pack([shard(pre, k) for k in ks], F32) for pre in ("", "m_", "v_")], name=f"adamw_sharded_{tag}")
        for j in range(4):
            piece_out.update({(j, k): v for k, v in zip(ks, _unpack(res[j], [shard("", k).shape for k in ks]))})
    rp = [n for n, ax in WEIGHTS if ax is None]
    rp_grads = {**mixer_grads, "norm_final": dg_final,
                **{n: jnp.stack([g0[n], g1[n]]) for n in ("norm_mix", "norm_ffn", "norm_pl")}}
    parts = _exchange(_pack([rp_grads[n].reshape(a[n].shape) for n in rp], F32), gather=TWO_LEVEL, name="gather_small_grads")
    rp_res = _adamw(parts, *[_pack([a[pre + n] for n in rp], F32) for pre in ("", "m_", "v_")], name="adamw_replicated")

    outs = []
    for j in range(4):
        rp_out = dict(zip(rp, _unpack(rp_res[j], [a[n].shape for n in rp])))
        outs.extend(rp_out[n] if ax is None else jnp.stack([piece_out[j, (n, i)] for i in range(a[n].shape[0])]) for n, ax in WEIGHTS)
    return (loss, dh[None], *outs)
```
